```python
import jax, jax.numpy as jnp
from jax import lax
import numpy as np

D_MODEL = 1024
BATCH = 8
SEQ = 2048
DEPTH = 2

FOURIER_GROUPS = 4
FOURIER_GROUP_DIM = D_MODEL // 8
FOURIER_WIDTH = FOURIER_GROUPS * FOURIER_GROUP_DIM
GMLP_GROUPS = 4
GMLP_GROUP_DIM = D_MODEL // 8
GMLP_WIDTH = GMLP_GROUPS * GMLP_GROUP_DIM
CHUNK = 128
N_BRANCH = 2
D_IN = FOURIER_WIDTH + 2 * GMLP_WIDTH + N_BRANCH * D_MODEL
D_FF_DENSE = 11 * D_MODEL // 4
N_EXPERTS = 8
TOP_K = 2
D_FF_EXPERT = 7 * D_MODEL // 2
N_DENSE = (DEPTH + 1) // 2
N_MOE = DEPTH // 2
RMS_EPS = 1e-6
LN_EPS = 1e-5

kernel_name = 'hybrid_fourier_gmlp_moe_encoder'


def rms_norm(x, g):
    xf = x.astype(jnp.float32)
    y = xf * lax.rsqrt(jnp.mean(xf * xf, axis=-1, keepdims=True) + RMS_EPS)
    return (y * g.astype(jnp.float32)).astype(x.dtype)


def layer_norm(x, g, b):
    xf = x.astype(jnp.float32)
    mu = jnp.mean(xf, axis=-1, keepdims=True)
    xc = xf - mu
    var = jnp.mean(xc * xc, axis=-1, keepdims=True)
    y = xc * lax.rsqrt(var + LN_EPS) * g.astype(jnp.float32) + b.astype(jnp.float32)
    return y.astype(x.dtype)


def fourier_mix(a):
    af = a.astype(jnp.float32)
    return jnp.fft.fft2(af, axes=(1, 3), norm='ortho').real.astype(a.dtype)


def spatial_gating(u, v, w_s, b_s, ln_g, ln_b):
    B, S, _ = v.shape
    v = layer_norm(v, ln_g, ln_b)
    v = v.reshape(B, S // CHUNK, CHUNK, GMLP_GROUPS, GMLP_GROUP_DIM)
    sv = jnp.einsum('gpq,bcqgd->bcpgd', w_s, v) + jnp.transpose(b_s)[:, :, None]
    return u * sv.reshape(B, S, GMLP_WIDTH)


def token_mixer(h, w_in, w_fourier, w_gmlp, w_out, w_s, b_s, ln_g, ln_b):
    B, S, D = h.shape
    z = jnp.einsum('bsd,de->bse', h, w_in)
    c1 = FOURIER_WIDTH
    c2 = c1 + GMLP_WIDTH
    c3 = c2 + GMLP_WIDTH
    za, zu, zv, zg = z[..., :c1], z[..., c1:c2], z[..., c2:c3], z[..., c3:]
    fa = fourier_mix(za.reshape(B, S, FOURIER_GROUPS, FOURIER_GROUP_DIM)).reshape(B, S, FOURIER_WIDTH)
    ya = jnp.einsum('bsf,fd->bsd', fa, w_fourier)
    sg = spatial_gating(jax.nn.gelu(zu, approximate=False), jax.nn.gelu(zv, approximate=False),
                        w_s, b_s, ln_g, ln_b)
    yb = jnp.einsum('bsf,fd->bsd', sg, w_gmlp)
    g = jax.nn.sigmoid(zg.reshape(B, S, N_BRANCH, D))
    m = g[:, :, 0, :] * ya + g[:, :, 1, :] * yb
    return jnp.einsum('bsd,de->bse', m, w_out)


def swiglu(h, w1, w3, w2):
    a = jnp.einsum('...d,df->...f', h, w1)
    b = jnp.einsum('...d,df->...f', h, w3)
    return jnp.einsum('...f,fd->...d', jax.nn.silu(a) * b, w2)


def moe_swiglu(h, router_w, router_b, w1, w3, w2):
    B, S, D = h.shape
    t = h.reshape(B * S, D)
    logits = jnp.einsum('td,de->te', t, router_w).astype(jnp.float32) + router_b.astype(jnp.float32)
    top_vals, top_idx = lax.top_k(logits, TOP_K)
    top_w = jax.nn.softmax(top_vals, axis=-1)
    gate = jnp.sum(jax.nn.one_hot(top_idx, N_EXPERTS, dtype=jnp.float32) * top_w[..., None], axis=1)
    gate = gate.astype(t.dtype)
    out = jnp.zeros_like(t)
    for e in range(N_EXPERTS):
        out = out + gate[:, e:e + 1] * swiglu(t, w1[e], w3[e], w2[e])
    return out.reshape(B, S, D)


def setup_inputs(seed: int = 0) -> dict:
    key = jax.random.key(seed)
    ks = jax.random.split(key, 24)
    f32 = jnp.float32
    nrm = lambda k, shape, fan_in: jax.random.normal(k, shape, f32) * (fan_in ** -0.5)
    gain = lambda k, shape: 1.0 + 0.02 * jax.random.normal(k, shape, f32)
    small = lambda k, shape, s: s * jax.random.normal(k, shape, f32)
    return {
        'x': jax.random.normal(ks[0], (BATCH, SEQ, D_MODEL), f32),
        'pre_mix_g': gain(ks[1], (DEPTH, D_MODEL)),
        'post_mix_g': gain(ks[2], (DEPTH, D_MODEL)),
        'w_in': nrm(ks[3], (DEPTH, D_MODEL, D_IN), D_MODEL),
        'w_fourier': nrm(ks[4], (DEPTH, FOURIER_WIDTH, D_MODEL), FOURIER_WIDTH),
        'w_gmlp': nrm(ks[5], (DEPTH, GMLP_WIDTH, D_MODEL), GMLP_WIDTH),
        'w_out': nrm(ks[6], (DEPTH, D_MODEL, D_MODEL), D_MODEL),
        'w_spatial': nrm(ks[7], (DEPTH, GMLP_GROUPS, CHUNK, CHUNK), CHUNK),
        'b_spatial': gain(ks[8], (DEPTH, GMLP_GROUPS, CHUNK)),
        'gmlp_ln_g': gain(ks[9], (DEPTH, GMLP_WIDTH)),
        'gmlp_ln_b': small(ks[10], (DEPTH, GMLP_WIDTH), 0.02),
        'pre_ffn_g': gain(ks[11], (DEPTH, D_MODEL)),
        'post_ffn_g': gain(ks[12], (DEPTH, D_MODEL)),
        'ffn_w1': nrm(ks[13], (N_DENSE, D_MODEL, D_FF_DENSE), D_MODEL),
        'ffn_w3': nrm(ks[14], (N_DENSE, D_MODEL, D_FF_DENSE), D_MODEL),
        'ffn_w2': nrm(ks[15], (N_DENSE, D_FF_DENSE, D_MODEL), D_FF_DENSE),
        'router_w': nrm(ks[16], (N_MOE, D_MODEL, N_EXPERTS), D_MODEL),
        'router_b': small(ks[17], (N_MOE, N_EXPERTS), 0.01),
        'moe_w1': nrm(ks[18], (N_MOE, N_EXPERTS, D_MODEL, D_FF_EXPERT), D_MODEL),
        'moe_w3': nrm(ks[19], (N_MOE, N_EXPERTS, D_MODEL, D_FF_EXPERT), D_MODEL),
        'moe_w2': nrm(ks[20], (N_MOE, N_EXPERTS, D_FF_EXPERT, D_MODEL), D_FF_EXPERT),
    }


def reference(x, pre_mix_g, post_mix_g, w_in, w_fourier, w_gmlp, w_out, w_spatial, b_spatial,
              gmlp_ln_g, gmlp_ln_b, pre_ffn_g, post_ffn_g, ffn_w1, ffn_w3, ffn_w2,
              router_w, router_b, moe_w1, moe_w3, moe_w2):
    for l in range(DEPTH):
        h = rms_norm(x, pre_mix_g[l])
        y = token_mixer(h, w_in[l], w_fourier[l], w_gmlp[l], w_out[l], w_spatial[l], b_spatial[l],
                        gmlp_ln_g[l], gmlp_ln_b[l])
        x = x + rms_norm(y, post_mix_g[l])
        h = rms_norm(x, pre_ffn_g[l])
        if l % 2 == 0:
            i = l // 2
            y = swiglu(h, ffn_w1[i], ffn_w3[i], ffn_w2[i])
        else:
            i = l // 2
            y = moe_swiglu(h, router_w[i], router_b[i], moe_w1[i], moe_w3[i], moe_w2[i])
        x = x + rms_norm(y, post_ffn_g[l])
    return x
```

```python
import functools

import numpy as np
import jax
import jax.numpy as jnp
from jax import lax
from jax.experimental import pallas as pl
from jax.experimental.pallas import tpu as pltpu

F32 = jnp.float32
BF16 = jnp.bfloat16

D_MODEL = 1024
BATCH = 8
SEQ = 2048
TOKENS = BATCH * SEQ
DEPTH = 2
N_GROUPS = 4
GROUP_DIM = 128
MIX_WIDTH = N_GROUPS * GROUP_DIM
CHUNK = 128
D_IN = 3 * MIX_WIDTH + 2 * D_MODEL
D_FF_DENSE = 2816
N_EXPERTS = 8
D_FF_EXPERT = 3584
RMS_EPS = 1e-6
LN_EPS = 1e-5

ROW_TILE = 512
FOURIER_ROW_TILE = 512
EXPERT_BLOCK = 512
DISPATCH_BLOCK = 256
DISPATCH_CHUNK = 512
COMBINE_TILE = 512
COMBINE_CHUNK = 256
EXPERT_FF_TILE = 1792

SORTED_ROWS = 2 * TOKENS + N_EXPERTS * EXPERT_BLOCK
N_EXPERT_BLOCKS = SORTED_ROWS // EXPERT_BLOCK
N_DISPATCH_BLOCKS = SORTED_ROWS // DISPATCH_BLOCK
N_DISPATCH_CHUNKS = TOKENS // DISPATCH_CHUNK
N_COMBINE_TILES = TOKENS // COMBINE_TILE
DISPATCH_STEPS = N_EXPERTS * (N_DISPATCH_CHUNKS - 1) + N_DISPATCH_BLOCKS
COMBINE_STEPS = SORTED_ROWS // COMBINE_CHUNK + N_EXPERTS * N_COMBINE_TILES

VMEM_LIMIT = 56 * 1024 * 1024


def _params(*sem):
    return pltpu.CompilerParams(dimension_semantics=sem, vmem_limit_bytes=VMEM_LIMIT)


def _resident(shape):
    nd = len(shape)
    return pl.BlockSpec(shape, lambda *_: (0,) * nd, pipeline_mode=pl.Buffered(1))


def _rms(x, g):
    return x * lax.rsqrt(jnp.mean(x * x, axis=-1, keepdims=True) + RMS_EPS) * g


def _dot(a, b):
    return jnp.dot(a, b, preferred_element_type=F32)


def _gelu(x):
    return 0.5 * x * (1.0 + lax.erf(x * np.float32(np.sqrt(0.5))))


def _dft_tables():
    n = np.arange(SEQ, dtype=np.int64)
    ang = 2.0 * np.pi * ((n[:, None] * n[None, :]) % SEQ).astype(np.float64) / SEQ
    scale = 1.0 / np.sqrt(SEQ)
    pos = np.concatenate([np.cos(ang) * scale, -np.sin(ang) * scale], axis=1)
    c = np.arange(GROUP_DIM, dtype=np.int64)
    angc = 2.0 * np.pi * ((c[:, None] * c[None, :]) % GROUP_DIM).astype(np.float64) / GROUP_DIM
    scalec = 1.0 / np.sqrt(GROUP_DIM)
    chan = np.concatenate([np.cos(angc) * scalec, np.sin(angc) * scalec], axis=1)
    return pos.astype(np.float32), chan.astype(np.float32)


_POS_DFT, _CHAN_DFT = _dft_tables()


def _front_kernel(x_ref, g_ref, win_ref, ws_ref, bs_ref, lng_ref, lnb_ref,
                  za_ref, sg_ref, gate_ref):
    h = _rms(x_ref[...], g_ref[...]).astype(BF16)
    za_ref[...] = _dot(h, win_ref[:, 0:MIX_WIDTH]).astype(BF16)
    u = _gelu(_dot(h, win_ref[:, MIX_WIDTH:2 * MIX_WIDTH]))
    v = _gelu(_dot(h, win_ref[:, 2 * MIX_WIDTH:3 * MIX_WIDTH]))
    mu = jnp.mean(v, axis=-1, keepdims=True)
    vc = v - mu
    var = jnp.mean(vc * vc, axis=-1, keepdims=True)
    vln = (vc * lax.rsqrt(var + LN_EPS) * lng_ref[...] + lnb_ref[...]).astype(BF16)
    for c in range(ROW_TILE // CHUNK):
        rows = slice(c * CHUNK, (c + 1) * CHUNK)
        for g in range(N_GROUPS):
            cols = slice(g * GROUP_DIM, (g + 1) * GROUP_DIM)
            sv = _dot(ws_ref[g], vln[rows, cols]) + bs_ref[:, cols]
            sg_ref[rows, cols] = (u[rows, cols] * sv).astype(BF16)
    zg = _dot(h, win_ref[:, 3 * MIX_WIDTH:D_IN])
    gate_ref[...] = jax.nn.sigmoid(zg).astype(BF16)


def _mixer_front(x, g, w_in, w_s, b_full, ln_g, ln_b):
    row = lambda w: pl.BlockSpec((ROW_TILE, w), lambda i: (i, 0))
    return pl.pallas_call(
        _front_kernel,
        grid=(TOKENS // ROW_TILE,),
        in_specs=[row(D_MODEL), _resident((1, D_MODEL)), _resident((D_MODEL, D_IN)),
                  _resident((N_GROUPS, CHUNK, CHUNK)), _resident((CHUNK, MIX_WIDTH)),
                  _resident((1, MIX_WIDTH)), _resident((1, MIX_WIDTH))],
        out_specs=[row(MIX_WIDTH), row(MIX_WIDTH), row(2 * D_MODEL)],
        out_shape=[jax.ShapeDtypeStruct((TOKENS, MIX_WIDTH), BF16),
                   jax.ShapeDtypeStruct((TOKENS, MIX_WIDTH), BF16),
                   jax.ShapeDtypeStruct((TOKENS, 2 * D_MODEL), BF16)],
        compiler_params=_params("arbitrary"),
        name="mixer_front",
    )(x, g, w_in, w_s, b_full, ln_g, ln_b)


def _fourier_kernel(za_ref, chan_ref, pos_ref, out_ref, rhs_ref):
    @pl.when(pl.program_id(1) == 0)
    def _():
        for g in range(N_GROUPS):
            cols = slice(g * GROUP_DIM, (g + 1) * GROUP_DIM)
            xcs = _dot(za_ref[0, :, cols], chan_ref[...])
            rhs_ref[0:SEQ, cols] = xcs[:, 0:GROUP_DIM].astype(BF16)
            rhs_ref[SEQ:2 * SEQ, cols] = xcs[:, GROUP_DIM:2 * GROUP_DIM].astype(BF16)

    out_ref[0] = _dot(pos_ref[...], rhs_ref[...]).astype(BF16)


def _fourier(za, chan, pos):
    return pl.pallas_call(
        _fourier_kernel,
        grid=(BATCH, SEQ // FOURIER_ROW_TILE),
        in_specs=[pl.BlockSpec((1, SEQ, MIX_WIDTH), lambda b, i: (b, 0, 0)),
                  _resident((GROUP_DIM, 2 * GROUP_DIM)),
                  pl.BlockSpec((FOURIER_ROW_TILE, 2 * SEQ), lambda b, i: (i, 0))],
        out_specs=pl.BlockSpec((1, FOURIER_ROW_TILE, MIX_WIDTH), lambda b, i: (b, i, 0)),
        out_shape=jax.ShapeDtypeStruct((BATCH, SEQ, MIX_WIDTH), BF16),
        scratch_shapes=[pltpu.VMEM((2 * SEQ, MIX_WIDTH), BF16)],
        compiler_params=_params("arbitrary", "arbitrary"),
        name="fourier_mix",
    )(za, chan, pos)


def _tail_common(fa_ref, sg_ref, gate_ref, x_ref, wf_ref, wg_ref, wo_ref, pg_ref):
    ya = _dot(fa_ref[...], wf_ref[...])
    yb = _dot(sg_ref[...], wg_ref[...])
    m = gate_ref[:, 0:D_MODEL].astype(F32) * ya + gate_ref[:, D_MODEL:2 * D_MODEL].astype(F32) * yb
    y = _dot(m.astype(BF16), wo_ref[...])
    return x_ref[...] + _rms(y, pg_ref[...])


def _tail_kernel(fa_ref, sg_ref, gate_ref, x_ref, wf_ref, wg_ref, wo_ref, pg_ref, x1_ref):
    x1_ref[...] = _tail_common(fa_ref, sg_ref, gate_ref, x_ref, wf_ref, wg_ref, wo_ref, pg_ref)


def _tail_router_kernel(fa_ref, sg_ref, gate_ref, x_ref, wf_ref, wg_ref, wo_ref, pg_ref,
                        fg_ref, rw_ref, rb_ref, tri_ref,
                        x1_ref, h_ref, gw_ref, sel_ref, rank_ref, cnt_ref, carry_ref):
    @pl.when(pl.program_id(0) == 0)
    def _():
        carry_ref[...] = jnp.zeros_like(carry_ref)

    x1 = _tail_common(fa_ref, sg_ref, gate_ref, x_ref, wf_ref, wg_ref, wo_ref, pg_ref)
    x1_ref[...] = x1
    h = _rms(x1, fg_ref[...])
    h_ref[...] = h.astype(BF16)

    logits = jnp.dot(h, rw_ref[...], preferred_element_type=F32,
                     precision=lax.Precision.HIGHEST) + rb_ref[...]
    col = lax.broadcasted_iota(jnp.int32, logits.shape, 1)
    m1 = jnp.max(logits, axis=-1, keepdims=True)
    i1 = jnp.min(jnp.where(logits == m1, col, N_EXPERTS), axis=-1, keepdims=True)
    rest = jnp.where(col == i1, -jnp.inf, logits)
    m2 = jnp.max(rest, axis=-1, keepdims=True)
    i2 = jnp.min(jnp.where(rest == m2, col, N_EXPERTS), axis=-1, keepdims=True)
    e2 = jnp.exp(m2 - m1)
    den = 1.0 + e2
    gw_ref[...] = jnp.where(col == i1, 1.0 / den, 0.0) + jnp.where(col == i2, e2 / den, 0.0)
    sel = jnp.where((col == i1) | (col == i2), 1.0, 0.0)
    sel_ref[...] = sel.astype(jnp.int32)
    before = _dot(tri_ref[...], sel.astype(BF16)) + carry_ref[...]
    rank_ref[...] = before.astype(jnp.int32)
    carry_ref[...] = carry_ref[...] + jnp.sum(sel, axis=0, keepdims=True)
    cnt_ref[...] = carry_ref[...].astype(jnp.int32)


def _tail_specs():
    row = lambda w: pl.BlockSpec((ROW_TILE, w), lambda i: (i, 0))
    in_specs = [row(MIX_WIDTH), row(MIX_WIDTH), row(2 * D_MODEL), row(D_MODEL),
                _resident((MIX_WIDTH, D_MODEL)), _resident((MIX_WIDTH, D_MODEL)),
                _resident((D_MODEL, D_MODEL)), _resident((1, D_MODEL))]
    return row, in_specs


def _mixer_tail(fa, sg, gate, x, w_f, w_g, w_o, post_g):
    row, in_specs = _tail_specs()
    return pl.pallas_call(
        _tail_kernel,
        grid=(TOKENS // ROW_TILE,),
        in_specs=in_specs,
        out_specs=row(D_MODEL),
        out_shape=jax.ShapeDtypeStruct((TOKENS, D_MODEL), F32),
        compiler_params=_params("arbitrary"),
        name="mixer_tail",
    )(fa, sg, gate, x, w_f, w_g, w_o, post_g)


def _mixer_tail_router(fa, sg, gate, x, w_f, w_g, w_o, post_g, ffn_g, router_w, router_b, tri):
    row, in_specs = _tail_specs()
    in_specs += [_resident((1, D_MODEL)), _resident((D_MODEL, N_EXPERTS)),
                 _resident((1, N_EXPERTS)), _resident((ROW_TILE, ROW_TILE))]
    return pl.pallas_call(
        _tail_router_kernel,
        grid=(TOKENS // ROW_TILE,),
        in_specs=in_specs,
        out_specs=[row(D_MODEL), row(D_MODEL), row(N_EXPERTS), row(N_EXPERTS), row(N_EXPERTS),
                   pl.BlockSpec((1, N_EXPERTS), lambda i: (0, 0))],
        out_shape=[jax.ShapeDtypeStruct((TOKENS, D_MODEL), F32),
                   jax.ShapeDtypeStruct((TOKENS, D_MODEL), BF16),
                   jax.ShapeDtypeStruct((TOKENS, N_EXPERTS), F32),
                   jax.ShapeDtypeStruct((TOKENS, N_EXPERTS), jnp.int32),
                   jax.ShapeDtypeStruct((TOKENS, N_EXPERTS), jnp.int32),
                   jax.ShapeDtypeStruct((1, N_EXPERTS), jnp.int32)],
        scratch_shapes=[pltpu.VMEM((1, N_EXPERTS), F32)],
        compiler_params=_params("arbitrary"),
        name="mixer_tail_router",
    )(fa, sg, gate, x, w_f, w_g, w_o, post_g, ffn_g, router_w, router_b, tri)


def _dense_ffn_kernel(x_ref, fg_ref, w1_ref, w3_ref, w2_ref, pg_ref, out_ref):
    x = x_ref[...]
    h = _rms(x, fg_ref[...]).astype(BF16)
    a = _dot(h, w1_ref[...])
    b = _dot(h, w3_ref[...])
    y = _dot((jax.nn.silu(a) * b).astype(BF16), w2_ref[...])
    out_ref[...] = x + _rms(y, pg_ref[...])


def _dense_ffn(x, ffn_g, w1, w3, w2, post_g):
    row = pl.BlockSpec((ROW_TILE, D_MODEL), lambda i: (i, 0))
    return pl.pallas_call(
        _dense_ffn_kernel,
        grid=(TOKENS // ROW_TILE,),
        in_specs=[row, _resident((1, D_MODEL)), _resident((D_MODEL, D_FF_DENSE)),
                  _resident((D_MODEL, D_FF_DENSE)), _resident((D_FF_DENSE, D_MODEL)),
                  _resident((1, D_MODEL))],
        out_specs=row,
        out_shape=jax.ShapeDtypeStruct((TOKENS, D_MODEL), F32),
        compiler_params=_params("arbitrary"),
        name="dense_ffn",
    )(x, ffn_g, w1, w3, w2, post_g)


def _dispatch_kernel(blk_ref, chk_ref, exp_ref, first_ref, valid_ref,
                     dest_ref, gw_ref, h_ref, xs_ref, gs_ref):
    s = pl.program_id(0)

    @pl.when(valid_ref[s] == 1)
    def _():
        rows = lax.broadcasted_iota(jnp.int32, (DISPATCH_BLOCK, DISPATCH_CHUNK), 0) \
            + blk_ref[s] * DISPATCH_BLOCK
        hit = rows == dest_ref[...]
        picked = _dot(jnp.where(hit, 1.0, 0.0).astype(BF16), h_ref[...])
        gsum = jnp.sum(jnp.where(hit, gw_ref[...], 0.0), axis=-1, keepdims=True)

        @pl.when(first_ref[s] == 1)
        def _():
            xs_ref[...] = picked.astype(BF16)
            gs_ref[...] = gsum

        @pl.when(first_ref[s] == 0)
        def _():
            xs_ref[...] = xs_ref[...] + picked.astype(BF16)
            gs_ref[...] = gs_ref[...] + gsum


def _dispatch(work, dest3, gw3, h):
    blk, chk, exp, first, valid = work
    grid_spec = pltpu.PrefetchScalarGridSpec(
        num_scalar_prefetch=5,
        grid=(DISPATCH_STEPS,),
        in_specs=[
            pl.BlockSpec((None, 1, DISPATCH_CHUNK), lambda s, b, c, e, f, v: (e[s], 0, c[s])),
            pl.BlockSpec((None, 1, DISPATCH_CHUNK), lambda s, b, c, e, f, v: (e[s], 0, c[s])),
            pl.BlockSpec((DISPATCH_CHUNK, D_MODEL), lambda s, b, c, e, f, v: (c[s], 0)),
        ],
        out_specs=[
            pl.BlockSpec((DISPATCH_BLOCK, D_MODEL), lambda s, b, c, e, f, v: (b[s], 0)),
            pl.BlockSpec((DISPATCH_BLOCK, 1), lambda s, b, c, e, f, v: (b[s], 0)),
        ],
    )
    return pl.pallas_call(
        _dispatch_kernel,
        grid_spec=grid_spec,
        out_shape=[jax.ShapeDtypeStruct((SORTED_ROWS, D_MODEL), BF16),
                   jax.ShapeDtypeStruct((SORTED_ROWS, 1), F32)],
        compiler_params=_params("arbitrary"),
        name="moe_dispatch",
    )(blk, chk, exp, first, valid, dest3, gw3, h)


def _expert_kernel(bexp_ref, nblk_ref, xs_ref, gs_ref, w1_ref, w3_ref, w2_ref, ys_ref, acc_ref):
    j = pl.program_id(0)
    f = pl.program_id(1)
    last = pl.num_programs(1) - 1

    @pl.when(j < nblk_ref[0])
    def _():
        x = xs_ref[...]
        a = _dot(x, w1_ref[...])
        b = _dot(x, w3_ref[...])
        part = _dot((jax.nn.silu(a) * b).astype(BF16), w2_ref[...])

        @pl.when(f == 0)
        def _():
            acc_ref[...] = part

        @pl.when(f > 0)
        def _():
            acc_ref[...] = acc_ref[...] + part

        @pl.when(f == last)
        def _():
            ys_ref[...] = (acc_ref[...] * gs_ref[...]).astype(BF16)

    @pl.when((j >= nblk_ref[0]) & (f == last))
    def _():
        ys_ref[...] = jnp.zeros_like(ys_ref)


def _experts(bexp, nblk, xs, gs, w1, w3, w2):
    grid_spec = pltpu.PrefetchScalarGridSpec(
        num_scalar_prefetch=2,
        grid=(N_EXPERT_BLOCKS, D_FF_EXPERT // EXPERT_FF_TILE),
        in_specs=[
            pl.BlockSpec((EXPERT_BLOCK, D_MODEL), lambda j, f, be, nb: (j, 0)),
            pl.BlockSpec((EXPERT_BLOCK, 1), lambda j, f, be, nb: (j, 0)),
            pl.BlockSpec((None, D_MODEL, EXPERT_FF_TILE), lambda j, f, be, nb: (be[j], 0, f)),
            pl.BlockSpec((None, D_MODEL, EXPERT_FF_TILE), lambda j, f, be, nb: (be[j], 0, f)),
            pl.BlockSpec((None, EXPERT_FF_TILE, D_MODEL), lambda j, f, be, nb: (be[j], f, 0)),
        ],
        out_specs=pl.BlockSpec((EXPERT_BLOCK, D_MODEL), lambda j, f, be, nb: (j, 0)),
        scratch_shapes=[pltpu.VMEM((EXPERT_BLOCK, D_MODEL), F32)],
    )
    return pl.pallas_call(
        _expert_kernel,
        grid_spec=grid_spec,
        out_shape=jax.ShapeDtypeStruct((SORTED_ROWS, D_MODEL), BF16),
        compiler_params=_params("arbitrary", "arbitrary"),
        name="moe_experts",
    )(bexp, nblk, xs, gs, w1, w3, w2)


def _combine_kernel(tile_ref, chk_ref, exp_ref, first_ref, last_ref, valid_ref,
                    dest_ref, ys_ref, x_ref, pg_ref, out_ref, acc_ref):
    s = pl.program_id(0)

    @pl.when(valid_ref[s] == 1)
    def _():
        rows = lax.broadcasted_iota(jnp.int32, (COMBINE_CHUNK, COMBINE_TILE), 0) \
            + chk_ref[s] * COMBINE_CHUNK
        hit = jnp.where(rows == dest_ref[...], 1.0, 0.0).astype(BF16)
        part = lax.dot_general(hit, ys_ref[...], (((0,), (0,)), ((), ())),
                               preferred_element_type=F32)

        @pl.when(first_ref[s] == 1)
        def _():
            acc_ref[...] = part

        @pl.when(first_ref[s] == 0)
        def _():
            acc_ref[...] = acc_ref[...] + part

        @pl.when(last_ref[s] == 1)
        def _():
            out_ref[...] = x_ref[...] + _rms(acc_ref[...], pg_ref[...])


def _combine(work, dest3, ys, x, post_g):
    tile, chk, exp, first, last, valid = work
    grid_spec = pltpu.PrefetchScalarGridSpec(
        num_scalar_prefetch=6,
        grid=(COMBINE_STEPS,),
        in_specs=[
            pl.BlockSpec((None, 1, COMBINE_TILE), lambda s, t, c, e, f, l, v: (e[s], 0, t[s])),
            pl.BlockSpec((COMBINE_CHUNK, D_MODEL), lambda s, t, c, e, f, l, v: (c[s], 0)),
            pl.BlockSpec((COMBINE_TILE, D_MODEL), lambda s, t, c, e, f, l, v: (t[s], 0)),
            pl.BlockSpec((1, D_MODEL), lambda s, t, c, e, f, l, v: (0, 0)),
        ],
        out_specs=pl.BlockSpec((COMBINE_TILE, D_MODEL), lambda s, t, c, e, f, l, v: (t[s], 0)),
        scratch_shapes=[pltpu.VMEM((COMBINE_TILE, D_MODEL), F32)],
    )
    return pl.pallas_call(
        _combine_kernel,
        grid_spec=grid_spec,
        out_shape=jax.ShapeDtypeStruct((TOKENS, D_MODEL), F32),
        compiler_params=_params("arbitrary"),
        name="moe_combine",
    )(tile, chk, exp, first, last, valid, dest3, ys, x, post_g)


def _searchsorted_right(sorted_vals, queries):
    return jnp.sum((sorted_vals[None, :] <= queries[:, None]).astype(jnp.int32), axis=1)


def _routing_plan(sel, rank, counts):
    i32 = jnp.int32
    counts = counts.reshape(N_EXPERTS)
    padded = ((counts + EXPERT_BLOCK - 1) // EXPERT_BLOCK) * EXPERT_BLOCK
    offs = jnp.cumsum(padded) - padded
    dest = jnp.where(sel == 1, rank + offs[None, :], -1)
    dest3 = dest.T.reshape(N_EXPERTS, 1, TOKENS)

    cum_blk = jnp.cumsum(padded // EXPERT_BLOCK)
    n_blk = cum_blk[-1]
    jb = jnp.arange(N_EXPERT_BLOCKS, dtype=i32)
    bexp = _searchsorted_right(cum_blk, jnp.minimum(jb, n_blk - 1))
    bexp = jnp.minimum(bexp, N_EXPERTS - 1).astype(i32)

    incl = rank + sel
    chunk_end = incl[DISPATCH_CHUNK - 1::DISPATCH_CHUNK, :].T
    jd = jnp.arange(N_DISPATCH_BLOCKS, dtype=i32)
    e_d = bexp[jd // (EXPERT_BLOCK // DISPATCH_BLOCK)]
    lo_row = jd * DISPATCH_BLOCK - offs[e_d]
    has_rows = (jd < n_blk * (EXPERT_BLOCK // DISPATCH_BLOCK)) & (lo_row < counts[e_d])
    hi_row = jnp.minimum(lo_row + DISPATCH_BLOCK, counts[e_d])
    ends = chunk_end[e_d]
    c_lo = jnp.sum((ends <= lo_row[:, None]).astype(i32), axis=1)
    c_hi = jnp.sum((ends < hi_row[:, None]).astype(i32), axis=1)
    c_lo = jnp.where(has_rows, c_lo, 0)
    c_hi = jnp.where(has_rows, c_hi, 0)
    n_ch = c_hi - c_lo + 1
    cum = jnp.cumsum(n_ch)
    start = cum - n_ch
    total = cum[-1]
    sd = jnp.arange(DISPATCH_STEPS, dtype=i32)
    valid = sd < total
    sdc = jnp.minimum(sd, total - 1)
    blk = jnp.minimum(_searchsorted_right(cum, sdc), N_DISPATCH_BLOCKS - 1).astype(i32)
    chk = (c_lo[blk] + sdc - start[blk]).astype(i32)
    first = ((sdc == start[blk]) & valid).astype(i32)
    d_work = (blk, chk, e_d[blk].astype(i32), first, valid.astype(i32))

    tile_start = rank[::COMBINE_TILE, :]
    tile_end = jnp.concatenate([tile_start[1:], counts[None, :]], axis=0)
    lo = (tile_start + offs[None, :]).reshape(-1)
    hi = (tile_end + offs[None, :]).reshape(-1)
    nonempty = hi > lo
    k_lo = lo // COMBINE_CHUNK
    k_hi = jnp.where(nonempty, (hi - 1) // COMBINE_CHUNK, k_lo - 1)
    n_k = k_hi - k_lo + 1
    cumk = jnp.cumsum(n_k)
    startk = cumk - n_k
    totalk = cumk[-1]
    sc = jnp.arange(COMBINE_STEPS, dtype=i32)
    validk = sc < totalk
    scc = jnp.minimum(sc, totalk - 1)
    pair = jnp.minimum(_searchsorted_right(cumk, scc), N_COMBINE_TILES * N_EXPERTS - 1).astype(i32)
    tile = pair // N_EXPERTS
    chunk = (k_lo[pair] + scc - startk[pair]).astype(i32)
    tile_first = startk[::N_EXPERTS]
    tile_last = jnp.concatenate([tile_first[1:], totalk[None]]) - 1
    firstk = ((scc == tile_first[tile]) & validk).astype(i32)
    lastk = ((scc == tile_last[tile]) & validk).astype(i32)
    c_work = (tile.astype(i32), chunk, (pair % N_EXPERTS).astype(i32), firstk, lastk,
              validk.astype(i32))
    return dest3, bexp, n_blk.reshape(1).astype(i32), d_work, c_work


def _moe(x1, h, gw, sel, rank, counts, w1, w3, w2, post_g):
    dest3, bexp, n_blk, d_work, c_work = _routing_plan(sel, rank, counts)
    gw3 = gw.T.reshape(N_EXPERTS, 1, TOKENS)
    xs, gs = _dispatch(d_work, dest3, gw3, h)
    ys = _experts(bexp, n_blk, xs, gs, w1, w3, w2)
    return _combine(c_work, dest3, ys, x1, post_g)


def kernel(x, pre_mix_g, post_mix_g, w_in, w_fourier, w_gmlp, w_out, w_spatial, b_spatial,
           gmlp_ln_g, gmlp_ln_b, pre_ffn_g, post_ffn_g, ffn_w1, ffn_w3, ffn_w2,
           router_w, router_b, moe_w1, moe_w3, moe_w2):
    pos = jnp.asarray(_POS_DFT).astype(BF16)
    chan = jnp.asarray(_CHAN_DFT).astype(BF16)
    tri = jnp.asarray(np.tril(np.ones((ROW_TILE, ROW_TILE), np.float32), -1), dtype=BF16)
    row = lambda v: v.reshape(1, -1)

    xf = x.reshape(TOKENS, D_MODEL)
    for l in range(DEPTH):
        b_full = jnp.repeat(b_spatial[l].T, GROUP_DIM, axis=1)
        za, sg, gate = _mixer_front(xf, row(pre_mix_g[l]), w_in[l].astype(BF16),
                                    w_spatial[l].astype(BF16), b_full,
                                    row(gmlp_ln_g[l]), row(gmlp_ln_b[l]))
        fa = _fourier(za.reshape(BATCH, SEQ, MIX_WIDTH), chan, pos).reshape(TOKENS, MIX_WIDTH)
        tail_args = (fa, sg, gate, xf, w_fourier[l].astype(BF16), w_gmlp[l].astype(BF16),
                     w_out[l].astype(BF16), row(post_mix_g[l]))
        i = l // 2
        if l % 2 == 0:
            x1 = _mixer_tail(*tail_args)
            xf = _dense_ffn(x1, row(pre_ffn_g[l]), ffn_w1[i].astype(BF16), ffn_w3[i].astype(BF16),
                            ffn_w2[i].astype(BF16), row(post_ffn_g[l]))
        else:
            x1, h, gw, sel, rank, counts = _mixer_tail_router(
                *tail_args, row(pre_ffn_g[l]), router_w[i], row(router_b[i]), tri)
            xf = _moe(x1, h, gw, sel, rank, counts, moe_w1[i].astype(BF16), moe_w3[i].astype(BF16),
                      moe_w2[i].astype(BF16), row(post_ffn_g[l]))
    return xf.reshape(BATCH, SEQ, D_MODEL)
```

```python
import numpy as np
import jax
import jax.numpy as jnp
from jax import lax
from jax.experimental import pallas as pl
from jax.experimental.pallas import tpu as pltpu

F32 = jnp.float32
BF16 = jnp.bfloat16
I32 = jnp.int32

D_MODEL = 1024
BATCH = 8
SEQ = 2048
TOKENS = BATCH * SEQ
DEPTH = 2
N_GROUPS = 4
GROUP_DIM = 128
MIX_WIDTH = N_GROUPS * GROUP_DIM
CHUNK = 128
D_IN = 3 * MIX_WIDTH + 2 * D_MODEL
D_FF_DENSE = 2816
N_EXPERTS = 8
D_FF_EXPERT = 3584
RMS_EPS = 1e-6
LN_EPS = 1e-5
LANES = 128

ROW_TILE = 512
FOURIER_ROW_TILE = 512
EXPERT_BLOCK = 512
SORT_BLOCK = 256
TOKEN_WINDOW = 1280
FINISH_TILE = 256
EXPERT_FF_TILE = 1792

SORTED_ROWS = 2 * TOKENS + N_EXPERTS * EXPERT_BLOCK
N_EXPERT_BLOCKS = SORTED_ROWS // EXPERT_BLOCK
N_SORT_BLOCKS = SORTED_ROWS // SORT_BLOCK
N_TOKEN_GROUPS = TOKENS // LANES
N_FINISH_TILES = TOKENS // FINISH_TILE

VMEM_LIMIT = 56 * 1024 * 1024


def _params(*sem):
    return pltpu.CompilerParams(dimension_semantics=sem, vmem_limit_bytes=VMEM_LIMIT)


def _resident(shape):
    nd = len(shape)
    return pl.BlockSpec(shape, lambda *_: (0,) * nd, pipeline_mode=pl.Buffered(1))


def _rms(x, g):
    return x * lax.rsqrt(jnp.mean(x * x, axis=-1, keepdims=True) + RMS_EPS) * g


def _dot(a, b):
    return jnp.dot(a, b, preferred_element_type=F32)


def _dot_nt(a, b):
    return lax.dot_general(a, b, (((1,), (1,)), ((), ())), preferred_element_type=F32)


def _dot_tn(a, b):
    return lax.dot_general(a, b, (((0,), (0,)), ((), ())), preferred_element_type=F32)


def _gelu(x):
    return 0.5 * x * (1.0 + lax.erf(x * np.float32(np.sqrt(0.5))))


def _split_bf16(x):
    hi = x.astype(BF16)
    return hi, (x - hi.astype(F32)).astype(BF16)


def _dft_tables():
    n = np.arange(SEQ, dtype=np.int64)
    ang = 2.0 * np.pi * ((n[:, None] * n[None, :]) % SEQ).astype(np.float64) / SEQ
    scale = 1.0 / np.sqrt(SEQ)
    pos = np.concatenate([np.cos(ang) * scale, -np.sin(ang) * scale], axis=1)
    c = np.arange(GROUP_DIM, dtype=np.int64)
    angc = 2.0 * np.pi * ((c[:, None] * c[None, :]) % GROUP_DIM).astype(np.float64) / GROUP_DIM
    scalec = 1.0 / np.sqrt(GROUP_DIM)
    chan = np.concatenate([np.cos(angc) * scalec, np.sin(angc) * scalec], axis=1)
    return pos.astype(np.float32), chan.astype(np.float32)


_POS_DFT, _CHAN_DFT = _dft_tables()


def _front_kernel(x_ref, g_ref, win_ref, ws_ref, bs_ref, lng_ref, lnb_ref,
                  za_ref, sg_ref, gate_ref):
    h = _rms(x_ref[...], g_ref[...]).astype(BF16)
    za_ref[...] = _dot(h, win_ref[:, 0:MIX_WIDTH]).astype(BF16)
    u = _gelu(_dot(h, win_ref[:, MIX_WIDTH:2 * MIX_WIDTH]))
    v = _gelu(_dot(h, win_ref[:, 2 * MIX_WIDTH:3 * MIX_WIDTH]))
    mu = jnp.mean(v, axis=-1, keepdims=True)
    vc = v - mu
    var = jnp.mean(vc * vc, axis=-1, keepdims=True)
    vln = (vc * lax.rsqrt(var + LN_EPS) * lng_ref[...] + lnb_ref[...]).astype(BF16)
    for c in range(ROW_TILE // CHUNK):
        rows = slice(c * CHUNK, (c + 1) * CHUNK)
        for g in range(N_GROUPS):
            cols = slice(g * GROUP_DIM, (g + 1) * GROUP_DIM)
            sv = _dot(ws_ref[g], vln[rows, cols]) + bs_ref[:, cols]
            sg_ref[rows, cols] = (u[rows, cols] * sv).astype(BF16)
    zg = _dot(h, win_ref[:, 3 * MIX_WIDTH:D_IN])
    gate_ref[...] = jax.nn.sigmoid(zg).astype(BF16)


def _mixer_front(x, g, w_in, w_s, b_full, ln_g, ln_b):
    row = lambda w: pl.BlockSpec((ROW_TILE, w), lambda i: (i, 0))
    return pl.pallas_call(
        _front_kernel,
        grid=(TOKENS // ROW_TILE,),
        in_specs=[row(D_MODEL), _resident((1, D_MODEL)), _resident((D_MODEL, D_IN)),
                  _resident((N_GROUPS, CHUNK, CHUNK)), _resident((CHUNK, MIX_WIDTH)),
                  _resident((1, MIX_WIDTH)), _resident((1, MIX_WIDTH))],
        out_specs=[row(MIX_WIDTH), row(MIX_WIDTH), row(2 * D_MODEL)],
        out_shape=[jax.ShapeDtypeStruct((TOKENS, MIX_WIDTH), BF16),
                   jax.ShapeDtypeStruct((TOKENS, MIX_WIDTH), BF16),
                   jax.ShapeDtypeStruct((TOKENS, 2 * D_MODEL), BF16)],
        compiler_params=_params("arbitrary"),
        name="mixer_front",
    )(x, g, w_in, w_s, b_full, ln_g, ln_b)


def _fourier_kernel(za_ref, chan_ref, pos_ref, out_ref, rhs_ref):
    @pl.when(pl.program_id(1) == 0)
    def _():
        for g in range(N_GROUPS):
            cols = slice(g * GROUP_DIM, (g + 1) * GROUP_DIM)
            xcs = _dot(za_ref[0, :, cols], chan_ref[...])
            rhs_ref[0:SEQ, cols] = xcs[:, 0:GROUP_DIM].astype(BF16)
            rhs_ref[SEQ:2 * SEQ, cols] = xcs[:, GROUP_DIM:2 * GROUP_DIM].astype(BF16)

    out_ref[0] = _dot(pos_ref[...], rhs_ref[...]).astype(BF16)


def _fourier(za, chan, pos):
    return pl.pallas_call(
        _fourier_kernel,
        grid=(BATCH, SEQ // FOURIER_ROW_TILE),
        in_specs=[pl.BlockSpec((1, SEQ, MIX_WIDTH), lambda b, i: (b, 0, 0)),
                  _resident((GROUP_DIM, 2 * GROUP_DIM)),
                  pl.BlockSpec((FOURIER_ROW_TILE, 2 * SEQ), lambda b, i: (i, 0))],
        out_specs=pl.BlockSpec((1, FOURIER_ROW_TILE, MIX_WIDTH), lambda b, i: (b, i, 0)),
        out_shape=jax.ShapeDtypeStruct((BATCH, SEQ, MIX_WIDTH), BF16),
        scratch_shapes=[pltpu.VMEM((2 * SEQ, MIX_WIDTH), BF16)],
        compiler_params=_params("arbitrary", "arbitrary"),
        name="fourier_mix",
    )(za, chan, pos)


def _tail_common(fa_ref, sg_ref, gate_ref, x_ref, wf_ref, wg_ref, wo_ref, pg_ref):
    ya = _dot(fa_ref[...], wf_ref[...])
    yb = _dot(sg_ref[...], wg_ref[...])
    m = gate_ref[:, 0:D_MODEL].astype(F32) * ya + gate_ref[:, D_MODEL:2 * D_MODEL].astype(F32) * yb
    y = _dot(m.astype(BF16), wo_ref[...])
    return x_ref[...] + _rms(y, pg_ref[...])


def _tail_kernel(fa_ref, sg_ref, gate_ref, x_ref, wf_ref, wg_ref, wo_ref, pg_ref, x1_ref):
    x1_ref[...] = _tail_common(fa_ref, sg_ref, gate_ref, x_ref, wf_ref, wg_ref, wo_ref, pg_ref)


def _tail_router_kernel(fa_ref, sg_ref, gate_ref, x_ref, wf_ref, wg_ref, wo_ref, pg_ref,
                        fg_ref, rwt_ref, rb_ref, tri_ref,
                        x1_ref, h_ref, gw_ref, sel_ref, rank_ref, cnt_ref, carry_ref):
    @pl.when(pl.program_id(0) == 0)
    def _():
        carry_ref[...] = jnp.zeros_like(carry_ref)

    x1 = _tail_common(fa_ref, sg_ref, gate_ref, x_ref, wf_ref, wg_ref, wo_ref, pg_ref)
    x1_ref[...] = x1
    h = _rms(x1, fg_ref[...])
    h_hi, h_lo = _split_bf16(h)
    h_ref[...] = h_hi

    w_hi, w_lo = _split_bf16(rwt_ref[...])
    logits = _dot_nt(w_hi, h_hi) + _dot_nt(w_hi, h_lo) + _dot_nt(w_lo, h_hi) + rb_ref[...]
    row = lax.broadcasted_iota(I32, logits.shape, 0)
    m1 = jnp.max(logits, axis=0, keepdims=True)
    i1 = jnp.min(jnp.where(logits == m1, row, N_EXPERTS), axis=0, keepdims=True)
    rest = jnp.where(row == i1, -jnp.inf, logits)
    m2 = jnp.max(rest, axis=0, keepdims=True)
    i2 = jnp.min(jnp.where(rest == m2, row, N_EXPERTS), axis=0, keepdims=True)
    e2 = jnp.exp(m2 - m1)
    den = 1.0 + e2
    gw_ref[...] = jnp.where(row == i1, 1.0 / den, 0.0) + jnp.where(row == i2, e2 / den, 0.0)
    sel = jnp.where((row == i1) | (row == i2), 1.0, 0.0)
    sel_ref[...] = sel.astype(I32)
    before = _dot(sel.astype(BF16), tri_ref[...]) + carry_ref[...]
    rank_ref[...] = before.astype(I32)
    carry_ref[...] = carry_ref[...] + jnp.sum(sel, axis=1, keepdims=True)
    cnt_ref[...] = carry_ref[...].astype(I32)


def _tail_specs():
    row = lambda w: pl.BlockSpec((ROW_TILE, w), lambda i: (i, 0))
    in_specs = [row(MIX_WIDTH), row(MIX_WIDTH), row(2 * D_MODEL), row(D_MODEL),
                _resident((MIX_WIDTH, D_MODEL)), _resident((MIX_WIDTH, D_MODEL)),
                _resident((D_MODEL, D_MODEL)), _resident((1, D_MODEL))]
    return row, in_specs


def _mixer_tail(fa, sg, gate, x, w_f, w_g, w_o, post_g):
    row, in_specs = _tail_specs()
    return pl.pallas_call(
        _tail_kernel,
        grid=(TOKENS // ROW_TILE,),
        in_specs=in_specs,
        out_specs=row(D_MODEL),
        out_shape=jax.ShapeDtypeStruct((TOKENS, D_MODEL), F32),
        compiler_params=_params("arbitrary"),
        name="mixer_tail",
    )(fa, sg, gate, x, w_f, w_g, w_o, post_g)


def _mixer_tail_router(fa, sg, gate, x, w_f, w_g, w_o, post_g, ffn_g, router_wt, router_b, tri):
    row, in_specs = _tail_specs()
    in_specs += [_resident((1, D_MODEL)), _resident((N_EXPERTS, D_MODEL)),
                 _resident((N_EXPERTS, 1)), _resident((ROW_TILE, ROW_TILE))]
    col = pl.BlockSpec((N_EXPERTS, ROW_TILE), lambda i: (0, i))
    per_token = lambda dt: jax.ShapeDtypeStruct((N_EXPERTS, TOKENS), dt)
    return pl.pallas_call(
        _tail_router_kernel,
        grid=(TOKENS // ROW_TILE,),
        in_specs=in_specs,
        out_specs=[row(D_MODEL), row(D_MODEL), col, col, col,
                   pl.BlockSpec((N_EXPERTS, 1), lambda i: (0, 0))],
        out_shape=[jax.ShapeDtypeStruct((TOKENS, D_MODEL), F32),
                   jax.ShapeDtypeStruct((TOKENS, D_MODEL), BF16),
                   per_token(F32), per_token(I32), per_token(I32),
                   jax.ShapeDtypeStruct((N_EXPERTS, 1), I32)],
        scratch_shapes=[pltpu.VMEM((N_EXPERTS, 1), F32)],
        compiler_params=_params("arbitrary"),
        name="mixer_tail_router",
    )(fa, sg, gate, x, w_f, w_g, w_o, post_g, ffn_g, router_wt, router_b, tri)


def _dense_ffn_kernel(x_ref, fg_ref, w1_ref, w3_ref, w2_ref, pg_ref, out_ref):
    x = x_ref[...]
    h = _rms(x, fg_ref[...]).astype(BF16)
    a = _dot(h, w1_ref[...])
    b = _dot(h, w3_ref[...])
    y = _dot((jax.nn.silu(a) * b).astype(BF16), w2_ref[...])
    out_ref[...] = x + _rms(y, pg_ref[...])


def _dense_ffn(x, ffn_g, w1, w3, w2, post_g):
    row = pl.BlockSpec((ROW_TILE, D_MODEL), lambda i: (i, 0))
    return pl.pallas_call(
        _dense_ffn_kernel,
        grid=(TOKENS // ROW_TILE,),
        in_specs=[row, _resident((1, D_MODEL)), _resident((D_MODEL, D_FF_DENSE)),
                  _resident((D_MODEL, D_FF_DENSE)), _resident((D_FF_DENSE, D_MODEL)),
                  _resident((1, D_MODEL))],
        out_specs=row,
        out_shape=jax.ShapeDtypeStruct((TOKENS, D_MODEL), F32),
        compiler_params=_params("arbitrary"),
        name="dense_ffn",
    )(x, ffn_g, w1, w3, w2, post_g)


def _window_hits(k, block, start_ref, dest_ref):
    wanted = start_ref[block] + k * TOKEN_WINDOW
    begin = pl.multiple_of(jnp.minimum(wanted, TOKENS - TOKEN_WINDOW), LANES)
    tok = lax.broadcasted_iota(I32, (1, TOKEN_WINDOW), 1) + begin
    dest = jnp.where(tok >= wanted, dest_ref[:, pl.ds(begin, TOKEN_WINDOW)], -1)
    rows = lax.broadcasted_iota(I32, (SORT_BLOCK, TOKEN_WINDOW), 0) + block * SORT_BLOCK
    return begin, rows == dest


def _dispatch_kernel(exp_ref, start_ref, nwin_ref, dest_ref, gw_ref, h_ref, xs_ref, gs_ref,
                     acc_ref):
    j = pl.program_id(0)
    acc_ref[...] = jnp.zeros_like(acc_ref)

    def window(k, gsum):
        begin, hit = _window_hits(k, j, start_ref, dest_ref)
        onehot = jnp.where(hit, 1.0, 0.0).astype(BF16)
        acc_ref[...] += _dot(onehot, h_ref[pl.ds(begin, TOKEN_WINDOW), :])
        gw = gw_ref[:, pl.ds(begin, TOKEN_WINDOW)]
        return gsum + jnp.sum(jnp.where(hit, gw, 0.0), axis=-1, keepdims=True)

    gs_ref[...] = lax.fori_loop(0, nwin_ref[j], window, jnp.zeros((SORT_BLOCK, 1), F32))
    xs_ref[...] = acc_ref[...].astype(BF16)


def _per_expert_row():
    return pl.BlockSpec((None, 1, TOKENS),
                        lambda j, e, s, n: (e[jnp.minimum(j, N_SORT_BLOCKS - 1)], 0, 0))


def _dispatch(plan, dest3, gw3, h):
    exp, start, nwin = plan
    grid_spec = pltpu.PrefetchScalarGridSpec(
        num_scalar_prefetch=3,
        grid=(N_SORT_BLOCKS,),
        in_specs=[_per_expert_row(), _per_expert_row(),
                  pl.BlockSpec((TOKENS, D_MODEL), lambda j, e, s, n: (0, 0),
                               pipeline_mode=pl.Buffered(1))],
        out_specs=[pl.BlockSpec((SORT_BLOCK, D_MODEL), lambda j, e, s, n: (j, 0)),
                   pl.BlockSpec((SORT_BLOCK, 1), lambda j, e, s, n: (j, 0))],
        scratch_shapes=[pltpu.VMEM((SORT_BLOCK, D_MODEL), F32)],
    )
    return pl.pallas_call(
        _dispatch_kernel,
        grid_spec=grid_spec,
        out_shape=[jax.ShapeDtypeStruct((SORTED_ROWS, D_MODEL), BF16),
                   jax.ShapeDtypeStruct((SORTED_ROWS, 1), F32)],
        compiler_params=_params("arbitrary"),
        name="moe_dispatch",
    )(exp, start, nwin, dest3, gw3, h)


def _expert_kernel(bexp_ref, nblk_ref, xs_ref, gs_ref, w1_ref, w3_ref, w2_ref, ys_ref, acc_ref):
    j = pl.program_id(0)
    f = pl.program_id(1)
    last = pl.num_programs(1) - 1

    @pl.when(j < nblk_ref[0])
    def _():
        x = xs_ref[...]
        a = _dot(x, w1_ref[...])
        b = _dot(x, w3_ref[...])
        part = _dot((jax.nn.silu(a) * b).astype(BF16), w2_ref[...])

        @pl.when(f == 0)
        def _():
            acc_ref[...] = part

        @pl.when(f > 0)
        def _():
            acc_ref[...] = acc_ref[...] + part

        @pl.when(f == last)
        def _():
            ys_ref[...] = (acc_ref[...] * gs_ref[...]).astype(BF16)

    @pl.when((j >= nblk_ref[0]) & (f == last))
    def _():
        ys_ref[...] = jnp.zeros_like(ys_ref)


def _experts(bexp, nblk, xs, gs, w1, w3, w2):
    grid_spec = pltpu.PrefetchScalarGridSpec(
        num_scalar_prefetch=2,
        grid=(N_EXPERT_BLOCKS, D_FF_EXPERT // EXPERT_FF_TILE),
        in_specs=[
            pl.BlockSpec((EXPERT_BLOCK, D_MODEL), lambda j, f, be, nb: (j, 0)),
            pl.BlockSpec((EXPERT_BLOCK, 1), lambda j, f, be, nb: (j, 0)),
            pl.BlockSpec((None, D_MODEL, EXPERT_FF_TILE), lambda j, f, be, nb: (be[j], 0, f)),
            pl.BlockSpec((None, D_MODEL, EXPERT_FF_TILE), lambda j, f, be, nb: (be[j], 0, f)),
            pl.BlockSpec((None, EXPERT_FF_TILE, D_MODEL), lambda j, f, be, nb: (be[j], f, 0)),
        ],
        out_specs=pl.BlockSpec((EXPERT_BLOCK, D_MODEL), lambda j, f, be, nb: (j, 0)),
        scratch_shapes=[pltpu.VMEM((EXPERT_BLOCK, D_MODEL), F32)],
    )
    return pl.pallas_call(
        _expert_kernel,
        grid_spec=grid_spec,
        out_shape=jax.ShapeDtypeStruct((SORTED_ROWS, D_MODEL), BF16),
        compiler_params=_params("arbitrary", "arbitrary"),
        name="moe_experts",
    )(bexp, nblk, xs, gs, w1, w3, w2)


def _combine_kernel(exp_ref, start_ref, nwin_ref, dest_ref, ys_ref, x_ref, pg_ref, out_ref,
                    acc_ref):
    s = pl.program_id(0)

    @pl.when(s == 0)
    def _():
        def clear(i, carry):
            acc_ref[pl.ds(pl.multiple_of(i * ROW_TILE, ROW_TILE), ROW_TILE), :] = \
                jnp.zeros((ROW_TILE, D_MODEL), BF16)
            return carry
        lax.fori_loop(0, TOKENS // ROW_TILE, clear, 0)

    @pl.when(s < N_SORT_BLOCKS)
    def _():
        def window(k, carry):
            begin, hit = _window_hits(k, s, start_ref, dest_ref)
            onehot = jnp.where(hit, 1.0, 0.0).astype(BF16)
            part = _dot_tn(onehot, ys_ref[...])
            sl = pl.ds(begin, TOKEN_WINDOW)
            acc_ref[sl, :] = (acc_ref[sl, :].astype(F32) + part).astype(BF16)
            return carry
        lax.fori_loop(0, nwin_ref[s], window, 0)

    @pl.when(s >= N_SORT_BLOCKS)
    def _():
        t0 = pl.multiple_of((s - N_SORT_BLOCKS) * FINISH_TILE, FINISH_TILE)
        y = acc_ref[pl.ds(t0, FINISH_TILE), :].astype(F32)
        out_ref[...] = x_ref[...] + _rms(y, pg_ref[...])


def _combine(plan, dest3, ys, x, post_g):
    exp, start, nwin = plan
    tile = lambda s, e, st, n: (jnp.maximum(s - N_SORT_BLOCKS, 0), 0)
    grid_spec = pltpu.PrefetchScalarGridSpec(
        num_scalar_prefetch=3,
        grid=(N_SORT_BLOCKS + N_FINISH_TILES,),
        in_specs=[
            _per_expert_row(),
            pl.BlockSpec((SORT_BLOCK, D_MODEL),
                         lambda s, e, st, n: (jnp.minimum(s, N_SORT_BLOCKS - 1), 0)),
            pl.BlockSpec((FINISH_TILE, D_MODEL), tile),
            pl.BlockSpec((1, D_MODEL), lambda s, e, st, n: (0, 0)),
        ],
        out_specs=pl.BlockSpec((FINISH_TILE, D_MODEL), tile),
        scratch_shapes=[pltpu.VMEM((TOKENS, D_MODEL), BF16)],
    )
    return pl.pallas_call(
        _combine_kernel,
        grid_spec=grid_spec,
        out_shape=jax.ShapeDtypeStruct((TOKENS, D_MODEL), F32),
        compiler_params=_params("arbitrary"),
        name="moe_combine",
    )(exp, start, nwin, dest3, ys, x, post_g)


def _routing_plan(sel, rank, counts):
    counts = counts.reshape(N_EXPERTS)
    padded = ((counts + EXPERT_BLOCK - 1) // EXPERT_BLOCK) * EXPERT_BLOCK
    offs = jnp.cumsum(padded) - padded
    dest3 = jnp.where(sel == 1, rank + offs[:, None], -1).reshape(N_EXPERTS, 1, TOKENS)

    cum_blk = jnp.cumsum(padded // EXPERT_BLOCK)
    n_blk = cum_blk[-1]
    jb = jnp.minimum(jnp.arange(N_EXPERT_BLOCKS, dtype=I32), n_blk - 1)
    bexp = jnp.sum((cum_blk[None, :] <= jb[:, None]).astype(I32), axis=1)
    bexp = jnp.minimum(bexp, N_EXPERTS - 1).astype(I32)

    group_end = (rank + sel)[:, LANES - 1::LANES]
    per_expert_block = EXPERT_BLOCK // SORT_BLOCK
    js = jnp.arange(N_SORT_BLOCKS, dtype=I32)
    e_s = bexp[js // per_expert_block]
    lo_row = js * SORT_BLOCK - offs[e_s]
    has_rows = (js < n_blk * per_expert_block) & (lo_row < counts[e_s])
    hi_row = jnp.minimum(lo_row + SORT_BLOCK, counts[e_s])
    ends = group_end[e_s]
    g_lo = jnp.sum((ends <= lo_row[:, None]).astype(I32), axis=1)
    g_hi = jnp.sum((ends < hi_row[:, None]).astype(I32), axis=1)
    span = (g_hi - g_lo + 1) * LANES
    n_win = jnp.where(has_rows, (span + TOKEN_WINDOW - 1) // TOKEN_WINDOW, 0).astype(I32)
    start = jnp.where(has_rows, g_lo * LANES, 0).astype(I32)
    return dest3, bexp, n_blk.reshape(1).astype(I32), (e_s.astype(I32), start, n_win)


def _moe(x1, h, gw, sel, rank, counts, w1, w3, w2, post_g):
    dest3, bexp, n_blk, plan = _routing_plan(sel, rank, counts)
    gw3 = gw.reshape(N_EXPERTS, 1, TOKENS)
    xs, gs = _dispatch(plan, dest3, gw3, h)
    ys = _experts(bexp, n_blk, xs, gs, w1, w3, w2)
    return _combine(plan, dest3, ys, x1, post_g)


def kernel(x, pre_mix_g, post_mix_g, w_in, w_fourier, w_gmlp, w_out, w_spatial, b_spatial,
           gmlp_ln_g, gmlp_ln_b, pre_ffn_g, post_ffn_g, ffn_w1, ffn_w3, ffn_w2,
           router_w, router_b, moe_w1, moe_w3, moe_w2):
    pos = jnp.asarray(_POS_DFT).astype(BF16)
    chan = jnp.asarray(_CHAN_DFT).astype(BF16)
    tri = jnp.asarray(np.triu(np.ones((ROW_TILE, ROW_TILE), np.float32), 1)).astype(BF16)
    row = lambda v: v.reshape(1, -1)

    xf = x.reshape(TOKENS, D_MODEL)
    for l in range(DEPTH):
        b_full = jnp.repeat(b_spatial[l].T, GROUP_DIM, axis=1)
        za, sg, gate = _mixer_front(xf, row(pre_mix_g[l]), w_in[l].astype(BF16),
                                    w_spatial[l].astype(BF16), b_full,
                                    row(gmlp_ln_g[l]), row(gmlp_ln_b[l]))
        fa = _fourier(za.reshape(BATCH, SEQ, MIX_WIDTH), chan, pos).reshape(TOKENS, MIX_WIDTH)
        tail_args = (fa, sg, gate, xf, w_fourier[l].astype(BF16), w_gmlp[l].astype(BF16),
                     w_out[l].astype(BF16), row(post_mix_g[l]))
        i = l // 2
        if l % 2 == 0:
            x1 = _mixer_tail(*tail_args)
            xf = _dense_ffn(x1, row(pre_ffn_g[l]), ffn_w1[i].astype(BF16), ffn_w3[i].astype(BF16),
                            ffn_w2[i].astype(BF16), row(post_ffn_g[l]))
        else:
            x1, h, gw, sel, rank, counts = _mixer_tail_router(
                *tail_args, row(pre_ffn_g[l]), router_w[i].T, router_b[i].reshape(N_EXPERTS, 1),
                tri)
            xf = _moe(x1, h, gw, sel, rank, counts, moe_w1[i].astype(BF16), moe_w3[i].astype(BF16),
                      moe_w2[i].astype(BF16), row(post_ffn_g[l]))
    return xf.reshape(BATCH, SEQ, D_MODEL)
```

```python
import numpy as np
import jax
import jax.numpy as jnp
from jax import lax
from jax.experimental import pallas as pl
from jax.experimental.pallas import tpu as pltpu

F32 = jnp.float32
BF16 = jnp.bfloat16
I32 = jnp.int32

D_MODEL = 1024
BATCH = 8
SEQ = 2048
TOKENS = BATCH * SEQ
DEPTH = 2
N_GROUPS = 4
GROUP_DIM = 128
MIX_WIDTH = N_GROUPS * GROUP_DIM
CHUNK = 128
D_IN = 3 * MIX_WIDTH + 2 * D_MODEL
D_FF_DENSE = 2816
N_EXPERTS = 8
D_FF_EXPERT = 3584
RMS_EPS = 1e-6
LN_EPS = 1e-5
LANES = 128

ROW_TILE = 512
FOURIER_ROW_TILE = 512
EXPERT_BLOCK = 512
SORT_BLOCK = 256
TOKEN_WINDOW = 1280
COMBINE_TILE = 256
COMBINE_WINDOW = 128
BF16_ROWS = 16
EXPERT_FF_TILE = 1792

SORTED_ROWS = 2 * TOKENS + N_EXPERTS * EXPERT_BLOCK
N_EXPERT_BLOCKS = SORTED_ROWS // EXPERT_BLOCK
N_SORT_BLOCKS = SORTED_ROWS // SORT_BLOCK
N_COMBINE_TILES = TOKENS // COMBINE_TILE

VMEM_LIMIT = 56 * 1024 * 1024


def _params(*sem):
    return pltpu.CompilerParams(dimension_semantics=sem, vmem_limit_bytes=VMEM_LIMIT)


def _resident(shape):
    nd = len(shape)
    return pl.BlockSpec(shape, lambda *_: (0,) * nd, pipeline_mode=pl.Buffered(1))


def _rms(x, g):
    return x * lax.rsqrt(jnp.mean(x * x, axis=-1, keepdims=True) + RMS_EPS) * g


def _dot(a, b):
    return jnp.dot(a, b, preferred_element_type=F32)


def _dot_nt(a, b):
    return lax.dot_general(a, b, (((1,), (1,)), ((), ())), preferred_element_type=F32)


def _dot_tn(a, b):
    return lax.dot_general(a, b, (((0,), (0,)), ((), ())), preferred_element_type=F32)


def _gelu(x):
    return 0.5 * x * (1.0 + lax.erf(x * np.float32(np.sqrt(0.5))))


def _split_bf16(x):
    hi = x.astype(BF16)
    return hi, (x - hi.astype(F32)).astype(BF16)


def _dft_tables():
    n = np.arange(SEQ, dtype=np.int64)
    ang = 2.0 * np.pi * ((n[:, None] * n[None, :]) % SEQ).astype(np.float64) / SEQ
    scale = 1.0 / np.sqrt(SEQ)
    pos = np.concatenate([np.cos(ang) * scale, -np.sin(ang) * scale], axis=1)
    c = np.arange(GROUP_DIM, dtype=np.int64)
    angc = 2.0 * np.pi * ((c[:, None] * c[None, :]) % GROUP_DIM).astype(np.float64) / GROUP_DIM
    scalec = 1.0 / np.sqrt(GROUP_DIM)
    chan = np.concatenate([np.cos(angc) * scalec, np.sin(angc) * scalec], axis=1)
    return pos.astype(np.float32), chan.astype(np.float32)


_POS_DFT, _CHAN_DFT = _dft_tables()


def _front_kernel(x_ref, g_ref, win_ref, ws_ref, bs_ref, lng_ref, lnb_ref,
                  za_ref, sg_ref, gate_ref):
    h = _rms(x_ref[...], g_ref[...]).astype(BF16)
    za_ref[...] = _dot(h, win_ref[:, 0:MIX_WIDTH]).astype(BF16)
    u = _gelu(_dot(h, win_ref[:, MIX_WIDTH:2 * MIX_WIDTH]))
    v = _gelu(_dot(h, win_ref[:, 2 * MIX_WIDTH:3 * MIX_WIDTH]))
    mu = jnp.mean(v, axis=-1, keepdims=True)
    vc = v - mu
    var = jnp.mean(vc * vc, axis=-1, keepdims=True)
    vln = (vc * lax.rsqrt(var + LN_EPS) * lng_ref[...] + lnb_ref[...]).astype(BF16)
    for c in range(ROW_TILE // CHUNK):
        rows = slice(c * CHUNK, (c + 1) * CHUNK)
        for g in range(N_GROUPS):
            cols = slice(g * GROUP_DIM, (g + 1) * GROUP_DIM)
            sv = _dot(ws_ref[g], vln[rows, cols]) + bs_ref[:, cols]
            sg_ref[rows, cols] = (u[rows, cols] * sv).astype(BF16)
    zg = _dot(h, win_ref[:, 3 * MIX_WIDTH:D_IN])
    gate_ref[...] = jax.nn.sigmoid(zg).astype(BF16)


def _mixer_front(x, g, w_in, w_s, b_full, ln_g, ln_b):
    row = lambda w: pl.BlockSpec((ROW_TILE, w), lambda i: (i, 0))
    return pl.pallas_call(
        _front_kernel,
        grid=(TOKENS // ROW_TILE,),
        in_specs=[row(D_MODEL), _resident((1, D_MODEL)), _resident((D_MODEL, D_IN)),
                  _resident((N_GROUPS, CHUNK, CHUNK)), _resident((CHUNK, MIX_WIDTH)),
                  _resident((1, MIX_WIDTH)), _resident((1, MIX_WIDTH))],
        out_specs=[row(MIX_WIDTH), row(MIX_WIDTH), row(2 * D_MODEL)],
        out_shape=[jax.ShapeDtypeStruct((TOKENS, MIX_WIDTH), BF16),
                   jax.ShapeDtypeStruct((TOKENS, MIX_WIDTH), BF16),
                   jax.ShapeDtypeStruct((TOKENS, 2 * D_MODEL), BF16)],
        compiler_params=_params("arbitrary"),
        name="mixer_front",
    )(x, g, w_in, w_s, b_full, ln_g, ln_b)


def _fourier_kernel(za_ref, chan_ref, pos_ref, out_ref, rhs_ref):
    @pl.when(pl.program_id(1) == 0)
    def _():
        for g in range(N_GROUPS):
            cols = slice(g * GROUP_DIM, (g + 1) * GROUP_DIM)
            xcs = _dot(za_ref[0, :, cols], chan_ref[...])
            rhs_ref[0:SEQ, cols] = xcs[:, 0:GROUP_DIM].astype(BF16)
            rhs_ref[SEQ:2 * SEQ, cols] = xcs[:, GROUP_DIM:2 * GROUP_DIM].astype(BF16)

    out_ref[0] = _dot(pos_ref[...], rhs_ref[...]).astype(BF16)


def _fourier(za, chan, pos):
    return pl.pallas_call(
        _fourier_kernel,
        grid=(BATCH, SEQ // FOURIER_ROW_TILE),
        in_specs=[pl.BlockSpec((1, SEQ, MIX_WIDTH), lambda b, i: (b, 0, 0)),
                  _resident((GROUP_DIM, 2 * GROUP_DIM)),
                  pl.BlockSpec((FOURIER_ROW_TILE, 2 * SEQ), lambda b, i: (i, 0))],
        out_specs=pl.BlockSpec((1, FOURIER_ROW_TILE, MIX_WIDTH), lambda b, i: (b, i, 0)),
        out_shape=jax.ShapeDtypeStruct((BATCH, SEQ, MIX_WIDTH), BF16),
        scratch_shapes=[pltpu.VMEM((2 * SEQ, MIX_WIDTH), BF16)],
        compiler_params=_params("arbitrary", "arbitrary"),
        name="fourier_mix",
    )(za, chan, pos)


def _tail_common(fa_ref, sg_ref, gate_ref, x_ref, wf_ref, wg_ref, wo_ref, pg_ref):
    ya = _dot(fa_ref[...], wf_ref[...])
    yb = _dot(sg_ref[...], wg_ref[...])
    m = gate_ref[:, 0:D_MODEL].astype(F32) * ya + gate_ref[:, D_MODEL:2 * D_MODEL].astype(F32) * yb
    y = _dot(m.astype(BF16), wo_ref[...])
    return x_ref[...] + _rms(y, pg_ref[...])


def _tail_kernel(fa_ref, sg_ref, gate_ref, x_ref, wf_ref, wg_ref, wo_ref, pg_ref, x1_ref):
    x1_ref[...] = _tail_common(fa_ref, sg_ref, gate_ref, x_ref, wf_ref, wg_ref, wo_ref, pg_ref)


def _tail_router_kernel(fa_ref, sg_ref, gate_ref, x_ref, wf_ref, wg_ref, wo_ref, pg_ref,
                        fg_ref, rwt_ref, rb_ref, tri_ref,
                        x1_ref, h_ref, gw_ref, sel_ref, rank_ref, cnt_ref, carry_ref):
    @pl.when(pl.program_id(0) == 0)
    def _():
        carry_ref[...] = jnp.zeros_like(carry_ref)

    x1 = _tail_common(fa_ref, sg_ref, gate_ref, x_ref, wf_ref, wg_ref, wo_ref, pg_ref)
    x1_ref[...] = x1
    h = _rms(x1, fg_ref[...])
    h_hi, h_lo = _split_bf16(h)
    h_ref[...] = h_hi

    w_hi, w_lo = _split_bf16(rwt_ref[...])
    logits = _dot_nt(w_hi, h_hi) + _dot_nt(w_hi, h_lo) + _dot_nt(w_lo, h_hi) + rb_ref[...]
    row = lax.broadcasted_iota(I32, logits.shape, 0)
    m1 = jnp.max(logits, axis=0, keepdims=True)
    i1 = jnp.min(jnp.where(logits == m1, row, N_EXPERTS), axis=0, keepdims=True)
    rest = jnp.where(row == i1, -jnp.inf, logits)
    m2 = jnp.max(rest, axis=0, keepdims=True)
    i2 = jnp.min(jnp.where(rest == m2, row, N_EXPERTS), axis=0, keepdims=True)
    e2 = jnp.exp(m2 - m1)
    den = 1.0 + e2
    gw_ref[...] = jnp.where(row == i1, 1.0 / den, 0.0) + jnp.where(row == i2, e2 / den, 0.0)
    sel = jnp.where((row == i1) | (row == i2), 1.0, 0.0)
    sel_ref[...] = sel.astype(I32)
    before = _dot(sel.astype(BF16), tri_ref[...]) + carry_ref[...]
    rank_ref[...] = before.astype(I32)
    carry_ref[...] = carry_ref[...] + jnp.sum(sel, axis=1, keepdims=True)
    cnt_ref[...] = carry_ref[...].astype(I32)


def _tail_specs():
    row = lambda w: pl.BlockSpec((ROW_TILE, w), lambda i: (i, 0))
    in_specs = [row(MIX_WIDTH), row(MIX_WIDTH), row(2 * D_MODEL), row(D_MODEL),
                _resident((MIX_WIDTH, D_MODEL)), _resident((MIX_WIDTH, D_MODEL)),
                _resident((D_MODEL, D_MODEL)), _resident((1, D_MODEL))]
    return row, in_specs


def _mixer_tail(fa, sg, gate, x, w_f, w_g, w_o, post_g):
    row, in_specs = _tail_specs()
    return pl.pallas_call(
        _tail_kernel,
        grid=(TOKENS // ROW_TILE,),
        in_specs=in_specs,
        out_specs=row(D_MODEL),
        out_shape=jax.ShapeDtypeStruct((TOKENS, D_MODEL), F32),
        compiler_params=_params("arbitrary"),
        name="mixer_tail",
    )(fa, sg, gate, x, w_f, w_g, w_o, post_g)


def _mixer_tail_router(fa, sg, gate, x, w_f, w_g, w_o, post_g, ffn_g, router_wt, router_b, tri):
    row, in_specs = _tail_specs()
    in_specs += [_resident((1, D_MODEL)), _resident((N_EXPERTS, D_MODEL)),
                 _resident((N_EXPERTS, 1)), _resident((ROW_TILE, ROW_TILE))]
    col = pl.BlockSpec((N_EXPERTS, ROW_TILE), lambda i: (0, i))
    per_token = lambda dt: jax.ShapeDtypeStruct((N_EXPERTS, TOKENS), dt)
    return pl.pallas_call(
        _tail_router_kernel,
        grid=(TOKENS // ROW_TILE,),
        in_specs=in_specs,
        out_specs=[row(D_MODEL), row(D_MODEL), col, col, col,
                   pl.BlockSpec((N_EXPERTS, 1), lambda i: (0, 0))],
        out_shape=[jax.ShapeDtypeStruct((TOKENS, D_MODEL), F32),
                   jax.ShapeDtypeStruct((TOKENS, D_MODEL), BF16),
                   per_token(F32), per_token(I32), per_token(I32),
                   jax.ShapeDtypeStruct((N_EXPERTS, 1), I32)],
        scratch_shapes=[pltpu.VMEM((N_EXPERTS, 1), F32)],
        compiler_params=_params("arbitrary"),
        name="mixer_tail_router",
    )(fa, sg, gate, x, w_f, w_g, w_o, post_g, ffn_g, router_wt, router_b, tri)


def _dense_ffn_kernel(x_ref, fg_ref, w1_ref, w3_ref, w2_ref, pg_ref, out_ref):
    x = x_ref[...]
    h = _rms(x, fg_ref[...]).astype(BF16)
    a = _dot(h, w1_ref[...])
    b = _dot(h, w3_ref[...])
    y = _dot((jax.nn.silu(a) * b).astype(BF16), w2_ref[...])
    out_ref[...] = x + _rms(y, pg_ref[...])


def _dense_ffn(x, ffn_g, w1, w3, w2, post_g):
    row = pl.BlockSpec((ROW_TILE, D_MODEL), lambda i: (i, 0))
    return pl.pallas_call(
        _dense_ffn_kernel,
        grid=(TOKENS // ROW_TILE,),
        in_specs=[row, _resident((1, D_MODEL)), _resident((D_MODEL, D_FF_DENSE)),
                  _resident((D_MODEL, D_FF_DENSE)), _resident((D_FF_DENSE, D_MODEL)),
                  _resident((1, D_MODEL))],
        out_specs=row,
        out_shape=jax.ShapeDtypeStruct((TOKENS, D_MODEL), F32),
        compiler_params=_params("arbitrary"),
        name="dense_ffn",
    )(x, ffn_g, w1, w3, w2, post_g)


def _window_hits(k, block, start_ref, dest_ref):
    wanted = start_ref[block] + k * TOKEN_WINDOW
    begin = pl.multiple_of(jnp.minimum(wanted, TOKENS - TOKEN_WINDOW), LANES)
    tok = lax.broadcasted_iota(I32, (1, TOKEN_WINDOW), 1) + begin
    dest = jnp.where(tok >= wanted, dest_ref[:, pl.ds(begin, TOKEN_WINDOW)], -1)
    rows = lax.broadcasted_iota(I32, (SORT_BLOCK, TOKEN_WINDOW), 0) + block * SORT_BLOCK
    return begin, rows == dest


def _dispatch_kernel(exp_ref, start_ref, nwin_ref, dest_ref, gw_ref, h_ref, xs_ref, gs_ref,
                     acc_ref):
    j = pl.program_id(0)
    acc_ref[...] = jnp.zeros_like(acc_ref)

    def window(k, gsum):
        begin, hit = _window_hits(k, j, start_ref, dest_ref)
        onehot = jnp.where(hit, 1.0, 0.0).astype(BF16)
        acc_ref[...] += _dot(onehot, h_ref[pl.ds(begin, TOKEN_WINDOW), :])
        gw = gw_ref[:, pl.ds(begin, TOKEN_WINDOW)]
        return gsum + jnp.sum(jnp.where(hit, gw, 0.0), axis=-1, keepdims=True)

    gs_ref[...] = lax.fori_loop(0, nwin_ref[j], window, jnp.zeros((SORT_BLOCK, 1), F32))
    xs_ref[...] = acc_ref[...].astype(BF16)


def _per_expert_row():
    return pl.BlockSpec((None, 1, TOKENS),
                        lambda j, e, s, n: (e[jnp.minimum(j, N_SORT_BLOCKS - 1)], 0, 0))


def _dispatch(plan, dest3, gw3, h):
    exp, start, nwin = plan
    grid_spec = pltpu.PrefetchScalarGridSpec(
        num_scalar_prefetch=3,
        grid=(N_SORT_BLOCKS,),
        in_specs=[_per_expert_row(), _per_expert_row(),
                  pl.BlockSpec((TOKENS, D_MODEL), lambda j, e, s, n: (0, 0),
                               pipeline_mode=pl.Buffered(1))],
        out_specs=[pl.BlockSpec((SORT_BLOCK, D_MODEL), lambda j, e, s, n: (j, 0)),
                   pl.BlockSpec((SORT_BLOCK, 1), lambda j, e, s, n: (j, 0))],
        scratch_shapes=[pltpu.VMEM((SORT_BLOCK, D_MODEL), F32)],
    )
    return pl.pallas_call(
        _dispatch_kernel,
        grid_spec=grid_spec,
        out_shape=[jax.ShapeDtypeStruct((SORTED_ROWS, D_MODEL), BF16),
                   jax.ShapeDtypeStruct((SORTED_ROWS, 1), F32)],
        compiler_params=_params("arbitrary"),
        name="moe_dispatch",
    )(exp, start, nwin, dest3, gw3, h)


def _expert_kernel(bexp_ref, nblk_ref, xs_ref, gs_ref, w1_ref, w3_ref, w2_ref, ys_ref, acc_ref):
    j = pl.program_id(0)
    f = pl.program_id(1)
    last = pl.num_programs(1) - 1

    @pl.when(j < nblk_ref[0])
    def _():
        x = xs_ref[...]
        a = _dot(x, w1_ref[...])
        b = _dot(x, w3_ref[...])
        part = _dot((jax.nn.silu(a) * b).astype(BF16), w2_ref[...])

        @pl.when(f == 0)
        def _():
            acc_ref[...] = part

        @pl.when(f > 0)
        def _():
            acc_ref[...] = acc_ref[...] + part

        @pl.when(f == last)
        def _():
            ys_ref[...] = (acc_ref[...] * gs_ref[...]).astype(BF16)

    @pl.when((j >= nblk_ref[0]) & (f == last))
    def _():
        ys_ref[...] = jnp.zeros_like(ys_ref)


def _experts(bexp, nblk, xs, gs, w1, w3, w2):
    grid_spec = pltpu.PrefetchScalarGridSpec(
        num_scalar_prefetch=2,
        grid=(N_EXPERT_BLOCKS, D_FF_EXPERT // EXPERT_FF_TILE),
        in_specs=[
            pl.BlockSpec((EXPERT_BLOCK, D_MODEL), lambda j, f, be, nb: (j, 0)),
            pl.BlockSpec((EXPERT_BLOCK, 1), lambda j, f, be, nb: (j, 0)),
            pl.BlockSpec((None, D_MODEL, EXPERT_FF_TILE), lambda j, f, be, nb: (be[j], 0, f)),
            pl.BlockSpec((None, D_MODEL, EXPERT_FF_TILE), lambda j, f, be, nb: (be[j], 0, f)),
            pl.BlockSpec((None, EXPERT_FF_TILE, D_MODEL), lambda j, f, be, nb: (be[j], f, 0)),
        ],
        out_specs=pl.BlockSpec((EXPERT_BLOCK, D_MODEL), lambda j, f, be, nb: (j, 0)),
        scratch_shapes=[pltpu.VMEM((EXPERT_BLOCK, D_MODEL), F32)],
    )
    return pl.pallas_call(
        _expert_kernel,
        grid_spec=grid_spec,
        out_shape=jax.ShapeDtypeStruct((SORTED_ROWS, D_MODEL), BF16),
        compiler_params=_params("arbitrary", "arbitrary"),
        name="moe_experts",
    )(bexp, nblk, xs, gs, w1, w3, w2)


def _combine_copies(wstart_ref, ys_hbm, buf_ref, sem_ref, tile, rnd, slot):
    copies = []
    for e in range(N_EXPERTS):
        wanted = wstart_ref[tile * N_EXPERTS + e] + rnd * COMBINE_WINDOW
        begin = pl.multiple_of(jnp.minimum(wanted, SORTED_ROWS - COMBINE_WINDOW), BF16_ROWS)
        copies.append(pltpu.make_async_copy(
            ys_hbm.at[pl.ds(begin, COMBINE_WINDOW), :],
            buf_ref.at[slot, pl.ds(e * COMBINE_WINDOW, COMBINE_WINDOW), :],
            sem_ref.at[slot, e]))
    return copies


def _combine_kernel(wstart_ref, nround_ref, dest_ref, ys_hbm, x_ref, pg_ref, out_ref,
                    buf_ref, sem_ref):
    i = pl.program_id(0)
    n_tiles = pl.num_programs(0)
    slot = lax.rem(i, 2)

    @pl.when(i == 0)
    def _():
        for c in _combine_copies(wstart_ref, ys_hbm, buf_ref, sem_ref, 0, 0, 0):
            c.start()

    @pl.when(i + 1 < n_tiles)
    def _():
        for c in _combine_copies(wstart_ref, ys_hbm, buf_ref, sem_ref, i + 1, 0, 1 - slot):
            c.start()

    def gather(rnd, fetch_slot):
        sub = lax.broadcasted_iota(I32, (COMBINE_WINDOW, 1), 0)
        pieces = []
        for e in range(N_EXPERTS):
            wanted = wstart_ref[i * N_EXPERTS + e] + rnd * COMBINE_WINDOW
            begin = jnp.minimum(wanted, SORTED_ROWS - COMBINE_WINDOW)
            rowid = sub + begin
            rowid = jnp.where(rowid >= wanted, rowid, -2)
            pieces.append(jnp.where(rowid == dest_ref[e:e + 1, :], 1.0, 0.0).astype(BF16))
        onehot = jnp.concatenate(pieces, axis=0)
        return _dot_tn(onehot, buf_ref[fetch_slot])

    for c in _combine_copies(wstart_ref, ys_hbm, buf_ref, sem_ref, i, 0, slot):
        c.wait()
    y = gather(0, slot)

    def extra(rnd, y):
        copies = _combine_copies(wstart_ref, ys_hbm, buf_ref, sem_ref, i, rnd, 2)
        for c in copies:
            c.start()
        for c in copies:
            c.wait()
        return y + gather(rnd, 2)

    y = lax.fori_loop(1, nround_ref[i], extra, y)
    out_ref[...] = x_ref[...] + _rms(y, pg_ref[...])


def _combine(wstart, nround, dest, ys, x, post_g):
    tile = lambda i, ws, nr: (i, 0)
    grid_spec = pltpu.PrefetchScalarGridSpec(
        num_scalar_prefetch=2,
        grid=(TOKENS // COMBINE_TILE,),
        in_specs=[
            pl.BlockSpec((N_EXPERTS, COMBINE_TILE), lambda i, ws, nr: (0, i)),
            pl.BlockSpec(memory_space=pl.ANY),
            pl.BlockSpec((COMBINE_TILE, D_MODEL), tile),
            pl.BlockSpec((1, D_MODEL), lambda i, ws, nr: (0, 0)),
        ],
        out_specs=pl.BlockSpec((COMBINE_TILE, D_MODEL), tile),
        scratch_shapes=[pltpu.VMEM((3, N_EXPERTS * COMBINE_WINDOW, D_MODEL), BF16),
                        pltpu.SemaphoreType.DMA((3, N_EXPERTS))],
    )
    return pl.pallas_call(
        _combine_kernel,
        grid_spec=grid_spec,
        out_shape=jax.ShapeDtypeStruct((TOKENS, D_MODEL), F32),
        compiler_params=_params("arbitrary"),
        name="moe_combine",
    )(wstart, nround, dest, ys, x, post_g)


def _routing_plan(sel, rank, counts):
    counts = counts.reshape(N_EXPERTS)
    padded = ((counts + EXPERT_BLOCK - 1) // EXPERT_BLOCK) * EXPERT_BLOCK
    offs = jnp.cumsum(padded) - padded
    dest = jnp.where(sel == 1, rank + offs[:, None], -1)

    cum_blk = jnp.cumsum(padded // EXPERT_BLOCK)
    n_blk = cum_blk[-1]
    jb = jnp.minimum(jnp.arange(N_EXPERT_BLOCKS, dtype=I32), n_blk - 1)
    bexp = jnp.sum((cum_blk[None, :] <= jb[:, None]).astype(I32), axis=1)
    bexp = jnp.minimum(bexp, N_EXPERTS - 1).astype(I32)

    group_end = (rank + sel)[:, LANES - 1::LANES]
    per_expert_block = EXPERT_BLOCK // SORT_BLOCK
    js = jnp.arange(N_SORT_BLOCKS, dtype=I32)
    e_s = bexp[js // per_expert_block]
    lo_row = js * SORT_BLOCK - offs[e_s]
    has_rows = (js < n_blk * per_expert_block) & (lo_row < counts[e_s])
    hi_row = jnp.minimum(lo_row + SORT_BLOCK, counts[e_s])
    ends = group_end[e_s]
    g_lo = jnp.sum((ends <= lo_row[:, None]).astype(I32), axis=1)
    g_hi = jnp.sum((ends < hi_row[:, None]).astype(I32), axis=1)
    span = (g_hi - g_lo + 1) * LANES
    n_win = jnp.where(has_rows, (span + TOKEN_WINDOW - 1) // TOKEN_WINDOW, 0).astype(I32)
    start = jnp.where(has_rows, g_lo * LANES, 0).astype(I32)
    d_plan = (e_s.astype(I32), start, n_win)

    tile_lo = rank[:, ::COMBINE_TILE]
    tile_hi = jnp.concatenate([tile_lo[:, 1:], counts[:, None]], axis=1)
    lo = tile_lo + offs[:, None]
    hi = tile_hi + offs[:, None]
    wstart = jnp.minimum((lo // BF16_ROWS) * BF16_ROWS, SORTED_ROWS - COMBINE_WINDOW)
    rounds = jnp.where(hi > lo, (hi - wstart + COMBINE_WINDOW - 1) // COMBINE_WINDOW, 0)
    n_round = jnp.maximum(jnp.max(rounds, axis=0), 1).astype(I32)
    c_plan = (wstart.T.reshape(-1).astype(I32), n_round)
    return dest, bexp, n_blk.reshape(1).astype(I32), d_plan, c_plan


def _moe(x1, h, gw, sel, rank, counts, w1, w3, w2, post_g):
    dest, bexp, n_blk, d_plan, c_plan = _routing_plan(sel, rank, counts)
    dest3 = dest.reshape(N_EXPERTS, 1, TOKENS)
    gw3 = gw.reshape(N_EXPERTS, 1, TOKENS)
    xs, gs = _dispatch(d_plan, dest3, gw3, h)
    ys = _experts(bexp, n_blk, xs, gs, w1, w3, w2)
    return _combine(*c_plan, dest, ys, x1, post_g)


def kernel(x, pre_mix_g, post_mix_g, w_in, w_fourier, w_gmlp, w_out, w_spatial, b_spatial,
           gmlp_ln_g, gmlp_ln_b, pre_ffn_g, post_ffn_g, ffn_w1, ffn_w3, ffn_w2,
           router_w, router_b, moe_w1, moe_w3, moe_w2):
    pos = jnp.asarray(_POS_DFT).astype(BF16)
    chan = jnp.asarray(_CHAN_DFT).astype(BF16)
    tri = jnp.asarray(np.triu(np.ones((ROW_TILE, ROW_TILE), np.float32), 1)).astype(BF16)
    row = lambda v: v.reshape(1, -1)

    xf = x.reshape(TOKENS, D_MODEL)
    for l in range(DEPTH):
        b_full = jnp.repeat(b_spatial[l].T, GROUP_DIM, axis=1)
        za, sg, gate = _mixer_front(xf, row(pre_mix_g[l]), w_in[l].astype(BF16),
                                    w_spatial[l].astype(BF16), b_full,
                                    row(gmlp_ln_g[l]), row(gmlp_ln_b[l]))
        fa = _fourier(za.reshape(BATCH, SEQ, MIX_WIDTH), chan, pos).reshape(TOKENS, MIX_WIDTH)
        tail_args = (fa, sg, gate, xf, w_fourier[l].astype(BF16), w_gmlp[l].astype(BF16),
                     w_out[l].astype(BF16), row(post_mix_g[l]))
        i = l // 2
        if l % 2 == 0:
            x1 = _mixer_tail(*tail_args)
            xf = _dense_ffn(x1, row(pre_ffn_g[l]), ffn_w1[i].astype(BF16), ffn_w3[i].astype(BF16),
                            ffn_w2[i].astype(BF16), row(post_ffn_g[l]))
        else:
            x1, h, gw, sel, rank, counts = _mixer_tail_router(
                *tail_args, row(pre_ffn_g[l]), router_w[i].T, router_b[i].reshape(N_EXPERTS, 1),
                tri)
            xf = _moe(x1, h, gw, sel, rank, counts, moe_w1[i].astype(BF16), moe_w3[i].astype(BF16),
                      moe_w2[i].astype(BF16), row(post_ffn_g[l]))
    return xf.reshape(BATCH, SEQ, D_MODEL)
```

```python
import numpy as np
import jax
import jax.numpy as jnp
from jax import lax
from jax.experimental import pallas as pl
from jax.experimental.pallas import tpu as pltpu

F32 = jnp.float32
BF16 = jnp.bfloat16
I32 = jnp.int32

D_MODEL = 1024
BATCH = 8
SEQ = 2048
TOKENS = BATCH * SEQ
DEPTH = 2
N_GROUPS = 4
GROUP_DIM = 128
MIX_WIDTH = N_GROUPS * GROUP_DIM
CHUNK = 128
D_IN = 3 * MIX_WIDTH + 2 * D_MODEL
D_FF_DENSE = 2816
N_EXPERTS = 8
D_FF_EXPERT = 3584
RMS_EPS = 1e-6
LN_EPS = 1e-5
LANES = 128

ROW_TILE = 512
FOURIER_ROW_TILE = 512
EXPERT_BLOCK = 512
EXPERT_REGION = 1024
SORT_BLOCK = 256
TOKEN_WINDOW = 1280
COMBINE_TILE = 256
COMBINE_WINDOW = 128
BF16_ROWS = 16
EXPERT_FF_TILE = 896

SORTED_ROWS = 2 * TOKENS + N_EXPERTS * EXPERT_REGION
N_REGIONS = SORTED_ROWS // EXPERT_REGION
N_SORT_BLOCKS = SORTED_ROWS // SORT_BLOCK
N_COMBINE_TILES = TOKENS // COMBINE_TILE

VMEM_LIMIT = 56 * 1024 * 1024


def _params(*sem):
    return pltpu.CompilerParams(dimension_semantics=sem, vmem_limit_bytes=VMEM_LIMIT)


def _resident(shape):
    nd = len(shape)
    return pl.BlockSpec(shape, lambda *_: (0,) * nd, pipeline_mode=pl.Buffered(1))


def _rms(x, g):
    return x * lax.rsqrt(jnp.mean(x * x, axis=-1, keepdims=True) + RMS_EPS) * g


def _dot(a, b):
    return jnp.dot(a, b, preferred_element_type=F32)


def _dot_nt(a, b):
    return lax.dot_general(a, b, (((1,), (1,)), ((), ())), preferred_element_type=F32)


def _dot_tn(a, b):
    return lax.dot_general(a, b, (((0,), (0,)), ((), ())), preferred_element_type=F32)


def _gelu(x):
    return 0.5 * x * (1.0 + lax.erf(x * np.float32(np.sqrt(0.5))))


def _cast_once(src_ref, dst_ref):
    chunk = 128

    @pl.when(pl.program_id(0) == 0)
    def _():
        def body(i, carry):
            rows = pl.ds(pl.multiple_of(i * chunk, chunk), chunk)
            dst_ref[rows, :] = src_ref[rows, :].astype(BF16)
            return carry
        lax.fori_loop(0, src_ref.shape[0] // chunk, body, 0)


def _split_bf16(x):
    hi = x.astype(BF16)
    return hi, (x - hi.astype(F32)).astype(BF16)


def _dft_tables():
    n = np.arange(SEQ, dtype=np.int64)
    ang = 2.0 * np.pi * ((n[:, None] * n[None, :]) % SEQ).astype(np.float64) / SEQ
    scale = 1.0 / np.sqrt(SEQ)
    pos = np.concatenate([np.cos(ang) * scale, -np.sin(ang) * scale], axis=1)
    c = np.arange(GROUP_DIM, dtype=np.int64)
    angc = 2.0 * np.pi * ((c[:, None] * c[None, :]) % GROUP_DIM).astype(np.float64) / GROUP_DIM
    scalec = 1.0 / np.sqrt(GROUP_DIM)
    chan = np.concatenate([np.cos(angc) * scalec, np.sin(angc) * scalec], axis=1)
    return pos.astype(np.float32), chan.astype(np.float32)


_POS_DFT, _CHAN_DFT = _dft_tables()


def _front_kernel(x_ref, g_ref, winf_ref, ws_ref, bs_ref, lng_ref, lnb_ref,
                  za_ref, sg_ref, gate_ref, win_ref):
    _cast_once(winf_ref, win_ref)
    ws = [ws_ref[g].astype(BF16) for g in range(N_GROUPS)]
    h = _rms(x_ref[...], g_ref[...]).astype(BF16)
    za_ref[...] = _dot(h, win_ref[:, 0:MIX_WIDTH]).astype(BF16)
    u = _gelu(_dot(h, win_ref[:, MIX_WIDTH:2 * MIX_WIDTH]))
    v = _gelu(_dot(h, win_ref[:, 2 * MIX_WIDTH:3 * MIX_WIDTH]))
    mu = jnp.mean(v, axis=-1, keepdims=True)
    vc = v - mu
    var = jnp.mean(vc * vc, axis=-1, keepdims=True)
    vln = (vc * lax.rsqrt(var + LN_EPS) * lng_ref[...] + lnb_ref[...]).astype(BF16)
    for c in range(ROW_TILE // CHUNK):
        rows = slice(c * CHUNK, (c + 1) * CHUNK)
        for g in range(N_GROUPS):
            cols = slice(g * GROUP_DIM, (g + 1) * GROUP_DIM)
            sv = _dot(ws[g], vln[rows, cols]) + bs_ref[:, cols]
            sg_ref[rows, cols] = (u[rows, cols] * sv).astype(BF16)
    zg = _dot(h, win_ref[:, 3 * MIX_WIDTH:D_IN])
    gate_ref[...] = jax.nn.sigmoid(zg).astype(BF16)


def _mixer_front(x, g, w_in, w_s, b_full, ln_g, ln_b):
    row = lambda w: pl.BlockSpec((ROW_TILE, w), lambda i: (i, 0))
    return pl.pallas_call(
        _front_kernel,
        grid=(TOKENS // ROW_TILE,),
        in_specs=[row(D_MODEL), _resident((1, D_MODEL)), _resident((D_MODEL, D_IN)),
                  _resident((N_GROUPS, CHUNK, CHUNK)), _resident((CHUNK, MIX_WIDTH)),
                  _resident((1, MIX_WIDTH)), _resident((1, MIX_WIDTH))],
        out_specs=[row(MIX_WIDTH), row(MIX_WIDTH), row(2 * D_MODEL)],
        out_shape=[jax.ShapeDtypeStruct((TOKENS, MIX_WIDTH), BF16),
                   jax.ShapeDtypeStruct((TOKENS, MIX_WIDTH), BF16),
                   jax.ShapeDtypeStruct((TOKENS, 2 * D_MODEL), BF16)],
        scratch_shapes=[pltpu.VMEM((D_MODEL, D_IN), BF16)],
        compiler_params=_params("arbitrary"),
        name="mixer_front",
    )(x, g, w_in, w_s, b_full, ln_g, ln_b)


def _fourier_kernel(za_ref, chan_ref, pos_ref, out_ref, rhs_ref):
    @pl.when(pl.program_id(1) == 0)
    def _():
        for g in range(N_GROUPS):
            cols = slice(g * GROUP_DIM, (g + 1) * GROUP_DIM)
            xcs = _dot(za_ref[0, :, cols], chan_ref[...])
            rhs_ref[0:SEQ, cols] = xcs[:, 0:GROUP_DIM].astype(BF16)
            rhs_ref[SEQ:2 * SEQ, cols] = xcs[:, GROUP_DIM:2 * GROUP_DIM].astype(BF16)

    out_ref[0] = _dot(pos_ref[...], rhs_ref[...]).astype(BF16)


def _fourier(za, chan, pos):
    return pl.pallas_call(
        _fourier_kernel,
        grid=(BATCH, SEQ // FOURIER_ROW_TILE),
        in_specs=[pl.BlockSpec((1, SEQ, MIX_WIDTH), lambda b, i: (b, 0, 0)),
                  _resident((GROUP_DIM, 2 * GROUP_DIM)),
                  pl.BlockSpec((FOURIER_ROW_TILE, 2 * SEQ), lambda b, i: (i, 0))],
        out_specs=pl.BlockSpec((1, FOURIER_ROW_TILE, MIX_WIDTH), lambda b, i: (b, i, 0)),
        out_shape=jax.ShapeDtypeStruct((BATCH, SEQ, MIX_WIDTH), BF16),
        scratch_shapes=[pltpu.VMEM((2 * SEQ, MIX_WIDTH), BF16)],
        compiler_params=_params("arbitrary", "arbitrary"),
        name="fourier_mix",
    )(za, chan, pos)


def _tail_common(fa_ref, sg_ref, gate_ref, x_ref, wff_ref, wgf_ref, wof_ref, pg_ref,
                 wf_ref, wg_ref, wo_ref):
    _cast_once(wff_ref, wf_ref)
    _cast_once(wgf_ref, wg_ref)
    _cast_once(wof_ref, wo_ref)
    ya = _dot(fa_ref[...], wf_ref[...])
    yb = _dot(sg_ref[...], wg_ref[...])
    m = gate_ref[:, 0:D_MODEL].astype(F32) * ya + gate_ref[:, D_MODEL:2 * D_MODEL].astype(F32) * yb
    y = _dot(m.astype(BF16), wo_ref[...])
    return x_ref[...] + _rms(y, pg_ref[...])


def _tail_kernel(fa_ref, sg_ref, gate_ref, x_ref, wff_ref, wgf_ref, wof_ref, pg_ref, x1_ref,
                 wf_ref, wg_ref, wo_ref):
    x1_ref[...] = _tail_common(fa_ref, sg_ref, gate_ref, x_ref, wff_ref, wgf_ref, wof_ref, pg_ref,
                               wf_ref, wg_ref, wo_ref)


def _tail_router_kernel(fa_ref, sg_ref, gate_ref, x_ref, wff_ref, wgf_ref, wof_ref, pg_ref,
                        fg_ref, rwt_ref, rb_ref, tri_ref,
                        x1_ref, h_ref, gw_ref, sel_ref, rank_ref, cnt_ref,
                        wf_ref, wg_ref, wo_ref, carry_ref):
    @pl.when(pl.program_id(0) == 0)
    def _():
        carry_ref[...] = jnp.zeros_like(carry_ref)

    x1 = _tail_common(fa_ref, sg_ref, gate_ref, x_ref, wff_ref, wgf_ref, wof_ref, pg_ref,
                      wf_ref, wg_ref, wo_ref)
    x1_ref[...] = x1
    h = _rms(x1, fg_ref[...])
    h_hi, h_lo = _split_bf16(h)
    h_ref[...] = h_hi

    w_hi, w_lo = _split_bf16(rwt_ref[...])
    logits = _dot_nt(w_hi, h_hi) + _dot_nt(w_hi, h_lo) + _dot_nt(w_lo, h_hi) + rb_ref[...]
    row = lax.broadcasted_iota(I32, logits.shape, 0)
    m1 = jnp.max(logits, axis=0, keepdims=True)
    i1 = jnp.min(jnp.where(logits == m1, row, N_EXPERTS), axis=0, keepdims=True)
    rest = jnp.where(row == i1, -jnp.inf, logits)
    m2 = jnp.max(rest, axis=0, keepdims=True)
    i2 = jnp.min(jnp.where(rest == m2, row, N_EXPERTS), axis=0, keepdims=True)
    e2 = jnp.exp(m2 - m1)
    den = 1.0 + e2
    gw_ref[...] = jnp.where(row == i1, 1.0 / den, 0.0) + jnp.where(row == i2, e2 / den, 0.0)
    sel = jnp.where((row == i1) | (row == i2), 1.0, 0.0)
    sel_ref[...] = sel.astype(I32)
    before = _dot(sel.astype(BF16), tri_ref[...]) + carry_ref[...]
    rank_ref[...] = before.astype(I32)
    carry_ref[...] = carry_ref[...] + jnp.sum(sel, axis=1, keepdims=True)
    cnt_ref[...] = carry_ref[...].astype(I32)


def _tail_specs():
    row = lambda w: pl.BlockSpec((ROW_TILE, w), lambda i: (i, 0))
    in_specs = [row(MIX_WIDTH), row(MIX_WIDTH), row(2 * D_MODEL), row(D_MODEL),
                _resident((MIX_WIDTH, D_MODEL)), _resident((MIX_WIDTH, D_MODEL)),
                _resident((D_MODEL, D_MODEL)), _resident((1, D_MODEL))]
    weight_scratch = [pltpu.VMEM((MIX_WIDTH, D_MODEL), BF16), pltpu.VMEM((MIX_WIDTH, D_MODEL), BF16),
                      pltpu.VMEM((D_MODEL, D_MODEL), BF16)]
    return row, in_specs, weight_scratch


def _mixer_tail(fa, sg, gate, x, w_f, w_g, w_o, post_g):
    row, in_specs, weight_scratch = _tail_specs()
    return pl.pallas_call(
        _tail_kernel,
        grid=(TOKENS // ROW_TILE,),
        in_specs=in_specs,
        out_specs=row(D_MODEL),
        out_shape=jax.ShapeDtypeStruct((TOKENS, D_MODEL), F32),
        scratch_shapes=weight_scratch,
        compiler_params=_params("arbitrary"),
        name="mixer_tail",
    )(fa, sg, gate, x, w_f, w_g, w_o, post_g)


def _mixer_tail_router(fa, sg, gate, x, w_f, w_g, w_o, post_g, ffn_g, router_wt, router_b, tri):
    row, in_specs, weight_scratch = _tail_specs()
    in_specs += [_resident((1, D_MODEL)), _resident((N_EXPERTS, D_MODEL)),
                 _resident((N_EXPERTS, 1)), _resident((ROW_TILE, ROW_TILE))]
    col = pl.BlockSpec((N_EXPERTS, ROW_TILE), lambda i: (0, i))
    per_token = lambda dt: jax.ShapeDtypeStruct((N_EXPERTS, TOKENS), dt)
    return pl.pallas_call(
        _tail_router_kernel,
        grid=(TOKENS // ROW_TILE,),
        in_specs=in_specs,
        out_specs=[row(D_MODEL), row(D_MODEL), col, col, col,
                   pl.BlockSpec((N_EXPERTS, 1), lambda i: (0, 0))],
        out_shape=[jax.ShapeDtypeStruct((TOKENS, D_MODEL), F32),
                   jax.ShapeDtypeStruct((TOKENS, D_MODEL), BF16),
                   per_token(F32), per_token(I32), per_token(I32),
                   jax.ShapeDtypeStruct((N_EXPERTS, 1), I32)],
        scratch_shapes=weight_scratch + [pltpu.VMEM((N_EXPERTS, 1), F32)],
        compiler_params=_params("arbitrary"),
        name="mixer_tail_router",
    )(fa, sg, gate, x, w_f, w_g, w_o, post_g, ffn_g, router_wt, router_b, tri)


def _dense_ffn_kernel(x_ref, fg_ref, w1_ref, w3_ref, w2_ref, pg_ref, out_ref):
    x = x_ref[...]
    h = _rms(x, fg_ref[...]).astype(BF16)
    a = _dot(h, w1_ref[...])
    b = _dot(h, w3_ref[...])
    y = _dot((jax.nn.silu(a) * b).astype(BF16), w2_ref[...])
    out_ref[...] = x + _rms(y, pg_ref[...])


def _dense_ffn(x, ffn_g, w1, w3, w2, post_g):
    row = pl.BlockSpec((ROW_TILE, D_MODEL), lambda i: (i, 0))
    return pl.pallas_call(
        _dense_ffn_kernel,
        grid=(TOKENS // ROW_TILE,),
        in_specs=[row, _resident((1, D_MODEL)), _resident((D_MODEL, D_FF_DENSE)),
                  _resident((D_MODEL, D_FF_DENSE)), _resident((D_FF_DENSE, D_MODEL)),
                  _resident((1, D_MODEL))],
        out_specs=row,
        out_shape=jax.ShapeDtypeStruct((TOKENS, D_MODEL), F32),
        compiler_params=_params("arbitrary"),
        name="dense_ffn",
    )(x, ffn_g, w1, w3, w2, post_g)


def _window_hits(k, block, start_ref, dest_ref):
    wanted = start_ref[block] + k * TOKEN_WINDOW
    begin = pl.multiple_of(jnp.minimum(wanted, TOKENS - TOKEN_WINDOW), LANES)
    tok = lax.broadcasted_iota(I32, (1, TOKEN_WINDOW), 1) + begin
    dest = jnp.where(tok >= wanted, dest_ref[:, pl.ds(begin, TOKEN_WINDOW)], -1)
    rows = lax.broadcasted_iota(I32, (SORT_BLOCK, TOKEN_WINDOW), 0) + block * SORT_BLOCK
    return begin, rows == dest


def _dispatch_kernel(exp_ref, start_ref, nwin_ref, dest_ref, gw_ref, h_ref, xs_ref, gs_ref,
                     acc_ref):
    j = pl.program_id(0)
    acc_ref[...] = jnp.zeros_like(acc_ref)

    def window(k, gsum):
        begin, hit = _window_hits(k, j, start_ref, dest_ref)
        onehot = jnp.where(hit, 1.0, 0.0).astype(BF16)
        acc_ref[...] += _dot(onehot, h_ref[pl.ds(begin, TOKEN_WINDOW), :])
        gw = gw_ref[:, pl.ds(begin, TOKEN_WINDOW)]
        return gsum + jnp.sum(jnp.where(hit, gw, 0.0), axis=-1, keepdims=True)

    gs_ref[...] = lax.fori_loop(0, nwin_ref[j], window, jnp.zeros((SORT_BLOCK, 1), F32))
    xs_ref[...] = acc_ref[...].astype(BF16)


def _per_expert_row():
    return pl.BlockSpec((None, 1, TOKENS),
                        lambda j, e, s, n: (e[jnp.minimum(j, N_SORT_BLOCKS - 1)], 0, 0))


def _dispatch(plan, dest3, gw3, h):
    exp, start, nwin = plan
    grid_spec = pltpu.PrefetchScalarGridSpec(
        num_scalar_prefetch=3,
        grid=(N_SORT_BLOCKS,),
        in_specs=[_per_expert_row(), _per_expert_row(),
                  pl.BlockSpec((TOKENS, D_MODEL), lambda j, e, s, n: (0, 0),
                               pipeline_mode=pl.Buffered(1))],
        out_specs=[pl.BlockSpec((SORT_BLOCK, D_MODEL), lambda j, e, s, n: (j, 0)),
                   pl.BlockSpec((SORT_BLOCK, 1), lambda j, e, s, n: (j, 0))],
        scratch_shapes=[pltpu.VMEM((SORT_BLOCK, D_MODEL), F32)],
    )
    return pl.pallas_call(
        _dispatch_kernel,
        grid_spec=grid_spec,
        out_shape=[jax.ShapeDtypeStruct((SORTED_ROWS, D_MODEL), BF16),
                   jax.ShapeDtypeStruct((SORTED_ROWS, 1), F32)],
        compiler_params=_params("arbitrary"),
        name="moe_dispatch",
    )(exp, start, nwin, dest3, gw3, h)


def _expert_kernel(rexp_ref, nhalf_ref, xs_ref, gs_ref, w1_ref, w3_ref, w2_ref, ys_ref, acc_ref):
    q = pl.program_id(0)
    f = pl.program_id(1)
    last = pl.num_programs(1) - 1
    nhalf = nhalf_ref[q]

    def swiglu_part(rows):
        x = xs_ref[0:rows, :]
        a = _dot(x, w1_ref[...].astype(BF16))
        b = _dot(x, w3_ref[...].astype(BF16))
        part = _dot((jax.nn.silu(a) * b).astype(BF16), w2_ref[...].astype(BF16))

        @pl.when(f == 0)
        def _():
            acc_ref[0:rows, :] = part

        @pl.when(f > 0)
        def _():
            acc_ref[0:rows, :] = acc_ref[0:rows, :] + part

        @pl.when(f == last)
        def _():
            ys_ref[0:rows, :] = (acc_ref[0:rows, :] * gs_ref[0:rows, :]).astype(BF16)

    @pl.when(nhalf == 2)
    def _():
        swiglu_part(EXPERT_REGION)

    @pl.when(nhalf == 1)
    def _():
        swiglu_part(EXPERT_BLOCK)

    @pl.when((nhalf < 2) & (f == last))
    def _():
        ys_ref[EXPERT_BLOCK:EXPERT_REGION, :] = jnp.zeros((EXPERT_REGION - EXPERT_BLOCK, D_MODEL), BF16)

    @pl.when((nhalf == 0) & (f == last))
    def _():
        ys_ref[0:EXPERT_BLOCK, :] = jnp.zeros((EXPERT_BLOCK, D_MODEL), BF16)


def _experts(rexp, nhalf, xs, gs, w1, w3, w2):
    grid_spec = pltpu.PrefetchScalarGridSpec(
        num_scalar_prefetch=2,
        grid=(N_REGIONS, D_FF_EXPERT // EXPERT_FF_TILE),
        in_specs=[
            pl.BlockSpec((EXPERT_REGION, D_MODEL), lambda q, f, re, nh: (q, 0)),
            pl.BlockSpec((EXPERT_REGION, 1), lambda q, f, re, nh: (q, 0)),
            pl.BlockSpec((None, D_MODEL, EXPERT_FF_TILE), lambda q, f, re, nh: (re[q], 0, f)),
            pl.BlockSpec((None, D_MODEL, EXPERT_FF_TILE), lambda q, f, re, nh: (re[q], 0, f)),
            pl.BlockSpec((None, EXPERT_FF_TILE, D_MODEL), lambda q, f, re, nh: (re[q], f, 0)),
        ],
        out_specs=pl.BlockSpec((EXPERT_REGION, D_MODEL), lambda q, f, re, nh: (q, 0)),
        scratch_shapes=[pltpu.VMEM((EXPERT_REGION, D_MODEL), F32)],
    )
    return pl.pallas_call(
        _expert_kernel,
        grid_spec=grid_spec,
        out_shape=jax.ShapeDtypeStruct((SORTED_ROWS, D_MODEL), BF16),
        compiler_params=_params("arbitrary", "arbitrary"),
        name="moe_experts",
    )(rexp, nhalf, xs, gs, w1, w3, w2)


def _combine_copies(wstart_ref, ys_hbm, buf_ref, sem_ref, tile, rnd, slot):
    copies = []
    for e in range(N_EXPERTS):
        wanted = wstart_ref[tile * N_EXPERTS + e] + rnd * COMBINE_WINDOW
        begin = pl.multiple_of(jnp.minimum(wanted, SORTED_ROWS - COMBINE_WINDOW), BF16_ROWS)
        copies.append(pltpu.make_async_copy(
            ys_hbm.at[pl.ds(begin, COMBINE_WINDOW), :],
            buf_ref.at[slot, pl.ds(e * COMBINE_WINDOW, COMBINE_WINDOW), :],
            sem_ref.at[slot, e]))
    return copies


def _combine_kernel(wstart_ref, nround_ref, dest_ref, ys_hbm, x_ref, pg_ref, out_ref,
                    buf_ref, sem_ref):
    i = pl.program_id(0)
    n_tiles = pl.num_programs(0)
    slot = lax.rem(i, 2)

    @pl.when(i == 0)
    def _():
        for c in _combine_copies(wstart_ref, ys_hbm, buf_ref, sem_ref, 0, 0, 0):
            c.start()

    @pl.when(i + 1 < n_tiles)
    def _():
        for c in _combine_copies(wstart_ref, ys_hbm, buf_ref, sem_ref, i + 1, 0, 1 - slot):
            c.start()

    def gather(rnd, fetch_slot):
        sub = lax.broadcasted_iota(I32, (COMBINE_WINDOW, 1), 0)
        pieces = []
        for e in range(N_EXPERTS):
            wanted = wstart_ref[i * N_EXPERTS + e] + rnd * COMBINE_WINDOW
            begin = jnp.minimum(wanted, SORTED_ROWS - COMBINE_WINDOW)
            rowid = sub + begin
            rowid = jnp.where(rowid >= wanted, rowid, -2)
            pieces.append(jnp.where(rowid == dest_ref[e:e + 1, :], 1.0, 0.0).astype(BF16))
        onehot = jnp.concatenate(pieces, axis=0)
        return _dot_tn(onehot, buf_ref[fetch_slot])

    for c in _combine_copies(wstart_ref, ys_hbm, buf_ref, sem_ref, i, 0, slot):
        c.wait()
    y = gather(0, slot)

    def extra(rnd, y):
        copies = _combine_copies(wstart_ref, ys_hbm, buf_ref, sem_ref, i, rnd, 2)
        for c in copies:
            c.start()
        for c in copies:
            c.wait()
        return y + gather(rnd, 2)

    y = lax.fori_loop(1, nround_ref[i], extra, y)
    out_ref[...] = x_ref[...] + _rms(y, pg_ref[...])


def _combine(wstart, nround, dest, ys, x, post_g):
    tile = lambda i, ws, nr: (i, 0)
    grid_spec = pltpu.PrefetchScalarGridSpec(
        num_scalar_prefetch=2,
        grid=(TOKENS // COMBINE_TILE,),
        in_specs=[
            pl.BlockSpec((N_EXPERTS, COMBINE_TILE), lambda i, ws, nr: (0, i)),
            pl.BlockSpec(memory_space=pl.ANY),
            pl.BlockSpec((COMBINE_TILE, D_MODEL), tile),
            pl.BlockSpec((1, D_MODEL), lambda i, ws, nr: (0, 0)),
        ],
        out_specs=pl.BlockSpec((COMBINE_TILE, D_MODEL), tile),
        scratch_shapes=[pltpu.VMEM((3, N_EXPERTS * COMBINE_WINDOW, D_MODEL), BF16),
                        pltpu.SemaphoreType.DMA((3, N_EXPERTS))],
    )
    return pl.pallas_call(
        _combine_kernel,
        grid_spec=grid_spec,
        out_shape=jax.ShapeDtypeStruct((TOKENS, D_MODEL), F32),
        compiler_params=_params("arbitrary"),
        name="moe_combine",
    )(wstart, nround, dest, ys, x, post_g)


def _routing_plan(sel, rank, counts):
    counts = counts.reshape(N_EXPERTS)
    padded = ((counts + EXPERT_BLOCK - 1) // EXPERT_BLOCK) * EXPERT_BLOCK
    region = ((counts + EXPERT_REGION - 1) // EXPERT_REGION) * EXPERT_REGION
    offs = jnp.cumsum(region) - region
    dest = jnp.where(sel == 1, rank + offs[:, None], -1)

    cum_reg = jnp.cumsum(region // EXPERT_REGION)
    qs = jnp.arange(N_REGIONS, dtype=I32)
    rexp = jnp.sum((cum_reg[None, :] <= jnp.minimum(qs, cum_reg[-1] - 1)[:, None]).astype(I32),
                   axis=1)
    rexp = jnp.minimum(rexp, N_EXPERTS - 1).astype(I32)
    rows_left = offs[rexp] + padded[rexp] - qs * EXPERT_REGION
    n_half = jnp.clip(rows_left // EXPERT_BLOCK, 0, EXPERT_REGION // EXPERT_BLOCK).astype(I32)

    group_end = (rank + sel)[:, LANES - 1::LANES]
    js = jnp.arange(N_SORT_BLOCKS, dtype=I32)
    e_s = rexp[js // (EXPERT_REGION // SORT_BLOCK)]
    lo_row = js * SORT_BLOCK - offs[e_s]
    has_rows = lo_row < counts[e_s]
    hi_row = jnp.minimum(lo_row + SORT_BLOCK, counts[e_s])
    ends = group_end[e_s]
    g_lo = jnp.sum((ends <= lo_row[:, None]).astype(I32), axis=1)
    g_hi = jnp.sum((ends < hi_row[:, None]).astype(I32), axis=1)
    span = (g_hi - g_lo + 1) * LANES
    n_win = jnp.where(has_rows, (span + TOKEN_WINDOW - 1) // TOKEN_WINDOW, 0).astype(I32)
    start = jnp.where(has_rows, g_lo * LANES, 0).astype(I32)
    d_plan = (e_s.astype(I32), start, n_win)

    tile_lo = rank[:, ::COMBINE_TILE]
    tile_hi = jnp.concatenate([tile_lo[:, 1:], counts[:, None]], axis=1)
    lo = tile_lo + offs[:, None]
    hi = tile_hi + offs[:, None]
    wstart = jnp.minimum((lo // BF16_ROWS) * BF16_ROWS, SORTED_ROWS - COMBINE_WINDOW)
    rounds = jnp.where(hi > lo, (hi - wstart + COMBINE_WINDOW - 1) // COMBINE_WINDOW, 0)
    n_round = jnp.maximum(jnp.max(rounds, axis=0), 1).astype(I32)
    c_plan = (wstart.T.reshape(-1).astype(I32), n_round)
    return dest, (rexp, n_half), d_plan, c_plan


def _moe(x1, h, gw, sel, rank, counts, w1, w3, w2, post_g):
    dest, e_plan, d_plan, c_plan = _routing_plan(sel, rank, counts)
    dest3 = dest.reshape(N_EXPERTS, 1, TOKENS)
    gw3 = gw.reshape(N_EXPERTS, 1, TOKENS)
    xs, gs = _dispatch(d_plan, dest3, gw3, h)
    ys = _experts(*e_plan, xs, gs, w1, w3, w2)
    return _combine(*c_plan, dest, ys, x1, post_g)


def kernel(x, pre_mix_g, post_mix_g, w_in, w_fourier, w_gmlp, w_out, w_spatial, b_spatial,
           gmlp_ln_g, gmlp_ln_b, pre_ffn_g, post_ffn_g, ffn_w1, ffn_w3, ffn_w2,
           router_w, router_b, moe_w1, moe_w3, moe_w2):
    pos = jnp.asarray(_POS_DFT).astype(BF16)
    chan = jnp.asarray(_CHAN_DFT).astype(BF16)
    tri = jnp.asarray(np.triu(np.ones((ROW_TILE, ROW_TILE), np.float32), 1)).astype(BF16)
    row = lambda v: v.reshape(1, -1)

    xf = x.reshape(TOKENS, D_MODEL)
    for l in range(DEPTH):
        b_full = jnp.repeat(b_spatial[l].T, GROUP_DIM, axis=1)
        za, sg, gate = _mixer_front(xf, row(pre_mix_g[l]), w_in[l], w_spatial[l], b_full,
                                    row(gmlp_ln_g[l]), row(gmlp_ln_b[l]))
        fa = _fourier(za.reshape(BATCH, SEQ, MIX_WIDTH), chan, pos).reshape(TOKENS, MIX_WIDTH)
        tail_args = (fa, sg, gate, xf, w_fourier[l], w_gmlp[l], w_out[l], row(post_mix_g[l]))
        i = l // 2
        if l % 2 == 0:
            x1 = _mixer_tail(*tail_args)
            xf = _dense_ffn(x1, row(pre_ffn_g[l]), ffn_w1[i].astype(BF16), ffn_w3[i].astype(BF16),
                            ffn_w2[i].astype(BF16), row(post_ffn_g[l]))
        else:
            x1, h, gw, sel, rank, counts = _mixer_tail_router(
                *tail_args, row(pre_ffn_g[l]), router_w[i].T, router_b[i].reshape(N_EXPERTS, 1),
                tri)
            xf = _moe(x1, h, gw, sel, rank, counts, moe_w1[i], moe_w3[i], moe_w2[i],
                      row(post_ffn_g[l]))
    return xf.reshape(BATCH, SEQ, D_MODEL)
```

```python
import numpy as np
import jax
import jax.numpy as jnp
from jax import lax
from jax.experimental import pallas as pl
from jax.experimental.pallas import tpu as pltpu

F32 = jnp.float32
BF16 = jnp.bfloat16
I32 = jnp.int32

D_MODEL = 1024
BATCH = 8
SEQ = 2048
TOKENS = BATCH * SEQ
DEPTH = 2
N_GROUPS = 4
GROUP_DIM = 128
MIX_WIDTH = N_GROUPS * GROUP_DIM
CHUNK = 128
D_IN = 3 * MIX_WIDTH + 2 * D_MODEL
D_FF_DENSE = 2816
N_EXPERTS = 8
D_FF_EXPERT = 3584
RMS_EPS = 1e-6
LN_EPS = 1e-5
LANES = 128

ROW_TILE = 512
FOURIER_ROW_TILE = 512
EXPERT_BLOCK = 512
EXPERT_REGION = 1024
SORT_BLOCK = 256
TOKEN_WINDOW = 1280
COMBINE_TILE = 256
COMBINE_WINDOW = 128
BF16_ROWS = 16
EXPERT_FF_TILE = 1792

SORTED_ROWS = 2 * TOKENS + N_EXPERTS * EXPERT_REGION
N_REGIONS = SORTED_ROWS // EXPERT_REGION
N_SORT_BLOCKS = SORTED_ROWS // SORT_BLOCK
N_COMBINE_TILES = TOKENS // COMBINE_TILE

VMEM_LIMIT = 56 * 1024 * 1024


def _params(*sem):
    return pltpu.CompilerParams(dimension_semantics=sem, vmem_limit_bytes=VMEM_LIMIT)


def _resident(shape):
    nd = len(shape)
    return pl.BlockSpec(shape, lambda *_: (0,) * nd, pipeline_mode=pl.Buffered(1))


def _rms(x, g):
    return x * lax.rsqrt(jnp.mean(x * x, axis=-1, keepdims=True) + RMS_EPS) * g


def _dot(a, b):
    return jnp.dot(a, b, preferred_element_type=F32)


def _dot_nt(a, b):
    return lax.dot_general(a, b, (((1,), (1,)), ((), ())), preferred_element_type=F32)


def _dot_tn(a, b):
    return lax.dot_general(a, b, (((0,), (0,)), ((), ())), preferred_element_type=F32)


def _gelu(x):
    return 0.5 * x * (1.0 + lax.erf(x * np.float32(np.sqrt(0.5))))


def _cast_once(src_ref, dst_ref):
    chunk = 128

    @pl.when(pl.program_id(0) == 0)
    def _():
        def body(i, carry):
            rows = pl.ds(pl.multiple_of(i * chunk, chunk), chunk)
            dst_ref[rows, :] = src_ref[rows, :].astype(BF16)
            return carry
        lax.fori_loop(0, src_ref.shape[0] // chunk, body, 0)


def _split_bf16(x):
    hi = x.astype(BF16)
    return hi, (x - hi.astype(F32)).astype(BF16)


def _dft_tables():
    n = np.arange(SEQ, dtype=np.int64)
    ang = 2.0 * np.pi * ((n[:, None] * n[None, :]) % SEQ).astype(np.float64) / SEQ
    scale = 1.0 / np.sqrt(SEQ)
    pos = np.concatenate([np.cos(ang) * scale, -np.sin(ang) * scale], axis=1)
    c = np.arange(GROUP_DIM, dtype=np.int64)
    angc = 2.0 * np.pi * ((c[:, None] * c[None, :]) % GROUP_DIM).astype(np.float64) / GROUP_DIM
    scalec = 1.0 / np.sqrt(GROUP_DIM)
    chan = np.concatenate([np.cos(angc) * scalec, np.sin(angc) * scalec], axis=1)
    return pos.astype(np.float32), chan.astype(np.float32)


_POS_DFT, _CHAN_DFT = _dft_tables()


def _front_kernel(x_ref, g_ref, winf_ref, ws_ref, bs_ref, lng_ref, lnb_ref,
                  za_ref, sg_ref, gate_ref, win_ref):
    _cast_once(winf_ref, win_ref)
    ws = [ws_ref[g].astype(BF16) for g in range(N_GROUPS)]
    h = _rms(x_ref[...], g_ref[...]).astype(BF16)
    za_ref[...] = _dot(h, win_ref[:, 0:MIX_WIDTH]).astype(BF16)
    u = _gelu(_dot(h, win_ref[:, MIX_WIDTH:2 * MIX_WIDTH]))
    v = _gelu(_dot(h, win_ref[:, 2 * MIX_WIDTH:3 * MIX_WIDTH]))
    mu = jnp.mean(v, axis=-1, keepdims=True)
    vc = v - mu
    var = jnp.mean(vc * vc, axis=-1, keepdims=True)
    vln = (vc * lax.rsqrt(var + LN_EPS) * lng_ref[...] + lnb_ref[...]).astype(BF16)
    for c in range(ROW_TILE // CHUNK):
        rows = slice(c * CHUNK, (c + 1) * CHUNK)
        for g in range(N_GROUPS):
            cols = slice(g * GROUP_DIM, (g + 1) * GROUP_DIM)
            sv = _dot(ws[g], vln[rows, cols]) + bs_ref[:, cols]
            sg_ref[rows, cols] = (u[rows, cols] * sv).astype(BF16)
    zg = _dot(h, win_ref[:, 3 * MIX_WIDTH:D_IN])
    gate_ref[...] = jax.nn.sigmoid(zg).astype(BF16)


def _mixer_front(x, g, w_in, w_s, b_full, ln_g, ln_b):
    row = lambda w: pl.BlockSpec((ROW_TILE, w), lambda i: (i, 0))
    return pl.pallas_call(
        _front_kernel,
        grid=(TOKENS // ROW_TILE,),
        in_specs=[row(D_MODEL), _resident((1, D_MODEL)), _resident((D_MODEL, D_IN)),
                  _resident((N_GROUPS, CHUNK, CHUNK)), _resident((CHUNK, MIX_WIDTH)),
                  _resident((1, MIX_WIDTH)), _resident((1, MIX_WIDTH))],
        out_specs=[row(MIX_WIDTH), row(MIX_WIDTH), row(2 * D_MODEL)],
        out_shape=[jax.ShapeDtypeStruct((TOKENS, MIX_WIDTH), BF16),
                   jax.ShapeDtypeStruct((TOKENS, MIX_WIDTH), BF16),
                   jax.ShapeDtypeStruct((TOKENS, 2 * D_MODEL), BF16)],
        scratch_shapes=[pltpu.VMEM((D_MODEL, D_IN), BF16)],
        compiler_params=_params("arbitrary"),
        name="mixer_front",
    )(x, g, w_in, w_s, b_full, ln_g, ln_b)


def _fourier_kernel(za_ref, chan_ref, pos_ref, out_ref, rhs_ref):
    @pl.when(pl.program_id(1) == 0)
    def _():
        for g in range(N_GROUPS):
            cols = slice(g * GROUP_DIM, (g + 1) * GROUP_DIM)
            xcs = _dot(za_ref[0, :, cols], chan_ref[...])
            rhs_ref[0:SEQ, cols] = xcs[:, 0:GROUP_DIM].astype(BF16)
            rhs_ref[SEQ:2 * SEQ, cols] = xcs[:, GROUP_DIM:2 * GROUP_DIM].astype(BF16)

    out_ref[0] = _dot(pos_ref[...], rhs_ref[...]).astype(BF16)


def _fourier(za, chan, pos):
    return pl.pallas_call(
        _fourier_kernel,
        grid=(BATCH, SEQ // FOURIER_ROW_TILE),
        in_specs=[pl.BlockSpec((1, SEQ, MIX_WIDTH), lambda b, i: (b, 0, 0)),
                  _resident((GROUP_DIM, 2 * GROUP_DIM)),
                  pl.BlockSpec((FOURIER_ROW_TILE, 2 * SEQ), lambda b, i: (i, 0))],
        out_specs=pl.BlockSpec((1, FOURIER_ROW_TILE, MIX_WIDTH), lambda b, i: (b, i, 0)),
        out_shape=jax.ShapeDtypeStruct((BATCH, SEQ, MIX_WIDTH), BF16),
        scratch_shapes=[pltpu.VMEM((2 * SEQ, MIX_WIDTH), BF16)],
        compiler_params=_params("arbitrary", "arbitrary"),
        name="fourier_mix",
    )(za, chan, pos)


def _tail_common(fa_ref, sg_ref, gate_ref, x_ref, wff_ref, wgf_ref, wof_ref, pg_ref,
                 wf_ref, wg_ref, wo_ref):
    _cast_once(wff_ref, wf_ref)
    _cast_once(wgf_ref, wg_ref)
    _cast_once(wof_ref, wo_ref)
    ya = _dot(fa_ref[...], wf_ref[...])
    yb = _dot(sg_ref[...], wg_ref[...])
    m = gate_ref[:, 0:D_MODEL].astype(F32) * ya + gate_ref[:, D_MODEL:2 * D_MODEL].astype(F32) * yb
    y = _dot(m.astype(BF16), wo_ref[...])
    return x_ref[...] + _rms(y, pg_ref[...])


def _tail_kernel(fa_ref, sg_ref, gate_ref, x_ref, wff_ref, wgf_ref, wof_ref, pg_ref, x1_ref,
                 wf_ref, wg_ref, wo_ref):
    x1_ref[...] = _tail_common(fa_ref, sg_ref, gate_ref, x_ref, wff_ref, wgf_ref, wof_ref, pg_ref,
                               wf_ref, wg_ref, wo_ref)


def _tail_router_kernel(fa_ref, sg_ref, gate_ref, x_ref, wff_ref, wgf_ref, wof_ref, pg_ref,
                        fg_ref, rwt_ref, rb_ref, tri_ref,
                        x1_ref, h_ref, gw_ref, sel_ref, rank_ref, cnt_ref,
                        wf_ref, wg_ref, wo_ref, carry_ref):
    @pl.when(pl.program_id(0) == 0)
    def _():
        carry_ref[...] = jnp.zeros_like(carry_ref)

    x1 = _tail_common(fa_ref, sg_ref, gate_ref, x_ref, wff_ref, wgf_ref, wof_ref, pg_ref,
                      wf_ref, wg_ref, wo_ref)
    x1_ref[...] = x1
    h = _rms(x1, fg_ref[...])
    h_hi, h_lo = _split_bf16(h)
    h_ref[...] = h_hi

    w_hi, w_lo = _split_bf16(rwt_ref[...])
    logits = _dot_nt(w_hi, h_hi) + _dot_nt(w_hi, h_lo) + _dot_nt(w_lo, h_hi) + rb_ref[...]
    row = lax.broadcasted_iota(I32, logits.shape, 0)
    m1 = jnp.max(logits, axis=0, keepdims=True)
    i1 = jnp.min(jnp.where(logits == m1, row, N_EXPERTS), axis=0, keepdims=True)
    rest = jnp.where(row == i1, -jnp.inf, logits)
    m2 = jnp.max(rest, axis=0, keepdims=True)
    i2 = jnp.min(jnp.where(rest == m2, row, N_EXPERTS), axis=0, keepdims=True)
    e2 = jnp.exp(m2 - m1)
    den = 1.0 + e2
    gw_ref[...] = jnp.where(row == i1, 1.0 / den, 0.0) + jnp.where(row == i2, e2 / den, 0.0)
    sel = jnp.where((row == i1) | (row == i2), 1.0, 0.0)
    sel_ref[...] = sel.astype(I32)
    before = _dot(sel.astype(BF16), tri_ref[...]) + carry_ref[...]
    rank_ref[...] = before.astype(I32)
    carry_ref[...] = carry_ref[...] + jnp.sum(sel, axis=1, keepdims=True)
    cnt_ref[...] = carry_ref[...].astype(I32)


def _tail_specs():
    row = lambda w: pl.BlockSpec((ROW_TILE, w), lambda i: (i, 0))
    in_specs = [row(MIX_WIDTH), row(MIX_WIDTH), row(2 * D_MODEL), row(D_MODEL),
                _resident((MIX_WIDTH, D_MODEL)), _resident((MIX_WIDTH, D_MODEL)),
                _resident((D_MODEL, D_MODEL)), _resident((1, D_MODEL))]
    weight_scratch = [pltpu.VMEM((MIX_WIDTH, D_MODEL), BF16), pltpu.VMEM((MIX_WIDTH, D_MODEL), BF16),
                      pltpu.VMEM((D_MODEL, D_MODEL), BF16)]
    return row, in_specs, weight_scratch


def _mixer_tail(fa, sg, gate, x, w_f, w_g, w_o, post_g):
    row, in_specs, weight_scratch = _tail_specs()
    return pl.pallas_call(
        _tail_kernel,
        grid=(TOKENS // ROW_TILE,),
        in_specs=in_specs,
        out_specs=row(D_MODEL),
        out_shape=jax.ShapeDtypeStruct((TOKENS, D_MODEL), F32),
        scratch_shapes=weight_scratch,
        compiler_params=_params("arbitrary"),
        name="mixer_tail",
    )(fa, sg, gate, x, w_f, w_g, w_o, post_g)


def _mixer_tail_router(fa, sg, gate, x, w_f, w_g, w_o, post_g, ffn_g, router_wt, router_b, tri):
    row, in_specs, weight_scratch = _tail_specs()
    in_specs += [_resident((1, D_MODEL)), _resident((N_EXPERTS, D_MODEL)),
                 _resident((N_EXPERTS, 1)), _resident((ROW_TILE, ROW_TILE))]
    col = pl.BlockSpec((N_EXPERTS, ROW_TILE), lambda i: (0, i))
    per_token = lambda dt: jax.ShapeDtypeStruct((N_EXPERTS, TOKENS), dt)
    return pl.pallas_call(
        _tail_router_kernel,
        grid=(TOKENS // ROW_TILE,),
        in_specs=in_specs,
        out_specs=[row(D_MODEL), row(D_MODEL), col, col, col,
                   pl.BlockSpec((N_EXPERTS, 1), lambda i: (0, 0))],
        out_shape=[jax.ShapeDtypeStruct((TOKENS, D_MODEL), F32),
                   jax.ShapeDtypeStruct((TOKENS, D_MODEL), BF16),
                   per_token(F32), per_token(I32), per_token(I32),
                   jax.ShapeDtypeStruct((N_EXPERTS, 1), I32)],
        scratch_shapes=weight_scratch + [pltpu.VMEM((N_EXPERTS, 1), F32)],
        compiler_params=_params("arbitrary"),
        name="mixer_tail_router",
    )(fa, sg, gate, x, w_f, w_g, w_o, post_g, ffn_g, router_wt, router_b, tri)


def _dense_ffn_kernel(x_ref, fg_ref, w1_ref, w3_ref, w2_ref, pg_ref, out_ref):
    x = x_ref[...]
    h = _rms(x, fg_ref[...]).astype(BF16)
    a = _dot(h, w1_ref[...])
    b = _dot(h, w3_ref[...])
    y = _dot((jax.nn.silu(a) * b).astype(BF16), w2_ref[...])
    out_ref[...] = x + _rms(y, pg_ref[...])


def _dense_ffn(x, ffn_g, w1, w3, w2, post_g):
    row = pl.BlockSpec((ROW_TILE, D_MODEL), lambda i: (i, 0))
    return pl.pallas_call(
        _dense_ffn_kernel,
        grid=(TOKENS // ROW_TILE,),
        in_specs=[row, _resident((1, D_MODEL)), _resident((D_MODEL, D_FF_DENSE)),
                  _resident((D_MODEL, D_FF_DENSE)), _resident((D_FF_DENSE, D_MODEL)),
                  _resident((1, D_MODEL))],
        out_specs=row,
        out_shape=jax.ShapeDtypeStruct((TOKENS, D_MODEL), F32),
        compiler_params=_params("arbitrary"),
        name="dense_ffn",
    )(x, ffn_g, w1, w3, w2, post_g)


def _window_hits(k, block, start_ref, dest_ref):
    wanted = start_ref[block] + k * TOKEN_WINDOW
    begin = pl.multiple_of(jnp.minimum(wanted, TOKENS - TOKEN_WINDOW), LANES)
    tok = lax.broadcasted_iota(I32, (1, TOKEN_WINDOW), 1) + begin
    dest = jnp.where(tok >= wanted, dest_ref[:, pl.ds(begin, TOKEN_WINDOW)], -1)
    rows = lax.broadcasted_iota(I32, (SORT_BLOCK, TOKEN_WINDOW), 0) + block * SORT_BLOCK
    return begin, rows == dest


def _dispatch_kernel(exp_ref, start_ref, nwin_ref, dest_ref, gw_ref, h_ref, xs_ref, gs_ref,
                     acc_ref):
    j = pl.program_id(0)
    acc_ref[...] = jnp.zeros_like(acc_ref)

    def window(k, gsum):
        begin, hit = _window_hits(k, j, start_ref, dest_ref)
        onehot = jnp.where(hit, 1.0, 0.0).astype(BF16)
        acc_ref[...] += _dot(onehot, h_ref[pl.ds(begin, TOKEN_WINDOW), :])
        gw = gw_ref[:, pl.ds(begin, TOKEN_WINDOW)]
        return gsum + jnp.sum(jnp.where(hit, gw, 0.0), axis=-1, keepdims=True)

    gs_ref[...] = lax.fori_loop(0, nwin_ref[j], window, jnp.zeros((SORT_BLOCK, 1), F32))
    xs_ref[...] = acc_ref[...].astype(BF16)


def _per_expert_row():
    return pl.BlockSpec((None, 1, TOKENS),
                        lambda j, e, s, n: (e[jnp.minimum(j, N_SORT_BLOCKS - 1)], 0, 0))


def _dispatch(plan, dest3, gw3, h):
    exp, start, nwin = plan
    grid_spec = pltpu.PrefetchScalarGridSpec(
        num_scalar_prefetch=3,
        grid=(N_SORT_BLOCKS,),
        in_specs=[_per_expert_row(), _per_expert_row(),
                  pl.BlockSpec((TOKENS, D_MODEL), lambda j, e, s, n: (0, 0),
                               pipeline_mode=pl.Buffered(1))],
        out_specs=[pl.BlockSpec((SORT_BLOCK, D_MODEL), lambda j, e, s, n: (j, 0)),
                   pl.BlockSpec((SORT_BLOCK, 1), lambda j, e, s, n: (j, 0))],
        scratch_shapes=[pltpu.VMEM((SORT_BLOCK, D_MODEL), F32)],
    )
    return pl.pallas_call(
        _dispatch_kernel,
        grid_spec=grid_spec,
        out_shape=[jax.ShapeDtypeStruct((SORTED_ROWS, D_MODEL), BF16),
                   jax.ShapeDtypeStruct((SORTED_ROWS, 1), F32)],
        compiler_params=_params("arbitrary"),
        name="moe_dispatch",
    )(exp, start, nwin, dest3, gw3, h)


def _expert_kernel(rexp_ref, nhalf_ref, xs_ref, gs_ref, w1_ref, w3_ref, w2_ref, ys_ref, acc_ref):
    q = pl.program_id(0)
    f = pl.program_id(1)
    last = pl.num_programs(1) - 1
    nhalf = nhalf_ref[q]

    @pl.when((q == 0) & (f == 0))
    def _():
        acc_ref[...] = jnp.zeros_like(acc_ref)

    def swiglu_part(n_blocks):
        for k in range(n_blocks):
            rows = slice(k * EXPERT_BLOCK, (k + 1) * EXPERT_BLOCK)
            x = xs_ref[rows, :]
            a = _dot(x, w1_ref[...])
            b = _dot(x, w3_ref[...])
            part = _dot((jax.nn.silu(a) * b).astype(BF16), w2_ref[...])
            acc_ref[rows, :] = jnp.where(f == 0, 0.0, acc_ref[rows, :]) + part

        @pl.when(f == last)
        def _():
            rows = slice(0, n_blocks * EXPERT_BLOCK)
            ys_ref[rows, :] = (acc_ref[rows, :] * gs_ref[rows, :]).astype(BF16)

    @pl.when(nhalf == 2)
    def _():
        swiglu_part(2)

    @pl.when(nhalf == 1)
    def _():
        swiglu_part(1)

    @pl.when((nhalf < 2) & (f == last))
    def _():
        ys_ref[EXPERT_BLOCK:EXPERT_REGION, :] = jnp.zeros((EXPERT_REGION - EXPERT_BLOCK, D_MODEL), BF16)

    @pl.when((nhalf == 0) & (f == last))
    def _():
        ys_ref[0:EXPERT_BLOCK, :] = jnp.zeros((EXPERT_BLOCK, D_MODEL), BF16)


def _experts(rexp, nhalf, xs, gs, w1, w3, w2):
    grid_spec = pltpu.PrefetchScalarGridSpec(
        num_scalar_prefetch=2,
        grid=(N_REGIONS, D_FF_EXPERT // EXPERT_FF_TILE),
        in_specs=[
            pl.BlockSpec((EXPERT_REGION, D_MODEL), lambda q, f, re, nh: (q, 0)),
            pl.BlockSpec((EXPERT_REGION, 1), lambda q, f, re, nh: (q, 0)),
            pl.BlockSpec((None, D_MODEL, EXPERT_FF_TILE), lambda q, f, re, nh: (re[q], 0, f)),
            pl.BlockSpec((None, D_MODEL, EXPERT_FF_TILE), lambda q, f, re, nh: (re[q], 0, f)),
            pl.BlockSpec((None, EXPERT_FF_TILE, D_MODEL), lambda q, f, re, nh: (re[q], f, 0)),
        ],
        out_specs=pl.BlockSpec((EXPERT_REGION, D_MODEL), lambda q, f, re, nh: (q, 0)),
        scratch_shapes=[pltpu.VMEM((EXPERT_REGION, D_MODEL), F32)],
    )
    return pl.pallas_call(
        _expert_kernel,
        grid_spec=grid_spec,
        out_shape=jax.ShapeDtypeStruct((SORTED_ROWS, D_MODEL), BF16),
        compiler_params=_params("arbitrary", "arbitrary"),
        name="moe_experts",
    )(rexp, nhalf, xs, gs, w1, w3, w2)


def _combine_copies(wstart_ref, ys_hbm, buf_ref, sem_ref, tile, rnd, slot):
    copies = []
    for e in range(N_EXPERTS):
        wanted = wstart_ref[tile * N_EXPERTS + e] + rnd * COMBINE_WINDOW
        begin = pl.multiple_of(jnp.minimum(wanted, SORTED_ROWS - COMBINE_WINDOW), BF16_ROWS)
        copies.append(pltpu.make_async_copy(
            ys_hbm.at[pl.ds(begin, COMBINE_WINDOW), :],
            buf_ref.at[slot, pl.ds(e * COMBINE_WINDOW, COMBINE_WINDOW), :],
            sem_ref.at[slot, e]))
    return copies


def _combine_kernel(wstart_ref, nround_ref, dest_ref, ys_hbm, x_ref, pg_ref, out_ref,
                    buf_ref, sem_ref):
    i = pl.program_id(0)
    n_tiles = pl.num_programs(0)
    slot = lax.rem(i, 2)

    @pl.when(i == 0)
    def _():
        for c in _combine_copies(wstart_ref, ys_hbm, buf_ref, sem_ref, 0, 0, 0):
            c.start()

    @pl.when(i + 1 < n_tiles)
    def _():
        for c in _combine_copies(wstart_ref, ys_hbm, buf_ref, sem_ref, i + 1, 0, 1 - slot):
            c.start()

    def gather(rnd, fetch_slot):
        sub = lax.broadcasted_iota(I32, (COMBINE_WINDOW, 1), 0)
        pieces = []
        for e in range(N_EXPERTS):
            wanted = wstart_ref[i * N_EXPERTS + e] + rnd * COMBINE_WINDOW
            begin = jnp.minimum(wanted, SORTED_ROWS - COMBINE_WINDOW)
            rowid = sub + begin
            rowid = jnp.where(rowid >= wanted, rowid, -2)
            pieces.append(jnp.where(rowid == dest_ref[e:e + 1, :], 1.0, 0.0).astype(BF16))
        onehot = jnp.concatenate(pieces, axis=0)
        return _dot_tn(onehot, buf_ref[fetch_slot])

    for c in _combine_copies(wstart_ref, ys_hbm, buf_ref, sem_ref, i, 0, slot):
        c.wait()
    y = gather(0, slot)

    def extra(rnd, y):
        copies = _combine_copies(wstart_ref, ys_hbm, buf_ref, sem_ref, i, rnd, 2)
        for c in copies:
            c.start()
        for c in copies:
            c.wait()
        return y + gather(rnd, 2)

    y = lax.fori_loop(1, nround_ref[i], extra, y)
    out_ref[...] = x_ref[...] + _rms(y, pg_ref[...])


def _combine(wstart, nround, dest, ys, x, post_g):
    tile = lambda i, ws, nr: (i, 0)
    grid_spec = pltpu.PrefetchScalarGridSpec(
        num_scalar_prefetch=2,
        grid=(TOKENS // COMBINE_TILE,),
        in_specs=[
            pl.BlockSpec((N_EXPERTS, COMBINE_TILE), lambda i, ws, nr: (0, i)),
            pl.BlockSpec(memory_space=pl.ANY),
            pl.BlockSpec((COMBINE_TILE, D_MODEL), tile),
            pl.BlockSpec((1, D_MODEL), lambda i, ws, nr: (0, 0)),
        ],
        out_specs=pl.BlockSpec((COMBINE_TILE, D_MODEL), tile),
        scratch_shapes=[pltpu.VMEM((3, N_EXPERTS * COMBINE_WINDOW, D_MODEL), BF16),
                        pltpu.SemaphoreType.DMA((3, N_EXPERTS))],
    )
    return pl.pallas_call(
        _combine_kernel,
        grid_spec=grid_spec,
        out_shape=jax.ShapeDtypeStruct((TOKENS, D_MODEL), F32),
        compiler_params=_params("arbitrary"),
        name="moe_combine",
    )(wstart, nround, dest, ys, x, post_g)


def _routing_plan(sel, rank, counts):
    counts = counts.reshape(N_EXPERTS)
    padded = ((counts + EXPERT_BLOCK - 1) // EXPERT_BLOCK) * EXPERT_BLOCK
    region = ((counts + EXPERT_REGION - 1) // EXPERT_REGION) * EXPERT_REGION
    offs = jnp.cumsum(region) - region
    dest = jnp.where(sel == 1, rank + offs[:, None], -1)

    cum_reg = jnp.cumsum(region // EXPERT_REGION)
    qs = jnp.arange(N_REGIONS, dtype=I32)
    rexp = jnp.sum((cum_reg[None, :] <= jnp.minimum(qs, cum_reg[-1] - 1)[:, None]).astype(I32),
                   axis=1)
    rexp = jnp.minimum(rexp, N_EXPERTS - 1).astype(I32)
    rows_left = offs[rexp] + padded[rexp] - qs * EXPERT_REGION
    n_half = jnp.clip(rows_left // EXPERT_BLOCK, 0, EXPERT_REGION // EXPERT_BLOCK).astype(I32)

    group_end = (rank + sel)[:, LANES - 1::LANES]
    js = jnp.arange(N_SORT_BLOCKS, dtype=I32)
    e_s = rexp[js // (EXPERT_REGION // SORT_BLOCK)]
    lo_row = js * SORT_BLOCK - offs[e_s]
    has_rows = lo_row < counts[e_s]
    hi_row = jnp.minimum(lo_row + SORT_BLOCK, counts[e_s])
    ends = group_end[e_s]
    g_lo = jnp.sum((ends <= lo_row[:, None]).astype(I32), axis=1)
    g_hi = jnp.sum((ends < hi_row[:, None]).astype(I32), axis=1)
    span = (g_hi - g_lo + 1) * LANES
    n_win = jnp.where(has_rows, (span + TOKEN_WINDOW - 1) // TOKEN_WINDOW, 0).astype(I32)
    start = jnp.where(has_rows, g_lo * LANES, 0).astype(I32)
    d_plan = (e_s.astype(I32), start, n_win)

    tile_lo = rank[:, ::COMBINE_TILE]
    tile_hi = jnp.concatenate([tile_lo[:, 1:], counts[:, None]], axis=1)
    lo = tile_lo + offs[:, None]
    hi = tile_hi + offs[:, None]
    wstart = jnp.minimum((lo // BF16_ROWS) * BF16_ROWS, SORTED_ROWS - COMBINE_WINDOW)
    rounds = jnp.where(hi > lo, (hi - wstart + COMBINE_WINDOW - 1) // COMBINE_WINDOW, 0)
    n_round = jnp.maximum(jnp.max(rounds, axis=0), 1).astype(I32)
    c_plan = (wstart.T.reshape(-1).astype(I32), n_round)
    return dest, (rexp, n_half), d_plan, c_plan


def _moe(x1, h, gw, sel, rank, counts, w1, w3, w2, post_g):
    dest, e_plan, d_plan, c_plan = _routing_plan(sel, rank, counts)
    dest3 = dest.reshape(N_EXPERTS, 1, TOKENS)
    gw3 = gw.reshape(N_EXPERTS, 1, TOKENS)
    xs, gs = _dispatch(d_plan, dest3, gw3, h)
    ys = _experts(*e_plan, xs, gs, w1, w3, w2)
    return _combine(*c_plan, dest, ys, x1, post_g)


def kernel(x, pre_mix_g, post_mix_g, w_in, w_fourier, w_gmlp, w_out, w_spatial, b_spatial,
           gmlp_ln_g, gmlp_ln_b, pre_ffn_g, post_ffn_g, ffn_w1, ffn_w3, ffn_w2,
           router_w, router_b, moe_w1, moe_w3, moe_w2):
    pos = jnp.asarray(_POS_DFT).astype(BF16)
    chan = jnp.asarray(_CHAN_DFT).astype(BF16)
    tri = jnp.asarray(np.triu(np.ones((ROW_TILE, ROW_TILE), np.float32), 1)).astype(BF16)
    row = lambda v: v.reshape(1, -1)

    xf = x.reshape(TOKENS, D_MODEL)
    for l in range(DEPTH):
        b_full = jnp.repeat(b_spatial[l].T, GROUP_DIM, axis=1)
        za, sg, gate = _mixer_front(xf, row(pre_mix_g[l]), w_in[l], w_spatial[l], b_full,
                                    row(gmlp_ln_g[l]), row(gmlp_ln_b[l]))
        fa = _fourier(za.reshape(BATCH, SEQ, MIX_WIDTH), chan, pos).reshape(TOKENS, MIX_WIDTH)
        tail_args = (fa, sg, gate, xf, w_fourier[l], w_gmlp[l], w_out[l], row(post_mix_g[l]))
        i = l // 2
        if l % 2 == 0:
            x1 = _mixer_tail(*tail_args)
            xf = _dense_ffn(x1, row(pre_ffn_g[l]), ffn_w1[i].astype(BF16), ffn_w3[i].astype(BF16),
                            ffn_w2[i].astype(BF16), row(post_ffn_g[l]))
        else:
            x1, h, gw, sel, rank, counts = _mixer_tail_router(
                *tail_args, row(pre_ffn_g[l]), router_w[i].T, router_b[i].reshape(N_EXPERTS, 1),
                tri)
            xf = _moe(x1, h, gw, sel, rank, counts, moe_w1[i].astype(BF16), moe_w3[i].astype(BF16),
                      moe_w2[i].astype(BF16), row(post_ffn_g[l]))
    return xf.reshape(BATCH, SEQ, D_MODEL)
```

```python
import numpy as np
import jax
import jax.numpy as jnp
from jax import lax
from jax.experimental import pallas as pl
from jax.experimental.pallas import tpu as pltpu

F32 = jnp.float32
BF16 = jnp.bfloat16
I32 = jnp.int32

D_MODEL = 1024
BATCH = 8
SEQ = 2048
HALF_SEQ = SEQ // 2
TOKENS = BATCH * SEQ
DEPTH = 2
N_GROUPS = 4
GROUP_DIM = 128
MIX_WIDTH = N_GROUPS * GROUP_DIM
CHUNK = 128
D_IN = 3 * MIX_WIDTH + 2 * D_MODEL
D_FF_DENSE = 2816
N_EXPERTS = 8
D_FF_EXPERT = 3584
RMS_EPS = 1e-6
LN_EPS = 1e-5
LANES = 128

ROW_TILE = 512
FOURIER_ROW_TILE = 512
EXPERT_BLOCK = 512
EXPERT_REGION = 1024
SORT_BLOCK = 256
TOKEN_WINDOW = 1280
COMBINE_TILE = 256
COMBINE_WINDOW = 128
BF16_ROWS = 16
EXPERT_FF_TILE = 1792

SORTED_ROWS = 2 * TOKENS + N_EXPERTS * EXPERT_REGION
N_REGIONS = SORTED_ROWS // EXPERT_REGION
N_SORT_BLOCKS = SORTED_ROWS // SORT_BLOCK
N_COMBINE_TILES = TOKENS // COMBINE_TILE

VMEM_LIMIT = 56 * 1024 * 1024


def _params(*sem):
    return pltpu.CompilerParams(dimension_semantics=sem, vmem_limit_bytes=VMEM_LIMIT)


def _resident(shape):
    nd = len(shape)
    return pl.BlockSpec(shape, lambda *_: (0,) * nd, pipeline_mode=pl.Buffered(1))


def _layer(shape, l):
    nd = len(shape)
    return pl.BlockSpec((None,) + tuple(shape), lambda *_: (l,) + (0,) * nd,
                        pipeline_mode=pl.Buffered(1))


def _rms(x, g):
    return x * lax.rsqrt(jnp.mean(x * x, axis=-1, keepdims=True) + RMS_EPS) * g


def _dot(a, b):
    return jnp.dot(a, b, preferred_element_type=F32)


def _dot_nt(a, b):
    return lax.dot_general(a, b, (((1,), (1,)), ((), ())), preferred_element_type=F32)


def _dot_tn(a, b):
    return lax.dot_general(a, b, (((0,), (0,)), ((), ())), preferred_element_type=F32)


def _gelu(x):
    return 0.5 * x * (1.0 + lax.erf(x * np.float32(np.sqrt(0.5))))


def _cast_once(src_ref, dst_ref):
    chunk = 128

    @pl.when(pl.program_id(0) == 0)
    def _():
        def body(i, carry):
            rows = pl.ds(pl.multiple_of(i * chunk, chunk), chunk)
            dst_ref[rows, :] = src_ref[rows, :].astype(BF16)
            return carry
        lax.fori_loop(0, src_ref.shape[0] // chunk, body, 0)


def _split_bf16(x):
    hi = x.astype(BF16)
    return hi, (x - hi.astype(F32)).astype(BF16)


def _dft_tables():
    k = np.arange(HALF_SEQ, dtype=np.int64)
    scale = 1.0 / np.sqrt(SEQ)
    halves = []
    for p in (0, 1):
        n = 2 * np.arange(HALF_SEQ, dtype=np.int64) + p
        ang = 2.0 * np.pi * ((k[:, None] * n[None, :]) % SEQ).astype(np.float64) / SEQ
        halves.append(np.concatenate([np.cos(ang) * scale, -np.sin(ang) * scale], axis=1))
    pos = np.stack(halves)
    c = np.arange(GROUP_DIM, dtype=np.int64)
    angc = 2.0 * np.pi * ((c[:, None] * c[None, :]) % GROUP_DIM).astype(np.float64) / GROUP_DIM
    scalec = 1.0 / np.sqrt(GROUP_DIM)
    chan = np.concatenate([np.cos(angc) * scalec, np.sin(angc) * scalec], axis=1)
    return pos.astype(np.float32), chan.astype(np.float32)


_POS_DFT, _CHAN_DFT = _dft_tables()


def _front_kernel(x_ref, g_ref, winf_ref, ws_ref, bs_ref, lng_ref, lnb_ref,
                  za_ref, sg_ref, gate_ref, win_ref):
    _cast_once(winf_ref, win_ref)
    ws = [ws_ref[g].astype(BF16) for g in range(N_GROUPS)]
    h = _rms(x_ref[...], g_ref[...]).astype(BF16)
    za_ref[...] = _dot(h, win_ref[:, 0:MIX_WIDTH]).astype(BF16)
    u = _gelu(_dot(h, win_ref[:, MIX_WIDTH:2 * MIX_WIDTH]))
    v = _gelu(_dot(h, win_ref[:, 2 * MIX_WIDTH:3 * MIX_WIDTH]))
    mu = jnp.mean(v, axis=-1, keepdims=True)
    vc = v - mu
    var = jnp.mean(vc * vc, axis=-1, keepdims=True)
    vln = (vc * lax.rsqrt(var + LN_EPS) * lng_ref[...] + lnb_ref[...]).astype(BF16)
    for c in range(ROW_TILE // CHUNK):
        rows = slice(c * CHUNK, (c + 1) * CHUNK)
        for g in range(N_GROUPS):
            cols = slice(g * GROUP_DIM, (g + 1) * GROUP_DIM)
            sv = _dot(ws[g], vln[rows, cols]) + bs_ref[:, cols]
            sg_ref[rows, cols] = (u[rows, cols] * sv).astype(BF16)
    zg = _dot(h, win_ref[:, 3 * MIX_WIDTH:D_IN])
    gate_ref[...] = jax.nn.sigmoid(zg).astype(BF16)


def _mixer_front(l, x, g, w_in, w_s, b_full, ln_g, ln_b):
    row = lambda w: pl.BlockSpec((ROW_TILE, w), lambda i: (i, 0))
    return pl.pallas_call(
        _front_kernel,
        grid=(TOKENS // ROW_TILE,),
        in_specs=[row(D_MODEL), _layer((1, D_MODEL), l), _layer((D_MODEL, D_IN), l),
                  _layer((N_GROUPS, CHUNK, CHUNK), l), _layer((CHUNK, MIX_WIDTH), l),
                  _layer((1, MIX_WIDTH), l), _layer((1, MIX_WIDTH), l)],
        out_specs=[row(MIX_WIDTH), row(MIX_WIDTH), row(2 * D_MODEL)],
        out_shape=[jax.ShapeDtypeStruct((TOKENS, MIX_WIDTH), BF16),
                   jax.ShapeDtypeStruct((TOKENS, MIX_WIDTH), BF16),
                   jax.ShapeDtypeStruct((TOKENS, 2 * D_MODEL), BF16)],
        scratch_shapes=[pltpu.VMEM((D_MODEL, D_IN), BF16)],
        compiler_params=_params("arbitrary"),
        name="mixer_front",
    )(x, g, w_in, w_s, b_full, ln_g, ln_b)


def _fourier_kernel(za_ref, chan_ref, pos_ref, out_ref, stage_ref, rhs_ref):
    @pl.when(pl.program_id(1) == 0)
    def _():
        for g in range(N_GROUPS):
            cols = slice(g * GROUP_DIM, (g + 1) * GROUP_DIM)
            xcs = _dot(za_ref[0, :, cols], chan_ref[...])
            stage_ref[g] = xcs[:, 0:GROUP_DIM]
            stage_ref[N_GROUPS + g] = xcs[:, GROUP_DIM:2 * GROUP_DIM]
        for g in range(N_GROUPS):
            cols = slice(g * GROUP_DIM, (g + 1) * GROUP_DIM)
            for p in (0, 1):
                rows = pl.ds(p, HALF_SEQ, stride=2)
                rhs_ref[p, 0:HALF_SEQ, cols] = stage_ref[g, rows, :].astype(BF16)
                rhs_ref[p, HALF_SEQ:SEQ, cols] = stage_ref[N_GROUPS + g, rows, :].astype(BF16)

    even = _dot(pos_ref[0], rhs_ref[0])
    odd = _dot(pos_ref[1], rhs_ref[1])
    out_ref[0, 0] = (even + odd).astype(BF16)
    out_ref[0, 1] = (even - odd).astype(BF16)


def _fourier(za, chan, pos):
    out = pl.pallas_call(
        _fourier_kernel,
        grid=(BATCH, HALF_SEQ // FOURIER_ROW_TILE),
        in_specs=[pl.BlockSpec((1, SEQ, MIX_WIDTH), lambda b, i: (b, 0, 0)),
                  _resident((GROUP_DIM, 2 * GROUP_DIM)),
                  pl.BlockSpec((2, FOURIER_ROW_TILE, SEQ), lambda b, i: (0, i, 0))],
        out_specs=pl.BlockSpec((1, 2, FOURIER_ROW_TILE, MIX_WIDTH), lambda b, i: (b, 0, i, 0)),
        out_shape=jax.ShapeDtypeStruct((BATCH, 2, HALF_SEQ, MIX_WIDTH), BF16),
        scratch_shapes=[pltpu.VMEM((2 * N_GROUPS, SEQ, GROUP_DIM), F32),
                        pltpu.VMEM((2, SEQ, MIX_WIDTH), BF16)],
        compiler_params=_params("arbitrary", "arbitrary"),
        name="fourier_mix",
    )(za, chan, pos)
    return out.reshape(TOKENS, MIX_WIDTH)


def _tail_common(fa_ref, sg_ref, gate_ref, x_ref, wff_ref, wgf_ref, wof_ref, pg_ref,
                 wf_ref, wg_ref, wo_ref):
    _cast_once(wff_ref, wf_ref)
    _cast_once(wgf_ref, wg_ref)
    _cast_once(wof_ref, wo_ref)
    ya = _dot(fa_ref[...], wf_ref[...])
    yb = _dot(sg_ref[...], wg_ref[...])
    m = gate_ref[:, 0:D_MODEL].astype(F32) * ya + gate_ref[:, D_MODEL:2 * D_MODEL].astype(F32) * yb
    y = _dot(m.astype(BF16), wo_ref[...])
    return x_ref[...] + _rms(y, pg_ref[...])


def _tail_kernel(fa_ref, sg_ref, gate_ref, x_ref, wff_ref, wgf_ref, wof_ref, pg_ref, x1_ref,
                 wf_ref, wg_ref, wo_ref):
    x1_ref[...] = _tail_common(fa_ref, sg_ref, gate_ref, x_ref, wff_ref, wgf_ref, wof_ref, pg_ref,
                               wf_ref, wg_ref, wo_ref)


def _tail_router_kernel(fa_ref, sg_ref, gate_ref, x_ref, wff_ref, wgf_ref, wof_ref, pg_ref,
                        fg_ref, rwt_ref, rb_ref, tri_ref,
                        x1_ref, h_ref, gw_ref, sel_ref, rank_ref, cnt_ref,
                        wf_ref, wg_ref, wo_ref, carry_ref):
    @pl.when(pl.program_id(0) == 0)
    def _():
        carry_ref[...] = jnp.zeros_like(carry_ref)

    x1 = _tail_common(fa_ref, sg_ref, gate_ref, x_ref, wff_ref, wgf_ref, wof_ref, pg_ref,
                      wf_ref, wg_ref, wo_ref)
    x1_ref[...] = x1
    h = _rms(x1, fg_ref[...])
    h_hi, h_lo = _split_bf16(h)
    h_ref[...] = h_hi

    w_hi, w_lo = _split_bf16(rwt_ref[...])
    logits = _dot_nt(w_hi, h_hi) + _dot_nt(w_hi, h_lo) + _dot_nt(w_lo, h_hi) + rb_ref[...]
    row = lax.broadcasted_iota(I32, logits.shape, 0)
    m1 = jnp.max(logits, axis=0, keepdims=True)
    i1 = jnp.min(jnp.where(logits == m1, row, N_EXPERTS), axis=0, keepdims=True)
    rest = jnp.where(row == i1, -jnp.inf, logits)
    m2 = jnp.max(rest, axis=0, keepdims=True)
    i2 = jnp.min(jnp.where(rest == m2, row, N_EXPERTS), axis=0, keepdims=True)
    e2 = jnp.exp(m2 - m1)
    den = 1.0 + e2
    gw_ref[...] = jnp.where(row == i1, 1.0 / den, 0.0) + jnp.where(row == i2, e2 / den, 0.0)
    sel = jnp.where((row == i1) | (row == i2), 1.0, 0.0)
    sel_ref[...] = sel.astype(I32)
    before = _dot(sel.astype(BF16), tri_ref[...]) + carry_ref[...]
    rank_ref[...] = before.astype(I32)
    carry_ref[...] = carry_ref[...] + jnp.sum(sel, axis=1, keepdims=True)
    cnt_ref[...] = carry_ref[...].astype(I32)


def _tail_specs(l):
    row = lambda w: pl.BlockSpec((ROW_TILE, w), lambda i: (i, 0))
    in_specs = [row(MIX_WIDTH), row(MIX_WIDTH), row(2 * D_MODEL), row(D_MODEL),
                _layer((MIX_WIDTH, D_MODEL), l), _layer((MIX_WIDTH, D_MODEL), l),
                _layer((D_MODEL, D_MODEL), l), _layer((1, D_MODEL), l)]
    weight_scratch = [pltpu.VMEM((MIX_WIDTH, D_MODEL), BF16), pltpu.VMEM((MIX_WIDTH, D_MODEL), BF16),
                      pltpu.VMEM((D_MODEL, D_MODEL), BF16)]
    return row, in_specs, weight_scratch


def _mixer_tail(l, fa, sg, gate, x, w_f, w_g, w_o, post_g):
    row, in_specs, weight_scratch = _tail_specs(l)
    return pl.pallas_call(
        _tail_kernel,
        grid=(TOKENS // ROW_TILE,),
        in_specs=in_specs,
        out_specs=row(D_MODEL),
        out_shape=jax.ShapeDtypeStruct((TOKENS, D_MODEL), F32),
        scratch_shapes=weight_scratch,
        compiler_params=_params("arbitrary"),
        name="mixer_tail",
    )(fa, sg, gate, x, w_f, w_g, w_o, post_g)


def _mixer_tail_router(l, fa, sg, gate, x, w_f, w_g, w_o, post_g, ffn_g, router_wt, router_b, tri):
    row, in_specs, weight_scratch = _tail_specs(l)
    in_specs += [_layer((1, D_MODEL), l), _layer((N_EXPERTS, D_MODEL), l // 2),
                 _layer((N_EXPERTS, 1), l // 2), _resident((ROW_TILE, ROW_TILE))]
    col = pl.BlockSpec((N_EXPERTS, ROW_TILE), lambda i: (0, i))
    per_token = lambda dt: jax.ShapeDtypeStruct((N_EXPERTS, TOKENS), dt)
    return pl.pallas_call(
        _tail_router_kernel,
        grid=(TOKENS // ROW_TILE,),
        in_specs=in_specs,
        out_specs=[row(D_MODEL), row(D_MODEL), col, col, col,
                   pl.BlockSpec((N_EXPERTS, 1), lambda i: (0, 0))],
        out_shape=[jax.ShapeDtypeStruct((TOKENS, D_MODEL), F32),
                   jax.ShapeDtypeStruct((TOKENS, D_MODEL), BF16),
                   per_token(F32), per_token(I32), per_token(I32),
                   jax.ShapeDtypeStruct((N_EXPERTS, 1), I32)],
        scratch_shapes=weight_scratch + [pltpu.VMEM((N_EXPERTS, 1), F32)],
        compiler_params=_params("arbitrary"),
        name="mixer_tail_router",
    )(fa, sg, gate, x, w_f, w_g, w_o, post_g, ffn_g, router_wt, router_b, tri)


def _dense_ffn_kernel(x_ref, fg_ref, w1_ref, w3_ref, w2_ref, pg_ref, out_ref):
    x = x_ref[...]
    h = _rms(x, fg_ref[...]).astype(BF16)
    a = _dot(h, w1_ref[...])
    b = _dot(h, w3_ref[...])
    y = _dot((jax.nn.silu(a) * b).astype(BF16), w2_ref[...])
    out_ref[...] = x + _rms(y, pg_ref[...])


def _dense_ffn(l, x, ffn_g, w1, w3, w2, post_g):
    row = pl.BlockSpec((ROW_TILE, D_MODEL), lambda i: (i, 0))
    return pl.pallas_call(
        _dense_ffn_kernel,
        grid=(TOKENS // ROW_TILE,),
        in_specs=[row, _layer((1, D_MODEL), l), _layer((D_MODEL, D_FF_DENSE), l // 2),
                  _layer((D_MODEL, D_FF_DENSE), l // 2), _layer((D_FF_DENSE, D_MODEL), l // 2),
                  _layer((1, D_MODEL), l)],
        out_specs=row,
        out_shape=jax.ShapeDtypeStruct((TOKENS, D_MODEL), F32),
        compiler_params=_params("arbitrary"),
        name="dense_ffn",
    )(x, ffn_g, w1, w3, w2, post_g)


def _window_hits(k, block, start_ref, dest_ref):
    wanted = start_ref[block] + k * TOKEN_WINDOW
    begin = pl.multiple_of(jnp.minimum(wanted, TOKENS - TOKEN_WINDOW), LANES)
    tok = lax.broadcasted_iota(I32, (1, TOKEN_WINDOW), 1) + begin
    dest = jnp.where(tok >= wanted, dest_ref[:, pl.ds(begin, TOKEN_WINDOW)], -1)
    rows = lax.broadcasted_iota(I32, (SORT_BLOCK, TOKEN_WINDOW), 0) + block * SORT_BLOCK
    return begin, rows == dest


def _dispatch_kernel(exp_ref, start_ref, nwin_ref, dest_ref, gw_ref, h_ref, xs_ref, gs_ref,
                     acc_ref):
    j = pl.program_id(0)
    acc_ref[...] = jnp.zeros_like(acc_ref)

    def window(k, gsum):
        begin, hit = _window_hits(k, j, start_ref, dest_ref)
        onehot = jnp.where(hit, 1.0, 0.0).astype(BF16)
        acc_ref[...] += _dot(onehot, h_ref[pl.ds(begin, TOKEN_WINDOW), :])
        gw = gw_ref[:, pl.ds(begin, TOKEN_WINDOW)]
        return gsum + jnp.sum(jnp.where(hit, gw, 0.0), axis=-1, keepdims=True)

    gs_ref[...] = lax.fori_loop(0, nwin_ref[j], window, jnp.zeros((SORT_BLOCK, 1), F32))
    xs_ref[...] = acc_ref[...].astype(BF16)


def _per_expert_row():
    return pl.BlockSpec((None, 1, TOKENS),
                        lambda j, e, s, n: (e[jnp.minimum(j, N_SORT_BLOCKS - 1)], 0, 0))


def _dispatch(plan, dest3, gw3, h):
    exp, start, nwin = plan
    grid_spec = pltpu.PrefetchScalarGridSpec(
        num_scalar_prefetch=3,
        grid=(N_SORT_BLOCKS,),
        in_specs=[_per_expert_row(), _per_expert_row(),
                  pl.BlockSpec((TOKENS, D_MODEL), lambda j, e, s, n: (0, 0),
                               pipeline_mode=pl.Buffered(1))],
        out_specs=[pl.BlockSpec((SORT_BLOCK, D_MODEL), lambda j, e, s, n: (j, 0)),
                   pl.BlockSpec((SORT_BLOCK, 1), lambda j, e, s, n: (j, 0))],
        scratch_shapes=[pltpu.VMEM((SORT_BLOCK, D_MODEL), F32)],
    )
    return pl.pallas_call(
        _dispatch_kernel,
        grid_spec=grid_spec,
        out_shape=[jax.ShapeDtypeStruct((SORTED_ROWS, D_MODEL), BF16),
                   jax.ShapeDtypeStruct((SORTED_ROWS, 1), F32)],
        compiler_params=_params("arbitrary"),
        name="moe_dispatch",
    )(exp, start, nwin, dest3, gw3, h)


def _expert_kernel(rexp_ref, nhalf_ref, xs_ref, gs_ref, w1_ref, w3_ref, w2_ref, ys_ref, acc_ref):
    q = pl.program_id(0)
    f = pl.program_id(1)
    last = pl.num_programs(1) - 1
    nhalf = nhalf_ref[q]

    @pl.when((q == 0) & (f == 0))
    def _():
        acc_ref[...] = jnp.zeros_like(acc_ref)

    def swiglu_part(n_blocks):
        for k in range(n_blocks):
            rows = slice(k * EXPERT_BLOCK, (k + 1) * EXPERT_BLOCK)
            x = xs_ref[rows, :]
            a = _dot(x, w1_ref[...])
            b = _dot(x, w3_ref[...])
            part = _dot((jax.nn.silu(a) * b).astype(BF16), w2_ref[...])
            acc_ref[rows, :] = jnp.where(f == 0, 0.0, acc_ref[rows, :]) + part

        @pl.when(f == last)
        def _():
            rows = slice(0, n_blocks * EXPERT_BLOCK)
            ys_ref[rows, :] = (acc_ref[rows, :] * gs_ref[rows, :]).astype(BF16)

    @pl.when(nhalf == 2)
    def _():
        swiglu_part(2)

    @pl.when(nhalf == 1)
    def _():
        swiglu_part(1)

    @pl.when((nhalf < 2) & (f == last))
    def _():
        ys_ref[EXPERT_BLOCK:EXPERT_REGION, :] = jnp.zeros((EXPERT_REGION - EXPERT_BLOCK, D_MODEL), BF16)

    @pl.when((nhalf == 0) & (f == last))
    def _():
        ys_ref[0:EXPERT_BLOCK, :] = jnp.zeros((EXPERT_BLOCK, D_MODEL), BF16)


def _experts(l, rexp, nhalf, xs, gs, w1, w3, w2):
    i = l // 2
    grid_spec = pltpu.PrefetchScalarGridSpec(
        num_scalar_prefetch=2,
        grid=(N_REGIONS, D_FF_EXPERT // EXPERT_FF_TILE),
        in_specs=[
            pl.BlockSpec((EXPERT_REGION, D_MODEL), lambda q, f, re, nh: (q, 0)),
            pl.BlockSpec((EXPERT_REGION, 1), lambda q, f, re, nh: (q, 0)),
            pl.BlockSpec((None, None, D_MODEL, EXPERT_FF_TILE),
                         lambda q, f, re, nh: (i, re[q], 0, f)),
            pl.BlockSpec((None, None, D_MODEL, EXPERT_FF_TILE),
                         lambda q, f, re, nh: (i, re[q], 0, f)),
            pl.BlockSpec((None, None, EXPERT_FF_TILE, D_MODEL),
                         lambda q, f, re, nh: (i, re[q], f, 0)),
        ],
        out_specs=pl.BlockSpec((EXPERT_REGION, D_MODEL), lambda q, f, re, nh: (q, 0)),
        scratch_shapes=[pltpu.VMEM((EXPERT_REGION, D_MODEL), F32)],
    )
    return pl.pallas_call(
        _expert_kernel,
        grid_spec=grid_spec,
        out_shape=jax.ShapeDtypeStruct((SORTED_ROWS, D_MODEL), BF16),
        compiler_params=_params("arbitrary", "arbitrary"),
        name="moe_experts",
    )(rexp, nhalf, xs, gs, w1, w3, w2)


def _combine_copies(wstart_ref, ys_hbm, buf_ref, sem_ref, tile, rnd, slot):
    copies = []
    for e in range(N_EXPERTS):
        wanted = wstart_ref[tile * N_EXPERTS + e] + rnd * COMBINE_WINDOW
        begin = pl.multiple_of(jnp.minimum(wanted, SORTED_ROWS - COMBINE_WINDOW), BF16_ROWS)
        copies.append(pltpu.make_async_copy(
            ys_hbm.at[pl.ds(begin, COMBINE_WINDOW), :],
            buf_ref.at[slot, pl.ds(e * COMBINE_WINDOW, COMBINE_WINDOW), :],
            sem_ref.at[slot, e]))
    return copies


def _combine_kernel(wstart_ref, nround_ref, dest_ref, ys_hbm, x_ref, pg_ref, out_ref,
                    buf_ref, sem_ref):
    i = pl.program_id(0)
    n_tiles = pl.num_programs(0)
    slot = lax.rem(i, 2)

    @pl.when(i == 0)
    def _():
        for c in _combine_copies(wstart_ref, ys_hbm, buf_ref, sem_ref, 0, 0, 0):
            c.start()

    @pl.when(i + 1 < n_tiles)
    def _():
        for c in _combine_copies(wstart_ref, ys_hbm, buf_ref, sem_ref, i + 1, 0, 1 - slot):
            c.start()

    def gather(rnd, fetch_slot):
        sub = lax.broadcasted_iota(I32, (COMBINE_WINDOW, 1), 0)
        pieces = []
        for e in range(N_EXPERTS):
            wanted = wstart_ref[i * N_EXPERTS + e] + rnd * COMBINE_WINDOW
            begin = jnp.minimum(wanted, SORTED_ROWS - COMBINE_WINDOW)
            rowid = sub + begin
            rowid = jnp.where(rowid >= wanted, rowid, -2)
            pieces.append(jnp.where(rowid == dest_ref[e:e + 1, :], 1.0, 0.0).astype(BF16))
        onehot = jnp.concatenate(pieces, axis=0)
        return _dot_tn(onehot, buf_ref[fetch_slot])

    for c in _combine_copies(wstart_ref, ys_hbm, buf_ref, sem_ref, i, 0, slot):
        c.wait()
    y = gather(0, slot)

    def extra(rnd, y):
        copies = _combine_copies(wstart_ref, ys_hbm, buf_ref, sem_ref, i, rnd, 2)
        for c in copies:
            c.start()
        for c in copies:
            c.wait()
        return y + gather(rnd, 2)

    y = lax.fori_loop(1, nround_ref[i], extra, y)
    out_ref[...] = x_ref[...] + _rms(y, pg_ref[...])


def _combine(l, wstart, nround, dest, ys, x, post_g):
    tile = lambda i, ws, nr: (i, 0)
    grid_spec = pltpu.PrefetchScalarGridSpec(
        num_scalar_prefetch=2,
        grid=(TOKENS // COMBINE_TILE,),
        in_specs=[
            pl.BlockSpec((N_EXPERTS, COMBINE_TILE), lambda i, ws, nr: (0, i)),
            pl.BlockSpec(memory_space=pl.ANY),
            pl.BlockSpec((COMBINE_TILE, D_MODEL), tile),
            pl.BlockSpec((None, 1, D_MODEL), lambda i, ws, nr: (l, 0, 0)),
        ],
        out_specs=pl.BlockSpec((COMBINE_TILE, D_MODEL), tile),
        scratch_shapes=[pltpu.VMEM((3, N_EXPERTS * COMBINE_WINDOW, D_MODEL), BF16),
                        pltpu.SemaphoreType.DMA((3, N_EXPERTS))],
    )
    return pl.pallas_call(
        _combine_kernel,
        grid_spec=grid_spec,
        out_shape=jax.ShapeDtypeStruct((TOKENS, D_MODEL), F32),
        compiler_params=_params("arbitrary"),
        name="moe_combine",
    )(wstart, nround, dest, ys, x, post_g)


def _routing_plan(sel, rank, counts):
    counts = counts.reshape(N_EXPERTS)
    padded = ((counts + EXPERT_BLOCK - 1) // EXPERT_BLOCK) * EXPERT_BLOCK
    region = ((counts + EXPERT_REGION - 1) // EXPERT_REGION) * EXPERT_REGION
    offs = jnp.cumsum(region) - region
    dest = jnp.where(sel == 1, rank + offs[:, None], -1)

    cum_reg = jnp.cumsum(region // EXPERT_REGION)
    qs = jnp.arange(N_REGIONS, dtype=I32)
    rexp = jnp.sum((cum_reg[None, :] <= jnp.minimum(qs, cum_reg[-1] - 1)[:, None]).astype(I32),
                   axis=1)
    rexp = jnp.minimum(rexp, N_EXPERTS - 1).astype(I32)
    rows_left = offs[rexp] + padded[rexp] - qs * EXPERT_REGION
    n_half = jnp.clip(rows_left // EXPERT_BLOCK, 0, EXPERT_REGION // EXPERT_BLOCK).astype(I32)

    group_end = (rank + sel)[:, LANES - 1::LANES]
    js = jnp.arange(N_SORT_BLOCKS, dtype=I32)
    e_s = rexp[js // (EXPERT_REGION // SORT_BLOCK)]
    lo_row = js * SORT_BLOCK - offs[e_s]
    has_rows = lo_row < counts[e_s]
    hi_row = jnp.minimum(lo_row + SORT_BLOCK, counts[e_s])
    ends = group_end[e_s]
    g_lo = jnp.sum((ends <= lo_row[:, None]).astype(I32), axis=1)
    g_hi = jnp.sum((ends < hi_row[:, None]).astype(I32), axis=1)
    span = (g_hi - g_lo + 1) * LANES
    n_win = jnp.where(has_rows, (span + TOKEN_WINDOW - 1) // TOKEN_WINDOW, 0).astype(I32)
    start = jnp.where(has_rows, g_lo * LANES, 0).astype(I32)
    d_plan = (e_s.astype(I32), start, n_win)

    tile_lo = rank[:, ::COMBINE_TILE]
    tile_hi = jnp.concatenate([tile_lo[:, 1:], counts[:, None]], axis=1)
    lo = tile_lo + offs[:, None]
    hi = tile_hi + offs[:, None]
    wstart = jnp.minimum((lo // BF16_ROWS) * BF16_ROWS, SORTED_ROWS - COMBINE_WINDOW)
    rounds = jnp.where(hi > lo, (hi - wstart + COMBINE_WINDOW - 1) // COMBINE_WINDOW, 0)
    n_round = jnp.maximum(jnp.max(rounds, axis=0), 1).astype(I32)
    c_plan = (wstart.T.reshape(-1).astype(I32), n_round)
    return dest, (rexp, n_half), d_plan, c_plan


def _moe(l, x1, h, gw, sel, rank, counts, w1, w3, w2, post_g):
    dest, e_plan, d_plan, c_plan = _routing_plan(sel, rank, counts)
    dest3 = dest.reshape(N_EXPERTS, 1, TOKENS)
    gw3 = gw.reshape(N_EXPERTS, 1, TOKENS)
    xs, gs = _dispatch(d_plan, dest3, gw3, h)
    ys = _experts(l, *e_plan, xs, gs, w1, w3, w2)
    return _combine(l, *c_plan, dest, ys, x1, post_g)


def kernel(x, pre_mix_g, post_mix_g, w_in, w_fourier, w_gmlp, w_out, w_spatial, b_spatial,
           gmlp_ln_g, gmlp_ln_b, pre_ffn_g, post_ffn_g, ffn_w1, ffn_w3, ffn_w2,
           router_w, router_b, moe_w1, moe_w3, moe_w2):
    pos = jnp.asarray(_POS_DFT).astype(BF16)
    chan = jnp.asarray(_CHAN_DFT).astype(BF16)
    tri = jnp.asarray(np.triu(np.ones((ROW_TILE, ROW_TILE), np.float32), 1)).astype(BF16)
    rows = lambda v: v.reshape(v.shape[0], 1, v.shape[1])
    pre_mix_g, post_mix_g, pre_ffn_g, post_ffn_g, gmlp_ln_g, gmlp_ln_b = map(
        rows, (pre_mix_g, post_mix_g, pre_ffn_g, post_ffn_g, gmlp_ln_g, gmlp_ln_b))
    b_full = jnp.repeat(jnp.swapaxes(b_spatial, 1, 2), GROUP_DIM, axis=2)
    router_wt = jnp.swapaxes(router_w, 1, 2)
    router_b = router_b.reshape(router_b.shape[0], N_EXPERTS, 1)
    ffn_w = [w.astype(BF16) for w in (ffn_w1, ffn_w3, ffn_w2)]
    moe_w = [w.astype(BF16) for w in (moe_w1, moe_w3, moe_w2)]

    xf = x.reshape(TOKENS, D_MODEL)
    for l in range(DEPTH):
        za, sg, gate = _mixer_front(l, xf, pre_mix_g, w_in, w_spatial, b_full, gmlp_ln_g, gmlp_ln_b)
        fa = _fourier(za.reshape(BATCH, SEQ, MIX_WIDTH), chan, pos)
        tail_args = (l, fa, sg, gate, xf, w_fourier, w_gmlp, w_out, post_mix_g)
        if l % 2 == 0:
            x1 = _mixer_tail(*tail_args)
            xf = _dense_ffn(l, x1, pre_ffn_g, *ffn_w, post_ffn_g)
        else:
            x1, h, gw, sel, rank, counts = _mixer_tail_router(
                *tail_args, pre_ffn_g, router_wt, router_b, tri)
            xf = _moe(l, x1, h, gw, sel, rank, counts, *moe_w, post_ffn_g)
    return xf.reshape(BATCH, SEQ, D_MODEL)
```

```python
import numpy as np
import jax
import jax.numpy as jnp
from jax import lax
from jax.experimental import pallas as pl
from jax.experimental.pallas import tpu as pltpu

F32 = jnp.float32
BF16 = jnp.bfloat16
I32 = jnp.int32

D_MODEL = 1024
BATCH = 8
SEQ = 2048
HALF_SEQ = SEQ // 2
TOKENS = BATCH * SEQ
DEPTH = 2
N_GROUPS = 4
GROUP_DIM = 128
MIX_WIDTH = N_GROUPS * GROUP_DIM
CHUNK = 128
D_IN = 3 * MIX_WIDTH + 2 * D_MODEL
D_FF_DENSE = 2816
N_EXPERTS = 8
D_FF_EXPERT = 3584
RMS_EPS = 1e-6
LN_EPS = 1e-5
LANES = 128

ROW_TILE = 512
FOURIER_ROW_TILE = 512
EXPERT_BLOCK = 512
EXPERT_REGION = 1024
SORT_BLOCK = 256
TOKEN_WINDOW = 1280
COMBINE_TILE = 512
COMBINE_WINDOW = 192
BF16_ROWS = 16
EXPERT_FF_TILE = 1792

SORTED_ROWS = 2 * TOKENS + N_EXPERTS * EXPERT_REGION
N_REGIONS = SORTED_ROWS // EXPERT_REGION
N_SORT_BLOCKS = SORTED_ROWS // SORT_BLOCK
N_COMBINE_TILES = TOKENS // COMBINE_TILE

VMEM_LIMIT = 56 * 1024 * 1024


def _params(*sem):
    return pltpu.CompilerParams(dimension_semantics=sem, vmem_limit_bytes=VMEM_LIMIT)


def _resident(shape):
    nd = len(shape)
    return pl.BlockSpec(shape, lambda *_: (0,) * nd, pipeline_mode=pl.Buffered(1))


def _layer(shape, l):
    nd = len(shape)
    return pl.BlockSpec((None,) + tuple(shape), lambda *_: (l,) + (0,) * nd,
                        pipeline_mode=pl.Buffered(1))


def _rms(x, g):
    return x * lax.rsqrt(jnp.mean(x * x, axis=-1, keepdims=True) + RMS_EPS) * g


def _dot(a, b):
    return jnp.dot(a, b, preferred_element_type=F32)


def _dot_nt(a, b):
    return lax.dot_general(a, b, (((1,), (1,)), ((), ())), preferred_element_type=F32)


def _dot_tn(a, b):
    return lax.dot_general(a, b, (((0,), (0,)), ((), ())), preferred_element_type=F32)


def _gelu(x):
    return 0.5 * x * (1.0 + lax.erf(x * np.float32(np.sqrt(0.5))))


def _cast_once(src_ref, dst_ref):
    chunk = 128

    @pl.when(pl.program_id(0) == 0)
    def _():
        def body(i, carry):
            rows = pl.ds(pl.multiple_of(i * chunk, chunk), chunk)
            dst_ref[rows, :] = src_ref[rows, :].astype(BF16)
            return carry
        lax.fori_loop(0, src_ref.shape[0] // chunk, body, 0)


def _split_bf16(x):
    hi = x.astype(BF16)
    return hi, (x - hi.astype(F32)).astype(BF16)


def _side_specs(arrays, steps):
    specs, shapes = [], []
    for a in arrays:
        rows, cols = a.shape
        chunk = rows // steps
        if chunk % BF16_ROWS == 0:
            spec = pl.BlockSpec((chunk, cols), lambda i: (i, 0))
        else:
            chunk *= 2
            spec = pl.BlockSpec((chunk, cols), lambda i: (i // 2, 0))
        assert rows % chunk == 0 and chunk % BF16_ROWS == 0, (a.shape, steps)
        specs.append(spec)
        shapes.append(jax.ShapeDtypeStruct(a.shape, BF16))
    return specs, shapes


def _with_side_casts(body, n_in, n_out, n_side):
    def kernel(*refs):
        ins, rest = refs[:n_in], refs[n_in:]
        srcs, rest = rest[:n_side], rest[n_side:]
        outs, rest = rest[:n_out], rest[n_out:]
        dsts, scratch = rest[:n_side], rest[n_side:]
        for src, dst in zip(srcs, dsts):
            dst[...] = src[...].astype(BF16)
        body(*ins, *outs, *scratch)
    return kernel


def _dft_tables():
    k = np.arange(HALF_SEQ, dtype=np.int64)
    scale = 1.0 / np.sqrt(SEQ)
    halves = []
    for p in (0, 1):
        n = 2 * np.arange(HALF_SEQ, dtype=np.int64) + p
        ang = 2.0 * np.pi * ((k[:, None] * n[None, :]) % SEQ).astype(np.float64) / SEQ
        halves.append(np.concatenate([np.cos(ang) * scale, -np.sin(ang) * scale], axis=1))
    pos = np.stack(halves)
    c = np.arange(GROUP_DIM, dtype=np.int64)
    angc = 2.0 * np.pi * ((c[:, None] * c[None, :]) % GROUP_DIM).astype(np.float64) / GROUP_DIM
    scalec = 1.0 / np.sqrt(GROUP_DIM)
    chan = np.concatenate([np.cos(angc) * scalec, np.sin(angc) * scalec], axis=1)
    return pos.astype(np.float32), chan.astype(np.float32)


_POS_DFT, _CHAN_DFT = _dft_tables()


def _front_kernel(x_ref, g_ref, winf_ref, ws_ref, bs_ref, lng_ref, lnb_ref,
                  za_ref, sg_ref, gate_ref, win_ref):
    _cast_once(winf_ref, win_ref)
    ws = [ws_ref[g].astype(BF16) for g in range(N_GROUPS)]
    h = _rms(x_ref[...], g_ref[...]).astype(BF16)
    za_ref[...] = _dot(h, win_ref[:, 0:MIX_WIDTH]).astype(BF16)
    u = _gelu(_dot(h, win_ref[:, MIX_WIDTH:2 * MIX_WIDTH]))
    v = _gelu(_dot(h, win_ref[:, 2 * MIX_WIDTH:3 * MIX_WIDTH]))
    mu = jnp.mean(v, axis=-1, keepdims=True)
    vc = v - mu
    var = jnp.mean(vc * vc, axis=-1, keepdims=True)
    vln = (vc * lax.rsqrt(var + LN_EPS) * lng_ref[...] + lnb_ref[...]).astype(BF16)
    for c in range(ROW_TILE // CHUNK):
        rows = slice(c * CHUNK, (c + 1) * CHUNK)
        for g in range(N_GROUPS):
            cols = slice(g * GROUP_DIM, (g + 1) * GROUP_DIM)
            sv = _dot(ws[g], vln[rows, cols]) + bs_ref[:, cols]
            sg_ref[rows, cols] = (u[rows, cols] * sv).astype(BF16)
    zg = _dot(h, win_ref[:, 3 * MIX_WIDTH:D_IN])
    gate_ref[...] = jax.nn.sigmoid(zg).astype(BF16)


def _mixer_front(l, x, g, w_in, w_s, b_full, ln_g, ln_b, side=()):
    row = lambda w: pl.BlockSpec((ROW_TILE, w), lambda i: (i, 0))
    steps = TOKENS // ROW_TILE
    side_specs, side_shapes = _side_specs(side, steps)
    outs = pl.pallas_call(
        _with_side_casts(_front_kernel, 7, 3, len(side)),
        grid=(steps,),
        in_specs=[row(D_MODEL), _layer((1, D_MODEL), l), _layer((D_MODEL, D_IN), l),
                  _layer((N_GROUPS, CHUNK, CHUNK), l), _layer((CHUNK, MIX_WIDTH), l),
                  _layer((1, MIX_WIDTH), l), _layer((1, MIX_WIDTH), l)] + side_specs,
        out_specs=[row(MIX_WIDTH), row(MIX_WIDTH), row(2 * D_MODEL)] + side_specs,
        out_shape=[jax.ShapeDtypeStruct((TOKENS, MIX_WIDTH), BF16),
                   jax.ShapeDtypeStruct((TOKENS, MIX_WIDTH), BF16),
                   jax.ShapeDtypeStruct((TOKENS, 2 * D_MODEL), BF16)] + side_shapes,
        scratch_shapes=[pltpu.VMEM((D_MODEL, D_IN), BF16)],
        compiler_params=_params("arbitrary"),
        name="mixer_front",
    )(x, g, w_in, w_s, b_full, ln_g, ln_b, *side)
    return outs[:3], outs[3:]


def _fourier_kernel(za_ref, chan_ref, pos_ref, out_ref, stage_ref, rhs_ref):
    @pl.when(pl.program_id(1) == 0)
    def _():
        for g in range(N_GROUPS):
            cols = slice(g * GROUP_DIM, (g + 1) * GROUP_DIM)
            xcs = _dot(za_ref[0, :, cols], chan_ref[...])
            stage_ref[g] = xcs[:, 0:GROUP_DIM]
            stage_ref[N_GROUPS + g] = xcs[:, GROUP_DIM:2 * GROUP_DIM]
        for g in range(N_GROUPS):
            cols = slice(g * GROUP_DIM, (g + 1) * GROUP_DIM)
            for p in (0, 1):
                rows = pl.ds(p, HALF_SEQ, stride=2)
                rhs_ref[p, 0:HALF_SEQ, cols] = stage_ref[g, rows, :].astype(BF16)
                rhs_ref[p, HALF_SEQ:SEQ, cols] = stage_ref[N_GROUPS + g, rows, :].astype(BF16)

    even = _dot(pos_ref[0], rhs_ref[0])
    odd = _dot(pos_ref[1], rhs_ref[1])
    out_ref[0, 0] = (even + odd).astype(BF16)
    out_ref[0, 1] = (even - odd).astype(BF16)


def _fourier(za, chan, pos):
    out = pl.pallas_call(
        _fourier_kernel,
        grid=(BATCH, HALF_SEQ // FOURIER_ROW_TILE),
        in_specs=[pl.BlockSpec((1, SEQ, MIX_WIDTH), lambda b, i: (b, 0, 0)),
                  _resident((GROUP_DIM, 2 * GROUP_DIM)),
                  pl.BlockSpec((2, FOURIER_ROW_TILE, SEQ), lambda b, i: (0, i, 0))],
        out_specs=pl.BlockSpec((1, 2, FOURIER_ROW_TILE, MIX_WIDTH), lambda b, i: (b, 0, i, 0)),
        out_shape=jax.ShapeDtypeStruct((BATCH, 2, HALF_SEQ, MIX_WIDTH), BF16),
        scratch_shapes=[pltpu.VMEM((2 * N_GROUPS, SEQ, GROUP_DIM), F32),
                        pltpu.VMEM((2, SEQ, MIX_WIDTH), BF16)],
        compiler_params=_params("arbitrary", "arbitrary"),
        name="fourier_mix",
    )(za, chan, pos)
    return out.reshape(TOKENS, MIX_WIDTH)


def _tail_common(fa_ref, sg_ref, gate_ref, x_ref, wff_ref, wgf_ref, wof_ref, pg_ref,
                 wf_ref, wg_ref, wo_ref):
    _cast_once(wff_ref, wf_ref)
    _cast_once(wgf_ref, wg_ref)
    _cast_once(wof_ref, wo_ref)
    ya = _dot(fa_ref[...], wf_ref[...])
    yb = _dot(sg_ref[...], wg_ref[...])
    m = gate_ref[:, 0:D_MODEL].astype(F32) * ya + gate_ref[:, D_MODEL:2 * D_MODEL].astype(F32) * yb
    y = _dot(m.astype(BF16), wo_ref[...])
    return x_ref[...] + _rms(y, pg_ref[...])


def _tail_kernel(fa_ref, sg_ref, gate_ref, x_ref, wff_ref, wgf_ref, wof_ref, pg_ref, x1_ref,
                 wf_ref, wg_ref, wo_ref):
    x1_ref[...] = _tail_common(fa_ref, sg_ref, gate_ref, x_ref, wff_ref, wgf_ref, wof_ref, pg_ref,
                               wf_ref, wg_ref, wo_ref)


def _tail_router_kernel(fa_ref, sg_ref, gate_ref, x_ref, wff_ref, wgf_ref, wof_ref, pg_ref,
                        fg_ref, rwt_ref, rb_ref, tri_ref,
                        x1_ref, h_ref, gw_ref, sel_ref, rank_ref, cnt_ref,
                        wf_ref, wg_ref, wo_ref, carry_ref):
    @pl.when(pl.program_id(0) == 0)
    def _():
        carry_ref[...] = jnp.zeros_like(carry_ref)

    x1 = _tail_common(fa_ref, sg_ref, gate_ref, x_ref, wff_ref, wgf_ref, wof_ref, pg_ref,
                      wf_ref, wg_ref, wo_ref)
    x1_ref[...] = x1
    h = _rms(x1, fg_ref[...])
    h_hi, h_lo = _split_bf16(h)
    h_ref[...] = h_hi

    w_hi, w_lo = _split_bf16(rwt_ref[...])
    logits = _dot_nt(w_hi, h_hi) + _dot_nt(w_hi, h_lo) + _dot_nt(w_lo, h_hi) + rb_ref[...]
    row = lax.broadcasted_iota(I32, logits.shape, 0)
    m1 = jnp.max(logits, axis=0, keepdims=True)
    i1 = jnp.min(jnp.where(logits == m1, row, N_EXPERTS), axis=0, keepdims=True)
    rest = jnp.where(row == i1, -jnp.inf, logits)
    m2 = jnp.max(rest, axis=0, keepdims=True)
    i2 = jnp.min(jnp.where(rest == m2, row, N_EXPERTS), axis=0, keepdims=True)
    e2 = jnp.exp(m2 - m1)
    den = 1.0 + e2
    gw_ref[...] = jnp.where(row == i1, 1.0 / den, 0.0) + jnp.where(row == i2, e2 / den, 0.0)
    sel = jnp.where((row == i1) | (row == i2), 1.0, 0.0)
    sel_ref[...] = sel.astype(I32)
    before = _dot(sel.astype(BF16), tri_ref[...]) + carry_ref[...]
    rank_ref[...] = before.astype(I32)
    carry_ref[...] = carry_ref[...] + jnp.sum(sel, axis=1, keepdims=True)
    cnt_ref[...] = carry_ref[...].astype(I32)


def _tail_specs(l):
    row = lambda w: pl.BlockSpec((ROW_TILE, w), lambda i: (i, 0))
    in_specs = [row(MIX_WIDTH), row(MIX_WIDTH), row(2 * D_MODEL), row(D_MODEL),
                _layer((MIX_WIDTH, D_MODEL), l), _layer((MIX_WIDTH, D_MODEL), l),
                _layer((D_MODEL, D_MODEL), l), _layer((1, D_MODEL), l)]
    weight_scratch = [pltpu.VMEM((MIX_WIDTH, D_MODEL), BF16), pltpu.VMEM((MIX_WIDTH, D_MODEL), BF16),
                      pltpu.VMEM((D_MODEL, D_MODEL), BF16)]
    return row, in_specs, weight_scratch


def _mixer_tail(l, fa, sg, gate, x, w_f, w_g, w_o, post_g, side=()):
    row, in_specs, weight_scratch = _tail_specs(l)
    steps = TOKENS // ROW_TILE
    side_specs, side_shapes = _side_specs(side, steps)
    outs = pl.pallas_call(
        _with_side_casts(_tail_kernel, 8, 1, len(side)),
        grid=(steps,),
        in_specs=in_specs + side_specs,
        out_specs=[row(D_MODEL)] + side_specs,
        out_shape=[jax.ShapeDtypeStruct((TOKENS, D_MODEL), F32)] + side_shapes,
        scratch_shapes=weight_scratch,
        compiler_params=_params("arbitrary"),
        name="mixer_tail",
    )(fa, sg, gate, x, w_f, w_g, w_o, post_g, *side)
    return outs[0], outs[1:]


def _mixer_tail_router(l, fa, sg, gate, x, w_f, w_g, w_o, post_g, ffn_g, router_wt, router_b, tri):
    row, in_specs, weight_scratch = _tail_specs(l)
    in_specs += [_layer((1, D_MODEL), l), _layer((N_EXPERTS, D_MODEL), l // 2),
                 _layer((N_EXPERTS, 1), l // 2), _resident((ROW_TILE, ROW_TILE))]
    col = pl.BlockSpec((N_EXPERTS, ROW_TILE), lambda i: (0, i))
    per_token = lambda dt: jax.ShapeDtypeStruct((N_EXPERTS, TOKENS), dt)
    return pl.pallas_call(
        _tail_router_kernel,
        grid=(TOKENS // ROW_TILE,),
        in_specs=in_specs,
        out_specs=[row(D_MODEL), row(D_MODEL), col, col, col,
                   pl.BlockSpec((N_EXPERTS, 1), lambda i: (0, 0))],
        out_shape=[jax.ShapeDtypeStruct((TOKENS, D_MODEL), F32),
                   jax.ShapeDtypeStruct((TOKENS, D_MODEL), BF16),
                   per_token(F32), per_token(I32), per_token(I32),
                   jax.ShapeDtypeStruct((N_EXPERTS, 1), I32)],
        scratch_shapes=weight_scratch + [pltpu.VMEM((N_EXPERTS, 1), F32)],
        compiler_params=_params("arbitrary"),
        name="mixer_tail_router",
    )(fa, sg, gate, x, w_f, w_g, w_o, post_g, ffn_g, router_wt, router_b, tri)


def _dense_ffn_kernel(x_ref, fg_ref, w1_ref, w3_ref, w2_ref, pg_ref, out_ref):
    x = x_ref[...]
    h = _rms(x, fg_ref[...]).astype(BF16)
    a = _dot(h, w1_ref[...])
    b = _dot(h, w3_ref[...])
    y = _dot((jax.nn.silu(a) * b).astype(BF16), w2_ref[...])
    out_ref[...] = x + _rms(y, pg_ref[...])


def _dense_ffn(l, x, ffn_g, w1, w3, w2, post_g, side=()):
    row = pl.BlockSpec((ROW_TILE, D_MODEL), lambda i: (i, 0))
    steps = TOKENS // ROW_TILE
    side_specs, side_shapes = _side_specs(side, steps)
    outs = pl.pallas_call(
        _with_side_casts(_dense_ffn_kernel, 6, 1, len(side)),
        grid=(steps,),
        in_specs=[row, _layer((1, D_MODEL), l), _resident((D_MODEL, D_FF_DENSE)),
                  _resident((D_MODEL, D_FF_DENSE)), _resident((D_FF_DENSE, D_MODEL)),
                  _layer((1, D_MODEL), l)] + side_specs,
        out_specs=[row] + side_specs,
        out_shape=[jax.ShapeDtypeStruct((TOKENS, D_MODEL), F32)] + side_shapes,
        compiler_params=_params("arbitrary"),
        name="dense_ffn",
    )(x, ffn_g, w1, w3, w2, post_g, *side)
    return outs[0], outs[1:]


def _window_hits(k, block, start_ref, dest_ref):
    wanted = start_ref[block] + k * TOKEN_WINDOW
    begin = pl.multiple_of(jnp.minimum(wanted, TOKENS - TOKEN_WINDOW), LANES)
    tok = lax.broadcasted_iota(I32, (1, TOKEN_WINDOW), 1) + begin
    dest = jnp.where(tok >= wanted, dest_ref[:, pl.ds(begin, TOKEN_WINDOW)], -1)
    rows = lax.broadcasted_iota(I32, (SORT_BLOCK, TOKEN_WINDOW), 0) + block * SORT_BLOCK
    return begin, rows == dest


def _dispatch_kernel(exp_ref, start_ref, nwin_ref, dest_ref, gw_ref, h_ref, xs_ref, gs_ref,
                     acc_ref):
    j = pl.program_id(0)
    acc_ref[...] = jnp.zeros_like(acc_ref)

    def window(k, gsum):
        begin, hit = _window_hits(k, j, start_ref, dest_ref)
        onehot = jnp.where(hit, 1.0, 0.0).astype(BF16)
        acc_ref[...] += _dot(onehot, h_ref[pl.ds(begin, TOKEN_WINDOW), :])
        gw = gw_ref[:, pl.ds(begin, TOKEN_WINDOW)]
        return gsum + jnp.sum(jnp.where(hit, gw, 0.0), axis=-1, keepdims=True)

    gs_ref[...] = lax.fori_loop(0, nwin_ref[j], window, jnp.zeros((SORT_BLOCK, 1), F32))
    xs_ref[...] = acc_ref[...].astype(BF16)


def _per_expert_row():
    return pl.BlockSpec((None, 1, TOKENS),
                        lambda j, e, s, n: (e[jnp.minimum(j, N_SORT_BLOCKS - 1)], 0, 0))


def _dispatch(plan, dest3, gw3, h):
    exp, start, nwin = plan
    grid_spec = pltpu.PrefetchScalarGridSpec(
        num_scalar_prefetch=3,
        grid=(N_SORT_BLOCKS,),
        in_specs=[_per_expert_row(), _per_expert_row(),
                  pl.BlockSpec((TOKENS, D_MODEL), lambda j, e, s, n: (0, 0),
                               pipeline_mode=pl.Buffered(1))],
        out_specs=[pl.BlockSpec((SORT_BLOCK, D_MODEL), lambda j, e, s, n: (j, 0)),
                   pl.BlockSpec((SORT_BLOCK, 1), lambda j, e, s, n: (j, 0))],
        scratch_shapes=[pltpu.VMEM((SORT_BLOCK, D_MODEL), F32)],
    )
    return pl.pallas_call(
        _dispatch_kernel,
        grid_spec=grid_spec,
        out_shape=[jax.ShapeDtypeStruct((SORTED_ROWS, D_MODEL), BF16),
                   jax.ShapeDtypeStruct((SORTED_ROWS, 1), F32)],
        compiler_params=_params("arbitrary"),
        name="moe_dispatch",
    )(exp, start, nwin, dest3, gw3, h)


def _expert_kernel(rexp_ref, nhalf_ref, xs_ref, gs_ref, w1_ref, w3_ref, w2_ref, ys_ref, acc_ref):
    q = pl.program_id(0)
    f = pl.program_id(1)
    last = pl.num_programs(1) - 1
    nhalf = nhalf_ref[q]

    @pl.when((q == 0) & (f == 0))
    def _():
        acc_ref[...] = jnp.zeros_like(acc_ref)

    def swiglu_part(n_blocks):
        for k in range(n_blocks):
            rows = slice(k * EXPERT_BLOCK, (k + 1) * EXPERT_BLOCK)
            x = xs_ref[rows, :]
            a = _dot(x, w1_ref[...])
            b = _dot(x, w3_ref[...])
            part = _dot((jax.nn.silu(a) * b).astype(BF16), w2_ref[...])
            acc_ref[rows, :] = jnp.where(f == 0, 0.0, acc_ref[rows, :]) + part

        @pl.when(f == last)
        def _():
            rows = slice(0, n_blocks * EXPERT_BLOCK)
            ys_ref[rows, :] = (acc_ref[rows, :] * gs_ref[rows, :]).astype(BF16)

    @pl.when(nhalf == 2)
    def _():
        swiglu_part(2)

    @pl.when(nhalf == 1)
    def _():
        swiglu_part(1)

    @pl.when((nhalf < 2) & (f == last))
    def _():
        ys_ref[EXPERT_BLOCK:EXPERT_REGION, :] = jnp.zeros((EXPERT_REGION - EXPERT_BLOCK, D_MODEL), BF16)

    @pl.when((nhalf == 0) & (f == last))
    def _():
        ys_ref[0:EXPERT_BLOCK, :] = jnp.zeros((EXPERT_BLOCK, D_MODEL), BF16)


def _experts(rexp, nhalf, xs, gs, w1, w3, w2):
    grid_spec = pltpu.PrefetchScalarGridSpec(
        num_scalar_prefetch=2,
        grid=(N_REGIONS, D_FF_EXPERT // EXPERT_FF_TILE),
        in_specs=[
            pl.BlockSpec((EXPERT_REGION, D_MODEL), lambda q, f, re, nh: (q, 0)),
            pl.BlockSpec((EXPERT_REGION, 1), lambda q, f, re, nh: (q, 0)),
            pl.BlockSpec((None, D_MODEL, EXPERT_FF_TILE), lambda q, f, re, nh: (re[q], 0, f)),
            pl.BlockSpec((None, D_MODEL, EXPERT_FF_TILE), lambda q, f, re, nh: (re[q], 0, f)),
            pl.BlockSpec((None, EXPERT_FF_TILE, D_MODEL), lambda q, f, re, nh: (re[q], f, 0)),
        ],
        out_specs=pl.BlockSpec((EXPERT_REGION, D_MODEL), lambda q, f, re, nh: (q, 0)),
        scratch_shapes=[pltpu.VMEM((EXPERT_REGION, D_MODEL), F32)],
    )
    return pl.pallas_call(
        _expert_kernel,
        grid_spec=grid_spec,
        out_shape=jax.ShapeDtypeStruct((SORTED_ROWS, D_MODEL), BF16),
        compiler_params=_params("arbitrary", "arbitrary"),
        name="moe_experts",
    )(rexp, nhalf, xs, gs, w1, w3, w2)


def _combine_copies(wstart_ref, ys_hbm, buf_ref, sem_ref, tile, rnd, slot):
    copies = []
    for e in range(N_EXPERTS):
        wanted = wstart_ref[tile * N_EXPERTS + e] + rnd * COMBINE_WINDOW
        begin = pl.multiple_of(jnp.minimum(wanted, SORTED_ROWS - COMBINE_WINDOW), BF16_ROWS)
        copies.append(pltpu.make_async_copy(
            ys_hbm.at[pl.ds(begin, COMBINE_WINDOW), :],
            buf_ref.at[slot, pl.ds(e * COMBINE_WINDOW, COMBINE_WINDOW), :],
            sem_ref.at[slot, e]))
    return copies


def _combine_kernel(wstart_ref, nround_ref, dest_ref, ys_hbm, x_ref, pg_ref, out_ref,
                    buf_ref, sem_ref):
    i = pl.program_id(0)
    n_tiles = pl.num_programs(0)
    slot = lax.rem(i, 2)

    @pl.when(i == 0)
    def _():
        for c in _combine_copies(wstart_ref, ys_hbm, buf_ref, sem_ref, 0, 0, 0):
            c.start()

    @pl.when(i + 1 < n_tiles)
    def _():
        for c in _combine_copies(wstart_ref, ys_hbm, buf_ref, sem_ref, i + 1, 0, 1 - slot):
            c.start()

    def gather(rnd, fetch_slot):
        sub = lax.broadcasted_iota(I32, (COMBINE_WINDOW, 1), 0)
        pieces = []
        for e in range(N_EXPERTS):
            wanted = wstart_ref[i * N_EXPERTS + e] + rnd * COMBINE_WINDOW
            begin = jnp.minimum(wanted, SORTED_ROWS - COMBINE_WINDOW)
            rowid = sub + begin
            rowid = jnp.where(rowid >= wanted, rowid, -2)
            pieces.append(jnp.where(rowid == dest_ref[e:e + 1, :], 1.0, 0.0).astype(BF16))
        onehot = jnp.concatenate(pieces, axis=0)
        return _dot_tn(onehot, buf_ref[fetch_slot])

    for c in _combine_copies(wstart_ref, ys_hbm, buf_ref, sem_ref, i, 0, slot):
        c.wait()
    y = gather(0, slot)

    def extra(rnd, y):
        copies = _combine_copies(wstart_ref, ys_hbm, buf_ref, sem_ref, i, rnd, 2)
        for c in copies:
            c.start()
        for c in copies:
            c.wait()
        return y + gather(rnd, 2)

    y = lax.fori_loop(1, nround_ref[i], extra, y)
    out_ref[...] = x_ref[...] + _rms(y, pg_ref[...])


def _combine(l, wstart, nround, dest, ys, x, post_g):
    tile = lambda i, ws, nr: (i, 0)
    grid_spec = pltpu.PrefetchScalarGridSpec(
        num_scalar_prefetch=2,
        grid=(TOKENS // COMBINE_TILE,),
        in_specs=[
            pl.BlockSpec((N_EXPERTS, COMBINE_TILE), lambda i, ws, nr: (0, i)),
            pl.BlockSpec(memory_space=pl.ANY),
            pl.BlockSpec((COMBINE_TILE, D_MODEL), tile),
            pl.BlockSpec((None, 1, D_MODEL), lambda i, ws, nr: (l, 0, 0)),
        ],
        out_specs=pl.BlockSpec((COMBINE_TILE, D_MODEL), tile),
        scratch_shapes=[pltpu.VMEM((3, N_EXPERTS * COMBINE_WINDOW, D_MODEL), BF16),
                        pltpu.SemaphoreType.DMA((3, N_EXPERTS))],
    )
    return pl.pallas_call(
        _combine_kernel,
        grid_spec=grid_spec,
        out_shape=jax.ShapeDtypeStruct((TOKENS, D_MODEL), F32),
        compiler_params=_params("arbitrary"),
        name="moe_combine",
    )(wstart, nround, dest, ys, x, post_g)


def _routing_plan(sel, rank, counts):
    counts = counts.reshape(N_EXPERTS)
    padded = ((counts + EXPERT_BLOCK - 1) // EXPERT_BLOCK) * EXPERT_BLOCK
    region = ((counts + EXPERT_REGION - 1) // EXPERT_REGION) * EXPERT_REGION
    offs = jnp.cumsum(region) - region
    dest = jnp.where(sel == 1, rank + offs[:, None], -1)

    cum_reg = jnp.cumsum(region // EXPERT_REGION)
    qs = jnp.arange(N_REGIONS, dtype=I32)
    rexp = jnp.sum((cum_reg[None, :] <= jnp.minimum(qs, cum_reg[-1] - 1)[:, None]).astype(I32),
                   axis=1)
    rexp = jnp.minimum(rexp, N_EXPERTS - 1).astype(I32)
    rows_left = offs[rexp] + padded[rexp] - qs * EXPERT_REGION
    n_half = jnp.clip(rows_left // EXPERT_BLOCK, 0, EXPERT_REGION // EXPERT_BLOCK).astype(I32)

    group_end = (rank + sel)[:, LANES - 1::LANES]
    js = jnp.arange(N_SORT_BLOCKS, dtype=I32)
    e_s = rexp[js // (EXPERT_REGION // SORT_BLOCK)]
    lo_row = js * SORT_BLOCK - offs[e_s]
    has_rows = lo_row < counts[e_s]
    hi_row = jnp.minimum(lo_row + SORT_BLOCK, counts[e_s])
    ends = group_end[e_s]
    g_lo = jnp.sum((ends <= lo_row[:, None]).astype(I32), axis=1)
    g_hi = jnp.sum((ends < hi_row[:, None]).astype(I32), axis=1)
    span = (g_hi - g_lo + 1) * LANES
    n_win = jnp.where(has_rows, (span + TOKEN_WINDOW - 1) // TOKEN_WINDOW, 0).astype(I32)
    start = jnp.where(has_rows, g_lo * LANES, 0).astype(I32)
    d_plan = (e_s.astype(I32), start, n_win)

    tile_lo = rank[:, ::COMBINE_TILE]
    tile_hi = jnp.concatenate([tile_lo[:, 1:], counts[:, None]], axis=1)
    lo = tile_lo + offs[:, None]
    hi = tile_hi + offs[:, None]
    wstart = jnp.minimum((lo // BF16_ROWS) * BF16_ROWS, SORTED_ROWS - COMBINE_WINDOW)
    rounds = jnp.where(hi > lo, (hi - wstart + COMBINE_WINDOW - 1) // COMBINE_WINDOW, 0)
    n_round = jnp.maximum(jnp.max(rounds, axis=0), 1).astype(I32)
    c_plan = (wstart.T.reshape(-1).astype(I32), n_round)
    return dest, (rexp, n_half), d_plan, c_plan


def _moe(l, x1, h, gw, sel, rank, counts, w1, w3, w2, post_g):
    dest, e_plan, d_plan, c_plan = _routing_plan(sel, rank, counts)
    dest3 = dest.reshape(N_EXPERTS, 1, TOKENS)
    gw3 = gw.reshape(N_EXPERTS, 1, TOKENS)
    xs, gs = _dispatch(d_plan, dest3, gw3, h)
    ys = _experts(*e_plan, xs, gs, w1, w3, w2)
    return _combine(l, *c_plan, dest, ys, x1, post_g)


def kernel(x, pre_mix_g, post_mix_g, w_in, w_fourier, w_gmlp, w_out, w_spatial, b_spatial,
           gmlp_ln_g, gmlp_ln_b, pre_ffn_g, post_ffn_g, ffn_w1, ffn_w3, ffn_w2,
           router_w, router_b, moe_w1, moe_w3, moe_w2):
    pos = jnp.asarray(_POS_DFT).astype(BF16)
    chan = jnp.asarray(_CHAN_DFT).astype(BF16)
    tri = jnp.asarray(np.triu(np.ones((ROW_TILE, ROW_TILE), np.float32), 1)).astype(BF16)
    rows = lambda v: v.reshape(v.shape[0], 1, v.shape[1])
    pre_mix_g, post_mix_g, pre_ffn_g, post_ffn_g, gmlp_ln_g, gmlp_ln_b = map(
        rows, (pre_mix_g, post_mix_g, pre_ffn_g, post_ffn_g, gmlp_ln_g, gmlp_ln_b))
    b_full = jnp.repeat(jnp.swapaxes(b_spatial, 1, 2), GROUP_DIM, axis=2)
    router_wt = jnp.swapaxes(router_w, 1, 2)
    router_b = router_b.reshape(router_b.shape[0], N_EXPERTS, 1)

    assert DEPTH == 2
    moe_rows = lambda w: w[0].reshape(-1, w.shape[-1])
    moe_back = lambda w, like: w.reshape(like.shape[1:])

    xf = x.reshape(TOKENS, D_MODEL)
    (za, sg, gate), (moe1,) = _mixer_front(0, xf, pre_mix_g, w_in, w_spatial, b_full, gmlp_ln_g,
                                           gmlp_ln_b, side=(moe_rows(moe_w1),))
    fa = _fourier(za.reshape(BATCH, SEQ, MIX_WIDTH), chan, pos)
    x1, (moe3, ffn1, ffn3, ffn2) = _mixer_tail(
        0, fa, sg, gate, xf, w_fourier, w_gmlp, w_out, post_mix_g,
        side=(moe_rows(moe_w3), ffn_w1[0], ffn_w3[0], ffn_w2[0]))
    xf, (moe2,) = _dense_ffn(0, x1, pre_ffn_g, ffn1, ffn3, ffn2, post_ffn_g,
                             side=(moe_rows(moe_w2),))

    (za, sg, gate), _ = _mixer_front(1, xf, pre_mix_g, w_in, w_spatial, b_full, gmlp_ln_g,
                                     gmlp_ln_b)
    fa = _fourier(za.reshape(BATCH, SEQ, MIX_WIDTH), chan, pos)
    x1, h, gw, sel, rank, counts = _mixer_tail_router(
        1, fa, sg, gate, xf, w_fourier, w_gmlp, w_out, post_mix_g, pre_ffn_g, router_wt, router_b,
        tri)
    xf = _moe(1, x1, h, gw, sel, rank, counts, moe_back(moe1, moe_w1), moe_back(moe3, moe_w3),
              moe_back(moe2, moe_w2), post_ffn_g)
    return xf.reshape(BATCH, SEQ, D_MODEL)
```

```python
import numpy as np
import jax
import jax.numpy as jnp
from jax import lax
from jax.experimental import pallas as pl
from jax.experimental.pallas import tpu as pltpu

F32 = jnp.float32
BF16 = jnp.bfloat16
I32 = jnp.int32

D_MODEL = 1024
BATCH = 8
SEQ = 2048
HALF_SEQ = SEQ // 2
TOKENS = BATCH * SEQ
DEPTH = 2
N_GROUPS = 4
GROUP_DIM = 128
MIX_WIDTH = N_GROUPS * GROUP_DIM
CHUNK = 128
D_IN = 3 * MIX_WIDTH + 2 * D_MODEL
D_FF_DENSE = 2816
N_EXPERTS = 8
D_FF_EXPERT = 3584
RMS_EPS = 1e-6
LN_EPS = 1e-5
LANES = 128

ROW_TILE = 512
FOURIER_ROW_TILE = 512
EXPERT_BLOCK = 512
EXPERT_REGION = 1024
SORT_BLOCK = 256
TOKEN_WINDOW = 1280
COMBINE_TILE = 256
COMBINE_WINDOW = 128
BF16_ROWS = 16
EXPERT_FF_TILE = 1792

SORTED_ROWS = 2 * TOKENS + N_EXPERTS * EXPERT_REGION
N_REGIONS = SORTED_ROWS // EXPERT_REGION
N_SORT_BLOCKS = SORTED_ROWS // SORT_BLOCK
N_COMBINE_TILES = TOKENS // COMBINE_TILE

VMEM_LIMIT = 56 * 1024 * 1024


def _params(*sem):
    return pltpu.CompilerParams(dimension_semantics=sem, vmem_limit_bytes=VMEM_LIMIT)


def _resident(shape):
    nd = len(shape)
    return pl.BlockSpec(shape, lambda *_: (0,) * nd, pipeline_mode=pl.Buffered(1))


def _layer(shape, l):
    nd = len(shape)
    return pl.BlockSpec((None,) + tuple(shape), lambda *_: (l,) + (0,) * nd,
                        pipeline_mode=pl.Buffered(1))


def _rms(x, g):
    return x * lax.rsqrt(jnp.mean(x * x, axis=-1, keepdims=True) + RMS_EPS) * g


def _dot(a, b):
    return jnp.dot(a, b, preferred_element_type=F32)


def _dot_nt(a, b):
    return lax.dot_general(a, b, (((1,), (1,)), ((), ())), preferred_element_type=F32)


def _dot_tn(a, b):
    return lax.dot_general(a, b, (((0,), (0,)), ((), ())), preferred_element_type=F32)


def _gelu(x):
    return 0.5 * x * (1.0 + lax.erf(x * np.float32(np.sqrt(0.5))))


def _cast_once(src_ref, dst_ref):
    chunk = 128

    @pl.when(pl.program_id(0) == 0)
    def _():
        def body(i, carry):
            rows = pl.ds(pl.multiple_of(i * chunk, chunk), chunk)
            dst_ref[rows, :] = src_ref[rows, :].astype(BF16)
            return carry
        lax.fori_loop(0, src_ref.shape[0] // chunk, body, 0)


def _split_bf16(x):
    hi = x.astype(BF16)
    return hi, (x - hi.astype(F32)).astype(BF16)


def _side_specs(arrays, steps):
    specs, shapes = [], []
    for a in arrays:
        rows, cols = a.shape
        chunk = rows // steps
        if chunk % BF16_ROWS == 0:
            spec = pl.BlockSpec((chunk, cols), lambda i: (i, 0))
        else:
            chunk *= 2
            spec = pl.BlockSpec((chunk, cols), lambda i: (i // 2, 0))
        assert rows % chunk == 0 and chunk % BF16_ROWS == 0, (a.shape, steps)
        specs.append(spec)
        shapes.append(jax.ShapeDtypeStruct(a.shape, BF16))
    return specs, shapes


def _with_side_casts(body, n_in, n_out, n_side):
    def kernel(*refs):
        ins, rest = refs[:n_in], refs[n_in:]
        srcs, rest = rest[:n_side], rest[n_side:]
        outs, rest = rest[:n_out], rest[n_out:]
        dsts, scratch = rest[:n_side], rest[n_side:]
        for src, dst in zip(srcs, dsts):
            dst[...] = src[...].astype(BF16)
        body(*ins, *outs, *scratch)
    return kernel


def _dft_tables():
    k = np.arange(HALF_SEQ, dtype=np.int64)
    scale = 1.0 / np.sqrt(SEQ)
    halves = []
    for p in (0, 1):
        n = 2 * np.arange(HALF_SEQ, dtype=np.int64) + p
        ang = 2.0 * np.pi * ((k[:, None] * n[None, :]) % SEQ).astype(np.float64) / SEQ
        halves.append(np.concatenate([np.cos(ang) * scale, -np.sin(ang) * scale], axis=1))
    pos = np.stack(halves)
    c = np.arange(GROUP_DIM, dtype=np.int64)
    angc = 2.0 * np.pi * ((c[:, None] * c[None, :]) % GROUP_DIM).astype(np.float64) / GROUP_DIM
    scalec = 1.0 / np.sqrt(GROUP_DIM)
    chan = np.concatenate([np.cos(angc) * scalec, np.sin(angc) * scalec], axis=1)
    return pos.astype(np.float32), chan.astype(np.float32)


_POS_DFT, _CHAN_DFT = _dft_tables()


def _front_kernel(x_ref, g_ref, winf_ref, ws_ref, bs_ref, lng_ref, lnb_ref,
                  za_ref, sg_ref, gate_ref, win_ref):
    _cast_once(winf_ref, win_ref)
    ws = [ws_ref[g].astype(BF16) for g in range(N_GROUPS)]
    h = _rms(x_ref[...], g_ref[...]).astype(BF16)
    za_ref[...] = _dot(h, win_ref[:, 0:MIX_WIDTH]).astype(BF16)
    u = _gelu(_dot(h, win_ref[:, MIX_WIDTH:2 * MIX_WIDTH]))
    v = _gelu(_dot(h, win_ref[:, 2 * MIX_WIDTH:3 * MIX_WIDTH]))
    mu = jnp.mean(v, axis=-1, keepdims=True)
    vc = v - mu
    var = jnp.mean(vc * vc, axis=-1, keepdims=True)
    vln = (vc * lax.rsqrt(var + LN_EPS) * lng_ref[...] + lnb_ref[...]).astype(BF16)
    for c in range(ROW_TILE // CHUNK):
        rows = slice(c * CHUNK, (c + 1) * CHUNK)
        for g in range(N_GROUPS):
            cols = slice(g * GROUP_DIM, (g + 1) * GROUP_DIM)
            sv = _dot(ws[g], vln[rows, cols]) + bs_ref[:, cols]
            sg_ref[rows, cols] = (u[rows, cols] * sv).astype(BF16)
    zg = _dot(h, win_ref[:, 3 * MIX_WIDTH:D_IN])
    gate_ref[...] = jax.nn.sigmoid(zg).astype(BF16)


def _mixer_front(l, x, g, w_in, w_s, b_full, ln_g, ln_b, side=()):
    row = lambda w: pl.BlockSpec((ROW_TILE, w), lambda i: (i, 0))
    steps = TOKENS // ROW_TILE
    side_specs, side_shapes = _side_specs(side, steps)
    outs = pl.pallas_call(
        _with_side_casts(_front_kernel, 7, 3, len(side)),
        grid=(steps,),
        in_specs=[row(D_MODEL), _layer((1, D_MODEL), l), _layer((D_MODEL, D_IN), l),
                  _layer((N_GROUPS, CHUNK, CHUNK), l), _layer((CHUNK, MIX_WIDTH), l),
                  _layer((1, MIX_WIDTH), l), _layer((1, MIX_WIDTH), l)] + side_specs,
        out_specs=[row(MIX_WIDTH), row(MIX_WIDTH), row(2 * D_MODEL)] + side_specs,
        out_shape=[jax.ShapeDtypeStruct((TOKENS, MIX_WIDTH), BF16),
                   jax.ShapeDtypeStruct((TOKENS, MIX_WIDTH), BF16),
                   jax.ShapeDtypeStruct((TOKENS, 2 * D_MODEL), BF16)] + side_shapes,
        scratch_shapes=[pltpu.VMEM((D_MODEL, D_IN), BF16)],
        compiler_params=_params("arbitrary"),
        name="mixer_front",
    )(x, g, w_in, w_s, b_full, ln_g, ln_b, *side)
    return outs[:3], outs[3:]


def _fourier_kernel(za_ref, chan_ref, pos_ref, out_ref, stage_ref, rhs_ref):
    @pl.when(pl.program_id(1) == 0)
    def _():
        for g in range(N_GROUPS):
            cols = slice(g * GROUP_DIM, (g + 1) * GROUP_DIM)
            xcs = _dot(za_ref[0, :, cols], chan_ref[...])
            stage_ref[g] = xcs[:, 0:GROUP_DIM]
            stage_ref[N_GROUPS + g] = xcs[:, GROUP_DIM:2 * GROUP_DIM]
        for g in range(N_GROUPS):
            cols = slice(g * GROUP_DIM, (g + 1) * GROUP_DIM)
            for p in (0, 1):
                rows = pl.ds(p, HALF_SEQ, stride=2)
                rhs_ref[p, 0:HALF_SEQ, cols] = stage_ref[g, rows, :].astype(BF16)
                rhs_ref[p, HALF_SEQ:SEQ, cols] = stage_ref[N_GROUPS + g, rows, :].astype(BF16)

    even = _dot(pos_ref[0], rhs_ref[0])
    odd = _dot(pos_ref[1], rhs_ref[1])
    out_ref[0, 0] = (even + odd).astype(BF16)
    out_ref[0, 1] = (even - odd).astype(BF16)


def _fourier(za, chan, pos):
    out = pl.pallas_call(
        _fourier_kernel,
        grid=(BATCH, HALF_SEQ // FOURIER_ROW_TILE),
        in_specs=[pl.BlockSpec((1, SEQ, MIX_WIDTH), lambda b, i: (b, 0, 0)),
                  _resident((GROUP_DIM, 2 * GROUP_DIM)),
                  pl.BlockSpec((2, FOURIER_ROW_TILE, SEQ), lambda b, i: (0, i, 0))],
        out_specs=pl.BlockSpec((1, 2, FOURIER_ROW_TILE, MIX_WIDTH), lambda b, i: (b, 0, i, 0)),
        out_shape=jax.ShapeDtypeStruct((BATCH, 2, HALF_SEQ, MIX_WIDTH), BF16),
        scratch_shapes=[pltpu.VMEM((2 * N_GROUPS, SEQ, GROUP_DIM), F32),
                        pltpu.VMEM((2, SEQ, MIX_WIDTH), BF16)],
        compiler_params=_params("arbitrary", "arbitrary"),
        name="fourier_mix",
    )(za, chan, pos)
    return out.reshape(TOKENS, MIX_WIDTH)


def _tail_common(fa_ref, sg_ref, gate_ref, x_ref, wff_ref, wgf_ref, wof_ref, pg_ref,
                 wf_ref, wg_ref, wo_ref):
    _cast_once(wff_ref, wf_ref)
    _cast_once(wgf_ref, wg_ref)
    _cast_once(wof_ref, wo_ref)
    ya = _dot(fa_ref[...], wf_ref[...])
    yb = _dot(sg_ref[...], wg_ref[...])
    m = gate_ref[:, 0:D_MODEL].astype(F32) * ya + gate_ref[:, D_MODEL:2 * D_MODEL].astype(F32) * yb
    y = _dot(m.astype(BF16), wo_ref[...])
    return x_ref[...] + _rms(y, pg_ref[...])


def _tail_kernel(fa_ref, sg_ref, gate_ref, x_ref, wff_ref, wgf_ref, wof_ref, pg_ref, x1_ref,
                 wf_ref, wg_ref, wo_ref):
    x1_ref[...] = _tail_common(fa_ref, sg_ref, gate_ref, x_ref, wff_ref, wgf_ref, wof_ref, pg_ref,
                               wf_ref, wg_ref, wo_ref)


def _tail_router_kernel(fa_ref, sg_ref, gate_ref, x_ref, wff_ref, wgf_ref, wof_ref, pg_ref,
                        fg_ref, rwt_ref, rb_ref, tri_ref,
                        x1_ref, h_ref, gw_ref, sel_ref, rank_ref, cnt_ref,
                        wf_ref, wg_ref, wo_ref, carry_ref):
    @pl.when(pl.program_id(0) == 0)
    def _():
        carry_ref[...] = jnp.zeros_like(carry_ref)

    x1 = _tail_common(fa_ref, sg_ref, gate_ref, x_ref, wff_ref, wgf_ref, wof_ref, pg_ref,
                      wf_ref, wg_ref, wo_ref)
    x1_ref[...] = x1
    h = _rms(x1, fg_ref[...])
    h_hi, h_lo = _split_bf16(h)
    h_ref[...] = h_hi

    w_hi, w_lo = _split_bf16(rwt_ref[...])
    logits = _dot_nt(w_hi, h_hi) + _dot_nt(w_hi, h_lo) + _dot_nt(w_lo, h_hi) + rb_ref[...]
    row = lax.broadcasted_iota(I32, logits.shape, 0)
    m1 = jnp.max(logits, axis=0, keepdims=True)
    i1 = jnp.min(jnp.where(logits == m1, row, N_EXPERTS), axis=0, keepdims=True)
    rest = jnp.where(row == i1, -jnp.inf, logits)
    m2 = jnp.max(rest, axis=0, keepdims=True)
    i2 = jnp.min(jnp.where(rest == m2, row, N_EXPERTS), axis=0, keepdims=True)
    e2 = jnp.exp(m2 - m1)
    den = 1.0 + e2
    gw_ref[...] = jnp.where(row == i1, 1.0 / den, 0.0) + jnp.where(row == i2, e2 / den, 0.0)
    sel = jnp.where((row == i1) | (row == i2), 1.0, 0.0)
    sel_ref[...] = sel.astype(I32)
    before = _dot(sel.astype(BF16), tri_ref[...]) + carry_ref[...]
    rank_ref[...] = before.astype(I32)
    carry_ref[...] = carry_ref[...] + jnp.sum(sel, axis=1, keepdims=True)
    cnt_ref[...] = carry_ref[...].astype(I32)


def _tail_specs(l):
    row = lambda w: pl.BlockSpec((ROW_TILE, w), lambda i: (i, 0))
    in_specs = [row(MIX_WIDTH), row(MIX_WIDTH), row(2 * D_MODEL), row(D_MODEL),
                _layer((MIX_WIDTH, D_MODEL), l), _layer((MIX_WIDTH, D_MODEL), l),
                _layer((D_MODEL, D_MODEL), l), _layer((1, D_MODEL), l)]
    weight_scratch = [pltpu.VMEM((MIX_WIDTH, D_MODEL), BF16), pltpu.VMEM((MIX_WIDTH, D_MODEL), BF16),
                      pltpu.VMEM((D_MODEL, D_MODEL), BF16)]
    return row, in_specs, weight_scratch


def _mixer_tail(l, fa, sg, gate, x, w_f, w_g, w_o, post_g, side=()):
    row, in_specs, weight_scratch = _tail_specs(l)
    steps = TOKENS // ROW_TILE
    side_specs, side_shapes = _side_specs(side, steps)
    outs = pl.pallas_call(
        _with_side_casts(_tail_kernel, 8, 1, len(side)),
        grid=(steps,),
        in_specs=in_specs + side_specs,
        out_specs=[row(D_MODEL)] + side_specs,
        out_shape=[jax.ShapeDtypeStruct((TOKENS, D_MODEL), F32)] + side_shapes,
        scratch_shapes=weight_scratch,
        compiler_params=_params("arbitrary"),
        name="mixer_tail",
    )(fa, sg, gate, x, w_f, w_g, w_o, post_g, *side)
    return outs[0], outs[1:]


def _mixer_tail_router(l, fa, sg, gate, x, w_f, w_g, w_o, post_g, ffn_g, router_wt, router_b, tri):
    row, in_specs, weight_scratch = _tail_specs(l)
    in_specs += [_layer((1, D_MODEL), l), _layer((N_EXPERTS, D_MODEL), l // 2),
                 _layer((N_EXPERTS, 1), l // 2), _resident((ROW_TILE, ROW_TILE))]
    col = pl.BlockSpec((N_EXPERTS, ROW_TILE), lambda i: (0, i))
    per_token = lambda dt: jax.ShapeDtypeStruct((N_EXPERTS, TOKENS), dt)
    return pl.pallas_call(
        _tail_router_kernel,
        grid=(TOKENS // ROW_TILE,),
        in_specs=in_specs,
        out_specs=[row(D_MODEL), row(D_MODEL), col, col, col,
                   pl.BlockSpec((N_EXPERTS, 1), lambda i: (0, 0))],
        out_shape=[jax.ShapeDtypeStruct((TOKENS, D_MODEL), F32),
                   jax.ShapeDtypeStruct((TOKENS, D_MODEL), BF16),
                   per_token(F32), per_token(I32), per_token(I32),
                   jax.ShapeDtypeStruct((N_EXPERTS, 1), I32)],
        scratch_shapes=weight_scratch + [pltpu.VMEM((N_EXPERTS, 1), F32)],
        compiler_params=_params("arbitrary"),
        name="mixer_tail_router",
    )(fa, sg, gate, x, w_f, w_g, w_o, post_g, ffn_g, router_wt, router_b, tri)


def _dense_ffn_kernel(x_ref, fg_ref, w1_ref, w3_ref, w2_ref, pg_ref, out_ref):
    x = x_ref[...]
    h = _rms(x, fg_ref[...]).astype(BF16)
    a = _dot(h, w1_ref[...])
    b = _dot(h, w3_ref[...])
    y = _dot((jax.nn.silu(a) * b).astype(BF16), w2_ref[...])
    out_ref[...] = x + _rms(y, pg_ref[...])


def _dense_ffn(l, x, ffn_g, w1, w3, w2, post_g, side=()):
    row = pl.BlockSpec((ROW_TILE, D_MODEL), lambda i: (i, 0))
    steps = TOKENS // ROW_TILE
    side_specs, side_shapes = _side_specs(side, steps)
    outs = pl.pallas_call(
        _with_side_casts(_dense_ffn_kernel, 6, 1, len(side)),
        grid=(steps,),
        in_specs=[row, _layer((1, D_MODEL), l), _resident((D_MODEL, D_FF_DENSE)),
                  _resident((D_MODEL, D_FF_DENSE)), _resident((D_FF_DENSE, D_MODEL)),
                  _layer((1, D_MODEL), l)] + side_specs,
        out_specs=[row] + side_specs,
        out_shape=[jax.ShapeDtypeStruct((TOKENS, D_MODEL), F32)] + side_shapes,
        compiler_params=_params("arbitrary"),
        name="dense_ffn",
    )(x, ffn_g, w1, w3, w2, post_g, *side)
    return outs[0], outs[1:]


def _window_hits(k, block, start_ref, dest_ref):
    wanted = start_ref[block] + k * TOKEN_WINDOW
    begin = pl.multiple_of(jnp.minimum(wanted, TOKENS - TOKEN_WINDOW), LANES)
    tok = lax.broadcasted_iota(I32, (1, TOKEN_WINDOW), 1) + begin
    dest = jnp.where(tok >= wanted, dest_ref[:, pl.ds(begin, TOKEN_WINDOW)], -1)
    rows = lax.broadcasted_iota(I32, (SORT_BLOCK, TOKEN_WINDOW), 0) + block * SORT_BLOCK
    return begin, rows == dest


def _dispatch_kernel(exp_ref, start_ref, nwin_ref, dest_ref, gw_ref, h_ref, xs_ref, gs_ref):
    j = pl.program_id(0)

    def window(k):
        begin, hit = _window_hits(k, j, start_ref, dest_ref)
        onehot = jnp.where(hit, 1.0, 0.0).astype(BF16)
        picked = _dot(onehot, h_ref[pl.ds(begin, TOKEN_WINDOW), :]).astype(BF16)
        gw = gw_ref[:, pl.ds(begin, TOKEN_WINDOW)]
        return picked, jnp.sum(jnp.where(hit, gw, 0.0), axis=-1, keepdims=True)

    @pl.when(nwin_ref[j] == 0)
    def _():
        xs_ref[...] = jnp.zeros_like(xs_ref)
        gs_ref[...] = jnp.zeros_like(gs_ref)

    @pl.when(nwin_ref[j] > 0)
    def _():
        xs_ref[...], gs_ref[...] = window(0)

        def more(k, carry):
            picked, gsum = window(k)
            xs_ref[...] += picked
            gs_ref[...] += gsum
            return carry
        lax.fori_loop(1, nwin_ref[j], more, 0)


def _per_expert_row():
    return pl.BlockSpec((None, 1, TOKENS),
                        lambda j, e, s, n: (e[jnp.minimum(j, N_SORT_BLOCKS - 1)], 0, 0))


def _dispatch(plan, dest3, gw3, h):
    exp, start, nwin = plan
    grid_spec = pltpu.PrefetchScalarGridSpec(
        num_scalar_prefetch=3,
        grid=(N_SORT_BLOCKS,),
        in_specs=[_per_expert_row(), _per_expert_row(),
                  pl.BlockSpec((TOKENS, D_MODEL), lambda j, e, s, n: (0, 0),
                               pipeline_mode=pl.Buffered(1))],
        out_specs=[pl.BlockSpec((SORT_BLOCK, D_MODEL), lambda j, e, s, n: (j, 0)),
                   pl.BlockSpec((SORT_BLOCK, 1), lambda j, e, s, n: (j, 0))],
    )
    return pl.pallas_call(
        _dispatch_kernel,
        grid_spec=grid_spec,
        out_shape=[jax.ShapeDtypeStruct((SORTED_ROWS, D_MODEL), BF16),
                   jax.ShapeDtypeStruct((SORTED_ROWS, 1), F32)],
        compiler_params=_params("arbitrary"),
        name="moe_dispatch",
    )(exp, start, nwin, dest3, gw3, h)


def _expert_kernel(rexp_ref, nhalf_ref, xs_ref, gs_ref, w1_ref, w3_ref, w2_ref, ys_ref, acc_ref):
    q = pl.program_id(0)
    f = pl.program_id(1)
    last = pl.num_programs(1) - 1
    nhalf = nhalf_ref[q]

    @pl.when((q == 0) & (f == 0))
    def _():
        acc_ref[...] = jnp.zeros_like(acc_ref)

    def swiglu_part(n_blocks):
        for k in range(n_blocks):
            rows = slice(k * EXPERT_BLOCK, (k + 1) * EXPERT_BLOCK)
            x = xs_ref[rows, :]
            a = _dot(x, w1_ref[...])
            b = _dot(x, w3_ref[...])
            part = _dot((jax.nn.silu(a) * b).astype(BF16), w2_ref[...])
            acc_ref[rows, :] = jnp.where(f == 0, 0.0, acc_ref[rows, :]) + part

        @pl.when(f == last)
        def _():
            rows = slice(0, n_blocks * EXPERT_BLOCK)
            ys_ref[rows, :] = (acc_ref[rows, :] * gs_ref[rows, :]).astype(BF16)

    @pl.when(nhalf == 2)
    def _():
        swiglu_part(2)

    @pl.when(nhalf == 1)
    def _():
        swiglu_part(1)

    @pl.when((nhalf < 2) & (f == last))
    def _():
        ys_ref[EXPERT_BLOCK:EXPERT_REGION, :] = jnp.zeros((EXPERT_REGION - EXPERT_BLOCK, D_MODEL), BF16)

    @pl.when((nhalf == 0) & (f == last))
    def _():
        ys_ref[0:EXPERT_BLOCK, :] = jnp.zeros((EXPERT_BLOCK, D_MODEL), BF16)


def _experts(rexp, nhalf, xs, gs, w1, w3, w2):
    grid_spec = pltpu.PrefetchScalarGridSpec(
        num_scalar_prefetch=2,
        grid=(N_REGIONS, D_FF_EXPERT // EXPERT_FF_TILE),
        in_specs=[
            pl.BlockSpec((EXPERT_REGION, D_MODEL), lambda q, f, re, nh: (q, 0)),
            pl.BlockSpec((EXPERT_REGION, 1), lambda q, f, re, nh: (q, 0)),
            pl.BlockSpec((None, D_MODEL, EXPERT_FF_TILE), lambda q, f, re, nh: (re[q], 0, f)),
            pl.BlockSpec((None, D_MODEL, EXPERT_FF_TILE), lambda q, f, re, nh: (re[q], 0, f)),
            pl.BlockSpec((None, EXPERT_FF_TILE, D_MODEL), lambda q, f, re, nh: (re[q], f, 0)),
        ],
        out_specs=pl.BlockSpec((EXPERT_REGION, D_MODEL), lambda q, f, re, nh: (q, 0)),
        scratch_shapes=[pltpu.VMEM((EXPERT_REGION, D_MODEL), F32)],
    )
    return pl.pallas_call(
        _expert_kernel,
        grid_spec=grid_spec,
        out_shape=jax.ShapeDtypeStruct((SORTED_ROWS, D_MODEL), BF16),
        compiler_params=_params("arbitrary", "arbitrary"),
        name="moe_experts",
    )(rexp, nhalf, xs, gs, w1, w3, w2)


def _combine_copies(wstart_ref, ys_hbm, buf_ref, sem_ref, tile, rnd, slot):
    copies = []
    for e in range(N_EXPERTS):
        wanted = wstart_ref[tile * N_EXPERTS + e] + rnd * COMBINE_WINDOW
        begin = pl.multiple_of(jnp.minimum(wanted, SORTED_ROWS - COMBINE_WINDOW), BF16_ROWS)
        copies.append(pltpu.make_async_copy(
            ys_hbm.at[pl.ds(begin, COMBINE_WINDOW), :],
            buf_ref.at[slot, pl.ds(e * COMBINE_WINDOW, COMBINE_WINDOW), :],
            sem_ref.at[slot, e]))
    return copies


def _combine_kernel(wstart_ref, nround_ref, dest_ref, ys_hbm, x_ref, pg_ref, out_ref,
                    buf_ref, sem_ref):
    i = pl.program_id(0)
    n_tiles = pl.num_programs(0)
    slot = lax.rem(i, 2)

    @pl.when(i == 0)
    def _():
        for c in _combine_copies(wstart_ref, ys_hbm, buf_ref, sem_ref, 0, 0, 0):
            c.start()

    @pl.when(i + 1 < n_tiles)
    def _():
        for c in _combine_copies(wstart_ref, ys_hbm, buf_ref, sem_ref, i + 1, 0, 1 - slot):
            c.start()

    def gather(rnd, fetch_slot):
        sub = lax.broadcasted_iota(I32, (COMBINE_WINDOW, 1), 0)
        pieces = []
        for e in range(N_EXPERTS):
            wanted = wstart_ref[i * N_EXPERTS + e] + rnd * COMBINE_WINDOW
            begin = jnp.minimum(wanted, SORTED_ROWS - COMBINE_WINDOW)
            rowid = sub + begin
            rowid = jnp.where(rowid >= wanted, rowid, -2)
            pieces.append(jnp.where(rowid == dest_ref[e:e + 1, :], 1.0, 0.0).astype(BF16))
        onehot = jnp.concatenate(pieces, axis=0)
        return _dot_tn(onehot, buf_ref[fetch_slot])

    for c in _combine_copies(wstart_ref, ys_hbm, buf_ref, sem_ref, i, 0, slot):
        c.wait()
    y = gather(0, slot)

    def extra(rnd, y):
        copies = _combine_copies(wstart_ref, ys_hbm, buf_ref, sem_ref, i, rnd, 2)
        for c in copies:
            c.start()
        for c in copies:
            c.wait()
        return y + gather(rnd, 2)

    y = lax.fori_loop(1, nround_ref[i], extra, y)
    out_ref[...] = x_ref[...] + _rms(y, pg_ref[...])


def _combine(l, wstart, nround, dest, ys, x, post_g):
    tile = lambda i, ws, nr: (i, 0)
    grid_spec = pltpu.PrefetchScalarGridSpec(
        num_scalar_prefetch=2,
        grid=(TOKENS // COMBINE_TILE,),
        in_specs=[
            pl.BlockSpec((N_EXPERTS, COMBINE_TILE), lambda i, ws, nr: (0, i)),
            pl.BlockSpec(memory_space=pl.ANY),
            pl.BlockSpec((COMBINE_TILE, D_MODEL), tile),
            pl.BlockSpec((None, 1, D_MODEL), lambda i, ws, nr: (l, 0, 0)),
        ],
        out_specs=pl.BlockSpec((COMBINE_TILE, D_MODEL), tile),
        scratch_shapes=[pltpu.VMEM((3, N_EXPERTS * COMBINE_WINDOW, D_MODEL), BF16),
                        pltpu.SemaphoreType.DMA((3, N_EXPERTS))],
    )
    return pl.pallas_call(
        _combine_kernel,
        grid_spec=grid_spec,
        out_shape=jax.ShapeDtypeStruct((TOKENS, D_MODEL), F32),
        compiler_params=_params("arbitrary"),
        name="moe_combine",
    )(wstart, nround, dest, ys, x, post_g)


def _routing_plan(sel, rank, counts):
    counts = counts.reshape(N_EXPERTS)
    padded = ((counts + EXPERT_BLOCK - 1) // EXPERT_BLOCK) * EXPERT_BLOCK
    region = ((counts + EXPERT_REGION - 1) // EXPERT_REGION) * EXPERT_REGION
    offs = jnp.cumsum(region) - region
    dest = jnp.where(sel == 1, rank + offs[:, None], -1)

    cum_reg = jnp.cumsum(region // EXPERT_REGION)
    qs = jnp.arange(N_REGIONS, dtype=I32)
    rexp = jnp.sum((cum_reg[None, :] <= jnp.minimum(qs, cum_reg[-1] - 1)[:, None]).astype(I32),
                   axis=1)
    rexp = jnp.minimum(rexp, N_EXPERTS - 1).astype(I32)
    rows_left = offs[rexp] + padded[rexp] - qs * EXPERT_REGION
    n_half = jnp.clip(rows_left // EXPERT_BLOCK, 0, EXPERT_REGION // EXPERT_BLOCK).astype(I32)

    group_end = (rank + sel)[:, LANES - 1::LANES]
    js = jnp.arange(N_SORT_BLOCKS, dtype=I32)
    e_s = rexp[js // (EXPERT_REGION // SORT_BLOCK)]
    lo_row = js * SORT_BLOCK - offs[e_s]
    has_rows = lo_row < counts[e_s]
    hi_row = jnp.minimum(lo_row + SORT_BLOCK, counts[e_s])
    ends = group_end[e_s]
    g_lo = jnp.sum((ends <= lo_row[:, None]).astype(I32), axis=1)
    g_hi = jnp.sum((ends < hi_row[:, None]).astype(I32), axis=1)
    span = (g_hi - g_lo + 1) * LANES
    n_win = jnp.where(has_rows, (span + TOKEN_WINDOW - 1) // TOKEN_WINDOW, 0).astype(I32)
    start = jnp.where(has_rows, g_lo * LANES, 0).astype(I32)
    d_plan = (e_s.astype(I32), start, n_win)

    tile_lo = rank[:, ::COMBINE_TILE]
    tile_hi = jnp.concatenate([tile_lo[:, 1:], counts[:, None]], axis=1)
    lo = tile_lo + offs[:, None]
    hi = tile_hi + offs[:, None]
    wstart = jnp.minimum((lo // BF16_ROWS) * BF16_ROWS, SORTED_ROWS - COMBINE_WINDOW)
    rounds = jnp.where(hi > lo, (hi - wstart + COMBINE_WINDOW - 1) // COMBINE_WINDOW, 0)
    n_round = jnp.maximum(jnp.max(rounds, axis=0), 1).astype(I32)
    c_plan = (wstart.T.reshape(-1).astype(I32), n_round)
    return dest, (rexp, n_half), d_plan, c_plan


def _moe(l, x1, h, gw, sel, rank, counts, w1, w3, w2, post_g):
    dest, e_plan, d_plan, c_plan = _routing_plan(sel, rank, counts)
    dest3 = dest.reshape(N_EXPERTS, 1, TOKENS)
    gw3 = gw.reshape(N_EXPERTS, 1, TOKENS)
    xs, gs = _dispatch(d_plan, dest3, gw3, h)
    ys = _experts(*e_plan, xs, gs, w1, w3, w2)
    return _combine(l, *c_plan, dest, ys, x1, post_g)


def kernel(x, pre_mix_g, post_mix_g, w_in, w_fourier, w_gmlp, w_out, w_spatial, b_spatial,
           gmlp_ln_g, gmlp_ln_b, pre_ffn_g, post_ffn_g, ffn_w1, ffn_w3, ffn_w2,
           router_w, router_b, moe_w1, moe_w3, moe_w2):
    pos = jnp.asarray(_POS_DFT).astype(BF16)
    chan = jnp.asarray(_CHAN_DFT).astype(BF16)
    tri = jnp.asarray(np.triu(np.ones((ROW_TILE, ROW_TILE), np.float32), 1)).astype(BF16)
    rows = lambda v: v.reshape(v.shape[0], 1, v.shape[1])
    pre_mix_g, post_mix_g, pre_ffn_g, post_ffn_g, gmlp_ln_g, gmlp_ln_b = map(
        rows, (pre_mix_g, post_mix_g, pre_ffn_g, post_ffn_g, gmlp_ln_g, gmlp_ln_b))
    b_full = jnp.repeat(jnp.swapaxes(b_spatial, 1, 2), GROUP_DIM, axis=2)
    router_wt = jnp.swapaxes(router_w, 1, 2)
    router_b = router_b.reshape(router_b.shape[0], N_EXPERTS, 1)

    assert DEPTH == 2
    moe_rows = lambda w: w[0].reshape(-1, w.shape[-1])
    moe_back = lambda w, like: w.reshape(like.shape[1:])

    xf = x.reshape(TOKENS, D_MODEL)
    (za, sg, gate), (moe1,) = _mixer_front(0, xf, pre_mix_g, w_in, w_spatial, b_full, gmlp_ln_g,
                                           gmlp_ln_b, side=(moe_rows(moe_w1),))
    fa = _fourier(za.reshape(BATCH, SEQ, MIX_WIDTH), chan, pos)
    x1, (ffn1, ffn3, ffn2) = _mixer_tail(
        0, fa, sg, gate, xf, w_fourier, w_gmlp, w_out, post_mix_g,
        side=(ffn_w1[0], ffn_w3[0], ffn_w2[0]))
    xf, (moe3, moe2) = _dense_ffn(0, x1, pre_ffn_g, ffn1, ffn3, ffn2, post_ffn_g,
                                  side=(moe_rows(moe_w3), moe_rows(moe_w2)))

    (za, sg, gate), _ = _mixer_front(1, xf, pre_mix_g, w_in, w_spatial, b_full, gmlp_ln_g,
                                     gmlp_ln_b)
    fa = _fourier(za.reshape(BATCH, SEQ, MIX_WIDTH), chan, pos)
    x1, h, gw, sel, rank, counts = _mixer_tail_router(
        1, fa, sg, gate, xf, w_fourier, w_gmlp, w_out, post_mix_g, pre_ffn_g, router_wt, router_b,
        tri)
    xf = _moe(1, x1, h, gw, sel, rank, counts, moe_back(moe1, moe_w1), moe_back(moe3, moe_w3),
              moe_back(moe2, moe_w2), post_ffn_g)
    return xf.reshape(BATCH, SEQ, D_MODEL)
```

```python
import numpy as np
import jax
import jax.numpy as jnp
from jax import lax
from jax.experimental import pallas as pl
from jax.experimental.pallas import tpu as pltpu

F32 = jnp.float32
BF16 = jnp.bfloat16
I32 = jnp.int32

D_MODEL = 1024
BATCH = 8
SEQ = 2048
HALF_SEQ = SEQ // 2
TOKENS = BATCH * SEQ
DEPTH = 2
N_GROUPS = 4
GROUP_DIM = 128
MIX_WIDTH = N_GROUPS * GROUP_DIM
CHUNK = 128
D_IN = 3 * MIX_WIDTH + 2 * D_MODEL
D_FF_DENSE = 2816
N_EXPERTS = 8
D_FF_EXPERT = 3584
RMS_EPS = 1e-6
LN_EPS = 1e-5
LANES = 128

ROW_TILE = 512
TAIL_TILE = 1024
FOURIER_ROW_TILE = 512
EXPERT_BLOCK = 512
EXPERT_REGION = 1024
SORT_BLOCK = 256
TOKEN_WINDOW = 1280
COMBINE_TILE = 256
COMBINE_WINDOW = 128
BF16_ROWS = 16
EXPERT_FF_TILE = 1792

SORTED_ROWS = 2 * TOKENS + N_EXPERTS * EXPERT_REGION
N_REGIONS = SORTED_ROWS // EXPERT_REGION
N_SORT_BLOCKS = SORTED_ROWS // SORT_BLOCK
N_COMBINE_TILES = TOKENS // COMBINE_TILE

VMEM_LIMIT = 56 * 1024 * 1024


def _params(*sem):
    return pltpu.CompilerParams(dimension_semantics=sem, vmem_limit_bytes=VMEM_LIMIT)


def _resident(shape):
    nd = len(shape)
    return pl.BlockSpec(shape, lambda *_: (0,) * nd, pipeline_mode=pl.Buffered(1))


def _layer(shape, l):
    nd = len(shape)
    return pl.BlockSpec((None,) + tuple(shape), lambda *_: (l,) + (0,) * nd,
                        pipeline_mode=pl.Buffered(1))


def _rms(x, g):
    return x * lax.rsqrt(jnp.mean(x * x, axis=-1, keepdims=True) + RMS_EPS) * g


def _dot(a, b):
    return jnp.dot(a, b, preferred_element_type=F32)


def _dot_nt(a, b):
    return lax.dot_general(a, b, (((1,), (1,)), ((), ())), preferred_element_type=F32)


def _dot_tn(a, b):
    return lax.dot_general(a, b, (((0,), (0,)), ((), ())), preferred_element_type=F32)


def _gelu(x):
    return 0.5 * x * (1.0 + lax.erf(x * np.float32(np.sqrt(0.5))))


def _cast_once(src_ref, dst_ref):
    chunk = 128

    @pl.when(pl.program_id(0) == 0)
    def _():
        def body(i, carry):
            rows = pl.ds(pl.multiple_of(i * chunk, chunk), chunk)
            dst_ref[rows, :] = src_ref[rows, :].astype(BF16)
            return carry
        lax.fori_loop(0, src_ref.shape[0] // chunk, body, 0)


def _split_bf16(x):
    hi = x.astype(BF16)
    return hi, (x - hi.astype(F32)).astype(BF16)


def _side_specs(arrays, steps):
    specs, shapes = [], []
    for a in arrays:
        rows, cols = a.shape
        chunk = rows // steps
        if chunk % BF16_ROWS == 0:
            spec = pl.BlockSpec((chunk, cols), lambda i: (i, 0))
        else:
            chunk *= 2
            spec = pl.BlockSpec((chunk, cols), lambda i: (i // 2, 0))
        assert rows % chunk == 0 and chunk % BF16_ROWS == 0, (a.shape, steps)
        specs.append(spec)
        shapes.append(jax.ShapeDtypeStruct(a.shape, BF16))
    return specs, shapes


def _with_side_casts(body, n_in, n_out, n_side):
    def kernel(*refs):
        ins, rest = refs[:n_in], refs[n_in:]
        srcs, rest = rest[:n_side], rest[n_side:]
        outs, rest = rest[:n_out], rest[n_out:]
        dsts, scratch = rest[:n_side], rest[n_side:]
        for src, dst in zip(srcs, dsts):
            dst[...] = src[...].astype(BF16)
        body(*ins, *outs, *scratch)
    return kernel


def _dft_tables():
    k = np.arange(HALF_SEQ, dtype=np.int64)
    scale = 1.0 / np.sqrt(SEQ)
    halves = []
    for p in (0, 1):
        n = 2 * np.arange(HALF_SEQ, dtype=np.int64) + p
        ang = 2.0 * np.pi * ((k[:, None] * n[None, :]) % SEQ).astype(np.float64) / SEQ
        halves.append(np.concatenate([np.cos(ang) * scale, -np.sin(ang) * scale], axis=1))
    pos = np.stack(halves)
    c = np.arange(GROUP_DIM, dtype=np.int64)
    angc = 2.0 * np.pi * ((c[:, None] * c[None, :]) % GROUP_DIM).astype(np.float64) / GROUP_DIM
    scalec = 1.0 / np.sqrt(GROUP_DIM)
    chan = np.concatenate([np.cos(angc) * scalec, np.sin(angc) * scalec], axis=1)
    return pos.astype(np.float32), chan.astype(np.float32)


_POS_DFT, _CHAN_DFT = _dft_tables()


def _front_kernel(x_ref, g_ref, winf_ref, ws_ref, bs_ref, lng_ref, lnb_ref,
                  za_ref, sg_ref, gate_ref, win_ref):
    _cast_once(winf_ref, win_ref)
    ws = [ws_ref[g].astype(BF16) for g in range(N_GROUPS)]
    h = _rms(x_ref[...], g_ref[...]).astype(BF16)
    za_ref[...] = _dot(h, win_ref[:, 0:MIX_WIDTH]).astype(BF16)
    u = _gelu(_dot(h, win_ref[:, MIX_WIDTH:2 * MIX_WIDTH]))
    v = _gelu(_dot(h, win_ref[:, 2 * MIX_WIDTH:3 * MIX_WIDTH]))
    mu = jnp.mean(v, axis=-1, keepdims=True)
    vc = v - mu
    var = jnp.mean(vc * vc, axis=-1, keepdims=True)
    vln = (vc * lax.rsqrt(var + LN_EPS) * lng_ref[...] + lnb_ref[...]).astype(BF16)
    for c in range(x_ref.shape[0] // CHUNK):
        rows = slice(c * CHUNK, (c + 1) * CHUNK)
        for g in range(N_GROUPS):
            cols = slice(g * GROUP_DIM, (g + 1) * GROUP_DIM)
            sv = _dot(ws[g], vln[rows, cols]) + bs_ref[:, cols]
            sg_ref[rows, cols] = (u[rows, cols] * sv).astype(BF16)
    zg = _dot(h, win_ref[:, 3 * MIX_WIDTH:D_IN])
    gate_ref[...] = jax.nn.sigmoid(zg).astype(BF16)


def _mixer_front(l, x, g, w_in, w_s, b_full, ln_g, ln_b, side=(), tile=ROW_TILE):
    row = lambda w: pl.BlockSpec((tile, w), lambda i: (i, 0))
    steps = TOKENS // tile
    side_specs, side_shapes = _side_specs(side, steps)
    outs = pl.pallas_call(
        _with_side_casts(_front_kernel, 7, 3, len(side)),
        grid=(steps,),
        in_specs=[row(D_MODEL), _layer((1, D_MODEL), l), _layer((D_MODEL, D_IN), l),
                  _layer((N_GROUPS, CHUNK, CHUNK), l), _layer((CHUNK, MIX_WIDTH), l),
                  _layer((1, MIX_WIDTH), l), _layer((1, MIX_WIDTH), l)] + side_specs,
        out_specs=[row(MIX_WIDTH), row(MIX_WIDTH), row(2 * D_MODEL)] + side_specs,
        out_shape=[jax.ShapeDtypeStruct((TOKENS, MIX_WIDTH), BF16),
                   jax.ShapeDtypeStruct((TOKENS, MIX_WIDTH), BF16),
                   jax.ShapeDtypeStruct((TOKENS, 2 * D_MODEL), BF16)] + side_shapes,
        scratch_shapes=[pltpu.VMEM((D_MODEL, D_IN), BF16)],
        compiler_params=_params("arbitrary"),
        name="mixer_front",
    )(x, g, w_in, w_s, b_full, ln_g, ln_b, *side)
    return outs[:3], outs[3:]


def _fourier_kernel(za_ref, chan_ref, pos_ref, out_ref, stage_ref, rhs_ref):
    @pl.when(pl.program_id(1) == 0)
    def _():
        for g in range(N_GROUPS):
            cols = slice(g * GROUP_DIM, (g + 1) * GROUP_DIM)
            xcs = _dot(za_ref[0, :, cols], chan_ref[...])
            stage_ref[g] = xcs[:, 0:GROUP_DIM]
            stage_ref[N_GROUPS + g] = xcs[:, GROUP_DIM:2 * GROUP_DIM]
        for g in range(N_GROUPS):
            cols = slice(g * GROUP_DIM, (g + 1) * GROUP_DIM)
            for p in (0, 1):
                rows = pl.ds(p, HALF_SEQ, stride=2)
                rhs_ref[p, 0:HALF_SEQ, cols] = stage_ref[g, rows, :].astype(BF16)
                rhs_ref[p, HALF_SEQ:SEQ, cols] = stage_ref[N_GROUPS + g, rows, :].astype(BF16)

    even = _dot(pos_ref[0], rhs_ref[0])
    odd = _dot(pos_ref[1], rhs_ref[1])
    out_ref[0, 0] = (even + odd).astype(BF16)
    out_ref[0, 1] = (even - odd).astype(BF16)


def _fourier(za, chan, pos):
    out = pl.pallas_call(
        _fourier_kernel,
        grid=(BATCH, HALF_SEQ // FOURIER_ROW_TILE),
        in_specs=[pl.BlockSpec((1, SEQ, MIX_WIDTH), lambda b, i: (b, 0, 0)),
                  _resident((GROUP_DIM, 2 * GROUP_DIM)),
                  pl.BlockSpec((2, FOURIER_ROW_TILE, SEQ), lambda b, i: (0, i, 0))],
        out_specs=pl.BlockSpec((1, 2, FOURIER_ROW_TILE, MIX_WIDTH), lambda b, i: (b, 0, i, 0)),
        out_shape=jax.ShapeDtypeStruct((BATCH, 2, HALF_SEQ, MIX_WIDTH), BF16),
        scratch_shapes=[pltpu.VMEM((2 * N_GROUPS, SEQ, GROUP_DIM), F32),
                        pltpu.VMEM((2, SEQ, MIX_WIDTH), BF16)],
        compiler_params=_params("arbitrary", "arbitrary"),
        name="fourier_mix",
    )(za, chan, pos)
    return out.reshape(TOKENS, MIX_WIDTH)


def _tail_common(fa_ref, sg_ref, gate_ref, x_ref, wff_ref, wgf_ref, wof_ref, pg_ref,
                 wf_ref, wg_ref, wo_ref):
    _cast_once(wff_ref, wf_ref)
    _cast_once(wgf_ref, wg_ref)
    _cast_once(wof_ref, wo_ref)
    ya = _dot(fa_ref[...], wf_ref[...])
    yb = _dot(sg_ref[...], wg_ref[...])
    m = gate_ref[:, 0:D_MODEL].astype(F32) * ya + gate_ref[:, D_MODEL:2 * D_MODEL].astype(F32) * yb
    y = _dot(m.astype(BF16), wo_ref[...])
    return x_ref[...] + _rms(y, pg_ref[...])


def _tail_kernel(fa_ref, sg_ref, gate_ref, x_ref, wff_ref, wgf_ref, wof_ref, pg_ref, x1_ref,
                 wf_ref, wg_ref, wo_ref):
    x1_ref[...] = _tail_common(fa_ref, sg_ref, gate_ref, x_ref, wff_ref, wgf_ref, wof_ref, pg_ref,
                               wf_ref, wg_ref, wo_ref)


def _tail_router_kernel(fa_ref, sg_ref, gate_ref, x_ref, wff_ref, wgf_ref, wof_ref, pg_ref,
                        fg_ref, rwt_ref, rb_ref, tri_ref,
                        x1_ref, h_ref, gw_ref, sel_ref, rank_ref, cnt_ref,
                        wf_ref, wg_ref, wo_ref, carry_ref):
    @pl.when(pl.program_id(0) == 0)
    def _():
        carry_ref[...] = jnp.zeros_like(carry_ref)

    x1 = _tail_common(fa_ref, sg_ref, gate_ref, x_ref, wff_ref, wgf_ref, wof_ref, pg_ref,
                      wf_ref, wg_ref, wo_ref)
    x1_ref[...] = x1
    h = _rms(x1, fg_ref[...])
    h_hi, h_lo = _split_bf16(h)
    h_ref[...] = h_hi

    w_hi, w_lo = _split_bf16(rwt_ref[...])
    logits = _dot_nt(w_hi, h_hi) + _dot_nt(w_hi, h_lo) + _dot_nt(w_lo, h_hi) + rb_ref[...]
    row = lax.broadcasted_iota(I32, logits.shape, 0)
    m1 = jnp.max(logits, axis=0, keepdims=True)
    i1 = jnp.min(jnp.where(logits == m1, row, N_EXPERTS), axis=0, keepdims=True)
    rest = jnp.where(row == i1, -jnp.inf, logits)
    m2 = jnp.max(rest, axis=0, keepdims=True)
    i2 = jnp.min(jnp.where(rest == m2, row, N_EXPERTS), axis=0, keepdims=True)
    e2 = jnp.exp(m2 - m1)
    den = 1.0 + e2
    gw_ref[...] = jnp.where(row == i1, 1.0 / den, 0.0) + jnp.where(row == i2, e2 / den, 0.0)
    sel = jnp.where((row == i1) | (row == i2), 1.0, 0.0)
    sel_ref[...] = sel.astype(I32)
    before = _dot(sel.astype(BF16), tri_ref[...]) + carry_ref[...]
    rank_ref[...] = before.astype(I32)
    carry_ref[...] = carry_ref[...] + jnp.sum(sel, axis=1, keepdims=True)
    cnt_ref[...] = carry_ref[...].astype(I32)


def _tail_specs(l):
    row = lambda w: pl.BlockSpec((TAIL_TILE, w), lambda i: (i, 0))
    in_specs = [row(MIX_WIDTH), row(MIX_WIDTH), row(2 * D_MODEL), row(D_MODEL),
                _layer((MIX_WIDTH, D_MODEL), l), _layer((MIX_WIDTH, D_MODEL), l),
                _layer((D_MODEL, D_MODEL), l), _layer((1, D_MODEL), l)]
    weight_scratch = [pltpu.VMEM((MIX_WIDTH, D_MODEL), BF16), pltpu.VMEM((MIX_WIDTH, D_MODEL), BF16),
                      pltpu.VMEM((D_MODEL, D_MODEL), BF16)]
    return row, in_specs, weight_scratch


def _mixer_tail(l, fa, sg, gate, x, w_f, w_g, w_o, post_g, side=()):
    row, in_specs, weight_scratch = _tail_specs(l)
    steps = TOKENS // TAIL_TILE
    side_specs, side_shapes = _side_specs(side, steps)
    outs = pl.pallas_call(
        _with_side_casts(_tail_kernel, 8, 1, len(side)),
        grid=(steps,),
        in_specs=in_specs + side_specs,
        out_specs=[row(D_MODEL)] + side_specs,
        out_shape=[jax.ShapeDtypeStruct((TOKENS, D_MODEL), F32)] + side_shapes,
        scratch_shapes=weight_scratch,
        compiler_params=_params("arbitrary"),
        name="mixer_tail",
    )(fa, sg, gate, x, w_f, w_g, w_o, post_g, *side)
    return outs[0], outs[1:]


def _mixer_tail_router(l, fa, sg, gate, x, w_f, w_g, w_o, post_g, ffn_g, router_wt, router_b, tri):
    row, in_specs, weight_scratch = _tail_specs(l)
    in_specs += [_layer((1, D_MODEL), l), _layer((N_EXPERTS, D_MODEL), l // 2),
                 _layer((N_EXPERTS, 1), l // 2), _resident((TAIL_TILE, TAIL_TILE))]
    col = pl.BlockSpec((N_EXPERTS, TAIL_TILE), lambda i: (0, i))
    per_token = lambda dt: jax.ShapeDtypeStruct((N_EXPERTS, TOKENS), dt)
    return pl.pallas_call(
        _tail_router_kernel,
        grid=(TOKENS // TAIL_TILE,),
        in_specs=in_specs,
        out_specs=[row(D_MODEL), row(D_MODEL), col, col, col,
                   pl.BlockSpec((N_EXPERTS, 1), lambda i: (0, 0))],
        out_shape=[jax.ShapeDtypeStruct((TOKENS, D_MODEL), F32),
                   jax.ShapeDtypeStruct((TOKENS, D_MODEL), BF16),
                   per_token(F32), per_token(I32), per_token(I32),
                   jax.ShapeDtypeStruct((N_EXPERTS, 1), I32)],
        scratch_shapes=weight_scratch + [pltpu.VMEM((N_EXPERTS, 1), F32)],
        compiler_params=_params("arbitrary"),
        name="mixer_tail_router",
    )(fa, sg, gate, x, w_f, w_g, w_o, post_g, ffn_g, router_wt, router_b, tri)


def _dense_ffn_kernel(x_ref, fg_ref, w1_ref, w3_ref, w2_ref, pg_ref, out_ref):
    x = x_ref[...]
    h = _rms(x, fg_ref[...]).astype(BF16)
    a = _dot(h, w1_ref[...])
    b = _dot(h, w3_ref[...])
    y = _dot((jax.nn.silu(a) * b).astype(BF16), w2_ref[...])
    out_ref[...] = x + _rms(y, pg_ref[...])


def _dense_ffn(l, x, ffn_g, w1, w3, w2, post_g, side=()):
    row = pl.BlockSpec((ROW_TILE, D_MODEL), lambda i: (i, 0))
    steps = TOKENS // ROW_TILE
    side_specs, side_shapes = _side_specs(side, steps)
    outs = pl.pallas_call(
        _with_side_casts(_dense_ffn_kernel, 6, 1, len(side)),
        grid=(steps,),
        in_specs=[row, _layer((1, D_MODEL), l), _resident((D_MODEL, D_FF_DENSE)),
                  _resident((D_MODEL, D_FF_DENSE)), _resident((D_FF_DENSE, D_MODEL)),
                  _layer((1, D_MODEL), l)] + side_specs,
        out_specs=[row] + side_specs,
        out_shape=[jax.ShapeDtypeStruct((TOKENS, D_MODEL), F32)] + side_shapes,
        compiler_params=_params("arbitrary"),
        name="dense_ffn",
    )(x, ffn_g, w1, w3, w2, post_g, *side)
    return outs[0], outs[1:]


def _window_hits(k, block, start_ref, dest_ref):
    wanted = start_ref[block] + k * TOKEN_WINDOW
    begin = pl.multiple_of(jnp.minimum(wanted, TOKENS - TOKEN_WINDOW), LANES)
    tok = lax.broadcasted_iota(I32, (1, TOKEN_WINDOW), 1) + begin
    dest = jnp.where(tok >= wanted, dest_ref[:, pl.ds(begin, TOKEN_WINDOW)], -1)
    rows = lax.broadcasted_iota(I32, (SORT_BLOCK, TOKEN_WINDOW), 0) + block * SORT_BLOCK
    return begin, rows == dest


def _dispatch_kernel(exp_ref, start_ref, nwin_ref, dest_ref, gw_ref, h_ref, xs_ref, gs_ref):
    j = pl.program_id(0)

    def window(k):
        begin, hit = _window_hits(k, j, start_ref, dest_ref)
        onehot = jnp.where(hit, 1.0, 0.0).astype(BF16)
        picked = _dot(onehot, h_ref[pl.ds(begin, TOKEN_WINDOW), :]).astype(BF16)
        gw = gw_ref[:, pl.ds(begin, TOKEN_WINDOW)]
        return picked, jnp.sum(jnp.where(hit, gw, 0.0), axis=-1, keepdims=True)

    @pl.when(nwin_ref[j] == 0)
    def _():
        xs_ref[...] = jnp.zeros_like(xs_ref)
        gs_ref[...] = jnp.zeros_like(gs_ref)

    @pl.when(nwin_ref[j] > 0)
    def _():
        xs_ref[...], gs_ref[...] = window(0)

        def more(k, carry):
            picked, gsum = window(k)
            xs_ref[...] += picked
            gs_ref[...] += gsum
            return carry
        lax.fori_loop(1, nwin_ref[j], more, 0)


def _per_expert_row():
    return pl.BlockSpec((None, 1, TOKENS),
                        lambda j, e, s, n: (e[jnp.minimum(j, N_SORT_BLOCKS - 1)], 0, 0))


def _dispatch(plan, dest3, gw3, h):
    exp, start, nwin = plan
    grid_spec = pltpu.PrefetchScalarGridSpec(
        num_scalar_prefetch=3,
        grid=(N_SORT_BLOCKS,),
        in_specs=[_per_expert_row(), _per_expert_row(),
                  pl.BlockSpec((TOKENS, D_MODEL), lambda j, e, s, n: (0, 0),
                               pipeline_mode=pl.Buffered(1))],
        out_specs=[pl.BlockSpec((SORT_BLOCK, D_MODEL), lambda j, e, s, n: (j, 0)),
                   pl.BlockSpec((SORT_BLOCK, 1), lambda j, e, s, n: (j, 0))],
    )
    return pl.pallas_call(
        _dispatch_kernel,
        grid_spec=grid_spec,
        out_shape=[jax.ShapeDtypeStruct((SORTED_ROWS, D_MODEL), BF16),
                   jax.ShapeDtypeStruct((SORTED_ROWS, 1), F32)],
        compiler_params=_params("arbitrary"),
        name="moe_dispatch",
    )(exp, start, nwin, dest3, gw3, h)


def _expert_kernel(rexp_ref, nhalf_ref, xs_ref, gs_ref, w1_ref, w3_ref, w2_ref, ys_ref, acc_ref):
    q = pl.program_id(0)
    f = pl.program_id(1)
    last = pl.num_programs(1) - 1
    nhalf = nhalf_ref[q]

    @pl.when((q == 0) & (f == 0))
    def _():
        acc_ref[...] = jnp.zeros_like(acc_ref)

    def swiglu_part(n_blocks):
        for k in range(n_blocks):
            rows = slice(k * EXPERT_BLOCK, (k + 1) * EXPERT_BLOCK)
            x = xs_ref[rows, :]
            a = _dot(x, w1_ref[...])
            b = _dot(x, w3_ref[...])
            part = _dot((jax.nn.silu(a) * b).astype(BF16), w2_ref[...])
            acc_ref[rows, :] = jnp.where(f == 0, 0.0, acc_ref[rows, :]) + part

        @pl.when(f == last)
        def _():
            rows = slice(0, n_blocks * EXPERT_BLOCK)
            ys_ref[rows, :] = (acc_ref[rows, :] * gs_ref[rows, :]).astype(BF16)

    @pl.when(nhalf == 2)
    def _():
        swiglu_part(2)

    @pl.when(nhalf == 1)
    def _():
        swiglu_part(1)

    @pl.when((nhalf < 2) & (f == last))
    def _():
        ys_ref[EXPERT_BLOCK:EXPERT_REGION, :] = jnp.zeros((EXPERT_REGION - EXPERT_BLOCK, D_MODEL), BF16)

    @pl.when((nhalf == 0) & (f == last))
    def _():
        ys_ref[0:EXPERT_BLOCK, :] = jnp.zeros((EXPERT_BLOCK, D_MODEL), BF16)


def _experts(rexp, nhalf, xs, gs, w1, w3, w2):
    n_ff = D_FF_EXPERT // EXPERT_FF_TILE
    tile = lambda q, f: jnp.where(q % 2 == 0, f, n_ff - 1 - f)
    grid_spec = pltpu.PrefetchScalarGridSpec(
        num_scalar_prefetch=2,
        grid=(N_REGIONS, D_FF_EXPERT // EXPERT_FF_TILE),
        in_specs=[
            pl.BlockSpec((EXPERT_REGION, D_MODEL), lambda q, f, re, nh: (q, 0)),
            pl.BlockSpec((EXPERT_REGION, 1), lambda q, f, re, nh: (q, 0)),
            pl.BlockSpec((None, D_MODEL, EXPERT_FF_TILE), lambda q, f, re, nh: (re[q], 0, tile(q, f))),
            pl.BlockSpec((None, D_MODEL, EXPERT_FF_TILE), lambda q, f, re, nh: (re[q], 0, tile(q, f))),
            pl.BlockSpec((None, EXPERT_FF_TILE, D_MODEL), lambda q, f, re, nh: (re[q], tile(q, f), 0)),
        ],
        out_specs=pl.BlockSpec((EXPERT_REGION, D_MODEL), lambda q, f, re, nh: (q, 0)),
        scratch_shapes=[pltpu.VMEM((EXPERT_REGION, D_MODEL), F32)],
    )
    return pl.pallas_call(
        _expert_kernel,
        grid_spec=grid_spec,
        out_shape=jax.ShapeDtypeStruct((SORTED_ROWS, D_MODEL), BF16),
        compiler_params=_params("arbitrary", "arbitrary"),
        name="moe_experts",
    )(rexp, nhalf, xs, gs, w1, w3, w2)


def _combine_copies(wstart_ref, ys_hbm, buf_ref, sem_ref, tile, rnd, slot):
    copies = []
    for e in range(N_EXPERTS):
        wanted = wstart_ref[tile * N_EXPERTS + e] + rnd * COMBINE_WINDOW
        begin = pl.multiple_of(jnp.minimum(wanted, SORTED_ROWS - COMBINE_WINDOW), BF16_ROWS)
        copies.append(pltpu.make_async_copy(
            ys_hbm.at[pl.ds(begin, COMBINE_WINDOW), :],
            buf_ref.at[slot, pl.ds(e * COMBINE_WINDOW, COMBINE_WINDOW), :],
            sem_ref.at[slot, e]))
    return copies


def _combine_kernel(wstart_ref, nround_ref, dest_ref, ys_hbm, x_ref, pg_ref, out_ref,
                    buf_ref, sem_ref):
    i = pl.program_id(0)
    n_tiles = pl.num_programs(0)
    slot = lax.rem(i, 2)

    @pl.when(i == 0)
    def _():
        for c in _combine_copies(wstart_ref, ys_hbm, buf_ref, sem_ref, 0, 0, 0):
            c.start()

    @pl.when(i + 1 < n_tiles)
    def _():
        for c in _combine_copies(wstart_ref, ys_hbm, buf_ref, sem_ref, i + 1, 0, 1 - slot):
            c.start()

    def gather(rnd, fetch_slot):
        sub = lax.broadcasted_iota(I32, (COMBINE_WINDOW, 1), 0)
        pieces = []
        for e in range(N_EXPERTS):
            wanted = wstart_ref[i * N_EXPERTS + e] + rnd * COMBINE_WINDOW
            begin = jnp.minimum(wanted, SORTED_ROWS - COMBINE_WINDOW)
            rowid = sub + begin
            rowid = jnp.where(rowid >= wanted, rowid, -2)
            pieces.append(jnp.where(rowid == dest_ref[e:e + 1, :], 1.0, 0.0).astype(BF16))
        onehot = jnp.concatenate(pieces, axis=0)
        return _dot_tn(onehot, buf_ref[fetch_slot])

    for c in _combine_copies(wstart_ref, ys_hbm, buf_ref, sem_ref, i, 0, slot):
        c.wait()
    y = gather(0, slot)

    def extra(rnd, y):
        copies = _combine_copies(wstart_ref, ys_hbm, buf_ref, sem_ref, i, rnd, 2)
        for c in copies:
            c.start()
        for c in copies:
            c.wait()
        return y + gather(rnd, 2)

    y = lax.fori_loop(1, nround_ref[i], extra, y)
    out_ref[...] = x_ref[...] + _rms(y, pg_ref[...])


def _combine(l, wstart, nround, dest, ys, x, post_g):
    tile = lambda i, ws, nr: (i, 0)
    grid_spec = pltpu.PrefetchScalarGridSpec(
        num_scalar_prefetch=2,
        grid=(TOKENS // COMBINE_TILE,),
        in_specs=[
            pl.BlockSpec((N_EXPERTS, COMBINE_TILE), lambda i, ws, nr: (0, i)),
            pl.BlockSpec(memory_space=pl.ANY),
            pl.BlockSpec((COMBINE_TILE, D_MODEL), tile),
            pl.BlockSpec((None, 1, D_MODEL), lambda i, ws, nr: (l, 0, 0)),
        ],
        out_specs=pl.BlockSpec((COMBINE_TILE, D_MODEL), tile),
        scratch_shapes=[pltpu.VMEM((3, N_EXPERTS * COMBINE_WINDOW, D_MODEL), BF16),
                        pltpu.SemaphoreType.DMA((3, N_EXPERTS))],
    )
    return pl.pallas_call(
        _combine_kernel,
        grid_spec=grid_spec,
        out_shape=jax.ShapeDtypeStruct((TOKENS, D_MODEL), F32),
        compiler_params=_params("arbitrary"),
        name="moe_combine",
    )(wstart, nround, dest, ys, x, post_g)


def _routing_plan(sel, rank, counts):
    counts = counts.reshape(N_EXPERTS)
    padded = ((counts + EXPERT_BLOCK - 1) // EXPERT_BLOCK) * EXPERT_BLOCK
    region = ((counts + EXPERT_REGION - 1) // EXPERT_REGION) * EXPERT_REGION
    offs = jnp.cumsum(region) - region
    dest = jnp.where(sel == 1, rank + offs[:, None], -1)

    cum_reg = jnp.cumsum(region // EXPERT_REGION)
    qs = jnp.arange(N_REGIONS, dtype=I32)
    rexp = jnp.sum((cum_reg[None, :] <= jnp.minimum(qs, cum_reg[-1] - 1)[:, None]).astype(I32),
                   axis=1)
    rexp = jnp.minimum(rexp, N_EXPERTS - 1).astype(I32)
    rows_left = offs[rexp] + padded[rexp] - qs * EXPERT_REGION
    n_half = jnp.clip(rows_left // EXPERT_BLOCK, 0, EXPERT_REGION // EXPERT_BLOCK).astype(I32)

    group_end = (rank + sel)[:, LANES - 1::LANES]
    js = jnp.arange(N_SORT_BLOCKS, dtype=I32)
    e_s = rexp[js // (EXPERT_REGION // SORT_BLOCK)]
    lo_row = js * SORT_BLOCK - offs[e_s]
    has_rows = lo_row < counts[e_s]
    hi_row = jnp.minimum(lo_row + SORT_BLOCK, counts[e_s])
    ends = group_end[e_s]
    g_lo = jnp.sum((ends <= lo_row[:, None]).astype(I32), axis=1)
    g_hi = jnp.sum((ends < hi_row[:, None]).astype(I32), axis=1)
    span = (g_hi - g_lo + 1) * LANES
    n_win = jnp.where(has_rows, (span + TOKEN_WINDOW - 1) // TOKEN_WINDOW, 0).astype(I32)
    start = jnp.where(has_rows, g_lo * LANES, 0).astype(I32)
    d_plan = (e_s.astype(I32), start, n_win)

    tile_lo = rank[:, ::COMBINE_TILE]
    tile_hi = jnp.concatenate([tile_lo[:, 1:], counts[:, None]], axis=1)
    lo = tile_lo + offs[:, None]
    hi = tile_hi + offs[:, None]
    wstart = jnp.minimum((lo // BF16_ROWS) * BF16_ROWS, SORTED_ROWS - COMBINE_WINDOW)
    rounds = jnp.where(hi > lo, (hi - wstart + COMBINE_WINDOW - 1) // COMBINE_WINDOW, 0)
    n_round = jnp.maximum(jnp.max(rounds, axis=0), 1).astype(I32)
    c_plan = (wstart.T.reshape(-1).astype(I32), n_round)
    return dest, (rexp, n_half), d_plan, c_plan


def _moe(l, x1, h, gw, sel, rank, counts, w1, w3, w2, post_g):
    dest, e_plan, d_plan, c_plan = _routing_plan(sel, rank, counts)
    dest3 = dest.reshape(N_EXPERTS, 1, TOKENS)
    gw3 = gw.reshape(N_EXPERTS, 1, TOKENS)
    xs, gs = _dispatch(d_plan, dest3, gw3, h)
    ys = _experts(*e_plan, xs, gs, w1, w3, w2)
    return _combine(l, *c_plan, dest, ys, x1, post_g)


def kernel(x, pre_mix_g, post_mix_g, w_in, w_fourier, w_gmlp, w_out, w_spatial, b_spatial,
           gmlp_ln_g, gmlp_ln_b, pre_ffn_g, post_ffn_g, ffn_w1, ffn_w3, ffn_w2,
           router_w, router_b, moe_w1, moe_w3, moe_w2):
    pos = jnp.asarray(_POS_DFT).astype(BF16)
    chan = jnp.asarray(_CHAN_DFT).astype(BF16)
    tri = jnp.asarray(np.triu(np.ones((TAIL_TILE, TAIL_TILE), np.float32), 1)).astype(BF16)
    rows = lambda v: v.reshape(v.shape[0], 1, v.shape[1])
    pre_mix_g, post_mix_g, pre_ffn_g, post_ffn_g, gmlp_ln_g, gmlp_ln_b = map(
        rows, (pre_mix_g, post_mix_g, pre_ffn_g, post_ffn_g, gmlp_ln_g, gmlp_ln_b))
    b_full = jnp.repeat(jnp.swapaxes(b_spatial, 1, 2), GROUP_DIM, axis=2)
    router_wt = jnp.swapaxes(router_w, 1, 2)
    router_b = router_b.reshape(router_b.shape[0], N_EXPERTS, 1)

    assert DEPTH == 2
    moe_rows = lambda w: w[0].reshape(-1, w.shape[-1])
    moe_back = lambda w, like: w.reshape(like.shape[1:])

    xf = x.reshape(TOKENS, D_MODEL)
    (za, sg, gate), (moe1,) = _mixer_front(0, xf, pre_mix_g, w_in, w_spatial, b_full, gmlp_ln_g,
                                           gmlp_ln_b, side=(moe_rows(moe_w1),))
    fa = _fourier(za.reshape(BATCH, SEQ, MIX_WIDTH), chan, pos)
    x1, (ffn1, ffn3, ffn2) = _mixer_tail(
        0, fa, sg, gate, xf, w_fourier, w_gmlp, w_out, post_mix_g,
        side=(ffn_w1[0], ffn_w3[0], ffn_w2[0]))
    xf, (moe3, moe2) = _dense_ffn(0, x1, pre_ffn_g, ffn1, ffn3, ffn2, post_ffn_g,
                                  side=(moe_rows(moe_w3), moe_rows(moe_w2)))

    (za, sg, gate), _ = _mixer_front(1, xf, pre_mix_g, w_in, w_spatial, b_full, gmlp_ln_g,
                                     gmlp_ln_b, tile=TAIL_TILE)
    fa = _fourier(za.reshape(BATCH, SEQ, MIX_WIDTH), chan, pos)
    x1, h, gw, sel, rank, counts = _mixer_tail_router(
        1, fa, sg, gate, xf, w_fourier, w_gmlp, w_out, post_mix_g, pre_ffn_g, router_wt, router_b,
        tri)
    xf = _moe(1, x1, h, gw, sel, rank, counts, moe_back(moe1, moe_w1), moe_back(moe3, moe_w3),
              moe_back(moe2, moe_w2), post_ffn_g)
    return xf.reshape(BATCH, SEQ, D_MODEL)
```

```python
import numpy as np
import jax
import jax.numpy as jnp
from jax import lax
from jax.experimental import pallas as pl
from jax.experimental.pallas import tpu as pltpu

F32 = jnp.float32
BF16 = jnp.bfloat16
I32 = jnp.int32

D_MODEL = 1024
BATCH = 8
SEQ = 2048
HALF_SEQ = SEQ // 2
TOKENS = BATCH * SEQ
DEPTH = 2
N_GROUPS = 4
GROUP_DIM = 128
MIX_WIDTH = N_GROUPS * GROUP_DIM
CHUNK = 128
D_IN = 3 * MIX_WIDTH + 2 * D_MODEL
D_FF_DENSE = 2816
N_EXPERTS = 8
D_FF_EXPERT = 3584
RMS_EPS = 1e-6
LN_EPS = 1e-5
LANES = 128

ROW_TILE = 512
TAIL_TILE = 1024
FOURIER_ROW_TILE = 512
EXPERT_BLOCK = 512
EXPERT_REGION = 1024
SORT_BLOCK = 256
TOKEN_WINDOW = 1280
COMBINE_TILE = 256
COMBINE_WINDOW = 128
BF16_ROWS = 16
EXPERT_FF_TILE = 1792

SORTED_ROWS = 2 * TOKENS + N_EXPERTS * EXPERT_REGION
N_REGIONS = SORTED_ROWS // EXPERT_REGION
N_SORT_BLOCKS = SORTED_ROWS // SORT_BLOCK
N_COMBINE_TILES = TOKENS // COMBINE_TILE

VMEM_LIMIT = 56 * 1024 * 1024


def _params(*sem):
    return pltpu.CompilerParams(dimension_semantics=sem, vmem_limit_bytes=VMEM_LIMIT)


def _resident(shape):
    nd = len(shape)
    return pl.BlockSpec(shape, lambda *_: (0,) * nd, pipeline_mode=pl.Buffered(1))


def _layer(shape, l):
    nd = len(shape)
    return pl.BlockSpec((None,) + tuple(shape), lambda *_: (l,) + (0,) * nd,
                        pipeline_mode=pl.Buffered(1))


def _rms(x, g):
    return x * lax.rsqrt(jnp.mean(x * x, axis=-1, keepdims=True) + RMS_EPS) * g


def _dot(a, b):
    return jnp.dot(a, b, preferred_element_type=F32)


def _dot_nt(a, b):
    return lax.dot_general(a, b, (((1,), (1,)), ((), ())), preferred_element_type=F32)


def _dot_tn(a, b):
    return lax.dot_general(a, b, (((0,), (0,)), ((), ())), preferred_element_type=F32)


def _gelu(x):
    return 0.5 * x * (1.0 + lax.erf(x * np.float32(np.sqrt(0.5))))


def _cast_once(*pairs):
    chunk = 128

    @pl.when(pl.program_id(0) == 0)
    def _():
        for src_ref, dst_ref in pairs:
            def body(i, carry, src_ref=src_ref, dst_ref=dst_ref):
                rows = pl.ds(pl.multiple_of(i * chunk, chunk), chunk)
                dst_ref[rows, :] = src_ref[rows, :].astype(BF16)
                return carry
            lax.fori_loop(0, src_ref.shape[0] // chunk, body, 0)


def _split_bf16(x):
    hi = x.astype(BF16)
    return hi, (x - hi.astype(F32)).astype(BF16)


def _side_specs(arrays, steps):
    specs, shapes = [], []
    for a in arrays:
        rows, cols = a.shape
        chunk = rows // steps
        if chunk % BF16_ROWS == 0:
            spec = pl.BlockSpec((chunk, cols), lambda i: (i, 0))
        else:
            chunk *= 2
            spec = pl.BlockSpec((chunk, cols), lambda i: (i // 2, 0))
        assert rows % chunk == 0 and chunk % BF16_ROWS == 0, (a.shape, steps)
        specs.append(spec)
        shapes.append(jax.ShapeDtypeStruct(a.shape, BF16))
    return specs, shapes


def _with_side_casts(body, n_in, n_out, n_side):
    def kernel(*refs):
        ins, rest = refs[:n_in], refs[n_in:]
        srcs, rest = rest[:n_side], rest[n_side:]
        outs, rest = rest[:n_out], rest[n_out:]
        dsts, scratch = rest[:n_side], rest[n_side:]
        body(*ins, *outs, *scratch)
        for src, dst in zip(srcs, dsts):
            dst[...] = src[...].astype(BF16)
    return kernel


def _dft_tables():
    k = np.arange(HALF_SEQ, dtype=np.int64)
    scale = 1.0 / np.sqrt(SEQ)
    halves = []
    for p in (0, 1):
        n = 2 * np.arange(HALF_SEQ, dtype=np.int64) + p
        ang = 2.0 * np.pi * ((k[:, None] * n[None, :]) % SEQ).astype(np.float64) / SEQ
        halves.append(np.concatenate([np.cos(ang) * scale, -np.sin(ang) * scale], axis=1))
    pos = np.stack(halves)
    c = np.arange(GROUP_DIM, dtype=np.int64)
    angc = 2.0 * np.pi * ((c[:, None] * c[None, :]) % GROUP_DIM).astype(np.float64) / GROUP_DIM
    scalec = 1.0 / np.sqrt(GROUP_DIM)
    chan = np.concatenate([np.cos(angc) * scalec, np.sin(angc) * scalec], axis=1)
    return pos.astype(np.float32), chan.astype(np.float32)


_POS_DFT, _CHAN_DFT = _dft_tables()


def _front_kernel(x_ref, g_ref, winf_ref, ws_ref, bs_ref, lng_ref, lnb_ref,
                  za_ref, sg_ref, gate_ref, win_ref):
    _cast_once((winf_ref, win_ref))
    ws = [ws_ref[g].astype(BF16) for g in range(N_GROUPS)]
    h = _rms(x_ref[...], g_ref[...]).astype(BF16)
    za_ref[...] = _dot(h, win_ref[:, 0:MIX_WIDTH]).astype(BF16)
    u = _gelu(_dot(h, win_ref[:, MIX_WIDTH:2 * MIX_WIDTH]))
    v = _gelu(_dot(h, win_ref[:, 2 * MIX_WIDTH:3 * MIX_WIDTH]))
    mu = jnp.mean(v, axis=-1, keepdims=True)
    vc = v - mu
    var = jnp.mean(vc * vc, axis=-1, keepdims=True)
    vln = (vc * lax.rsqrt(var + LN_EPS) * lng_ref[...] + lnb_ref[...]).astype(BF16)
    for c in range(x_ref.shape[0] // CHUNK):
        rows = slice(c * CHUNK, (c + 1) * CHUNK)
        for g in range(N_GROUPS):
            cols = slice(g * GROUP_DIM, (g + 1) * GROUP_DIM)
            sv = _dot(ws[g], vln[rows, cols]) + bs_ref[:, cols]
            sg_ref[rows, cols] = (u[rows, cols] * sv).astype(BF16)
    zg = _dot(h, win_ref[:, 3 * MIX_WIDTH:D_IN])
    gate_ref[...] = jax.nn.sigmoid(zg).astype(BF16)


def _mixer_front(l, x, g, w_in, w_s, b_full, ln_g, ln_b, side=(), tile=ROW_TILE):
    row = lambda w: pl.BlockSpec((tile, w), lambda i: (i, 0))
    steps = TOKENS // tile
    side_specs, side_shapes = _side_specs(side, steps)
    outs = pl.pallas_call(
        _with_side_casts(_front_kernel, 7, 3, len(side)),
        grid=(steps,),
        in_specs=[row(D_MODEL), _layer((1, D_MODEL), l), _layer((D_MODEL, D_IN), l),
                  _layer((N_GROUPS, CHUNK, CHUNK), l), _layer((CHUNK, MIX_WIDTH), l),
                  _layer((1, MIX_WIDTH), l), _layer((1, MIX_WIDTH), l)] + side_specs,
        out_specs=[row(MIX_WIDTH), row(MIX_WIDTH), row(2 * D_MODEL)] + side_specs,
        out_shape=[jax.ShapeDtypeStruct((TOKENS, MIX_WIDTH), BF16),
                   jax.ShapeDtypeStruct((TOKENS, MIX_WIDTH), BF16),
                   jax.ShapeDtypeStruct((TOKENS, 2 * D_MODEL), BF16)] + side_shapes,
        scratch_shapes=[pltpu.VMEM((D_MODEL, D_IN), BF16)],
        compiler_params=_params("arbitrary"),
        name="mixer_front",
    )(x, g, w_in, w_s, b_full, ln_g, ln_b, *side)
    return outs[:3], outs[3:]


def _fourier_kernel(za_ref, chan_ref, pos_ref, out_ref, stage_ref, rhs_ref):
    @pl.when(pl.program_id(1) == 0)
    def _():
        for g in range(N_GROUPS):
            cols = slice(g * GROUP_DIM, (g + 1) * GROUP_DIM)
            xcs = _dot(za_ref[0, :, cols], chan_ref[...])
            stage_ref[g] = xcs[:, 0:GROUP_DIM]
            stage_ref[N_GROUPS + g] = xcs[:, GROUP_DIM:2 * GROUP_DIM]
        for g in range(N_GROUPS):
            cols = slice(g * GROUP_DIM, (g + 1) * GROUP_DIM)
            for p in (0, 1):
                rows = pl.ds(p, HALF_SEQ, stride=2)
                rhs_ref[p, 0:HALF_SEQ, cols] = stage_ref[g, rows, :].astype(BF16)
                rhs_ref[p, HALF_SEQ:SEQ, cols] = stage_ref[N_GROUPS + g, rows, :].astype(BF16)

    even = _dot(pos_ref[0], rhs_ref[0])
    odd = _dot(pos_ref[1], rhs_ref[1])
    out_ref[0, 0] = (even + odd).astype(BF16)
    out_ref[0, 1] = (even - odd).astype(BF16)


def _fourier(za, chan, pos):
    out = pl.pallas_call(
        _fourier_kernel,
        grid=(BATCH, HALF_SEQ // FOURIER_ROW_TILE),
        in_specs=[pl.BlockSpec((1, SEQ, MIX_WIDTH), lambda b, i: (b, 0, 0)),
                  _resident((GROUP_DIM, 2 * GROUP_DIM)),
                  pl.BlockSpec((2, FOURIER_ROW_TILE, SEQ), lambda b, i: (0, i, 0))],
        out_specs=pl.BlockSpec((1, 2, FOURIER_ROW_TILE, MIX_WIDTH), lambda b, i: (b, 0, i, 0)),
        out_shape=jax.ShapeDtypeStruct((BATCH, 2, HALF_SEQ, MIX_WIDTH), BF16),
        scratch_shapes=[pltpu.VMEM((2 * N_GROUPS, SEQ, GROUP_DIM), F32),
                        pltpu.VMEM((2, SEQ, MIX_WIDTH), BF16)],
        compiler_params=_params("arbitrary", "arbitrary"),
        name="fourier_mix",
    )(za, chan, pos)
    return out.reshape(TOKENS, MIX_WIDTH)


def _tail_common(fa_ref, sg_ref, gate_ref, x_ref, wff_ref, wgf_ref, wof_ref, pg_ref,
                 wf_ref, wg_ref, wo_ref):
    _cast_once((wff_ref, wf_ref), (wgf_ref, wg_ref), (wof_ref, wo_ref))
    ya = _dot(fa_ref[...], wf_ref[...])
    yb = _dot(sg_ref[...], wg_ref[...])
    m = gate_ref[:, 0:D_MODEL].astype(F32) * ya + gate_ref[:, D_MODEL:2 * D_MODEL].astype(F32) * yb
    y = _dot(m.astype(BF16), wo_ref[...])
    return x_ref[...] + _rms(y, pg_ref[...])


def _tail_kernel(fa_ref, sg_ref, gate_ref, x_ref, wff_ref, wgf_ref, wof_ref, pg_ref, x1_ref,
                 wf_ref, wg_ref, wo_ref):
    x1_ref[...] = _tail_common(fa_ref, sg_ref, gate_ref, x_ref, wff_ref, wgf_ref, wof_ref, pg_ref,
                               wf_ref, wg_ref, wo_ref)


def _tail_router_kernel(fa_ref, sg_ref, gate_ref, x_ref, wff_ref, wgf_ref, wof_ref, pg_ref,
                        fg_ref, rwt_ref, rb_ref, tri_ref,
                        x1_ref, h_ref, gw_ref, sel_ref, rank_ref, cnt_ref,
                        wf_ref, wg_ref, wo_ref, carry_ref):
    @pl.when(pl.program_id(0) == 0)
    def _():
        carry_ref[...] = jnp.zeros_like(carry_ref)

    x1 = _tail_common(fa_ref, sg_ref, gate_ref, x_ref, wff_ref, wgf_ref, wof_ref, pg_ref,
                      wf_ref, wg_ref, wo_ref)
    x1_ref[...] = x1
    h = _rms(x1, fg_ref[...])
    h_hi, h_lo = _split_bf16(h)
    h_ref[...] = h_hi

    w_hi, w_lo = _split_bf16(rwt_ref[...])
    logits = _dot_nt(w_hi, h_hi) + _dot_nt(w_hi, h_lo) + _dot_nt(w_lo, h_hi) + rb_ref[...]
    row = lax.broadcasted_iota(I32, logits.shape, 0)
    m1 = jnp.max(logits, axis=0, keepdims=True)
    i1 = jnp.min(jnp.where(logits == m1, row, N_EXPERTS), axis=0, keepdims=True)
    rest = jnp.where(row == i1, -jnp.inf, logits)
    m2 = jnp.max(rest, axis=0, keepdims=True)
    i2 = jnp.min(jnp.where(rest == m2, row, N_EXPERTS), axis=0, keepdims=True)
    e2 = jnp.exp(m2 - m1)
    den = 1.0 + e2
    gw_ref[...] = jnp.where(row == i1, 1.0 / den, 0.0) + jnp.where(row == i2, e2 / den, 0.0)
    sel = jnp.where((row == i1) | (row == i2), 1.0, 0.0)
    sel_ref[...] = sel.astype(I32)
    before = _dot(sel.astype(BF16), tri_ref[...]) + carry_ref[...]
    rank_ref[...] = before.astype(I32)
    carry_ref[...] = carry_ref[...] + jnp.sum(sel, axis=1, keepdims=True)
    cnt_ref[...] = carry_ref[...].astype(I32)


def _tail_specs(l):
    row = lambda w: pl.BlockSpec((TAIL_TILE, w), lambda i: (i, 0))
    in_specs = [row(MIX_WIDTH), row(MIX_WIDTH), row(2 * D_MODEL), row(D_MODEL),
                _layer((MIX_WIDTH, D_MODEL), l), _layer((MIX_WIDTH, D_MODEL), l),
                _layer((D_MODEL, D_MODEL), l), _layer((1, D_MODEL), l)]
    weight_scratch = [pltpu.VMEM((MIX_WIDTH, D_MODEL), BF16), pltpu.VMEM((MIX_WIDTH, D_MODEL), BF16),
                      pltpu.VMEM((D_MODEL, D_MODEL), BF16)]
    return row, in_specs, weight_scratch


def _mixer_tail(l, fa, sg, gate, x, w_f, w_g, w_o, post_g, side=()):
    row, in_specs, weight_scratch = _tail_specs(l)
    steps = TOKENS // TAIL_TILE
    side_specs, side_shapes = _side_specs(side, steps)
    outs = pl.pallas_call(
        _with_side_casts(_tail_kernel, 8, 1, len(side)),
        grid=(steps,),
        in_specs=in_specs + side_specs,
        out_specs=[row(D_MODEL)] + side_specs,
        out_shape=[jax.ShapeDtypeStruct((TOKENS, D_MODEL), F32)] + side_shapes,
        scratch_shapes=weight_scratch,
        compiler_params=_params("arbitrary"),
        name="mixer_tail",
    )(fa, sg, gate, x, w_f, w_g, w_o, post_g, *side)
    return outs[0], outs[1:]


def _mixer_tail_router(l, fa, sg, gate, x, w_f, w_g, w_o, post_g, ffn_g, router_wt, router_b, tri):
    row, in_specs, weight_scratch = _tail_specs(l)
    in_specs += [_layer((1, D_MODEL), l), _layer((N_EXPERTS, D_MODEL), l // 2),
                 _layer((N_EXPERTS, 1), l // 2), _resident((TAIL_TILE, TAIL_TILE))]
    col = pl.BlockSpec((N_EXPERTS, TAIL_TILE), lambda i: (0, i))
    per_token = lambda dt: jax.ShapeDtypeStruct((N_EXPERTS, TOKENS), dt)
    return pl.pallas_call(
        _tail_router_kernel,
        grid=(TOKENS // TAIL_TILE,),
        in_specs=in_specs,
        out_specs=[row(D_MODEL), row(D_MODEL), col, col, col,
                   pl.BlockSpec((N_EXPERTS, 1), lambda i: (0, 0))],
        out_shape=[jax.ShapeDtypeStruct((TOKENS, D_MODEL), F32),
                   jax.ShapeDtypeStruct((TOKENS, D_MODEL), BF16),
                   per_token(F32), per_token(I32), per_token(I32),
                   jax.ShapeDtypeStruct((N_EXPERTS, 1), I32)],
        scratch_shapes=weight_scratch + [pltpu.VMEM((N_EXPERTS, 1), F32)],
        compiler_params=_params("arbitrary"),
        name="mixer_tail_router",
    )(fa, sg, gate, x, w_f, w_g, w_o, post_g, ffn_g, router_wt, router_b, tri)


def _dense_ffn_kernel(x_ref, fg_ref, w1_ref, w3_ref, w2_ref, pg_ref, out_ref):
    x = x_ref[...]
    h = _rms(x, fg_ref[...]).astype(BF16)
    a = _dot(h, w1_ref[...])
    b = _dot(h, w3_ref[...])
    y = _dot((jax.nn.silu(a) * b).astype(BF16), w2_ref[...])
    out_ref[...] = x + _rms(y, pg_ref[...])


def _dense_ffn(l, x, ffn_g, w1, w3, w2, post_g, side=()):
    row = pl.BlockSpec((ROW_TILE, D_MODEL), lambda i: (i, 0))
    steps = TOKENS // ROW_TILE
    side_specs, side_shapes = _side_specs(side, steps)
    outs = pl.pallas_call(
        _with_side_casts(_dense_ffn_kernel, 6, 1, len(side)),
        grid=(steps,),
        in_specs=[row, _layer((1, D_MODEL), l), _resident((D_MODEL, D_FF_DENSE)),
                  _resident((D_MODEL, D_FF_DENSE)), _resident((D_FF_DENSE, D_MODEL)),
                  _layer((1, D_MODEL), l)] + side_specs,
        out_specs=[row] + side_specs,
        out_shape=[jax.ShapeDtypeStruct((TOKENS, D_MODEL), F32)] + side_shapes,
        compiler_params=_params("arbitrary"),
        name="dense_ffn",
    )(x, ffn_g, w1, w3, w2, post_g, *side)
    return outs[0], outs[1:]


def _window_hits(k, block, start_ref, dest_ref):
    wanted = start_ref[block] + k * TOKEN_WINDOW
    begin = pl.multiple_of(jnp.minimum(wanted, TOKENS - TOKEN_WINDOW), LANES)
    tok = lax.broadcasted_iota(I32, (1, TOKEN_WINDOW), 1) + begin
    dest = jnp.where(tok >= wanted, dest_ref[:, pl.ds(begin, TOKEN_WINDOW)], -1)
    rows = lax.broadcasted_iota(I32, (SORT_BLOCK, TOKEN_WINDOW), 0) + block * SORT_BLOCK
    return begin, rows == dest


def _dispatch_kernel(exp_ref, start_ref, nwin_ref, dest_ref, gw_ref, h_ref, xs_ref, gs_ref):
    j = pl.program_id(0)

    def window(k):
        begin, hit = _window_hits(k, j, start_ref, dest_ref)
        onehot = jnp.where(hit, 1.0, 0.0).astype(BF16)
        picked = _dot(onehot, h_ref[pl.ds(begin, TOKEN_WINDOW), :]).astype(BF16)
        gw = gw_ref[:, pl.ds(begin, TOKEN_WINDOW)]
        return picked, jnp.sum(jnp.where(hit, gw, 0.0), axis=-1, keepdims=True)

    @pl.when(nwin_ref[j] == 0)
    def _():
        xs_ref[...] = jnp.zeros_like(xs_ref)
        gs_ref[...] = jnp.zeros_like(gs_ref)

    @pl.when(nwin_ref[j] > 0)
    def _():
        xs_ref[...], gs_ref[...] = window(0)

        def more(k, carry):
            picked, gsum = window(k)
            xs_ref[...] += picked
            gs_ref[...] += gsum
            return carry
        lax.fori_loop(1, nwin_ref[j], more, 0)


def _per_expert_row():
    return pl.BlockSpec((None, 1, TOKENS),
                        lambda j, e, s, n: (e[jnp.minimum(j, N_SORT_BLOCKS - 1)], 0, 0))


def _dispatch(plan, dest3, gw3, h):
    exp, start, nwin = plan
    grid_spec = pltpu.PrefetchScalarGridSpec(
        num_scalar_prefetch=3,
        grid=(N_SORT_BLOCKS,),
        in_specs=[_per_expert_row(), _per_expert_row(),
                  pl.BlockSpec((TOKENS, D_MODEL), lambda j, e, s, n: (0, 0),
                               pipeline_mode=pl.Buffered(1))],
        out_specs=[pl.BlockSpec((SORT_BLOCK, D_MODEL), lambda j, e, s, n: (j, 0)),
                   pl.BlockSpec((SORT_BLOCK, 1), lambda j, e, s, n: (j, 0))],
    )
    return pl.pallas_call(
        _dispatch_kernel,
        grid_spec=grid_spec,
        out_shape=[jax.ShapeDtypeStruct((SORTED_ROWS, D_MODEL), BF16),
                   jax.ShapeDtypeStruct((SORTED_ROWS, 1), F32)],
        compiler_params=_params("arbitrary"),
        name="moe_dispatch",
    )(exp, start, nwin, dest3, gw3, h)


def _expert_kernel(rexp_ref, nhalf_ref, xs_ref, gs_ref, w1_ref, w3_ref, w2_ref, ys_ref, acc_ref):
    q = pl.program_id(0)
    f = pl.program_id(1)
    last = pl.num_programs(1) - 1
    nhalf = nhalf_ref[q]

    @pl.when((q == 0) & (f == 0))
    def _():
        acc_ref[...] = jnp.zeros_like(acc_ref)

    def swiglu_part(n_blocks):
        for k in range(n_blocks):
            rows = slice(k * EXPERT_BLOCK, (k + 1) * EXPERT_BLOCK)
            x = xs_ref[rows, :]
            a = _dot(x, w1_ref[...])
            b = _dot(x, w3_ref[...])
            part = _dot((jax.nn.silu(a) * b).astype(BF16), w2_ref[...])
            acc_ref[rows, :] = jnp.where(f == 0, 0.0, acc_ref[rows, :]) + part

        @pl.when(f == last)
        def _():
            rows = slice(0, n_blocks * EXPERT_BLOCK)
            ys_ref[rows, :] = (acc_ref[rows, :] * gs_ref[rows, :]).astype(BF16)

    @pl.when(nhalf == 2)
    def _():
        swiglu_part(2)

    @pl.when(nhalf == 1)
    def _():
        swiglu_part(1)

    @pl.when((nhalf < 2) & (f == last))
    def _():
        ys_ref[EXPERT_BLOCK:EXPERT_REGION, :] = jnp.zeros((EXPERT_REGION - EXPERT_BLOCK, D_MODEL), BF16)

    @pl.when((nhalf == 0) & (f == last))
    def _():
        ys_ref[0:EXPERT_BLOCK, :] = jnp.zeros((EXPERT_BLOCK, D_MODEL), BF16)


def _experts(rexp, nhalf, xs, gs, w1, w3, w2):
    n_ff = D_FF_EXPERT // EXPERT_FF_TILE
    tile = lambda q, f: jnp.where(q % 2 == 0, f, n_ff - 1 - f)
    grid_spec = pltpu.PrefetchScalarGridSpec(
        num_scalar_prefetch=2,
        grid=(N_REGIONS, D_FF_EXPERT // EXPERT_FF_TILE),
        in_specs=[
            pl.BlockSpec((EXPERT_REGION, D_MODEL), lambda q, f, re, nh: (q, 0)),
            pl.BlockSpec((EXPERT_REGION, 1), lambda q, f, re, nh: (q, 0)),
            pl.BlockSpec((None, D_MODEL, EXPERT_FF_TILE), lambda q, f, re, nh: (re[q], 0, tile(q, f))),
            pl.BlockSpec((None, D_MODEL, EXPERT_FF_TILE), lambda q, f, re, nh: (re[q], 0, tile(q, f))),
            pl.BlockSpec((None, EXPERT_FF_TILE, D_MODEL), lambda q, f, re, nh: (re[q], tile(q, f), 0)),
        ],
        out_specs=pl.BlockSpec((EXPERT_REGION, D_MODEL), lambda q, f, re, nh: (q, 0)),
        scratch_shapes=[pltpu.VMEM((EXPERT_REGION, D_MODEL), F32)],
    )
    return pl.pallas_call(
        _expert_kernel,
        grid_spec=grid_spec,
        out_shape=jax.ShapeDtypeStruct((SORTED_ROWS, D_MODEL), BF16),
        compiler_params=_params("arbitrary", "arbitrary"),
        name="moe_experts",
    )(rexp, nhalf, xs, gs, w1, w3, w2)


def _combine_copies(wstart_ref, ys_hbm, buf_ref, sem_ref, tile, rnd, slot):
    copies = []
    for e in range(N_EXPERTS):
        wanted = wstart_ref[tile * N_EXPERTS + e] + rnd * COMBINE_WINDOW
        begin = pl.multiple_of(jnp.minimum(wanted, SORTED_ROWS - COMBINE_WINDOW), BF16_ROWS)
        copies.append(pltpu.make_async_copy(
            ys_hbm.at[pl.ds(begin, COMBINE_WINDOW), :],
            buf_ref.at[slot, pl.ds(e * COMBINE_WINDOW, COMBINE_WINDOW), :],
            sem_ref.at[slot, e]))
    return copies


def _combine_kernel(wstart_ref, nround_ref, dest_ref, ys_hbm, x_ref, pg_ref, out_ref,
                    buf_ref, sem_ref):
    i = pl.program_id(0)
    n_tiles = pl.num_programs(0)
    slot = lax.rem(i, 2)

    @pl.when(i == 0)
    def _():
        for c in _combine_copies(wstart_ref, ys_hbm, buf_ref, sem_ref, 0, 0, 0):
            c.start()

    @pl.when(i + 1 < n_tiles)
    def _():
        for c in _combine_copies(wstart_ref, ys_hbm, buf_ref, sem_ref, i + 1, 0, 1 - slot):
            c.start()

    def gather(rnd, fetch_slot):
        sub = lax.broadcasted_iota(I32, (COMBINE_WINDOW, 1), 0)
        pieces = []
        for e in range(N_EXPERTS):
            wanted = wstart_ref[i * N_EXPERTS + e] + rnd * COMBINE_WINDOW
            begin = jnp.minimum(wanted, SORTED_ROWS - COMBINE_WINDOW)
            rowid = sub + begin
            rowid = jnp.where(rowid >= wanted, rowid, -2)
            pieces.append(jnp.where(rowid == dest_ref[e:e + 1, :], 1.0, 0.0).astype(BF16))
        onehot = jnp.concatenate(pieces, axis=0)
        return _dot_tn(onehot, buf_ref[fetch_slot])

    for c in _combine_copies(wstart_ref, ys_hbm, buf_ref, sem_ref, i, 0, slot):
        c.wait()
    y = gather(0, slot)

    def extra(rnd, y):
        copies = _combine_copies(wstart_ref, ys_hbm, buf_ref, sem_ref, i, rnd, 2)
        for c in copies:
            c.start()
        for c in copies:
            c.wait()
        return y + gather(rnd, 2)

    y = lax.fori_loop(1, nround_ref[i], extra, y)
    out_ref[...] = x_ref[...] + _rms(y, pg_ref[...])


def _combine(l, wstart, nround, dest, ys, x, post_g):
    tile = lambda i, ws, nr: (i, 0)
    grid_spec = pltpu.PrefetchScalarGridSpec(
        num_scalar_prefetch=2,
        grid=(TOKENS // COMBINE_TILE,),
        in_specs=[
            pl.BlockSpec((N_EXPERTS, COMBINE_TILE), lambda i, ws, nr: (0, i)),
            pl.BlockSpec(memory_space=pl.ANY),
            pl.BlockSpec((COMBINE_TILE, D_MODEL), tile),
            pl.BlockSpec((None, 1, D_MODEL), lambda i, ws, nr: (l, 0, 0)),
        ],
        out_specs=pl.BlockSpec((COMBINE_TILE, D_MODEL), tile),
        scratch_shapes=[pltpu.VMEM((3, N_EXPERTS * COMBINE_WINDOW, D_MODEL), BF16),
                        pltpu.SemaphoreType.DMA((3, N_EXPERTS))],
    )
    return pl.pallas_call(
        _combine_kernel,
        grid_spec=grid_spec,
        out_shape=jax.ShapeDtypeStruct((TOKENS, D_MODEL), F32),
        compiler_params=_params("arbitrary"),
        name="moe_combine",
    )(wstart, nround, dest, ys, x, post_g)


def _routing_plan(sel, rank, counts):
    counts = counts.reshape(N_EXPERTS)
    padded = ((counts + EXPERT_BLOCK - 1) // EXPERT_BLOCK) * EXPERT_BLOCK
    region = ((counts + EXPERT_REGION - 1) // EXPERT_REGION) * EXPERT_REGION
    offs = jnp.cumsum(region) - region
    dest = jnp.where(sel == 1, rank + offs[:, None], -1)

    n_reg = region // EXPERT_REGION
    cum_reg = jnp.cumsum(n_reg)
    qs = jnp.arange(N_REGIONS, dtype=I32)
    qc = jnp.minimum(qs, cum_reg[-1] - 1)[:, None]
    owner = ((qc >= (cum_reg - n_reg)[None, :]) & (qc < cum_reg[None, :])).astype(I32)
    pick = lambda own, v: jnp.sum(own * v[None, :], axis=1)
    experts = jnp.arange(N_EXPERTS, dtype=I32)
    rexp = pick(owner, experts).astype(I32)
    rows_left = pick(owner, offs + padded) - qs * EXPERT_REGION
    n_half = jnp.clip(rows_left // EXPERT_BLOCK, 0, EXPERT_REGION // EXPERT_BLOCK).astype(I32)

    group_end = (rank + sel)[:, LANES - 1::LANES]
    js = jnp.arange(N_SORT_BLOCKS, dtype=I32)
    owner_s = jnp.repeat(owner, EXPERT_REGION // SORT_BLOCK, axis=0)
    e_s = pick(owner_s, experts)
    counts_s = pick(owner_s, counts)
    lo_row = js * SORT_BLOCK - pick(owner_s, offs)
    has_rows = lo_row < counts_s
    hi_row = jnp.minimum(lo_row + SORT_BLOCK, counts_s)
    ends = jnp.sum(owner_s[:, :, None] * group_end[None, :, :], axis=1)
    g_lo = jnp.sum((ends <= lo_row[:, None]).astype(I32), axis=1)
    g_hi = jnp.sum((ends < hi_row[:, None]).astype(I32), axis=1)
    span = (g_hi - g_lo + 1) * LANES
    n_win = jnp.where(has_rows, (span + TOKEN_WINDOW - 1) // TOKEN_WINDOW, 0).astype(I32)
    start = jnp.where(has_rows, g_lo * LANES, 0).astype(I32)
    d_plan = (e_s.astype(I32), start, n_win)

    tile_lo = rank[:, ::COMBINE_TILE]
    tile_hi = jnp.concatenate([tile_lo[:, 1:], counts[:, None]], axis=1)
    lo = tile_lo + offs[:, None]
    hi = tile_hi + offs[:, None]
    wstart = jnp.minimum((lo // BF16_ROWS) * BF16_ROWS, SORTED_ROWS - COMBINE_WINDOW)
    rounds = jnp.where(hi > lo, (hi - wstart + COMBINE_WINDOW - 1) // COMBINE_WINDOW, 0)
    n_round = jnp.maximum(jnp.max(rounds, axis=0), 1).astype(I32)
    c_plan = (wstart.T.reshape(-1).astype(I32), n_round)
    return dest, (rexp, n_half), d_plan, c_plan


def _moe(l, x1, h, gw, sel, rank, counts, w1, w3, w2, post_g):
    dest, e_plan, d_plan, c_plan = _routing_plan(sel, rank, counts)
    dest3 = dest.reshape(N_EXPERTS, 1, TOKENS)
    gw3 = gw.reshape(N_EXPERTS, 1, TOKENS)
    xs, gs = _dispatch(d_plan, dest3, gw3, h)
    ys = _experts(*e_plan, xs, gs, w1, w3, w2)
    return _combine(l, *c_plan, dest, ys, x1, post_g)


def kernel(x, pre_mix_g, post_mix_g, w_in, w_fourier, w_gmlp, w_out, w_spatial, b_spatial,
           gmlp_ln_g, gmlp_ln_b, pre_ffn_g, post_ffn_g, ffn_w1, ffn_w3, ffn_w2,
           router_w, router_b, moe_w1, moe_w3, moe_w2):
    pos = jnp.asarray(_POS_DFT).astype(BF16)
    chan = jnp.asarray(_CHAN_DFT).astype(BF16)
    tri = jnp.asarray(np.triu(np.ones((TAIL_TILE, TAIL_TILE), np.float32), 1)).astype(BF16)
    rows = lambda v: v.reshape(v.shape[0], 1, v.shape[1])
    pre_mix_g, post_mix_g, pre_ffn_g, post_ffn_g, gmlp_ln_g, gmlp_ln_b = map(
        rows, (pre_mix_g, post_mix_g, pre_ffn_g, post_ffn_g, gmlp_ln_g, gmlp_ln_b))
    b_full = jnp.repeat(jnp.swapaxes(b_spatial, 1, 2), GROUP_DIM, axis=2)
    router_wt = jnp.swapaxes(router_w, 1, 2)
    router_b = router_b.reshape(router_b.shape[0], N_EXPERTS, 1)

    assert DEPTH == 2
    moe_rows = lambda w: w[0].reshape(-1, w.shape[-1])
    moe_back = lambda w, like: w.reshape(like.shape[1:])

    xf = x.reshape(TOKENS, D_MODEL)
    (za, sg, gate), (moe1,) = _mixer_front(0, xf, pre_mix_g, w_in, w_spatial, b_full, gmlp_ln_g,
                                           gmlp_ln_b, side=(moe_rows(moe_w1),))
    fa = _fourier(za.reshape(BATCH, SEQ, MIX_WIDTH), chan, pos)
    x1, (ffn1, ffn3, ffn2) = _mixer_tail(
        0, fa, sg, gate, xf, w_fourier, w_gmlp, w_out, post_mix_g,
        side=(ffn_w1[0], ffn_w3[0], ffn_w2[0]))
    xf, (moe3, moe2) = _dense_ffn(0, x1, pre_ffn_g, ffn1, ffn3, ffn2, post_ffn_g,
                                  side=(moe_rows(moe_w3), moe_rows(moe_w2)))

    (za, sg, gate), _ = _mixer_front(1, xf, pre_mix_g, w_in, w_spatial, b_full, gmlp_ln_g,
                                     gmlp_ln_b, tile=TAIL_TILE)
    fa = _fourier(za.reshape(BATCH, SEQ, MIX_WIDTH), chan, pos)
    x1, h, gw, sel, rank, counts = _mixer_tail_router(
        1, fa, sg, gate, xf, w_fourier, w_gmlp, w_out, post_mix_g, pre_ffn_g, router_wt, router_b,
        tri)
    xf = _moe(1, x1, h, gw, sel, rank, counts, moe_back(moe1, moe_w1), moe_back(moe3, moe_w3),
              moe_back(moe2, moe_w2), post_ffn_g)
    return xf.reshape(BATCH, SEQ, D_MODEL)
```

```python
import functools

import numpy as np
import jax
import jax.numpy as jnp
from jax import lax
from jax.experimental import pallas as pl
from jax.experimental.pallas import tpu as pltpu

F32 = jnp.float32
BF16 = jnp.bfloat16
I32 = jnp.int32

D_MODEL = 1024
BATCH = 8
SEQ = 2048
HALF_SEQ = SEQ // 2
TOKENS = BATCH * SEQ
DEPTH = 2
N_GROUPS = 4
GROUP_DIM = 128
MIX_WIDTH = N_GROUPS * GROUP_DIM
CHUNK = 128
D_IN = 3 * MIX_WIDTH + 2 * D_MODEL
D_FF_DENSE = 2816
N_EXPERTS = 8
D_FF_EXPERT = 3584
RMS_EPS = 1e-6
LN_EPS = 1e-5
LANES = 128

ROW_TILE = 512
TAIL_TILE = 1024
FOURIER_ROW_TILE = 1024
EXPERT_BLOCK = 512
EXPERT_PAD = 256
EXPERT_REGION = 1024
SORT_BLOCK = 256
TOKEN_WINDOW = 1280
COMBINE_TILE = 256
COMBINE_WINDOW = 128
BF16_ROWS = 16
EXPERT_FF_TILE = 1792

SORTED_ROWS = 2 * TOKENS + N_EXPERTS * EXPERT_REGION
N_REGIONS = SORTED_ROWS // EXPERT_REGION
N_SORT_BLOCKS = SORTED_ROWS // SORT_BLOCK
N_COMBINE_TILES = TOKENS // COMBINE_TILE

VMEM_LIMIT = 56 * 1024 * 1024


def _params(*sem):
    return pltpu.CompilerParams(dimension_semantics=sem, vmem_limit_bytes=VMEM_LIMIT)


def _resident(shape):
    nd = len(shape)
    return pl.BlockSpec(shape, lambda *_: (0,) * nd, pipeline_mode=pl.Buffered(1))


def _layer(shape, l):
    nd = len(shape)
    return pl.BlockSpec((None,) + tuple(shape), lambda *_: (l,) + (0,) * nd,
                        pipeline_mode=pl.Buffered(1))


def _rms(x, g):
    return x * lax.rsqrt(jnp.mean(x * x, axis=-1, keepdims=True) + RMS_EPS) * g


def _dot(a, b):
    return jnp.dot(a, b, preferred_element_type=F32)


def _dot_nt(a, b):
    return lax.dot_general(a, b, (((1,), (1,)), ((), ())), preferred_element_type=F32)


def _dot_tn(a, b):
    return lax.dot_general(a, b, (((0,), (0,)), ((), ())), preferred_element_type=F32)


def _gelu(x):
    return 0.5 * x * (1.0 + lax.erf(x * np.float32(np.sqrt(0.5))))


def _cast_once(*pairs):
    chunk = 128

    @pl.when(pl.program_id(0) == 0)
    def _():
        for src_ref, dst_ref in pairs:
            def body(i, carry, src_ref=src_ref, dst_ref=dst_ref):
                rows = pl.ds(pl.multiple_of(i * chunk, chunk), chunk)
                dst_ref[rows, :] = src_ref[rows, :].astype(BF16)
                return carry
            lax.fori_loop(0, src_ref.shape[0] // chunk, body, 0)


def _split_bf16(x):
    hi = x.astype(BF16)
    return hi, (x - hi.astype(F32)).astype(BF16)


def _side_specs(arrays, steps):
    specs, shapes = [], []
    for a in arrays:
        rows, cols = a.shape
        chunk = rows // steps
        if chunk % BF16_ROWS == 0:
            spec = pl.BlockSpec((chunk, cols), lambda i: (i, 0))
        else:
            chunk *= 2
            spec = pl.BlockSpec((chunk, cols), lambda i: (i // 2, 0))
        assert rows % chunk == 0 and chunk % BF16_ROWS == 0, (a.shape, steps)
        specs.append(spec)
        shapes.append(jax.ShapeDtypeStruct(a.shape, BF16))
    return specs, shapes


def _with_side_casts(body, n_in, n_out, n_side):
    def kernel(*refs):
        ins, rest = refs[:n_in], refs[n_in:]
        srcs, rest = rest[:n_side], rest[n_side:]
        outs, rest = rest[:n_out], rest[n_out:]
        dsts, scratch = rest[:n_side], rest[n_side:]
        body(*ins, *outs, *scratch)
        for src, dst in zip(srcs, dsts):
            dst[...] = src[...].astype(BF16)
    return kernel


def _dft_tables():
    k = np.arange(HALF_SEQ, dtype=np.int64)
    scale = 1.0 / np.sqrt(SEQ)
    halves = []
    for p in (0, 1):
        n = 2 * np.arange(HALF_SEQ, dtype=np.int64) + p
        ang = 2.0 * np.pi * ((k[:, None] * n[None, :]) % SEQ).astype(np.float64) / SEQ
        halves.append(np.concatenate([np.cos(ang) * scale, -np.sin(ang) * scale], axis=1))
    pos = np.stack(halves)
    c = np.arange(GROUP_DIM, dtype=np.int64)
    angc = 2.0 * np.pi * ((c[:, None] * c[None, :]) % GROUP_DIM).astype(np.float64) / GROUP_DIM
    scalec = 1.0 / np.sqrt(GROUP_DIM)
    chan = np.concatenate([np.cos(angc) * scalec, np.sin(angc) * scalec], axis=1)
    return pos.astype(np.float32), chan.astype(np.float32)


_POS_DFT, _CHAN_DFT = _dft_tables()


def _front_kernel(x_ref, g_ref, winf_ref, ws_ref, bs_ref, lng_ref, lnb_ref,
                  za_ref, sg_ref, gate_ref, win_ref):
    _cast_once((winf_ref, win_ref))
    ws = [ws_ref[g].astype(BF16) for g in range(N_GROUPS)]
    h = _rms(x_ref[...], g_ref[...]).astype(BF16)
    za_ref[...] = _dot(h, win_ref[:, 0:MIX_WIDTH]).astype(BF16)
    u = _gelu(_dot(h, win_ref[:, MIX_WIDTH:2 * MIX_WIDTH]))
    v = _gelu(_dot(h, win_ref[:, 2 * MIX_WIDTH:3 * MIX_WIDTH]))
    mu = jnp.mean(v, axis=-1, keepdims=True)
    vc = v - mu
    var = jnp.mean(vc * vc, axis=-1, keepdims=True)
    vln = (vc * lax.rsqrt(var + LN_EPS) * lng_ref[...] + lnb_ref[...]).astype(BF16)
    for c in range(x_ref.shape[0] // CHUNK):
        rows = slice(c * CHUNK, (c + 1) * CHUNK)
        for g in range(N_GROUPS):
            cols = slice(g * GROUP_DIM, (g + 1) * GROUP_DIM)
            sv = _dot(ws[g], vln[rows, cols]) + bs_ref[:, cols]
            sg_ref[rows, cols] = (u[rows, cols] * sv).astype(BF16)
    zg = _dot(h, win_ref[:, 3 * MIX_WIDTH:D_IN])
    gate_ref[...] = jax.nn.sigmoid(zg).astype(BF16)


def _mixer_front(l, x, g, w_in, w_s, b_full, ln_g, ln_b, side=(), tile=ROW_TILE):
    row = lambda w: pl.BlockSpec((tile, w), lambda i: (i, 0))
    steps = TOKENS // tile
    side_specs, side_shapes = _side_specs(side, steps)
    outs = pl.pallas_call(
        _with_side_casts(_front_kernel, 7, 3, len(side)),
        grid=(steps,),
        in_specs=[row(D_MODEL), _layer((1, D_MODEL), l), _layer((D_MODEL, D_IN), l),
                  _layer((N_GROUPS, CHUNK, CHUNK), l), _layer((CHUNK, MIX_WIDTH), l),
                  _layer((1, MIX_WIDTH), l), _layer((1, MIX_WIDTH), l)] + side_specs,
        out_specs=[row(MIX_WIDTH), row(MIX_WIDTH), row(2 * D_MODEL)] + side_specs,
        out_shape=[jax.ShapeDtypeStruct((TOKENS, MIX_WIDTH), BF16),
                   jax.ShapeDtypeStruct((TOKENS, MIX_WIDTH), BF16),
                   jax.ShapeDtypeStruct((TOKENS, 2 * D_MODEL), BF16)] + side_shapes,
        scratch_shapes=[pltpu.VMEM((D_MODEL, D_IN), BF16)],
        compiler_params=_params("arbitrary"),
        name="mixer_front",
    )(x, g, w_in, w_s, b_full, ln_g, ln_b, *side)
    return outs[:3], outs[3:]


def _fourier_kernel(za_ref, chan_ref, pos_ref, out_ref, stage_ref, rhs_ref):
    def channel_stage():
        for g in range(N_GROUPS):
            cols = slice(g * GROUP_DIM, (g + 1) * GROUP_DIM)
            xcs = _dot(za_ref[0, :, cols], chan_ref[...])
            stage_ref[g] = xcs[:, 0:GROUP_DIM]
            stage_ref[N_GROUPS + g] = xcs[:, GROUP_DIM:2 * GROUP_DIM]
        for g in range(N_GROUPS):
            cols = slice(g * GROUP_DIM, (g + 1) * GROUP_DIM)
            for p in (0, 1):
                rows = pl.ds(p, HALF_SEQ, stride=2)
                rhs_ref[p, 0:HALF_SEQ, cols] = stage_ref[g, rows, :].astype(BF16)
                rhs_ref[p, HALF_SEQ:SEQ, cols] = stage_ref[N_GROUPS + g, rows, :].astype(BF16)

    if HALF_SEQ == FOURIER_ROW_TILE:
        channel_stage()
    else:
        pl.when(pl.program_id(1) == 0)(channel_stage)

    even = _dot(pos_ref[0], rhs_ref[0])
    odd = _dot(pos_ref[1], rhs_ref[1])
    out_ref[0, 0] = (even + odd).astype(BF16)
    out_ref[0, 1] = (even - odd).astype(BF16)


def _fourier(za, chan, pos):
    out = pl.pallas_call(
        _fourier_kernel,
        grid=(BATCH, HALF_SEQ // FOURIER_ROW_TILE),
        in_specs=[pl.BlockSpec((1, SEQ, MIX_WIDTH), lambda b, i: (b, 0, 0)),
                  _resident((GROUP_DIM, 2 * GROUP_DIM)),
                  pl.BlockSpec((2, FOURIER_ROW_TILE, SEQ), lambda b, i: (0, i, 0))],
        out_specs=pl.BlockSpec((1, 2, FOURIER_ROW_TILE, MIX_WIDTH), lambda b, i: (b, 0, i, 0)),
        out_shape=jax.ShapeDtypeStruct((BATCH, 2, HALF_SEQ, MIX_WIDTH), BF16),
        scratch_shapes=[pltpu.VMEM((2 * N_GROUPS, SEQ, GROUP_DIM), F32),
                        pltpu.VMEM((2, SEQ, MIX_WIDTH), BF16)],
        compiler_params=_params("arbitrary", "arbitrary"),
        name="fourier_mix",
    )(za, chan, pos)
    return out.reshape(TOKENS, MIX_WIDTH)


def _tail_common(fa_ref, sg_ref, gate_ref, x_ref, wff_ref, wgf_ref, wof_ref, pg_ref,
                 wf_ref, wg_ref, wo_ref):
    _cast_once((wff_ref, wf_ref), (wgf_ref, wg_ref), (wof_ref, wo_ref))
    ya = _dot(fa_ref[...], wf_ref[...])
    yb = _dot(sg_ref[...], wg_ref[...])
    m = gate_ref[:, 0:D_MODEL].astype(F32) * ya + gate_ref[:, D_MODEL:2 * D_MODEL].astype(F32) * yb
    y = _dot(m.astype(BF16), wo_ref[...])
    return x_ref[...] + _rms(y, pg_ref[...])


def _tail_kernel(fa_ref, sg_ref, gate_ref, x_ref, wff_ref, wgf_ref, wof_ref, pg_ref, x1_ref,
                 wf_ref, wg_ref, wo_ref):
    x1_ref[...] = _tail_common(fa_ref, sg_ref, gate_ref, x_ref, wff_ref, wgf_ref, wof_ref, pg_ref,
                               wf_ref, wg_ref, wo_ref)


def _tail_router_kernel(fa_ref, sg_ref, gate_ref, x_ref, wff_ref, wgf_ref, wof_ref, pg_ref,
                        fg_ref, rwt_ref, rb_ref, tri_ref,
                        x1_ref, h_ref, gw_ref, sel_ref, rank_ref, cnt_ref,
                        wf_ref, wg_ref, wo_ref, carry_ref):
    @pl.when(pl.program_id(0) == 0)
    def _():
        carry_ref[...] = jnp.zeros_like(carry_ref)

    x1 = _tail_common(fa_ref, sg_ref, gate_ref, x_ref, wff_ref, wgf_ref, wof_ref, pg_ref,
                      wf_ref, wg_ref, wo_ref)
    x1_ref[...] = x1
    h = _rms(x1, fg_ref[...])
    h_hi, h_lo = _split_bf16(h)
    h_ref[...] = h_hi

    w_hi, w_lo = _split_bf16(rwt_ref[...])
    logits = _dot_nt(w_hi, h_hi) + _dot_nt(w_hi, h_lo) + _dot_nt(w_lo, h_hi) + rb_ref[...]
    row = lax.broadcasted_iota(I32, logits.shape, 0)
    m1 = jnp.max(logits, axis=0, keepdims=True)
    i1 = jnp.min(jnp.where(logits == m1, row, N_EXPERTS), axis=0, keepdims=True)
    rest = jnp.where(row == i1, -jnp.inf, logits)
    m2 = jnp.max(rest, axis=0, keepdims=True)
    i2 = jnp.min(jnp.where(rest == m2, row, N_EXPERTS), axis=0, keepdims=True)
    e2 = jnp.exp(m2 - m1)
    den = 1.0 + e2
    gw_ref[...] = jnp.where(row == i1, 1.0 / den, 0.0) + jnp.where(row == i2, e2 / den, 0.0)
    sel = jnp.where((row == i1) | (row == i2), 1.0, 0.0)
    sel_ref[...] = sel.astype(I32)
    before = _dot(sel.astype(BF16), tri_ref[...]) + carry_ref[...]
    rank_ref[...] = before.astype(I32)
    carry_ref[...] = carry_ref[...] + jnp.sum(sel, axis=1, keepdims=True)
    cnt_ref[...] = carry_ref[...].astype(I32)


def _tail_specs(l):
    row = lambda w: pl.BlockSpec((TAIL_TILE, w), lambda i: (i, 0))
    in_specs = [row(MIX_WIDTH), row(MIX_WIDTH), row(2 * D_MODEL), row(D_MODEL),
                _layer((MIX_WIDTH, D_MODEL), l), _layer((MIX_WIDTH, D_MODEL), l),
                _layer((D_MODEL, D_MODEL), l), _layer((1, D_MODEL), l)]
    weight_scratch = [pltpu.VMEM((MIX_WIDTH, D_MODEL), BF16), pltpu.VMEM((MIX_WIDTH, D_MODEL), BF16),
                      pltpu.VMEM((D_MODEL, D_MODEL), BF16)]
    return row, in_specs, weight_scratch


def _mixer_tail(l, fa, sg, gate, x, w_f, w_g, w_o, post_g, side=()):
    row, in_specs, weight_scratch = _tail_specs(l)
    steps = TOKENS // TAIL_TILE
    side_specs, side_shapes = _side_specs(side, steps)
    outs = pl.pallas_call(
        _with_side_casts(_tail_kernel, 8, 1, len(side)),
        grid=(steps,),
        in_specs=in_specs + side_specs,
        out_specs=[row(D_MODEL)] + side_specs,
        out_shape=[jax.ShapeDtypeStruct((TOKENS, D_MODEL), F32)] + side_shapes,
        scratch_shapes=weight_scratch,
        compiler_params=_params("arbitrary"),
        name="mixer_tail",
    )(fa, sg, gate, x, w_f, w_g, w_o, post_g, *side)
    return outs[0], outs[1:]


def _mixer_tail_router(l, fa, sg, gate, x, w_f, w_g, w_o, post_g, ffn_g, router_wt, router_b, tri):
    row, in_specs, weight_scratch = _tail_specs(l)
    in_specs += [_layer((1, D_MODEL), l), _layer((N_EXPERTS, D_MODEL), l // 2),
                 _layer((N_EXPERTS, 1), l // 2), _resident((TAIL_TILE, TAIL_TILE))]
    col = pl.BlockSpec((N_EXPERTS, TAIL_TILE), lambda i: (0, i))
    per_token = lambda dt: jax.ShapeDtypeStruct((N_EXPERTS, TOKENS), dt)
    return pl.pallas_call(
        _tail_router_kernel,
        grid=(TOKENS // TAIL_TILE,),
        in_specs=in_specs,
        out_specs=[row(D_MODEL), row(D_MODEL), col, col, col,
                   pl.BlockSpec((N_EXPERTS, 1), lambda i: (0, 0))],
        out_shape=[jax.ShapeDtypeStruct((TOKENS, D_MODEL), F32),
                   jax.ShapeDtypeStruct((TOKENS, D_MODEL), BF16),
                   per_token(F32), per_token(I32), per_token(I32),
                   jax.ShapeDtypeStruct((N_EXPERTS, 1), I32)],
        scratch_shapes=weight_scratch + [pltpu.VMEM((N_EXPERTS, 1), F32)],
        compiler_params=_params("arbitrary"),
        name="mixer_tail_router",
    )(fa, sg, gate, x, w_f, w_g, w_o, post_g, ffn_g, router_wt, router_b, tri)


def _dense_ffn_kernel(x_ref, fg_ref, w1_ref, w3_ref, w2_ref, pg_ref, out_ref):
    x = x_ref[...]
    h = _rms(x, fg_ref[...]).astype(BF16)
    a = _dot(h, w1_ref[...])
    b = _dot(h, w3_ref[...])
    y = _dot((jax.nn.silu(a) * b).astype(BF16), w2_ref[...])
    out_ref[...] = x + _rms(y, pg_ref[...])


def _dense_ffn(l, x, ffn_g, w1, w3, w2, post_g, side=()):
    row = pl.BlockSpec((ROW_TILE, D_MODEL), lambda i: (i, 0))
    steps = TOKENS // ROW_TILE
    side_specs, side_shapes = _side_specs(side, steps)
    outs = pl.pallas_call(
        _with_side_casts(_dense_ffn_kernel, 6, 1, len(side)),
        grid=(steps,),
        in_specs=[row, _layer((1, D_MODEL), l), _resident((D_MODEL, D_FF_DENSE)),
                  _resident((D_MODEL, D_FF_DENSE)), _resident((D_FF_DENSE, D_MODEL)),
                  _layer((1, D_MODEL), l)] + side_specs,
        out_specs=[row] + side_specs,
        out_shape=[jax.ShapeDtypeStruct((TOKENS, D_MODEL), F32)] + side_shapes,
        compiler_params=_params("arbitrary"),
        name="dense_ffn",
    )(x, ffn_g, w1, w3, w2, post_g, *side)
    return outs[0], outs[1:]


def _window_hits(k, block, start_ref, dest_ref):
    wanted = start_ref[block] + k * TOKEN_WINDOW
    begin = pl.multiple_of(jnp.minimum(wanted, TOKENS - TOKEN_WINDOW), LANES)
    tok = lax.broadcasted_iota(I32, (1, TOKEN_WINDOW), 1) + begin
    dest = jnp.where(tok >= wanted, dest_ref[:, pl.ds(begin, TOKEN_WINDOW)], -1)
    rows = lax.broadcasted_iota(I32, (SORT_BLOCK, TOKEN_WINDOW), 0) + block * SORT_BLOCK
    return begin, rows == dest


def _dispatch_kernel(exp_ref, start_ref, nwin_ref, dest_ref, gw_ref, h_ref, xs_ref, gs_ref):
    j = pl.program_id(0)

    def window(k):
        begin, hit = _window_hits(k, j, start_ref, dest_ref)
        onehot = jnp.where(hit, 1.0, 0.0).astype(BF16)
        picked = _dot(onehot, h_ref[pl.ds(begin, TOKEN_WINDOW), :]).astype(BF16)
        gw = gw_ref[:, pl.ds(begin, TOKEN_WINDOW)]
        return picked, jnp.sum(jnp.where(hit, gw, 0.0), axis=-1, keepdims=True)

    @pl.when(nwin_ref[j] == 0)
    def _():
        xs_ref[...] = jnp.zeros_like(xs_ref)
        gs_ref[...] = jnp.zeros_like(gs_ref)

    @pl.when(nwin_ref[j] > 0)
    def _():
        xs_ref[...], gs_ref[...] = window(0)

        def more(k, carry):
            picked, gsum = window(k)
            xs_ref[...] += picked
            gs_ref[...] += gsum
            return carry
        lax.fori_loop(1, nwin_ref[j], more, 0)


def _per_expert_row():
    return pl.BlockSpec((None, 1, TOKENS),
                        lambda j, e, s, n: (e[jnp.minimum(j, N_SORT_BLOCKS - 1)], 0, 0))


def _dispatch(plan, dest3, gw3, h):
    exp, start, nwin = plan
    grid_spec = pltpu.PrefetchScalarGridSpec(
        num_scalar_prefetch=3,
        grid=(N_SORT_BLOCKS,),
        in_specs=[_per_expert_row(), _per_expert_row(),
                  pl.BlockSpec((TOKENS, D_MODEL), lambda j, e, s, n: (0, 0),
                               pipeline_mode=pl.Buffered(1))],
        out_specs=[pl.BlockSpec((SORT_BLOCK, D_MODEL), lambda j, e, s, n: (j, 0)),
                   pl.BlockSpec((SORT_BLOCK, 1), lambda j, e, s, n: (j, 0))],
    )
    return pl.pallas_call(
        _dispatch_kernel,
        grid_spec=grid_spec,
        out_shape=[jax.ShapeDtypeStruct((SORTED_ROWS, D_MODEL), BF16),
                   jax.ShapeDtypeStruct((SORTED_ROWS, 1), F32)],
        compiler_params=_params("arbitrary"),
        name="moe_dispatch",
    )(exp, start, nwin, dest3, gw3, h)


def _expert_kernel(rexp_ref, nunit_ref, xs_ref, gs_ref, w1_ref, w3_ref, w2_ref, ys_ref, acc_ref):
    q = pl.program_id(0)
    f = pl.program_id(1)
    last = pl.num_programs(1) - 1
    nunit = nunit_ref[q]

    @pl.when((q == 0) & (f == 0))
    def _():
        acc_ref[...] = jnp.zeros_like(acc_ref)

    def swiglu_part(n_rows):
        for start in range(0, n_rows, EXPERT_BLOCK):
            rows = slice(start, min(start + EXPERT_BLOCK, n_rows))
            x = xs_ref[rows, :]
            a = _dot(x, w1_ref[...])
            b = _dot(x, w3_ref[...])
            part = _dot((jax.nn.silu(a) * b).astype(BF16), w2_ref[...])
            acc_ref[rows, :] = jnp.where(f == 0, 0.0, acc_ref[rows, :]) + part

        @pl.when(f == last)
        def _():
            ys_ref[0:n_rows, :] = (acc_ref[0:n_rows, :] * gs_ref[0:n_rows, :]).astype(BF16)
            if n_rows < EXPERT_REGION:
                ys_ref[n_rows:EXPERT_REGION, :] = jnp.zeros((EXPERT_REGION - n_rows, D_MODEL), BF16)

    for units in range(1, EXPERT_REGION // EXPERT_PAD + 1):
        pl.when(nunit == units)(functools.partial(swiglu_part, units * EXPERT_PAD))

    @pl.when((nunit == 0) & (f == last))
    def _():
        ys_ref[...] = jnp.zeros_like(ys_ref)


def _experts(rexp, nunit, xs, gs, w1, w3, w2):
    n_ff = D_FF_EXPERT // EXPERT_FF_TILE
    tile = lambda q, f: jnp.where(q % 2 == 0, f, n_ff - 1 - f)
    grid_spec = pltpu.PrefetchScalarGridSpec(
        num_scalar_prefetch=2,
        grid=(N_REGIONS, D_FF_EXPERT // EXPERT_FF_TILE),
        in_specs=[
            pl.BlockSpec((EXPERT_REGION, D_MODEL), lambda q, f, re, nh: (q, 0)),
            pl.BlockSpec((EXPERT_REGION, 1), lambda q, f, re, nh: (q, 0)),
            pl.BlockSpec((None, D_MODEL, EXPERT_FF_TILE), lambda q, f, re, nh: (re[q], 0, tile(q, f))),
            pl.BlockSpec((None, D_MODEL, EXPERT_FF_TILE), lambda q, f, re, nh: (re[q], 0, tile(q, f))),
            pl.BlockSpec((None, EXPERT_FF_TILE, D_MODEL), lambda q, f, re, nh: (re[q], tile(q, f), 0)),
        ],
        out_specs=pl.BlockSpec((EXPERT_REGION, D_MODEL), lambda q, f, re, nh: (q, 0)),
        scratch_shapes=[pltpu.VMEM((EXPERT_REGION, D_MODEL), F32)],
    )
    return pl.pallas_call(
        _expert_kernel,
        grid_spec=grid_spec,
        out_shape=jax.ShapeDtypeStruct((SORTED_ROWS, D_MODEL), BF16),
        compiler_params=_params("arbitrary", "arbitrary"),
        name="moe_experts",
    )(rexp, nunit, xs, gs, w1, w3, w2)


def _combine_copies(wstart_ref, ys_hbm, buf_ref, sem_ref, tile, rnd, slot):
    copies = []
    for e in range(N_EXPERTS):
        wanted = wstart_ref[tile * N_EXPERTS + e] + rnd * COMBINE_WINDOW
        begin = pl.multiple_of(jnp.minimum(wanted, SORTED_ROWS - COMBINE_WINDOW), BF16_ROWS)
        copies.append(pltpu.make_async_copy(
            ys_hbm.at[pl.ds(begin, COMBINE_WINDOW), :],
            buf_ref.at[slot, pl.ds(e * COMBINE_WINDOW, COMBINE_WINDOW), :],
            sem_ref.at[slot, e]))
    return copies


def _combine_kernel(wstart_ref, nround_ref, dest_ref, ys_hbm, x_ref, pg_ref, out_ref,
                    buf_ref, sem_ref):
    i = pl.program_id(0)
    n_tiles = pl.num_programs(0)
    slot = lax.rem(i, 2)

    @pl.when(i == 0)
    def _():
        for c in _combine_copies(wstart_ref, ys_hbm, buf_ref, sem_ref, 0, 0, 0):
            c.start()

    @pl.when(i + 1 < n_tiles)
    def _():
        for c in _combine_copies(wstart_ref, ys_hbm, buf_ref, sem_ref, i + 1, 0, 1 - slot):
            c.start()

    def gather(rnd, fetch_slot):
        sub = lax.broadcasted_iota(I32, (COMBINE_WINDOW, 1), 0)
        pieces = []
        for e in range(N_EXPERTS):
            wanted = wstart_ref[i * N_EXPERTS + e] + rnd * COMBINE_WINDOW
            begin = jnp.minimum(wanted, SORTED_ROWS - COMBINE_WINDOW)
            rowid = sub + begin
            rowid = jnp.where(rowid >= wanted, rowid, -2)
            pieces.append(jnp.where(rowid == dest_ref[e:e + 1, :], 1.0, 0.0).astype(BF16))
        onehot = jnp.concatenate(pieces, axis=0)
        return _dot_tn(onehot, buf_ref[fetch_slot])

    for c in _combine_copies(wstart_ref, ys_hbm, buf_ref, sem_ref, i, 0, slot):
        c.wait()
    y = gather(0, slot)

    def extra(rnd, y):
        copies = _combine_copies(wstart_ref, ys_hbm, buf_ref, sem_ref, i, rnd, 2)
        for c in copies:
            c.start()
        for c in copies:
            c.wait()
        return y + gather(rnd, 2)

    y = lax.fori_loop(1, nround_ref[i], extra, y)
    out_ref[...] = x_ref[...] + _rms(y, pg_ref[...])


def _combine(l, wstart, nround, dest, ys, x, post_g):
    tile = lambda i, ws, nr: (i, 0)
    grid_spec = pltpu.PrefetchScalarGridSpec(
        num_scalar_prefetch=2,
        grid=(TOKENS // COMBINE_TILE,),
        in_specs=[
            pl.BlockSpec((N_EXPERTS, COMBINE_TILE), lambda i, ws, nr: (0, i)),
            pl.BlockSpec(memory_space=pl.ANY),
            pl.BlockSpec((COMBINE_TILE, D_MODEL), tile),
            pl.BlockSpec((None, 1, D_MODEL), lambda i, ws, nr: (l, 0, 0)),
        ],
        out_specs=pl.BlockSpec((COMBINE_TILE, D_MODEL), tile),
        scratch_shapes=[pltpu.VMEM((3, N_EXPERTS * COMBINE_WINDOW, D_MODEL), BF16),
                        pltpu.SemaphoreType.DMA((3, N_EXPERTS))],
    )
    return pl.pallas_call(
        _combine_kernel,
        grid_spec=grid_spec,
        out_shape=jax.ShapeDtypeStruct((TOKENS, D_MODEL), F32),
        compiler_params=_params("arbitrary"),
        name="moe_combine",
    )(wstart, nround, dest, ys, x, post_g)


def _routing_plan(sel, rank, counts):
    counts = counts.reshape(N_EXPERTS)
    padded = ((counts + EXPERT_PAD - 1) // EXPERT_PAD) * EXPERT_PAD
    region = ((counts + EXPERT_REGION - 1) // EXPERT_REGION) * EXPERT_REGION
    offs = jnp.cumsum(region) - region
    dest = jnp.where(sel == 1, rank + offs[:, None], -1)

    n_reg = region // EXPERT_REGION
    cum_reg = jnp.cumsum(n_reg)
    qs = jnp.arange(N_REGIONS, dtype=I32)
    qc = jnp.minimum(qs, cum_reg[-1] - 1)[:, None]
    owner = ((qc >= (cum_reg - n_reg)[None, :]) & (qc < cum_reg[None, :])).astype(I32)
    pick = lambda own, v: jnp.sum(own * v[None, :], axis=1)
    experts = jnp.arange(N_EXPERTS, dtype=I32)
    rexp = pick(owner, experts).astype(I32)
    rows_left = pick(owner, offs + padded) - qs * EXPERT_REGION
    n_unit = jnp.clip(rows_left // EXPERT_PAD, 0, EXPERT_REGION // EXPERT_PAD).astype(I32)

    group_end = (rank + sel)[:, LANES - 1::LANES]
    js = jnp.arange(N_SORT_BLOCKS, dtype=I32)
    owner_s = jnp.repeat(owner, EXPERT_REGION // SORT_BLOCK, axis=0)
    e_s = pick(owner_s, experts)
    counts_s = pick(owner_s, counts)
    lo_row = js * SORT_BLOCK - pick(owner_s, offs)
    has_rows = lo_row < counts_s
    hi_row = jnp.minimum(lo_row + SORT_BLOCK, counts_s)
    ends = jnp.sum(owner_s[:, :, None] * group_end[None, :, :], axis=1)
    g_lo = jnp.sum((ends <= lo_row[:, None]).astype(I32), axis=1)
    g_hi = jnp.sum((ends < hi_row[:, None]).astype(I32), axis=1)
    span = (g_hi - g_lo + 1) * LANES
    n_win = jnp.where(has_rows, (span + TOKEN_WINDOW - 1) // TOKEN_WINDOW, 0).astype(I32)
    start = jnp.where(has_rows, g_lo * LANES, 0).astype(I32)
    d_plan = (e_s.astype(I32), start, n_win)

    tile_lo = rank[:, ::COMBINE_TILE]
    tile_hi = jnp.concatenate([tile_lo[:, 1:], counts[:, None]], axis=1)
    lo = tile_lo + offs[:, None]
    hi = tile_hi + offs[:, None]
    wstart = jnp.minimum((lo // BF16_ROWS) * BF16_ROWS, SORTED_ROWS - COMBINE_WINDOW)
    rounds = jnp.where(hi > lo, (hi - wstart + COMBINE_WINDOW - 1) // COMBINE_WINDOW, 0)
    n_round = jnp.maximum(jnp.max(rounds, axis=0), 1).astype(I32)
    c_plan = (wstart.T.reshape(-1).astype(I32), n_round)
    return dest, (rexp, n_unit), d_plan, c_plan


def _moe(l, x1, h, gw, sel, rank, counts, w1, w3, w2, post_g):
    dest, e_plan, d_plan, c_plan = _routing_plan(sel, rank, counts)
    dest3 = dest.reshape(N_EXPERTS, 1, TOKENS)
    gw3 = gw.reshape(N_EXPERTS, 1, TOKENS)
    xs, gs = _dispatch(d_plan, dest3, gw3, h)
    ys = _experts(*e_plan, xs, gs, w1, w3, w2)
    return _combine(l, *c_plan, dest, ys, x1, post_g)


def kernel(x, pre_mix_g, post_mix_g, w_in, w_fourier, w_gmlp, w_out, w_spatial, b_spatial,
           gmlp_ln_g, gmlp_ln_b, pre_ffn_g, post_ffn_g, ffn_w1, ffn_w3, ffn_w2,
           router_w, router_b, moe_w1, moe_w3, moe_w2):
    pos = jnp.asarray(_POS_DFT).astype(BF16)
    chan = jnp.asarray(_CHAN_DFT).astype(BF16)
    tri = jnp.asarray(np.triu(np.ones((TAIL_TILE, TAIL_TILE), np.float32), 1)).astype(BF16)
    rows = lambda v: v.reshape(v.shape[0], 1, v.shape[1])
    pre_mix_g, post_mix_g, pre_ffn_g, post_ffn_g, gmlp_ln_g, gmlp_ln_b = map(
        rows, (pre_mix_g, post_mix_g, pre_ffn_g, post_ffn_g, gmlp_ln_g, gmlp_ln_b))
    b_full = jnp.repeat(jnp.swapaxes(b_spatial, 1, 2), GROUP_DIM, axis=2)
    router_wt = jnp.swapaxes(router_w, 1, 2)
    router_b = router_b.reshape(router_b.shape[0], N_EXPERTS, 1)

    assert DEPTH == 2
    moe_rows = lambda w: w[0].reshape(-1, w.shape[-1])
    moe_back = lambda w, like: w.reshape(like.shape[1:])

    xf = x.reshape(TOKENS, D_MODEL)
    (za, sg, gate), (moe1,) = _mixer_front(0, xf, pre_mix_g, w_in, w_spatial, b_full, gmlp_ln_g,
                                           gmlp_ln_b, side=(moe_rows(moe_w1),))
    fa = _fourier(za.reshape(BATCH, SEQ, MIX_WIDTH), chan, pos)
    x1, (ffn1, ffn3, ffn2) = _mixer_tail(
        0, fa, sg, gate, xf, w_fourier, w_gmlp, w_out, post_mix_g,
        side=(ffn_w1[0], ffn_w3[0], ffn_w2[0]))
    xf, (moe3, moe2) = _dense_ffn(0, x1, pre_ffn_g, ffn1, ffn3, ffn2, post_ffn_g,
                                  side=(moe_rows(moe_w3), moe_rows(moe_w2)))

    (za, sg, gate), _ = _mixer_front(1, xf, pre_mix_g, w_in, w_spatial, b_full, gmlp_ln_g,
                                     gmlp_ln_b, tile=TAIL_TILE)
    fa = _fourier(za.reshape(BATCH, SEQ, MIX_WIDTH), chan, pos)
    x1, h, gw, sel, rank, counts = _mixer_tail_router(
        1, fa, sg, gate, xf, w_fourier, w_gmlp, w_out, post_mix_g, pre_ffn_g, router_wt, router_b,
        tri)
    xf = _moe(1, x1, h, gw, sel, rank, counts, moe_back(moe1, moe_w1), moe_back(moe3, moe_w3),
              moe_back(moe2, moe_w2), post_ffn_g)
    return xf.reshape(BATCH, SEQ, D_MODEL)
```

```python
import functools

import numpy as np
import jax
import jax.numpy as jnp
from jax import lax
from jax.experimental import pallas as pl
from jax.experimental.pallas import tpu as pltpu

F32 = jnp.float32
BF16 = jnp.bfloat16
I32 = jnp.int32

D_MODEL = 1024
BATCH = 8
SEQ = 2048
HALF_SEQ = SEQ // 2
TOKENS = BATCH * SEQ
DEPTH = 2
N_GROUPS = 4
GROUP_DIM = 128
MIX_WIDTH = N_GROUPS * GROUP_DIM
CHUNK = 128
D_IN = 3 * MIX_WIDTH + 2 * D_MODEL
D_FF_DENSE = 2816
N_EXPERTS = 8
D_FF_EXPERT = 3584
RMS_EPS = 1e-6
LN_EPS = 1e-5
LANES = 128

ROW_TILE = 512
MIX_TILE = 512
ZA_TILE = 1024
FOURIER_ROW_TILE = 1024
EXPERT_BLOCK = 512
EXPERT_PAD = 256
EXPERT_REGION = 1024
SORT_BLOCK = 256
TOKEN_WINDOW = 1280
COMBINE_TILE = 256
COMBINE_WINDOW = 128
BF16_ROWS = 16
EXPERT_FF_TILE = 1792

SORTED_ROWS = 2 * TOKENS + N_EXPERTS * EXPERT_REGION
N_REGIONS = SORTED_ROWS // EXPERT_REGION
N_SORT_BLOCKS = SORTED_ROWS // SORT_BLOCK
N_COMBINE_TILES = TOKENS // COMBINE_TILE

VMEM_LIMIT = 56 * 1024 * 1024


def _params(*sem):
    return pltpu.CompilerParams(dimension_semantics=sem, vmem_limit_bytes=VMEM_LIMIT)


def _resident(shape):
    nd = len(shape)
    return pl.BlockSpec(shape, lambda *_: (0,) * nd, pipeline_mode=pl.Buffered(1))


def _layer(shape, l):
    nd = len(shape)
    return pl.BlockSpec((None,) + tuple(shape), lambda *_: (l,) + (0,) * nd,
                        pipeline_mode=pl.Buffered(1))


def _rms(x, g):
    return x * lax.rsqrt(jnp.mean(x * x, axis=-1, keepdims=True) + RMS_EPS) * g


def _dot(a, b):
    return jnp.dot(a, b, preferred_element_type=F32)


def _dot_nt(a, b):
    return lax.dot_general(a, b, (((1,), (1,)), ((), ())), preferred_element_type=F32)


def _dot_tn(a, b):
    return lax.dot_general(a, b, (((0,), (0,)), ((), ())), preferred_element_type=F32)


def _gelu(x):
    return 0.5 * x * (1.0 + lax.erf(x * np.float32(np.sqrt(0.5))))


def _cast_once(*pairs):
    chunk = 128

    @pl.when(pl.program_id(0) == 0)
    def _():
        for src_ref, dst_ref in pairs:
            def body(i, carry, src_ref=src_ref, dst_ref=dst_ref):
                rows = pl.ds(pl.multiple_of(i * chunk, chunk), chunk)
                dst_ref[rows, :] = src_ref[rows, :].astype(BF16)
                return carry
            lax.fori_loop(0, src_ref.shape[0] // chunk, body, 0)


def _split_bf16(x):
    hi = x.astype(BF16)
    return hi, (x - hi.astype(F32)).astype(BF16)


def _side_specs(arrays, steps):
    in_specs, out_specs, shapes, operands = [], [], [], []
    for entry in arrays:
        a, l = entry if isinstance(entry, tuple) else (entry, None)
        rows, cols = a.shape[-2:]
        chunk, per = rows // steps, 1
        if chunk % BF16_ROWS:
            chunk, per = 2 * chunk, 2
        assert rows % chunk == 0 and chunk % BF16_ROWS == 0, (a.shape, steps)
        out_specs.append(pl.BlockSpec((chunk, cols), lambda i, per=per: (i // per, 0)))
        if l is None:
            in_specs.append(out_specs[-1])
        else:
            in_specs.append(pl.BlockSpec((None, chunk, cols), lambda i, per=per, l=l: (l, i // per, 0)))
        shapes.append(jax.ShapeDtypeStruct((rows, cols), BF16))
        operands.append(a)
    return in_specs, out_specs, shapes, operands


def _with_side_casts(body, n_in, n_out, n_side):
    def kernel(*refs):
        ins, rest = refs[:n_in], refs[n_in:]
        srcs, rest = rest[:n_side], rest[n_side:]
        outs, rest = rest[:n_out], rest[n_out:]
        dsts, scratch = rest[:n_side], rest[n_side:]
        body(*ins, *outs, *scratch)
        for src, dst in zip(srcs, dsts):
            dst[...] = src[...].astype(BF16)
    return kernel


def _dft_tables():
    k = np.arange(HALF_SEQ, dtype=np.int64)
    scale = 1.0 / np.sqrt(SEQ)
    halves = []
    for p in (0, 1):
        n = 2 * np.arange(HALF_SEQ, dtype=np.int64) + p
        ang = 2.0 * np.pi * ((k[:, None] * n[None, :]) % SEQ).astype(np.float64) / SEQ
        halves.append(np.concatenate([np.cos(ang) * scale, -np.sin(ang) * scale], axis=1))
    pos = np.stack(halves)
    c = np.arange(GROUP_DIM, dtype=np.int64)
    angc = 2.0 * np.pi * ((c[:, None] * c[None, :]) % GROUP_DIM).astype(np.float64) / GROUP_DIM
    scalec = 1.0 / np.sqrt(GROUP_DIM)
    chan = np.concatenate([np.cos(angc) * scalec, np.sin(angc) * scalec], axis=1)
    return pos.astype(np.float32), chan.astype(np.float32)


_POS_DFT, _CHAN_DFT = _dft_tables()


def _za_kernel(x_ref, g_ref, wf32_ref, za_ref, w_ref):
    _cast_once((wf32_ref, w_ref))
    h = _rms(x_ref[...], g_ref[...]).astype(BF16)
    za_ref[...] = _dot(h, w_ref[...]).astype(BF16)


def _mixer_za(l, x, g, w_in, side=()):
    row = lambda w: pl.BlockSpec((ZA_TILE, w), lambda i: (i, 0))
    steps = TOKENS // ZA_TILE
    side_in, side_out, side_shapes, side_ops = _side_specs(side, steps)
    outs = pl.pallas_call(
        _with_side_casts(_za_kernel, 3, 1, len(side)),
        grid=(steps,),
        in_specs=[row(D_MODEL), _layer((1, D_MODEL), l), _layer((D_MODEL, MIX_WIDTH), l)] + side_in,
        out_specs=[row(MIX_WIDTH)] + side_out,
        out_shape=[jax.ShapeDtypeStruct((TOKENS, MIX_WIDTH), BF16)] + side_shapes,
        scratch_shapes=[pltpu.VMEM((D_MODEL, MIX_WIDTH), BF16)],
        compiler_params=_params("arbitrary"),
        name="mixer_za",
    )(x, g, w_in, *side_ops)
    return outs[0], outs[1:]


def _fourier_kernel(za_ref, chan_ref, pos_ref, out_ref, stage_ref, rhs_ref):
    def channel_stage():
        for g in range(N_GROUPS):
            cols = slice(g * GROUP_DIM, (g + 1) * GROUP_DIM)
            xcs = _dot(za_ref[0, :, cols], chan_ref[...])
            stage_ref[g] = xcs[:, 0:GROUP_DIM]
            stage_ref[N_GROUPS + g] = xcs[:, GROUP_DIM:2 * GROUP_DIM]
        for g in range(N_GROUPS):
            cols = slice(g * GROUP_DIM, (g + 1) * GROUP_DIM)
            for p in (0, 1):
                rows = pl.ds(p, HALF_SEQ, stride=2)
                rhs_ref[p, 0:HALF_SEQ, cols] = stage_ref[g, rows, :].astype(BF16)
                rhs_ref[p, HALF_SEQ:SEQ, cols] = stage_ref[N_GROUPS + g, rows, :].astype(BF16)

    if HALF_SEQ == FOURIER_ROW_TILE:
        channel_stage()
    else:
        pl.when(pl.program_id(1) == 0)(channel_stage)

    even = _dot(pos_ref[0], rhs_ref[0])
    odd = _dot(pos_ref[1], rhs_ref[1])
    out_ref[0, 0] = (even + odd).astype(BF16)
    out_ref[0, 1] = (even - odd).astype(BF16)


def _fourier(za, chan, pos):
    out = pl.pallas_call(
        _fourier_kernel,
        grid=(BATCH, HALF_SEQ // FOURIER_ROW_TILE),
        in_specs=[pl.BlockSpec((1, SEQ, MIX_WIDTH), lambda b, i: (b, 0, 0)),
                  _resident((GROUP_DIM, 2 * GROUP_DIM)),
                  pl.BlockSpec((2, FOURIER_ROW_TILE, SEQ), lambda b, i: (0, i, 0))],
        out_specs=pl.BlockSpec((1, 2, FOURIER_ROW_TILE, MIX_WIDTH), lambda b, i: (b, 0, i, 0)),
        out_shape=jax.ShapeDtypeStruct((BATCH, 2, HALF_SEQ, MIX_WIDTH), BF16),
        scratch_shapes=[pltpu.VMEM((2 * N_GROUPS, SEQ, GROUP_DIM), F32),
                        pltpu.VMEM((2, SEQ, MIX_WIDTH), BF16)],
        compiler_params=_params("arbitrary", "arbitrary"),
        name="fourier_mix",
    )(za, chan, pos)
    return out.reshape(TOKENS, MIX_WIDTH)


def _mix_common(x_ref, fa_ref, prg_ref, win_ref, ws_ref, bs_ref, lng_ref, lnb_ref,
                wf_ref, wg_ref, wo_ref, pg_ref, sg_ref):
    x = x_ref[...]
    h = _rms(x, prg_ref[...]).astype(BF16)
    u = _gelu(_dot(h, win_ref[:, MIX_WIDTH:2 * MIX_WIDTH]))
    v = _gelu(_dot(h, win_ref[:, 2 * MIX_WIDTH:3 * MIX_WIDTH]))
    mu = jnp.mean(v, axis=-1, keepdims=True)
    vc = v - mu
    var = jnp.mean(vc * vc, axis=-1, keepdims=True)
    vln = (vc * lax.rsqrt(var + LN_EPS) * lng_ref[...] + lnb_ref[...]).astype(BF16)
    ws = [ws_ref[g].astype(BF16) for g in range(N_GROUPS)]
    for c in range(x_ref.shape[0] // CHUNK):
        rows = slice(c * CHUNK, (c + 1) * CHUNK)
        for g in range(N_GROUPS):
            cols = slice(g * GROUP_DIM, (g + 1) * GROUP_DIM)
            sv = _dot(ws[g], vln[rows, cols]) + bs_ref[:, cols]
            sg_ref[rows, cols] = (u[rows, cols] * sv).astype(BF16)
    gate = jax.nn.sigmoid(_dot(h, win_ref[:, 3 * MIX_WIDTH:D_IN]))
    ya = _dot(fa_ref[...], wf_ref[...])
    yb = _dot(sg_ref[...], wg_ref[...])
    m = gate[:, 0:D_MODEL] * ya + gate[:, D_MODEL:2 * D_MODEL] * yb
    y = _dot(m.astype(BF16), wo_ref[...])
    return x + _rms(y, pg_ref[...])


def _mix_kernel(*refs):
    *ins, x1_ref, sg_ref = refs
    x1_ref[...] = _mix_common(*ins, sg_ref)


def _mix_router_kernel(*refs):
    (*ins, fg_ref, rwt_ref, rb_ref, tri_ref,
     x1_ref, h_ref, gw_ref, sel_ref, rank_ref, cnt_ref, sg_ref, carry_ref) = refs

    @pl.when(pl.program_id(0) == 0)
    def _():
        carry_ref[...] = jnp.zeros_like(carry_ref)

    x1 = _mix_common(*ins, sg_ref)
    x1_ref[...] = x1
    h = _rms(x1, fg_ref[...])
    h_hi, h_lo = _split_bf16(h)
    h_ref[...] = h_hi

    w_hi, w_lo = _split_bf16(rwt_ref[...])
    logits = _dot_nt(w_hi, h_hi) + _dot_nt(w_hi, h_lo) + _dot_nt(w_lo, h_hi) + rb_ref[...]
    row = lax.broadcasted_iota(I32, logits.shape, 0)
    m1 = jnp.max(logits, axis=0, keepdims=True)
    i1 = jnp.min(jnp.where(logits == m1, row, N_EXPERTS), axis=0, keepdims=True)
    rest = jnp.where(row == i1, -jnp.inf, logits)
    m2 = jnp.max(rest, axis=0, keepdims=True)
    i2 = jnp.min(jnp.where(rest == m2, row, N_EXPERTS), axis=0, keepdims=True)
    e2 = jnp.exp(m2 - m1)
    den = 1.0 + e2
    gw_ref[...] = jnp.where(row == i1, 1.0 / den, 0.0) + jnp.where(row == i2, e2 / den, 0.0)
    sel = jnp.where((row == i1) | (row == i2), 1.0, 0.0)
    sel_ref[...] = sel.astype(I32)
    before = _dot(sel.astype(BF16), tri_ref[...]) + carry_ref[...]
    rank_ref[...] = before.astype(I32)
    carry_ref[...] = carry_ref[...] + jnp.sum(sel, axis=1, keepdims=True)
    cnt_ref[...] = carry_ref[...].astype(I32)


def _mix_specs(l):
    row = lambda w: pl.BlockSpec((MIX_TILE, w), lambda i: (i, 0))
    in_specs = [row(D_MODEL), row(MIX_WIDTH), _layer((1, D_MODEL), l), _resident((D_MODEL, D_IN)),
                _layer((N_GROUPS, CHUNK, CHUNK), l), _layer((CHUNK, MIX_WIDTH), l),
                _layer((1, MIX_WIDTH), l), _layer((1, MIX_WIDTH), l),
                _resident((MIX_WIDTH, D_MODEL)), _resident((MIX_WIDTH, D_MODEL)),
                _resident((D_MODEL, D_MODEL)), _layer((1, D_MODEL), l)]
    return row, in_specs, [pltpu.VMEM((MIX_TILE, MIX_WIDTH), BF16)]


def _mixer_body(l, x, fa, pre_g, w_in, w_s, b_full, ln_g, ln_b, w_f, w_g, w_o, post_g, side=()):
    row, in_specs, scratch = _mix_specs(l)
    steps = TOKENS // MIX_TILE
    side_in, side_out, side_shapes, side_ops = _side_specs(side, steps)
    outs = pl.pallas_call(
        _with_side_casts(_mix_kernel, len(in_specs), 1, len(side)),
        grid=(steps,),
        in_specs=in_specs + side_in,
        out_specs=[row(D_MODEL)] + side_out,
        out_shape=[jax.ShapeDtypeStruct((TOKENS, D_MODEL), F32)] + side_shapes,
        scratch_shapes=scratch,
        compiler_params=_params("arbitrary"),
        name="mixer_body",
    )(x, fa, pre_g, w_in, w_s, b_full, ln_g, ln_b, w_f, w_g, w_o, post_g, *side_ops)
    return outs[0], outs[1:]


def _mixer_body_router(l, x, fa, pre_g, w_in, w_s, b_full, ln_g, ln_b, w_f, w_g, w_o, post_g,
                       ffn_g, router_wt, router_b, tri):
    row, in_specs, scratch = _mix_specs(l)
    in_specs += [_layer((1, D_MODEL), l), _layer((N_EXPERTS, D_MODEL), l // 2),
                 _layer((N_EXPERTS, 1), l // 2), _resident((MIX_TILE, MIX_TILE))]
    col = pl.BlockSpec((N_EXPERTS, MIX_TILE), lambda i: (0, i))
    per_token = lambda dt: jax.ShapeDtypeStruct((N_EXPERTS, TOKENS), dt)
    return pl.pallas_call(
        _mix_router_kernel,
        grid=(TOKENS // MIX_TILE,),
        in_specs=in_specs,
        out_specs=[row(D_MODEL), row(D_MODEL), col, col, col,
                   pl.BlockSpec((N_EXPERTS, 1), lambda i: (0, 0))],
        out_shape=[jax.ShapeDtypeStruct((TOKENS, D_MODEL), F32),
                   jax.ShapeDtypeStruct((TOKENS, D_MODEL), BF16),
                   per_token(F32), per_token(I32), per_token(I32),
                   jax.ShapeDtypeStruct((N_EXPERTS, 1), I32)],
        scratch_shapes=scratch + [pltpu.VMEM((N_EXPERTS, 1), F32)],
        compiler_params=_params("arbitrary"),
        name="mixer_body_router",
    )(x, fa, pre_g, w_in, w_s, b_full, ln_g, ln_b, w_f, w_g, w_o, post_g,
      ffn_g, router_wt, router_b, tri)


def _dense_ffn_kernel(x_ref, fg_ref, w1_ref, w3_ref, w2_ref, pg_ref, out_ref):
    x = x_ref[...]
    h = _rms(x, fg_ref[...]).astype(BF16)
    a = _dot(h, w1_ref[...])
    b = _dot(h, w3_ref[...])
    y = _dot((jax.nn.silu(a) * b).astype(BF16), w2_ref[...])
    out_ref[...] = x + _rms(y, pg_ref[...])


def _dense_ffn(l, x, ffn_g, w1, w3, w2, post_g, side=()):
    row = pl.BlockSpec((ROW_TILE, D_MODEL), lambda i: (i, 0))
    steps = TOKENS // ROW_TILE
    side_in, side_out, side_shapes, side_ops = _side_specs(side, steps)
    outs = pl.pallas_call(
        _with_side_casts(_dense_ffn_kernel, 6, 1, len(side)),
        grid=(steps,),
        in_specs=[row, _layer((1, D_MODEL), l), _resident((D_MODEL, D_FF_DENSE)),
                  _resident((D_MODEL, D_FF_DENSE)), _resident((D_FF_DENSE, D_MODEL)),
                  _layer((1, D_MODEL), l)] + side_in,
        out_specs=[row] + side_out,
        out_shape=[jax.ShapeDtypeStruct((TOKENS, D_MODEL), F32)] + side_shapes,
        compiler_params=_params("arbitrary"),
        name="dense_ffn",
    )(x, ffn_g, w1, w3, w2, post_g, *side_ops)
    return outs[0], outs[1:]


def _window_hits(k, block, start_ref, dest_ref):
    wanted = start_ref[block] + k * TOKEN_WINDOW
    begin = pl.multiple_of(jnp.minimum(wanted, TOKENS - TOKEN_WINDOW), LANES)
    tok = lax.broadcasted_iota(I32, (1, TOKEN_WINDOW), 1) + begin
    dest = jnp.where(tok >= wanted, dest_ref[:, pl.ds(begin, TOKEN_WINDOW)], -1)
    rows = lax.broadcasted_iota(I32, (SORT_BLOCK, TOKEN_WINDOW), 0) + block * SORT_BLOCK
    return begin, rows == dest


def _dispatch_kernel(exp_ref, start_ref, nwin_ref, dest_ref, gw_ref, h_ref, xs_ref, gs_ref):
    j = pl.program_id(0)

    def window(k):
        begin, hit = _window_hits(k, j, start_ref, dest_ref)
        onehot = jnp.where(hit, 1.0, 0.0).astype(BF16)
        picked = _dot(onehot, h_ref[pl.ds(begin, TOKEN_WINDOW), :]).astype(BF16)
        gw = gw_ref[:, pl.ds(begin, TOKEN_WINDOW)]
        return picked, jnp.sum(jnp.where(hit, gw, 0.0), axis=-1, keepdims=True)

    @pl.when(nwin_ref[j] == 0)
    def _():
        xs_ref[...] = jnp.zeros_like(xs_ref)
        gs_ref[...] = jnp.zeros_like(gs_ref)

    @pl.when(nwin_ref[j] > 0)
    def _():
        xs_ref[...], gs_ref[...] = window(0)

        def more(k, carry):
            picked, gsum = window(k)
            xs_ref[...] += picked
            gs_ref[...] += gsum
            return carry
        lax.fori_loop(1, nwin_ref[j], more, 0)


def _per_expert_row():
    return pl.BlockSpec((None, 1, TOKENS),
                        lambda j, e, s, n: (e[jnp.minimum(j, N_SORT_BLOCKS - 1)], 0, 0))


def _dispatch(plan, dest3, gw3, h):
    exp, start, nwin = plan
    grid_spec = pltpu.PrefetchScalarGridSpec(
        num_scalar_prefetch=3,
        grid=(N_SORT_BLOCKS,),
        in_specs=[_per_expert_row(), _per_expert_row(),
                  pl.BlockSpec((TOKENS, D_MODEL), lambda j, e, s, n: (0, 0),
                               pipeline_mode=pl.Buffered(1))],
        out_specs=[pl.BlockSpec((SORT_BLOCK, D_MODEL), lambda j, e, s, n: (j, 0)),
                   pl.BlockSpec((SORT_BLOCK, 1), lambda j, e, s, n: (j, 0))],
    )
    return pl.pallas_call(
        _dispatch_kernel,
        grid_spec=grid_spec,
        out_shape=[jax.ShapeDtypeStruct((SORTED_ROWS, D_MODEL), BF16),
                   jax.ShapeDtypeStruct((SORTED_ROWS, 1), F32)],
        compiler_params=_params("arbitrary"),
        name="moe_dispatch",
    )(exp, start, nwin, dest3, gw3, h)


def _expert_kernel(rexp_ref, nunit_ref, xs_ref, gs_ref, w1_ref, w3_ref, w2_ref, ys_ref, acc_ref):
    q = pl.program_id(0)
    f = pl.program_id(1)
    last = pl.num_programs(1) - 1
    nunit = nunit_ref[q]

    @pl.when((q == 0) & (f == 0))
    def _():
        acc_ref[...] = jnp.zeros_like(acc_ref)

    def swiglu_part(n_rows):
        for start in range(0, n_rows, EXPERT_BLOCK):
            rows = slice(start, min(start + EXPERT_BLOCK, n_rows))
            x = xs_ref[rows, :]
            a = _dot(x, w1_ref[...])
            b = _dot(x, w3_ref[...])
            part = _dot((jax.nn.silu(a) * b).astype(BF16), w2_ref[...])
            acc_ref[rows, :] = jnp.where(f == 0, 0.0, acc_ref[rows, :]) + part

        @pl.when(f == last)
        def _():
            ys_ref[0:n_rows, :] = (acc_ref[0:n_rows, :] * gs_ref[0:n_rows, :]).astype(BF16)
            if n_rows < EXPERT_REGION:
                ys_ref[n_rows:EXPERT_REGION, :] = jnp.zeros((EXPERT_REGION - n_rows, D_MODEL), BF16)

    for units in range(1, EXPERT_REGION // EXPERT_PAD + 1):
        pl.when(nunit == units)(functools.partial(swiglu_part, units * EXPERT_PAD))

    @pl.when((nunit == 0) & (f == last))
    def _():
        ys_ref[...] = jnp.zeros_like(ys_ref)


def _experts(rexp, nunit, xs, gs, w1, w3, w2):
    n_ff = D_FF_EXPERT // EXPERT_FF_TILE
    tile = lambda q, f: jnp.where(q % 2 == 0, f, n_ff - 1 - f)
    grid_spec = pltpu.PrefetchScalarGridSpec(
        num_scalar_prefetch=2,
        grid=(N_REGIONS, D_FF_EXPERT // EXPERT_FF_TILE),
        in_specs=[
            pl.BlockSpec((EXPERT_REGION, D_MODEL), lambda q, f, re, nh: (q, 0)),
            pl.BlockSpec((EXPERT_REGION, 1), lambda q, f, re, nh: (q, 0)),
            pl.BlockSpec((None, D_MODEL, EXPERT_FF_TILE), lambda q, f, re, nh: (re[q], 0, tile(q, f))),
            pl.BlockSpec((None, D_MODEL, EXPERT_FF_TILE), lambda q, f, re, nh: (re[q], 0, tile(q, f))),
            pl.BlockSpec((None, EXPERT_FF_TILE, D_MODEL), lambda q, f, re, nh: (re[q], tile(q, f), 0)),
        ],
        out_specs=pl.BlockSpec((EXPERT_REGION, D_MODEL), lambda q, f, re, nh: (q, 0)),
        scratch_shapes=[pltpu.VMEM((EXPERT_REGION, D_MODEL), F32)],
    )
    return pl.pallas_call(
        _expert_kernel,
        grid_spec=grid_spec,
        out_shape=jax.ShapeDtypeStruct((SORTED_ROWS, D_MODEL), BF16),
        compiler_params=_params("arbitrary", "arbitrary"),
        name="moe_experts",
    )(rexp, nunit, xs, gs, w1, w3, w2)


def _combine_copies(wstart_ref, ys_hbm, buf_ref, sem_ref, tile, rnd, slot):
    copies = []
    for e in range(N_EXPERTS):
        wanted = wstart_ref[tile * N_EXPERTS + e] + rnd * COMBINE_WINDOW
        begin = pl.multiple_of(jnp.minimum(wanted, SORTED_ROWS - COMBINE_WINDOW), BF16_ROWS)
        copies.append(pltpu.make_async_copy(
            ys_hbm.at[pl.ds(begin, COMBINE_WINDOW), :],
            buf_ref.at[slot, pl.ds(e * COMBINE_WINDOW, COMBINE_WINDOW), :],
            sem_ref.at[slot, e]))
    return copies


def _combine_kernel(wstart_ref, nround_ref, dest_ref, ys_hbm, x_ref, pg_ref, out_ref,
                    buf_ref, sem_ref):
    i = pl.program_id(0)
    n_tiles = pl.num_programs(0)
    slot = lax.rem(i, 2)

    @pl.when(i == 0)
    def _():
        for c in _combine_copies(wstart_ref, ys_hbm, buf_ref, sem_ref, 0, 0, 0):
            c.start()

    @pl.when(i + 1 < n_tiles)
    def _():
        for c in _combine_copies(wstart_ref, ys_hbm, buf_ref, sem_ref, i + 1, 0, 1 - slot):
            c.start()

    def gather(rnd, fetch_slot):
        sub = lax.broadcasted_iota(I32, (COMBINE_WINDOW, 1), 0)
        pieces = []
        for e in range(N_EXPERTS):
            wanted = wstart_ref[i * N_EXPERTS + e] + rnd * COMBINE_WINDOW
            begin = jnp.minimum(wanted, SORTED_ROWS - COMBINE_WINDOW)
            rowid = sub + begin
            rowid = jnp.where(rowid >= wanted, rowid, -2)
            pieces.append(jnp.where(rowid == dest_ref[e:e + 1, :], 1.0, 0.0).astype(BF16))
        onehot = jnp.concatenate(pieces, axis=0)
        return _dot_tn(onehot, buf_ref[fetch_slot])

    for c in _combine_copies(wstart_ref, ys_hbm, buf_ref, sem_ref, i, 0, slot):
        c.wait()
    y = gather(0, slot)

    def extra(rnd, y):
        copies = _combine_copies(wstart_ref, ys_hbm, buf_ref, sem_ref, i, rnd, 2)
        for c in copies:
            c.start()
        for c in copies:
            c.wait()
        return y + gather(rnd, 2)

    y = lax.fori_loop(1, nround_ref[i], extra, y)
    out_ref[...] = x_ref[...] + _rms(y, pg_ref[...])


def _combine(l, wstart, nround, dest, ys, x, post_g):
    tile = lambda i, ws, nr: (i, 0)
    grid_spec = pltpu.PrefetchScalarGridSpec(
        num_scalar_prefetch=2,
        grid=(TOKENS // COMBINE_TILE,),
        in_specs=[
            pl.BlockSpec((N_EXPERTS, COMBINE_TILE), lambda i, ws, nr: (0, i)),
            pl.BlockSpec(memory_space=pl.ANY),
            pl.BlockSpec((COMBINE_TILE, D_MODEL), tile),
            pl.BlockSpec((None, 1, D_MODEL), lambda i, ws, nr: (l, 0, 0)),
        ],
        out_specs=pl.BlockSpec((COMBINE_TILE, D_MODEL), tile),
        scratch_shapes=[pltpu.VMEM((3, N_EXPERTS * COMBINE_WINDOW, D_MODEL), BF16),
                        pltpu.SemaphoreType.DMA((3, N_EXPERTS))],
    )
    return pl.pallas_call(
        _combine_kernel,
        grid_spec=grid_spec,
        out_shape=jax.ShapeDtypeStruct((TOKENS, D_MODEL), F32),
        compiler_params=_params("arbitrary"),
        name="moe_combine",
    )(wstart, nround, dest, ys, x, post_g)


def _routing_plan(sel, rank, counts):
    counts = counts.reshape(N_EXPERTS)
    padded = ((counts + EXPERT_PAD - 1) // EXPERT_PAD) * EXPERT_PAD
    region = ((counts + EXPERT_REGION - 1) // EXPERT_REGION) * EXPERT_REGION
    offs = jnp.cumsum(region) - region
    dest = jnp.where(sel == 1, rank + offs[:, None], -1)

    n_reg = region // EXPERT_REGION
    cum_reg = jnp.cumsum(n_reg)
    qs = jnp.arange(N_REGIONS, dtype=I32)
    qc = jnp.minimum(qs, cum_reg[-1] - 1)[:, None]
    owner = ((qc >= (cum_reg - n_reg)[None, :]) & (qc < cum_reg[None, :])).astype(I32)
    pick = lambda own, v: jnp.sum(own * v[None, :], axis=1)
    experts = jnp.arange(N_EXPERTS, dtype=I32)
    rexp = pick(owner, experts).astype(I32)
    rows_left = pick(owner, offs + padded) - qs * EXPERT_REGION
    n_unit = jnp.clip(rows_left // EXPERT_PAD, 0, EXPERT_REGION // EXPERT_PAD).astype(I32)

    group_end = (rank + sel)[:, LANES - 1::LANES]
    js = jnp.arange(N_SORT_BLOCKS, dtype=I32)
    owner_s = jnp.repeat(owner, EXPERT_REGION // SORT_BLOCK, axis=0)
    e_s = pick(owner_s, experts)
    counts_s = pick(owner_s, counts)
    lo_row = js * SORT_BLOCK - pick(owner_s, offs)
    has_rows = lo_row < counts_s
    hi_row = jnp.minimum(lo_row + SORT_BLOCK, counts_s)
    ends = jnp.sum(owner_s[:, :, None] * group_end[None, :, :], axis=1)
    g_lo = jnp.sum((ends <= lo_row[:, None]).astype(I32), axis=1)
    g_hi = jnp.sum((ends < hi_row[:, None]).astype(I32), axis=1)
    span = (g_hi - g_lo + 1) * LANES
    n_win = jnp.where(has_rows, (span + TOKEN_WINDOW - 1) // TOKEN_WINDOW, 0).astype(I32)
    start = jnp.where(has_rows, g_lo * LANES, 0).astype(I32)
    d_plan = (e_s.astype(I32), start, n_win)

    tile_lo = rank[:, ::COMBINE_TILE]
    tile_hi = jnp.concatenate([tile_lo[:, 1:], counts[:, None]], axis=1)
    lo = tile_lo + offs[:, None]
    hi = tile_hi + offs[:, None]
    wstart = jnp.minimum((lo // BF16_ROWS) * BF16_ROWS, SORTED_ROWS - COMBINE_WINDOW)
    rounds = jnp.where(hi > lo, (hi - wstart + COMBINE_WINDOW - 1) // COMBINE_WINDOW, 0)
    n_round = jnp.maximum(jnp.max(rounds, axis=0), 1).astype(I32)
    c_plan = (wstart.T.reshape(-1).astype(I32), n_round)
    return dest, (rexp, n_unit), d_plan, c_plan


def _moe(l, x1, h, gw, sel, rank, counts, w1, w3, w2, post_g):
    dest, e_plan, d_plan, c_plan = _routing_plan(sel, rank, counts)
    dest3 = dest.reshape(N_EXPERTS, 1, TOKENS)
    gw3 = gw.reshape(N_EXPERTS, 1, TOKENS)
    xs, gs = _dispatch(d_plan, dest3, gw3, h)
    ys = _experts(*e_plan, xs, gs, w1, w3, w2)
    return _combine(l, *c_plan, dest, ys, x1, post_g)


def kernel(x, pre_mix_g, post_mix_g, w_in, w_fourier, w_gmlp, w_out, w_spatial, b_spatial,
           gmlp_ln_g, gmlp_ln_b, pre_ffn_g, post_ffn_g, ffn_w1, ffn_w3, ffn_w2,
           router_w, router_b, moe_w1, moe_w3, moe_w2):
    pos = jnp.asarray(_POS_DFT).astype(BF16)
    chan = jnp.asarray(_CHAN_DFT).astype(BF16)
    tri = jnp.asarray(np.triu(np.ones((MIX_TILE, MIX_TILE), np.float32), 1)).astype(BF16)

    rows = lambda v: v.reshape(v.shape[0], 1, v.shape[1])
    pre_mix_g, post_mix_g, pre_ffn_g, post_ffn_g, gmlp_ln_g, gmlp_ln_b = map(
        rows, (pre_mix_g, post_mix_g, pre_ffn_g, post_ffn_g, gmlp_ln_g, gmlp_ln_b))
    b_full = jnp.repeat(jnp.swapaxes(b_spatial, 1, 2), GROUP_DIM, axis=2)
    router_wt = jnp.swapaxes(router_w, 1, 2)
    router_b = router_b.reshape(router_b.shape[0], N_EXPERTS, 1)

    assert DEPTH == 2
    moe_rows = lambda w: w[0].reshape(-1, w.shape[-1])
    moe_back = lambda w, like: w.reshape(like.shape[1:])
    mixer_weights = lambda l: [(w, l) for w in (w_in, w_fourier, w_gmlp, w_out)]

    xf = x.reshape(TOKENS, D_MODEL)
    za, (win, wf, wg, wo) = _mixer_za(0, xf, pre_mix_g, w_in, side=mixer_weights(0))
    fa = _fourier(za.reshape(BATCH, SEQ, MIX_WIDTH), chan, pos)
    x1, (ffn1, ffn3, ffn2, moe1) = _mixer_body(
        0, xf, fa, pre_mix_g, win, w_spatial, b_full, gmlp_ln_g, gmlp_ln_b, wf, wg, wo, post_mix_g,
        side=(ffn_w1[0], ffn_w3[0], ffn_w2[0], moe_rows(moe_w1)))
    xf, (moe3, moe2) = _dense_ffn(0, x1, pre_ffn_g, ffn1, ffn3, ffn2, post_ffn_g,
                                  side=(moe_rows(moe_w3), moe_rows(moe_w2)))

    za, (win, wf, wg, wo) = _mixer_za(1, xf, pre_mix_g, w_in, side=mixer_weights(1))
    fa = _fourier(za.reshape(BATCH, SEQ, MIX_WIDTH), chan, pos)
    x1, h, gw, sel, rank, counts = _mixer_body_router(
        1, xf, fa, pre_mix_g, win, w_spatial, b_full, gmlp_ln_g, gmlp_ln_b, wf, wg, wo, post_mix_g,
        pre_ffn_g, router_wt, router_b, tri)
    xf = _moe(1, x1, h, gw, sel, rank, counts, moe_back(moe1, moe_w1), moe_back(moe3, moe_w3),
              moe_back(moe2, moe_w2), post_ffn_g)
    return xf.reshape(BATCH, SEQ, D_MODEL)
```

```python
import functools

import numpy as np
import jax
import jax.numpy as jnp
from jax import lax
from jax.experimental import pallas as pl
from jax.experimental.pallas import tpu as pltpu

F32 = jnp.float32
BF16 = jnp.bfloat16
I32 = jnp.int32

D_MODEL = 1024
BATCH = 8
SEQ = 2048
HALF_SEQ = SEQ // 2
TOKENS = BATCH * SEQ
DEPTH = 2
N_GROUPS = 4
GROUP_DIM = 128
MIX_WIDTH = N_GROUPS * GROUP_DIM
CHUNK = 128
D_IN = 3 * MIX_WIDTH + 2 * D_MODEL
D_FF_DENSE = 2816
N_EXPERTS = 8
D_FF_EXPERT = 3584
RMS_EPS = 1e-6
LN_EPS = 1e-5
LANES = 128

ROW_TILE = 512
MIX_TILE = 512
EXPERT_BLOCK = 512
EXPERT_PAD = 256
EXPERT_REGION = 1024
SORT_BLOCK = 256
TOKEN_WINDOW = 1280
COMBINE_TILE = 256
COMBINE_WINDOW = 128
BF16_ROWS = 16
EXPERT_FF_TILE = 1792

SORTED_ROWS = 2 * TOKENS + N_EXPERTS * EXPERT_REGION
N_REGIONS = SORTED_ROWS // EXPERT_REGION
N_SORT_BLOCKS = SORTED_ROWS // SORT_BLOCK
N_COMBINE_TILES = TOKENS // COMBINE_TILE

VMEM_LIMIT = 56 * 1024 * 1024


def _params(*sem):
    return pltpu.CompilerParams(dimension_semantics=sem, vmem_limit_bytes=VMEM_LIMIT)


def _resident(shape):
    nd = len(shape)
    return pl.BlockSpec(shape, lambda *_: (0,) * nd, pipeline_mode=pl.Buffered(1))


def _layer(shape, l):
    nd = len(shape)
    return pl.BlockSpec((None,) + tuple(shape), lambda *_: (l,) + (0,) * nd,
                        pipeline_mode=pl.Buffered(1))


def _rms(x, g):
    return x * lax.rsqrt(jnp.mean(x * x, axis=-1, keepdims=True) + RMS_EPS) * g


def _dot(a, b):
    return jnp.dot(a, b, preferred_element_type=F32)


def _dot_nt(a, b):
    return lax.dot_general(a, b, (((1,), (1,)), ((), ())), preferred_element_type=F32)


def _dot_tn(a, b):
    return lax.dot_general(a, b, (((0,), (0,)), ((), ())), preferred_element_type=F32)


def _gelu(x):
    return 0.5 * x * (1.0 + lax.erf(x * np.float32(np.sqrt(0.5))))


def _cast_once(*pairs):
    chunk = 128

    @pl.when(pl.program_id(0) == 0)
    def _():
        for src_ref, dst_ref in pairs:
            def body(i, carry, src_ref=src_ref, dst_ref=dst_ref):
                rows = pl.ds(pl.multiple_of(i * chunk, chunk), chunk)
                dst_ref[rows, :] = src_ref[rows, :].astype(BF16)
                return carry
            lax.fori_loop(0, src_ref.shape[0] // chunk, body, 0)


def _split_bf16(x):
    hi = x.astype(BF16)
    return hi, (x - hi.astype(F32)).astype(BF16)


def _side_specs(arrays, steps):
    in_specs, out_specs, shapes, operands = [], [], [], []
    for entry in arrays:
        a, l = entry if isinstance(entry, tuple) else (entry, None)
        rows, cols = a.shape[-2:]
        chunk, per = rows // steps, 1
        if chunk % BF16_ROWS:
            chunk, per = 2 * chunk, 2
        assert rows % chunk == 0 and chunk % BF16_ROWS == 0, (a.shape, steps)
        out_specs.append(pl.BlockSpec((chunk, cols), lambda i, per=per: (i // per, 0)))
        if l is None:
            in_specs.append(out_specs[-1])
        else:
            in_specs.append(pl.BlockSpec((None, chunk, cols), lambda i, per=per, l=l: (l, i // per, 0)))
        shapes.append(jax.ShapeDtypeStruct((rows, cols), BF16))
        operands.append(a)
    return in_specs, out_specs, shapes, operands


def _with_side_casts(body, n_in, n_out, n_side):
    def kernel(*refs):
        ins, rest = refs[:n_in], refs[n_in:]
        srcs, rest = rest[:n_side], rest[n_side:]
        outs, rest = rest[:n_out], rest[n_out:]
        dsts, scratch = rest[:n_side], rest[n_side:]
        body(*ins, *outs, *scratch)
        for src, dst in zip(srcs, dsts):
            dst[...] = src[...].astype(BF16)
    return kernel


def _dft_tables():
    k = np.arange(HALF_SEQ, dtype=np.int64)
    scale = 1.0 / np.sqrt(SEQ)
    halves = []
    for p in (0, 1):
        n = 2 * np.arange(HALF_SEQ, dtype=np.int64) + p
        ang = 2.0 * np.pi * ((k[:, None] * n[None, :]) % SEQ).astype(np.float64) / SEQ
        halves.append(np.concatenate([np.cos(ang) * scale, -np.sin(ang) * scale], axis=1))
    pos = np.stack(halves)
    c = np.arange(GROUP_DIM, dtype=np.int64)
    angc = 2.0 * np.pi * ((c[:, None] * c[None, :]) % GROUP_DIM).astype(np.float64) / GROUP_DIM
    scalec = 1.0 / np.sqrt(GROUP_DIM)
    chan = np.concatenate([np.cos(angc) * scalec, np.sin(angc) * scalec], axis=1)
    return pos.astype(np.float32), chan.astype(np.float32)


_POS_DFT, _CHAN_DFT = _dft_tables()


def _fourier_kernel(x_ref, g_ref, wa32_ref, chan_ref, pos_ref, out_ref, stage_ref, rhs_ref, wa_ref):
    _cast_once((wa32_ref, wa_ref))
    for half in range(SEQ // HALF_SEQ):
        rows = slice(half * HALF_SEQ, (half + 1) * HALF_SEQ)
        h = _rms(x_ref[0, rows, :], g_ref[...]).astype(BF16)
        za = _dot(h, wa_ref[...]).astype(BF16)
        for g in range(N_GROUPS):
            xcs = _dot(za[:, g * GROUP_DIM:(g + 1) * GROUP_DIM], chan_ref[...])
            stage_ref[g, rows, :] = xcs[:, 0:GROUP_DIM]
            stage_ref[N_GROUPS + g, rows, :] = xcs[:, GROUP_DIM:2 * GROUP_DIM]
    for g in range(N_GROUPS):
        cols = slice(g * GROUP_DIM, (g + 1) * GROUP_DIM)
        for p in (0, 1):
            rows = pl.ds(p, HALF_SEQ, stride=2)
            rhs_ref[p, 0:HALF_SEQ, cols] = stage_ref[g, rows, :].astype(BF16)
            rhs_ref[p, HALF_SEQ:SEQ, cols] = stage_ref[N_GROUPS + g, rows, :].astype(BF16)

    even = _dot(pos_ref[0], rhs_ref[0])
    odd = _dot(pos_ref[1], rhs_ref[1])
    out_ref[0, 0] = (even + odd).astype(BF16)
    out_ref[0, 1] = (even - odd).astype(BF16)


def _fourier(l, x, g, w_in, chan, pos, side=()):
    side_in, side_out, side_shapes, side_ops = _side_specs(side, BATCH)
    outs = pl.pallas_call(
        _with_side_casts(_fourier_kernel, 5, 1, len(side)),
        grid=(BATCH,),
        in_specs=[pl.BlockSpec((1, SEQ, D_MODEL), lambda b: (b, 0, 0)),
                  _layer((1, D_MODEL), l), _layer((D_MODEL, MIX_WIDTH), l),
                  _resident((GROUP_DIM, 2 * GROUP_DIM)), _resident((2, HALF_SEQ, SEQ))] + side_in,
        out_specs=[pl.BlockSpec((1, 2, HALF_SEQ, MIX_WIDTH), lambda b: (b, 0, 0, 0))] + side_out,
        out_shape=[jax.ShapeDtypeStruct((BATCH, 2, HALF_SEQ, MIX_WIDTH), BF16)] + side_shapes,
        scratch_shapes=[pltpu.VMEM((2 * N_GROUPS, SEQ, GROUP_DIM), F32),
                        pltpu.VMEM((2, SEQ, MIX_WIDTH), BF16),
                        pltpu.VMEM((D_MODEL, MIX_WIDTH), BF16)],
        compiler_params=_params("arbitrary"),
        name="fourier_mix",
    )(x, g, w_in, chan, pos, *side_ops)
    return outs[0].reshape(TOKENS, MIX_WIDTH), outs[1:]


def _mix_common(x_ref, fa_ref, prg_ref, win_ref, ws_ref, bs_ref, lng_ref, lnb_ref,
                wf_ref, wg_ref, wo_ref, pg_ref, sg_ref):
    x = x_ref[...]
    h = _rms(x, prg_ref[...]).astype(BF16)
    u = _gelu(_dot(h, win_ref[:, MIX_WIDTH:2 * MIX_WIDTH]))
    v = _gelu(_dot(h, win_ref[:, 2 * MIX_WIDTH:3 * MIX_WIDTH]))
    mu = jnp.mean(v, axis=-1, keepdims=True)
    vc = v - mu
    var = jnp.mean(vc * vc, axis=-1, keepdims=True)
    vln = (vc * lax.rsqrt(var + LN_EPS) * lng_ref[...] + lnb_ref[...]).astype(BF16)
    ws = [ws_ref[g].astype(BF16) for g in range(N_GROUPS)]
    for c in range(x_ref.shape[0] // CHUNK):
        rows = slice(c * CHUNK, (c + 1) * CHUNK)
        for g in range(N_GROUPS):
            cols = slice(g * GROUP_DIM, (g + 1) * GROUP_DIM)
            sv = _dot(ws[g], vln[rows, cols]) + bs_ref[:, cols]
            sg_ref[rows, cols] = (u[rows, cols] * sv).astype(BF16)
    gate = jax.nn.sigmoid(_dot(h, win_ref[:, 3 * MIX_WIDTH:D_IN]))
    ya = _dot(fa_ref[...], wf_ref[...])
    yb = _dot(sg_ref[...], wg_ref[...])
    m = gate[:, 0:D_MODEL] * ya + gate[:, D_MODEL:2 * D_MODEL] * yb
    y = _dot(m.astype(BF16), wo_ref[...])
    return x + _rms(y, pg_ref[...])


def _mix_kernel(*refs):
    *ins, x1_ref, sg_ref = refs
    x1_ref[...] = _mix_common(*ins, sg_ref)


def _mix_router_kernel(*refs):
    (*ins, fg_ref, rwt_ref, rb_ref, tri_ref,
     x1_ref, h_ref, gw_ref, sel_ref, rank_ref, cnt_ref, sg_ref, carry_ref) = refs

    @pl.when(pl.program_id(0) == 0)
    def _():
        carry_ref[...] = jnp.zeros_like(carry_ref)

    x1 = _mix_common(*ins, sg_ref)
    x1_ref[...] = x1
    h = _rms(x1, fg_ref[...])
    h_hi, h_lo = _split_bf16(h)
    h_ref[...] = h_hi

    w_hi, w_lo = _split_bf16(rwt_ref[...])
    logits = _dot_nt(w_hi, h_hi) + _dot_nt(w_hi, h_lo) + _dot_nt(w_lo, h_hi) + rb_ref[...]
    row = lax.broadcasted_iota(I32, logits.shape, 0)
    m1 = jnp.max(logits, axis=0, keepdims=True)
    i1 = jnp.min(jnp.where(logits == m1, row, N_EXPERTS), axis=0, keepdims=True)
    rest = jnp.where(row == i1, -jnp.inf, logits)
    m2 = jnp.max(rest, axis=0, keepdims=True)
    i2 = jnp.min(jnp.where(rest == m2, row, N_EXPERTS), axis=0, keepdims=True)
    e2 = jnp.exp(m2 - m1)
    den = 1.0 + e2
    gw_ref[...] = jnp.where(row == i1, 1.0 / den, 0.0) + jnp.where(row == i2, e2 / den, 0.0)
    sel = jnp.where((row == i1) | (row == i2), 1.0, 0.0)
    sel_ref[...] = sel.astype(I32)
    before = _dot(sel.astype(BF16), tri_ref[...]) + carry_ref[...]
    rank_ref[...] = before.astype(I32)
    carry_ref[...] = carry_ref[...] + jnp.sum(sel, axis=1, keepdims=True)
    cnt_ref[...] = carry_ref[...].astype(I32)


def _mix_specs(l):
    row = lambda w: pl.BlockSpec((MIX_TILE, w), lambda i: (i, 0))
    in_specs = [row(D_MODEL), row(MIX_WIDTH), _layer((1, D_MODEL), l), _resident((D_MODEL, D_IN)),
                _layer((N_GROUPS, CHUNK, CHUNK), l), _layer((CHUNK, MIX_WIDTH), l),
                _layer((1, MIX_WIDTH), l), _layer((1, MIX_WIDTH), l),
                _resident((MIX_WIDTH, D_MODEL)), _resident((MIX_WIDTH, D_MODEL)),
                _resident((D_MODEL, D_MODEL)), _layer((1, D_MODEL), l)]
    return row, in_specs, [pltpu.VMEM((MIX_TILE, MIX_WIDTH), BF16)]


def _mixer_body(l, x, fa, pre_g, w_in, w_s, b_full, ln_g, ln_b, w_f, w_g, w_o, post_g, side=()):
    row, in_specs, scratch = _mix_specs(l)
    steps = TOKENS // MIX_TILE
    side_in, side_out, side_shapes, side_ops = _side_specs(side, steps)
    outs = pl.pallas_call(
        _with_side_casts(_mix_kernel, len(in_specs), 1, len(side)),
        grid=(steps,),
        in_specs=in_specs + side_in,
        out_specs=[row(D_MODEL)] + side_out,
        out_shape=[jax.ShapeDtypeStruct((TOKENS, D_MODEL), F32)] + side_shapes,
        scratch_shapes=scratch,
        compiler_params=_params("arbitrary"),
        name="mixer_body",
    )(x, fa, pre_g, w_in, w_s, b_full, ln_g, ln_b, w_f, w_g, w_o, post_g, *side_ops)
    return outs[0], outs[1:]


def _mixer_body_router(l, x, fa, pre_g, w_in, w_s, b_full, ln_g, ln_b, w_f, w_g, w_o, post_g,
                       ffn_g, router_wt, router_b, tri):
    row, in_specs, scratch = _mix_specs(l)
    in_specs += [_layer((1, D_MODEL), l), _layer((N_EXPERTS, D_MODEL), l // 2),
                 _layer((N_EXPERTS, 1), l // 2), _resident((MIX_TILE, MIX_TILE))]
    col = pl.BlockSpec((N_EXPERTS, MIX_TILE), lambda i: (0, i))
    per_token = lambda dt: jax.ShapeDtypeStruct((N_EXPERTS, TOKENS), dt)
    return pl.pallas_call(
        _mix_router_kernel,
        grid=(TOKENS // MIX_TILE,),
        in_specs=in_specs,
        out_specs=[row(D_MODEL), row(D_MODEL), col, col, col,
                   pl.BlockSpec((N_EXPERTS, 1), lambda i: (0, 0))],
        out_shape=[jax.ShapeDtypeStruct((TOKENS, D_MODEL), F32),
                   jax.ShapeDtypeStruct((TOKENS, D_MODEL), BF16),
                   per_token(F32), per_token(I32), per_token(I32),
                   jax.ShapeDtypeStruct((N_EXPERTS, 1), I32)],
        scratch_shapes=scratch + [pltpu.VMEM((N_EXPERTS, 1), F32)],
        compiler_params=_params("arbitrary"),
        name="mixer_body_router",
    )(x, fa, pre_g, w_in, w_s, b_full, ln_g, ln_b, w_f, w_g, w_o, post_g,
      ffn_g, router_wt, router_b, tri)


def _dense_ffn_kernel(x_ref, fg_ref, w1_ref, w3_ref, w2_ref, pg_ref, out_ref):
    x = x_ref[...]
    h = _rms(x, fg_ref[...]).astype(BF16)
    a = _dot(h, w1_ref[...])
    b = _dot(h, w3_ref[...])
    y = _dot((jax.nn.silu(a) * b).astype(BF16), w2_ref[...])
    out_ref[...] = x + _rms(y, pg_ref[...])


def _dense_ffn(l, x, ffn_g, w1, w3, w2, post_g, side=()):
    row = pl.BlockSpec((ROW_TILE, D_MODEL), lambda i: (i, 0))
    steps = TOKENS // ROW_TILE
    side_in, side_out, side_shapes, side_ops = _side_specs(side, steps)
    outs = pl.pallas_call(
        _with_side_casts(_dense_ffn_kernel, 6, 1, len(side)),
        grid=(steps,),
        in_specs=[row, _layer((1, D_MODEL), l), _resident((D_MODEL, D_FF_DENSE)),
                  _resident((D_MODEL, D_FF_DENSE)), _resident((D_FF_DENSE, D_MODEL)),
                  _layer((1, D_MODEL), l)] + side_in,
        out_specs=[row] + side_out,
        out_shape=[jax.ShapeDtypeStruct((TOKENS, D_MODEL), F32)] + side_shapes,
        compiler_params=_params("arbitrary"),
        name="dense_ffn",
    )(x, ffn_g, w1, w3, w2, post_g, *side_ops)
    return outs[0], outs[1:]


def _window_hits(k, block, start_ref, dest_ref):
    wanted = start_ref[block] + k * TOKEN_WINDOW
    begin = pl.multiple_of(jnp.minimum(wanted, TOKENS - TOKEN_WINDOW), LANES)
    tok = lax.broadcasted_iota(I32, (1, TOKEN_WINDOW), 1) + begin
    dest = jnp.where(tok >= wanted, dest_ref[:, pl.ds(begin, TOKEN_WINDOW)], -1)
    rows = lax.broadcasted_iota(I32, (SORT_BLOCK, TOKEN_WINDOW), 0) + block * SORT_BLOCK
    return begin, rows == dest


def _dispatch_kernel(exp_ref, start_ref, nwin_ref, dest_ref, gw_ref, h_ref, xs_ref, gs_ref):
    j = pl.program_id(0)

    def window(k):
        begin, hit = _window_hits(k, j, start_ref, dest_ref)
        onehot = jnp.where(hit, 1.0, 0.0).astype(BF16)
        picked = _dot(onehot, h_ref[pl.ds(begin, TOKEN_WINDOW), :]).astype(BF16)
        gw = gw_ref[:, pl.ds(begin, TOKEN_WINDOW)]
        return picked, jnp.sum(jnp.where(hit, gw, 0.0), axis=-1, keepdims=True)

    @pl.when(nwin_ref[j] == 0)
    def _():
        xs_ref[...] = jnp.zeros_like(xs_ref)
        gs_ref[...] = jnp.zeros_like(gs_ref)

    @pl.when(nwin_ref[j] > 0)
    def _():
        xs_ref[...], gs_ref[...] = window(0)

        def more(k, carry):
            picked, gsum = window(k)
            xs_ref[...] += picked
            gs_ref[...] += gsum
            return carry
        lax.fori_loop(1, nwin_ref[j], more, 0)


def _per_expert_row():
    return pl.BlockSpec((None, 1, TOKENS),
                        lambda j, e, s, n: (e[jnp.minimum(j, N_SORT_BLOCKS - 1)], 0, 0))


def _dispatch(plan, dest3, gw3, h):
    exp, start, nwin = plan
    grid_spec = pltpu.PrefetchScalarGridSpec(
        num_scalar_prefetch=3,
        grid=(N_SORT_BLOCKS,),
        in_specs=[_per_expert_row(), _per_expert_row(),
                  pl.BlockSpec((TOKENS, D_MODEL), lambda j, e, s, n: (0, 0),
                               pipeline_mode=pl.Buffered(1))],
        out_specs=[pl.BlockSpec((SORT_BLOCK, D_MODEL), lambda j, e, s, n: (j, 0)),
                   pl.BlockSpec((SORT_BLOCK, 1), lambda j, e, s, n: (j, 0))],
    )
    return pl.pallas_call(
        _dispatch_kernel,
        grid_spec=grid_spec,
        out_shape=[jax.ShapeDtypeStruct((SORTED_ROWS, D_MODEL), BF16),
                   jax.ShapeDtypeStruct((SORTED_ROWS, 1), F32)],
        compiler_params=_params("arbitrary"),
        name="moe_dispatch",
    )(exp, start, nwin, dest3, gw3, h)


def _expert_kernel(rexp_ref, nunit_ref, xs_ref, gs_ref, w1_ref, w3_ref, w2_ref, ys_ref, acc_ref):
    q = pl.program_id(0)
    f = pl.program_id(1)
    last = pl.num_programs(1) - 1
    nunit = nunit_ref[q]

    @pl.when((q == 0) & (f == 0))
    def _():
        acc_ref[...] = jnp.zeros_like(acc_ref)

    def swiglu_part(n_rows):
        for start in range(0, n_rows, EXPERT_BLOCK):
            rows = slice(start, min(start + EXPERT_BLOCK, n_rows))
            x = xs_ref[rows, :]
            a = _dot(x, w1_ref[...])
            b = _dot(x, w3_ref[...])
            part = _dot((jax.nn.silu(a) * b).astype(BF16), w2_ref[...])
            acc_ref[rows, :] = jnp.where(f == 0, 0.0, acc_ref[rows, :]) + part

        @pl.when(f == last)
        def _():
            ys_ref[0:n_rows, :] = (acc_ref[0:n_rows, :] * gs_ref[0:n_rows, :]).astype(BF16)
            if n_rows < EXPERT_REGION:
                ys_ref[n_rows:EXPERT_REGION, :] = jnp.zeros((EXPERT_REGION - n_rows, D_MODEL), BF16)

    for units in range(1, EXPERT_REGION // EXPERT_PAD + 1):
        pl.when(nunit == units)(functools.partial(swiglu_part, units * EXPERT_PAD))

    @pl.when((nunit == 0) & (f == last))
    def _():
        ys_ref[...] = jnp.zeros_like(ys_ref)


def _experts(rexp, nunit, xs, gs, w1, w3, w2):
    n_ff = D_FF_EXPERT // EXPERT_FF_TILE
    tile = lambda q, f: jnp.where(q % 2 == 0, f, n_ff - 1 - f)
    grid_spec = pltpu.PrefetchScalarGridSpec(
        num_scalar_prefetch=2,
        grid=(N_REGIONS, D_FF_EXPERT // EXPERT_FF_TILE),
        in_specs=[
            pl.BlockSpec((EXPERT_REGION, D_MODEL), lambda q, f, re, nh: (q, 0)),
            pl.BlockSpec((EXPERT_REGION, 1), lambda q, f, re, nh: (q, 0)),
            pl.BlockSpec((None, D_MODEL, EXPERT_FF_TILE), lambda q, f, re, nh: (re[q], 0, tile(q, f))),
            pl.BlockSpec((None, D_MODEL, EXPERT_FF_TILE), lambda q, f, re, nh: (re[q], 0, tile(q, f))),
            pl.BlockSpec((None, EXPERT_FF_TILE, D_MODEL), lambda q, f, re, nh: (re[q], tile(q, f), 0)),
        ],
        out_specs=pl.BlockSpec((EXPERT_REGION, D_MODEL), lambda q, f, re, nh: (q, 0)),
        scratch_shapes=[pltpu.VMEM((EXPERT_REGION, D_MODEL), F32)],
    )
    return pl.pallas_call(
        _expert_kernel,
        grid_spec=grid_spec,
        out_shape=jax.ShapeDtypeStruct((SORTED_ROWS, D_MODEL), BF16),
        compiler_params=_params("arbitrary", "arbitrary"),
        name="moe_experts",
    )(rexp, nunit, xs, gs, w1, w3, w2)


def _combine_copies(wstart_ref, ys_hbm, buf_ref, sem_ref, tile, rnd, slot):
    copies = []
    for e in range(N_EXPERTS):
        wanted = wstart_ref[tile * N_EXPERTS + e] + rnd * COMBINE_WINDOW
        begin = pl.multiple_of(jnp.minimum(wanted, SORTED_ROWS - COMBINE_WINDOW), BF16_ROWS)
        copies.append(pltpu.make_async_copy(
            ys_hbm.at[pl.ds(begin, COMBINE_WINDOW), :],
            buf_ref.at[slot, pl.ds(e * COMBINE_WINDOW, COMBINE_WINDOW), :],
            sem_ref.at[slot, e]))
    return copies


def _combine_kernel(wstart_ref, nround_ref, dest_ref, ys_hbm, x_ref, pg_ref, out_ref,
                    buf_ref, sem_ref):
    i = pl.program_id(0)
    n_tiles = pl.num_programs(0)
    slot = lax.rem(i, 2)

    @pl.when(i == 0)
    def _():
        for c in _combine_copies(wstart_ref, ys_hbm, buf_ref, sem_ref, 0, 0, 0):
            c.start()

    @pl.when(i + 1 < n_tiles)
    def _():
        for c in _combine_copies(wstart_ref, ys_hbm, buf_ref, sem_ref, i + 1, 0, 1 - slot):
            c.start()

    def gather(rnd, fetch_slot):
        sub = lax.broadcasted_iota(I32, (COMBINE_WINDOW, 1), 0)
        pieces = []
        for e in range(N_EXPERTS):
            wanted = wstart_ref[i * N_EXPERTS + e] + rnd * COMBINE_WINDOW
            begin = jnp.minimum(wanted, SORTED_ROWS - COMBINE_WINDOW)
            rowid = sub + begin
            rowid = jnp.where(rowid >= wanted, rowid, -2)
            pieces.append(jnp.where(rowid == dest_ref[e:e + 1, :], 1.0, 0.0).astype(BF16))
        onehot = jnp.concatenate(pieces, axis=0)
        return _dot_tn(onehot, buf_ref[fetch_slot])

    for c in _combine_copies(wstart_ref, ys_hbm, buf_ref, sem_ref, i, 0, slot):
        c.wait()
    y = gather(0, slot)

    def extra(rnd, y):
        copies = _combine_copies(wstart_ref, ys_hbm, buf_ref, sem_ref, i, rnd, 2)
        for c in copies:
            c.start()
        for c in copies:
            c.wait()
        return y + gather(rnd, 2)

    y = lax.fori_loop(1, nround_ref[i], extra, y)
    out_ref[...] = x_ref[...] + _rms(y, pg_ref[...])


def _combine(l, wstart, nround, dest, ys, x, post_g):
    tile = lambda i, ws, nr: (i, 0)
    grid_spec = pltpu.PrefetchScalarGridSpec(
        num_scalar_prefetch=2,
        grid=(TOKENS // COMBINE_TILE,),
        in_specs=[
            pl.BlockSpec((N_EXPERTS, COMBINE_TILE), lambda i, ws, nr: (0, i)),
            pl.BlockSpec(memory_space=pl.ANY),
            pl.BlockSpec((COMBINE_TILE, D_MODEL), tile),
            pl.BlockSpec((None, 1, D_MODEL), lambda i, ws, nr: (l, 0, 0)),
        ],
        out_specs=pl.BlockSpec((COMBINE_TILE, D_MODEL), tile),
        scratch_shapes=[pltpu.VMEM((3, N_EXPERTS * COMBINE_WINDOW, D_MODEL), BF16),
                        pltpu.SemaphoreType.DMA((3, N_EXPERTS))],
    )
    return pl.pallas_call(
        _combine_kernel,
        grid_spec=grid_spec,
        out_shape=jax.ShapeDtypeStruct((TOKENS, D_MODEL), F32),
        compiler_params=_params("arbitrary"),
        name="moe_combine",
    )(wstart, nround, dest, ys, x, post_g)


def _routing_plan(sel, rank, counts):
    counts = counts.reshape(N_EXPERTS)
    padded = ((counts + EXPERT_PAD - 1) // EXPERT_PAD) * EXPERT_PAD
    region = ((counts + EXPERT_REGION - 1) // EXPERT_REGION) * EXPERT_REGION
    offs = jnp.cumsum(region) - region
    dest = jnp.where(sel == 1, rank + offs[:, None], -1)

    n_reg = region // EXPERT_REGION
    cum_reg = jnp.cumsum(n_reg)
    qs = jnp.arange(N_REGIONS, dtype=I32)
    qc = jnp.minimum(qs, cum_reg[-1] - 1)[:, None]
    owner = ((qc >= (cum_reg - n_reg)[None, :]) & (qc < cum_reg[None, :])).astype(I32)
    pick = lambda own, v: jnp.sum(own * v[None, :], axis=1)
    experts = jnp.arange(N_EXPERTS, dtype=I32)
    rexp = pick(owner, experts).astype(I32)
    rows_left = pick(owner, offs + padded) - qs * EXPERT_REGION
    n_unit = jnp.clip(rows_left // EXPERT_PAD, 0, EXPERT_REGION // EXPERT_PAD).astype(I32)

    group_end = (rank + sel)[:, LANES - 1::LANES]
    js = jnp.arange(N_SORT_BLOCKS, dtype=I32)
    owner_s = jnp.repeat(owner, EXPERT_REGION // SORT_BLOCK, axis=0)
    e_s = pick(owner_s, experts)
    counts_s = pick(owner_s, counts)
    lo_row = js * SORT_BLOCK - pick(owner_s, offs)
    has_rows = lo_row < counts_s
    hi_row = jnp.minimum(lo_row + SORT_BLOCK, counts_s)
    ends = jnp.sum(owner_s[:, :, None] * group_end[None, :, :], axis=1)
    g_lo = jnp.sum((ends <= lo_row[:, None]).astype(I32), axis=1)
    g_hi = jnp.sum((ends < hi_row[:, None]).astype(I32), axis=1)
    span = (g_hi - g_lo + 1) * LANES
    n_win = jnp.where(has_rows, (span + TOKEN_WINDOW - 1) // TOKEN_WINDOW, 0).astype(I32)
    start = jnp.where(has_rows, g_lo * LANES, 0).astype(I32)
    d_plan = (e_s.astype(I32), start, n_win)

    tile_lo = rank[:, ::COMBINE_TILE]
    tile_hi = jnp.concatenate([tile_lo[:, 1:], counts[:, None]], axis=1)
    lo = tile_lo + offs[:, None]
    hi = tile_hi + offs[:, None]
    wstart = jnp.minimum((lo // BF16_ROWS) * BF16_ROWS, SORTED_ROWS - COMBINE_WINDOW)
    rounds = jnp.where(hi > lo, (hi - wstart + COMBINE_WINDOW - 1) // COMBINE_WINDOW, 0)
    n_round = jnp.maximum(jnp.max(rounds, axis=0), 1).astype(I32)
    c_plan = (wstart.T.reshape(-1).astype(I32), n_round)
    return dest, (rexp, n_unit), d_plan, c_plan


def _moe(l, x1, h, gw, sel, rank, counts, w1, w3, w2, post_g):
    dest, e_plan, d_plan, c_plan = _routing_plan(sel, rank, counts)
    dest3 = dest.reshape(N_EXPERTS, 1, TOKENS)
    gw3 = gw.reshape(N_EXPERTS, 1, TOKENS)
    xs, gs = _dispatch(d_plan, dest3, gw3, h)
    ys = _experts(*e_plan, xs, gs, w1, w3, w2)
    return _combine(l, *c_plan, dest, ys, x1, post_g)


def kernel(x, pre_mix_g, post_mix_g, w_in, w_fourier, w_gmlp, w_out, w_spatial, b_spatial,
           gmlp_ln_g, gmlp_ln_b, pre_ffn_g, post_ffn_g, ffn_w1, ffn_w3, ffn_w2,
           router_w, router_b, moe_w1, moe_w3, moe_w2):
    pos = jnp.asarray(_POS_DFT).astype(BF16)
    chan = jnp.asarray(_CHAN_DFT).astype(BF16)
    tri = jnp.asarray(np.triu(np.ones((MIX_TILE, MIX_TILE), np.float32), 1)).astype(BF16)

    rows = lambda v: v.reshape(v.shape[0], 1, v.shape[1])
    pre_mix_g, post_mix_g, pre_ffn_g, post_ffn_g, gmlp_ln_g, gmlp_ln_b = map(
        rows, (pre_mix_g, post_mix_g, pre_ffn_g, post_ffn_g, gmlp_ln_g, gmlp_ln_b))
    b_full = jnp.repeat(jnp.swapaxes(b_spatial, 1, 2), GROUP_DIM, axis=2)
    router_wt = jnp.swapaxes(router_w, 1, 2)
    router_b = router_b.reshape(router_b.shape[0], N_EXPERTS, 1)

    assert DEPTH == 2
    moe_rows = lambda w: w[0].reshape(-1, w.shape[-1])
    moe_back = lambda w, like: w.reshape(like.shape[1:])
    mixer_weights = lambda l: [(w, l) for w in (w_in, w_fourier, w_gmlp, w_out)]

    xf = x.reshape(TOKENS, D_MODEL)
    fa, (win, wf, wg, wo) = _fourier(0, xf.reshape(BATCH, SEQ, D_MODEL), pre_mix_g, w_in, chan, pos,
                                     side=mixer_weights(0))
    x1, (ffn1, ffn3, ffn2, moe1) = _mixer_body(
        0, xf, fa, pre_mix_g, win, w_spatial, b_full, gmlp_ln_g, gmlp_ln_b, wf, wg, wo, post_mix_g,
        side=(ffn_w1[0], ffn_w3[0], ffn_w2[0], moe_rows(moe_w1)))
    xf, (moe3, moe2) = _dense_ffn(0, x1, pre_ffn_g, ffn1, ffn3, ffn2, post_ffn_g,
                                  side=(moe_rows(moe_w3), moe_rows(moe_w2)))

    fa, (win, wf, wg, wo) = _fourier(1, xf.reshape(BATCH, SEQ, D_MODEL), pre_mix_g, w_in, chan, pos,
                                     side=mixer_weights(1))
    x1, h, gw, sel, rank, counts = _mixer_body_router(
        1, xf, fa, pre_mix_g, win, w_spatial, b_full, gmlp_ln_g, gmlp_ln_b, wf, wg, wo, post_mix_g,
        pre_ffn_g, router_wt, router_b, tri)
    xf = _moe(1, x1, h, gw, sel, rank, counts, moe_back(moe1, moe_w1), moe_back(moe3, moe_w3),
              moe_back(moe2, moe_w2), post_ffn_g)
    return xf.reshape(BATCH, SEQ, D_MODEL)
```

```python
import functools

import numpy as np
import jax
import jax.numpy as jnp
from jax import lax
from jax.experimental import pallas as pl
from jax.experimental.pallas import tpu as pltpu

F32 = jnp.float32
BF16 = jnp.bfloat16
I32 = jnp.int32

D_MODEL = 1024
BATCH = 8
SEQ = 2048
HALF_SEQ = SEQ // 2
TOKENS = BATCH * SEQ
DEPTH = 2
N_GROUPS = 4
GROUP_DIM = 128
MIX_WIDTH = N_GROUPS * GROUP_DIM
CHUNK = 128
D_IN = 3 * MIX_WIDTH + 2 * D_MODEL
D_FF_DENSE = 2816
N_EXPERTS = 8
D_FF_EXPERT = 3584
RMS_EPS = 1e-6
LN_EPS = 1e-5
LANES = 128

ROW_TILE = 512
MIX_TILE = 512
EXPERT_BLOCK = 512
EXPERT_PAD = 256
EXPERT_REGION = 1024
SORT_BLOCK = 256
TOKEN_WINDOW = 1280
COMBINE_TILE = 256
COMBINE_WINDOW = 128
BF16_ROWS = 16
EXPERT_FF_TILE = 1792

SORTED_ROWS = 2 * TOKENS + N_EXPERTS * EXPERT_REGION
N_REGIONS = SORTED_ROWS // EXPERT_REGION
N_SORT_BLOCKS = SORTED_ROWS // SORT_BLOCK
N_COMBINE_TILES = TOKENS // COMBINE_TILE

VMEM_LIMIT = 56 * 1024 * 1024


def _params(*sem):
    return pltpu.CompilerParams(dimension_semantics=sem, vmem_limit_bytes=VMEM_LIMIT)


def _resident(shape):
    nd = len(shape)
    return pl.BlockSpec(shape, lambda *_: (0,) * nd, pipeline_mode=pl.Buffered(1))


def _layer(shape, l):
    nd = len(shape)
    return pl.BlockSpec((None,) + tuple(shape), lambda *_: (l,) + (0,) * nd,
                        pipeline_mode=pl.Buffered(1))


def _rms(x, g):
    return x * lax.rsqrt(jnp.mean(x * x, axis=-1, keepdims=True) + RMS_EPS) * g


def _dot(a, b):
    return jnp.dot(a, b, preferred_element_type=F32)


def _dot_nt(a, b):
    return lax.dot_general(a, b, (((1,), (1,)), ((), ())), preferred_element_type=F32)


def _dot_tn(a, b):
    return lax.dot_general(a, b, (((0,), (0,)), ((), ())), preferred_element_type=F32)


def _gelu(x):
    return 0.5 * x * (1.0 + lax.erf(x * np.float32(np.sqrt(0.5))))


def _cast_once(*pairs):
    chunk = 128

    @pl.when(pl.program_id(0) == 0)
    def _():
        for src_ref, dst_ref in pairs:
            def body(i, carry, src_ref=src_ref, dst_ref=dst_ref):
                rows = pl.ds(pl.multiple_of(i * chunk, chunk), chunk)
                dst_ref[rows, :] = src_ref[rows, :].astype(BF16)
                return carry
            lax.fori_loop(0, src_ref.shape[0] // chunk, body, 0)


def _split_bf16(x):
    hi = x.astype(BF16)
    return hi, (x - hi.astype(F32)).astype(BF16)


def _side_specs(arrays, steps):
    in_specs, out_specs, shapes, operands = [], [], [], []
    for entry in arrays:
        a, l = entry if isinstance(entry, tuple) else (entry, None)
        rows, cols = a.shape[-2:]
        chunk, per = rows // steps, 1
        if chunk % BF16_ROWS:
            chunk, per = 2 * chunk, 2
        assert rows % chunk == 0 and chunk % BF16_ROWS == 0, (a.shape, steps)
        out_specs.append(pl.BlockSpec((chunk, cols), lambda i, per=per: (i // per, 0)))
        if l is None:
            in_specs.append(out_specs[-1])
        else:
            in_specs.append(pl.BlockSpec((None, chunk, cols), lambda i, per=per, l=l: (l, i // per, 0)))
        shapes.append(jax.ShapeDtypeStruct((rows, cols), BF16))
        operands.append(a)
    return in_specs, out_specs, shapes, operands


def _with_side_casts(body, n_in, n_out, n_side):
    def kernel(*refs):
        ins, rest = refs[:n_in], refs[n_in:]
        srcs, rest = rest[:n_side], rest[n_side:]
        outs, rest = rest[:n_out], rest[n_out:]
        dsts, scratch = rest[:n_side], rest[n_side:]
        body(*ins, *outs, *scratch)
        for src, dst in zip(srcs, dsts):
            dst[...] = src[...].astype(BF16)
    return kernel


def _dft_tables():
    k = np.arange(HALF_SEQ, dtype=np.int64)
    scale = 1.0 / np.sqrt(SEQ)
    halves = []
    for p in (0, 1):
        n = 2 * np.arange(HALF_SEQ, dtype=np.int64) + p
        ang = 2.0 * np.pi * ((k[:, None] * n[None, :]) % SEQ).astype(np.float64) / SEQ
        halves.append(np.concatenate([np.cos(ang) * scale, -np.sin(ang) * scale], axis=1))
    pos = np.stack(halves)
    c = np.arange(GROUP_DIM, dtype=np.int64)
    angc = 2.0 * np.pi * ((c[:, None] * c[None, :]) % GROUP_DIM).astype(np.float64) / GROUP_DIM
    scalec = 1.0 / np.sqrt(GROUP_DIM)
    chan = np.concatenate([np.cos(angc) * scalec, np.sin(angc) * scalec], axis=1)
    return pos.astype(np.float32), chan.astype(np.float32)


_POS_DFT, _CHAN_DFT = _dft_tables()


def _fourier_kernel(x_ref, g_ref, wa32_ref, chan_ref, pos_ref, out_ref, stage_ref, rhs_ref, wa_ref):
    _cast_once((wa32_ref, wa_ref))
    for half in range(SEQ // HALF_SEQ):
        rows = slice(half * HALF_SEQ, (half + 1) * HALF_SEQ)
        h = _rms(x_ref[0, rows, :], g_ref[...]).astype(BF16)
        za = _dot(h, wa_ref[...]).astype(BF16)
        for g in range(N_GROUPS):
            xcs = _dot(za[:, g * GROUP_DIM:(g + 1) * GROUP_DIM], chan_ref[...])
            stage_ref[g, rows, :] = xcs[:, 0:GROUP_DIM]
            stage_ref[N_GROUPS + g, rows, :] = xcs[:, GROUP_DIM:2 * GROUP_DIM]
    for g in range(N_GROUPS):
        cols = slice(g * GROUP_DIM, (g + 1) * GROUP_DIM)
        for p in (0, 1):
            rows = pl.ds(p, HALF_SEQ, stride=2)
            rhs_ref[p, 0:HALF_SEQ, cols] = stage_ref[g, rows, :].astype(BF16)
            rhs_ref[p, HALF_SEQ:SEQ, cols] = stage_ref[N_GROUPS + g, rows, :].astype(BF16)

    even = _dot(pos_ref[0], rhs_ref[0])
    odd = _dot(pos_ref[1], rhs_ref[1])
    out_ref[0, 0] = (even + odd).astype(BF16)
    out_ref[0, 1] = (even - odd).astype(BF16)


def _fourier(l, x, g, w_in, chan, pos, side=()):
    side_in, side_out, side_shapes, side_ops = _side_specs(side, BATCH)
    outs = pl.pallas_call(
        _with_side_casts(_fourier_kernel, 5, 1, len(side)),
        grid=(BATCH,),
        in_specs=[pl.BlockSpec((1, SEQ, D_MODEL), lambda b: (b, 0, 0)),
                  _layer((1, D_MODEL), l), _layer((D_MODEL, MIX_WIDTH), l),
                  _resident((GROUP_DIM, 2 * GROUP_DIM)), _resident((2, HALF_SEQ, SEQ))] + side_in,
        out_specs=[pl.BlockSpec((1, 2, HALF_SEQ, MIX_WIDTH), lambda b: (b, 0, 0, 0))] + side_out,
        out_shape=[jax.ShapeDtypeStruct((BATCH, 2, HALF_SEQ, MIX_WIDTH), BF16)] + side_shapes,
        scratch_shapes=[pltpu.VMEM((2 * N_GROUPS, SEQ, GROUP_DIM), F32),
                        pltpu.VMEM((2, SEQ, MIX_WIDTH), BF16),
                        pltpu.VMEM((D_MODEL, MIX_WIDTH), BF16)],
        compiler_params=_params("arbitrary"),
        name="fourier_mix",
    )(x, g, w_in, chan, pos, *side_ops)
    return outs[0].reshape(TOKENS, MIX_WIDTH), outs[1:]


def _mix_common(x_ref, fa_ref, prg_ref, win_ref, ws_ref, bs_ref, lng_ref, lnb_ref,
                wf_ref, wg_ref, wo_ref, pg_ref, x1_ref, sg_ref):
    x = x_ref[...]
    h = _rms(x, prg_ref[...]).astype(BF16)
    v = _gelu(_dot(h, win_ref[:, 2 * MIX_WIDTH:3 * MIX_WIDTH]))
    u = _gelu(_dot(h, win_ref[:, MIX_WIDTH:2 * MIX_WIDTH]))
    gate = jax.nn.sigmoid(_dot(h, win_ref[:, 3 * MIX_WIDTH:D_IN]))
    ya = _dot(fa_ref[...], wf_ref[...])
    mu = jnp.mean(v, axis=-1, keepdims=True)
    vc = v - mu
    var = jnp.mean(vc * vc, axis=-1, keepdims=True)
    vln = (vc * lax.rsqrt(var + LN_EPS) * lng_ref[...] + lnb_ref[...]).astype(BF16)
    ws = [ws_ref[g].astype(BF16) for g in range(N_GROUPS)]
    for c in range(x_ref.shape[0] // CHUNK):
        rows = slice(c * CHUNK, (c + 1) * CHUNK)
        for g in range(N_GROUPS):
            cols = slice(g * GROUP_DIM, (g + 1) * GROUP_DIM)
            sv = _dot(ws[g], vln[rows, cols]) + bs_ref[:, cols]
            sg_ref[rows, cols] = (u[rows, cols] * sv).astype(BF16)
    yb = _dot(sg_ref[...], wg_ref[...])
    m = (gate[:, 0:D_MODEL] * ya + gate[:, D_MODEL:2 * D_MODEL] * yb).astype(BF16)
    half = x_ref.shape[0] // 2
    for r in range(2):
        rows = slice(r * half, (r + 1) * half)
        y = _dot(m[rows, :], wo_ref[...])
        x1_ref[rows, :] = x[rows, :] + _rms(y, pg_ref[...])


def _mix_kernel(*refs):
    _mix_common(*refs)


def _mix_router_kernel(*refs):
    (*ins, fg_ref, rwt_ref, rb_ref, tri_ref,
     x1_ref, h_ref, gw_ref, sel_ref, rank_ref, cnt_ref, sg_ref, carry_ref) = refs

    @pl.when(pl.program_id(0) == 0)
    def _():
        carry_ref[...] = jnp.zeros_like(carry_ref)

    _mix_common(*ins, x1_ref, sg_ref)
    h = _rms(x1_ref[...], fg_ref[...])
    h_hi, h_lo = _split_bf16(h)
    h_ref[...] = h_hi

    w_hi, w_lo = _split_bf16(rwt_ref[...])
    logits = _dot_nt(w_hi, h_hi) + _dot_nt(w_hi, h_lo) + _dot_nt(w_lo, h_hi) + rb_ref[...]
    row = lax.broadcasted_iota(I32, logits.shape, 0)
    m1 = jnp.max(logits, axis=0, keepdims=True)
    i1 = jnp.min(jnp.where(logits == m1, row, N_EXPERTS), axis=0, keepdims=True)
    rest = jnp.where(row == i1, -jnp.inf, logits)
    m2 = jnp.max(rest, axis=0, keepdims=True)
    i2 = jnp.min(jnp.where(rest == m2, row, N_EXPERTS), axis=0, keepdims=True)
    e2 = jnp.exp(m2 - m1)
    den = 1.0 + e2
    gw_ref[...] = jnp.where(row == i1, 1.0 / den, 0.0) + jnp.where(row == i2, e2 / den, 0.0)
    sel = jnp.where((row == i1) | (row == i2), 1.0, 0.0)
    sel_ref[...] = sel.astype(I32)
    before = _dot(sel.astype(BF16), tri_ref[...]) + carry_ref[...]
    rank_ref[...] = before.astype(I32)
    carry_ref[...] = carry_ref[...] + jnp.sum(sel, axis=1, keepdims=True)
    cnt_ref[...] = carry_ref[...].astype(I32)


def _mix_specs(l):
    row = lambda w: pl.BlockSpec((MIX_TILE, w), lambda i: (i, 0))
    in_specs = [row(D_MODEL), row(MIX_WIDTH), _layer((1, D_MODEL), l), _resident((D_MODEL, D_IN)),
                _layer((N_GROUPS, CHUNK, CHUNK), l), _layer((CHUNK, MIX_WIDTH), l),
                _layer((1, MIX_WIDTH), l), _layer((1, MIX_WIDTH), l),
                _resident((MIX_WIDTH, D_MODEL)), _resident((MIX_WIDTH, D_MODEL)),
                _resident((D_MODEL, D_MODEL)), _layer((1, D_MODEL), l)]
    return row, in_specs, [pltpu.VMEM((MIX_TILE, MIX_WIDTH), BF16)]


def _mixer_body(l, x, fa, pre_g, w_in, w_s, b_full, ln_g, ln_b, w_f, w_g, w_o, post_g, side=()):
    row, in_specs, scratch = _mix_specs(l)
    steps = TOKENS // MIX_TILE
    side_in, side_out, side_shapes, side_ops = _side_specs(side, steps)
    outs = pl.pallas_call(
        _with_side_casts(_mix_kernel, len(in_specs), 1, len(side)),
        grid=(steps,),
        in_specs=in_specs + side_in,
        out_specs=[row(D_MODEL)] + side_out,
        out_shape=[jax.ShapeDtypeStruct((TOKENS, D_MODEL), F32)] + side_shapes,
        scratch_shapes=scratch,
        compiler_params=_params("arbitrary"),
        name="mixer_body",
    )(x, fa, pre_g, w_in, w_s, b_full, ln_g, ln_b, w_f, w_g, w_o, post_g, *side_ops)
    return outs[0], outs[1:]


def _mixer_body_router(l, x, fa, pre_g, w_in, w_s, b_full, ln_g, ln_b, w_f, w_g, w_o, post_g,
                       ffn_g, router_wt, router_b, tri):
    row, in_specs, scratch = _mix_specs(l)
    in_specs += [_layer((1, D_MODEL), l), _layer((N_EXPERTS, D_MODEL), l // 2),
                 _layer((N_EXPERTS, 1), l // 2), _resident((MIX_TILE, MIX_TILE))]
    col = pl.BlockSpec((N_EXPERTS, MIX_TILE), lambda i: (0, i))
    per_token = lambda dt: jax.ShapeDtypeStruct((N_EXPERTS, TOKENS), dt)
    return pl.pallas_call(
        _mix_router_kernel,
        grid=(TOKENS // MIX_TILE,),
        in_specs=in_specs,
        out_specs=[row(D_MODEL), row(D_MODEL), col, col, col,
                   pl.BlockSpec((N_EXPERTS, 1), lambda i: (0, 0))],
        out_shape=[jax.ShapeDtypeStruct((TOKENS, D_MODEL), F32),
                   jax.ShapeDtypeStruct((TOKENS, D_MODEL), BF16),
                   per_token(F32), per_token(I32), per_token(I32),
                   jax.ShapeDtypeStruct((N_EXPERTS, 1), I32)],
        scratch_shapes=scratch + [pltpu.VMEM((N_EXPERTS, 1), F32)],
        compiler_params=_params("arbitrary"),
        name="mixer_body_router",
    )(x, fa, pre_g, w_in, w_s, b_full, ln_g, ln_b, w_f, w_g, w_o, post_g,
      ffn_g, router_wt, router_b, tri)


def _dense_ffn_kernel(x_ref, fg_ref, w1_ref, w3_ref, w2_ref, pg_ref, out_ref):
    half = x_ref.shape[0] // 2
    for r in range(2):
        rows = slice(r * half, (r + 1) * half)
        x = x_ref[rows, :]
        h = _rms(x, fg_ref[...]).astype(BF16)
        a = _dot(h, w1_ref[...])
        b = _dot(h, w3_ref[...])
        y = _dot((jax.nn.silu(a) * b).astype(BF16), w2_ref[...])
        out_ref[rows, :] = x + _rms(y, pg_ref[...])


def _dense_ffn(l, x, ffn_g, w1, w3, w2, post_g, side=()):
    row = pl.BlockSpec((ROW_TILE, D_MODEL), lambda i: (i, 0))
    steps = TOKENS // ROW_TILE
    side_in, side_out, side_shapes, side_ops = _side_specs(side, steps)
    outs = pl.pallas_call(
        _with_side_casts(_dense_ffn_kernel, 6, 1, len(side)),
        grid=(steps,),
        in_specs=[row, _layer((1, D_MODEL), l), _resident((D_MODEL, D_FF_DENSE)),
                  _resident((D_MODEL, D_FF_DENSE)), _resident((D_FF_DENSE, D_MODEL)),
                  _layer((1, D_MODEL), l)] + side_in,
        out_specs=[row] + side_out,
        out_shape=[jax.ShapeDtypeStruct((TOKENS, D_MODEL), F32)] + side_shapes,
        compiler_params=_params("arbitrary"),
        name="dense_ffn",
    )(x, ffn_g, w1, w3, w2, post_g, *side_ops)
    return outs[0], outs[1:]


def _window_hits(k, block, start_ref, dest_ref):
    wanted = start_ref[block] + k * TOKEN_WINDOW
    begin = pl.multiple_of(jnp.minimum(wanted, TOKENS - TOKEN_WINDOW), LANES)
    tok = lax.broadcasted_iota(I32, (1, TOKEN_WINDOW), 1) + begin
    dest = jnp.where(tok >= wanted, dest_ref[:, pl.ds(begin, TOKEN_WINDOW)], -1)
    rows = lax.broadcasted_iota(I32, (SORT_BLOCK, TOKEN_WINDOW), 0) + block * SORT_BLOCK
    return begin, rows == dest


def _dispatch_kernel(exp_ref, start_ref, nwin_ref, dest_ref, gw_ref, h_ref, xs_ref, gs_ref):
    j = pl.program_id(0)

    def window(k):
        begin, hit = _window_hits(k, j, start_ref, dest_ref)
        onehot = jnp.where(hit, 1.0, 0.0).astype(BF16)
        picked = _dot(onehot, h_ref[pl.ds(begin, TOKEN_WINDOW), :]).astype(BF16)
        gw = gw_ref[:, pl.ds(begin, TOKEN_WINDOW)]
        return picked, jnp.sum(jnp.where(hit, gw, 0.0), axis=-1, keepdims=True)

    @pl.when(nwin_ref[j] == 0)
    def _():
        xs_ref[...] = jnp.zeros_like(xs_ref)
        gs_ref[...] = jnp.zeros_like(gs_ref)

    @pl.when(nwin_ref[j] > 0)
    def _():
        xs_ref[...], gs_ref[...] = window(0)

        def more(k, carry):
            picked, gsum = window(k)
            xs_ref[...] += picked
            gs_ref[...] += gsum
            return carry
        lax.fori_loop(1, nwin_ref[j], more, 0)


def _per_expert_row():
    return pl.BlockSpec((None, 1, TOKENS),
                        lambda j, e, s, n: (e[jnp.minimum(j, N_SORT_BLOCKS - 1)], 0, 0))


def _dispatch(plan, dest3, gw3, h):
    exp, start, nwin = plan
    grid_spec = pltpu.PrefetchScalarGridSpec(
        num_scalar_prefetch=3,
        grid=(N_SORT_BLOCKS,),
        in_specs=[_per_expert_row(), _per_expert_row(),
                  pl.BlockSpec((TOKENS, D_MODEL), lambda j, e, s, n: (0, 0),
                               pipeline_mode=pl.Buffered(1))],
        out_specs=[pl.BlockSpec((SORT_BLOCK, D_MODEL), lambda j, e, s, n: (j, 0)),
                   pl.BlockSpec((SORT_BLOCK, 1), lambda j, e, s, n: (j, 0))],
    )
    return pl.pallas_call(
        _dispatch_kernel,
        grid_spec=grid_spec,
        out_shape=[jax.ShapeDtypeStruct((SORTED_ROWS, D_MODEL), BF16),
                   jax.ShapeDtypeStruct((SORTED_ROWS, 1), F32)],
        compiler_params=_params("arbitrary"),
        name="moe_dispatch",
    )(exp, start, nwin, dest3, gw3, h)


def _expert_kernel(rexp_ref, nunit_ref, xs_ref, gs_ref, w1_ref, w3_ref, w2_ref, ys_ref, acc_ref):
    q = pl.program_id(0)
    f = pl.program_id(1)
    last = pl.num_programs(1) - 1
    nunit = nunit_ref[q]

    @pl.when((q == 0) & (f == 0))
    def _():
        acc_ref[...] = jnp.zeros_like(acc_ref)

    def swiglu_part(n_rows):
        for start in range(0, n_rows, EXPERT_BLOCK):
            rows = slice(start, min(start + EXPERT_BLOCK, n_rows))
            x = xs_ref[rows, :]
            a = _dot(x, w1_ref[...])
            b = _dot(x, w3_ref[...])
            part = _dot((jax.nn.silu(a) * b).astype(BF16), w2_ref[...])
            acc_ref[rows, :] = jnp.where(f == 0, 0.0, acc_ref[rows, :]) + part

        @pl.when(f == last)
        def _():
            ys_ref[0:n_rows, :] = (acc_ref[0:n_rows, :] * gs_ref[0:n_rows, :]).astype(BF16)
            if n_rows < EXPERT_REGION:
                ys_ref[n_rows:EXPERT_REGION, :] = jnp.zeros((EXPERT_REGION - n_rows, D_MODEL), BF16)

    for units in range(1, EXPERT_REGION // EXPERT_PAD + 1):
        pl.when(nunit == units)(functools.partial(swiglu_part, units * EXPERT_PAD))

    @pl.when((nunit == 0) & (f == last))
    def _():
        ys_ref[...] = jnp.zeros_like(ys_ref)


def _experts(rexp, nunit, xs, gs, w1, w3, w2):
    n_ff = D_FF_EXPERT // EXPERT_FF_TILE
    tile = lambda q, f: jnp.where(q % 2 == 0, f, n_ff - 1 - f)
    grid_spec = pltpu.PrefetchScalarGridSpec(
        num_scalar_prefetch=2,
        grid=(N_REGIONS, D_FF_EXPERT // EXPERT_FF_TILE),
        in_specs=[
            pl.BlockSpec((EXPERT_REGION, D_MODEL), lambda q, f, re, nh: (q, 0)),
            pl.BlockSpec((EXPERT_REGION, 1), lambda q, f, re, nh: (q, 0)),
            pl.BlockSpec((None, D_MODEL, EXPERT_FF_TILE), lambda q, f, re, nh: (re[q], 0, tile(q, f))),
            pl.BlockSpec((None, D_MODEL, EXPERT_FF_TILE), lambda q, f, re, nh: (re[q], 0, tile(q, f))),
            pl.BlockSpec((None, EXPERT_FF_TILE, D_MODEL), lambda q, f, re, nh: (re[q], tile(q, f), 0)),
        ],
        out_specs=pl.BlockSpec((EXPERT_REGION, D_MODEL), lambda q, f, re, nh: (q, 0)),
        scratch_shapes=[pltpu.VMEM((EXPERT_REGION, D_MODEL), F32)],
    )
    return pl.pallas_call(
        _expert_kernel,
        grid_spec=grid_spec,
        out_shape=jax.ShapeDtypeStruct((SORTED_ROWS, D_MODEL), BF16),
        compiler_params=_params("arbitrary", "arbitrary"),
        name="moe_experts",
    )(rexp, nunit, xs, gs, w1, w3, w2)


def _combine_copies(wstart_ref, ys_hbm, buf_ref, sem_ref, tile, rnd, slot):
    copies = []
    for e in range(N_EXPERTS):
        wanted = wstart_ref[tile * N_EXPERTS + e] + rnd * COMBINE_WINDOW
        begin = pl.multiple_of(jnp.minimum(wanted, SORTED_ROWS - COMBINE_WINDOW), BF16_ROWS)
        copies.append(pltpu.make_async_copy(
            ys_hbm.at[pl.ds(begin, COMBINE_WINDOW), :],
            buf_ref.at[slot, pl.ds(e * COMBINE_WINDOW, COMBINE_WINDOW), :],
            sem_ref.at[slot, e]))
    return copies


def _combine_kernel(wstart_ref, nround_ref, dest_ref, ys_hbm, x_ref, pg_ref, out_ref,
                    buf_ref, sem_ref):
    i = pl.program_id(0)
    n_tiles = pl.num_programs(0)
    slot = lax.rem(i, 2)

    @pl.when(i == 0)
    def _():
        for c in _combine_copies(wstart_ref, ys_hbm, buf_ref, sem_ref, 0, 0, 0):
            c.start()

    @pl.when(i + 1 < n_tiles)
    def _():
        for c in _combine_copies(wstart_ref, ys_hbm, buf_ref, sem_ref, i + 1, 0, 1 - slot):
            c.start()

    def gather(rnd, fetch_slot):
        sub = lax.broadcasted_iota(I32, (COMBINE_WINDOW, 1), 0)
        pieces = []
        for e in range(N_EXPERTS):
            wanted = wstart_ref[i * N_EXPERTS + e] + rnd * COMBINE_WINDOW
            begin = jnp.minimum(wanted, SORTED_ROWS - COMBINE_WINDOW)
            rowid = sub + begin
            rowid = jnp.where(rowid >= wanted, rowid, -2)
            pieces.append(jnp.where(rowid == dest_ref[e:e + 1, :], 1.0, 0.0).astype(BF16))
        onehot = jnp.concatenate(pieces, axis=0)
        return _dot_tn(onehot, buf_ref[fetch_slot])

    for c in _combine_copies(wstart_ref, ys_hbm, buf_ref, sem_ref, i, 0, slot):
        c.wait()
    y = gather(0, slot)

    def extra(rnd, y):
        copies = _combine_copies(wstart_ref, ys_hbm, buf_ref, sem_ref, i, rnd, 2)
        for c in copies:
            c.start()
        for c in copies:
            c.wait()
        return y + gather(rnd, 2)

    y = lax.fori_loop(1, nround_ref[i], extra, y)
    out_ref[...] = x_ref[...] + _rms(y, pg_ref[...])


def _combine(l, wstart, nround, dest, ys, x, post_g):
    tile = lambda i, ws, nr: (i, 0)
    grid_spec = pltpu.PrefetchScalarGridSpec(
        num_scalar_prefetch=2,
        grid=(TOKENS // COMBINE_TILE,),
        in_specs=[
            pl.BlockSpec((N_EXPERTS, COMBINE_TILE), lambda i, ws, nr: (0, i)),
            pl.BlockSpec(memory_space=pl.ANY),
            pl.BlockSpec((COMBINE_TILE, D_MODEL), tile),
            pl.BlockSpec((None, 1, D_MODEL), lambda i, ws, nr: (l, 0, 0)),
        ],
        out_specs=pl.BlockSpec((COMBINE_TILE, D_MODEL), tile),
        scratch_shapes=[pltpu.VMEM((3, N_EXPERTS * COMBINE_WINDOW, D_MODEL), BF16),
                        pltpu.SemaphoreType.DMA((3, N_EXPERTS))],
    )
    return pl.pallas_call(
        _combine_kernel,
        grid_spec=grid_spec,
        out_shape=jax.ShapeDtypeStruct((TOKENS, D_MODEL), F32),
        compiler_params=_params("arbitrary"),
        name="moe_combine",
    )(wstart, nround, dest, ys, x, post_g)


def _routing_plan(sel, rank, counts):
    counts = counts.reshape(N_EXPERTS)
    padded = ((counts + EXPERT_PAD - 1) // EXPERT_PAD) * EXPERT_PAD
    region = ((counts + EXPERT_REGION - 1) // EXPERT_REGION) * EXPERT_REGION
    offs = jnp.cumsum(region) - region
    dest = jnp.where(sel == 1, rank + offs[:, None], -1)

    n_reg = region // EXPERT_REGION
    cum_reg = jnp.cumsum(n_reg)
    qs = jnp.arange(N_REGIONS, dtype=I32)
    qc = jnp.minimum(qs, cum_reg[-1] - 1)[:, None]
    owner = ((qc >= (cum_reg - n_reg)[None, :]) & (qc < cum_reg[None, :])).astype(I32)
    pick = lambda own, v: jnp.sum(own * v[None, :], axis=1)
    experts = jnp.arange(N_EXPERTS, dtype=I32)
    rexp = pick(owner, experts).astype(I32)
    rows_left = pick(owner, offs + padded) - qs * EXPERT_REGION
    n_unit = jnp.clip(rows_left // EXPERT_PAD, 0, EXPERT_REGION // EXPERT_PAD).astype(I32)

    group_end = (rank + sel)[:, LANES - 1::LANES]
    js = jnp.arange(N_SORT_BLOCKS, dtype=I32)
    owner_s = jnp.repeat(owner, EXPERT_REGION // SORT_BLOCK, axis=0)
    e_s = pick(owner_s, experts)
    counts_s = pick(owner_s, counts)
    lo_row = js * SORT_BLOCK - pick(owner_s, offs)
    has_rows = lo_row < counts_s
    hi_row = jnp.minimum(lo_row + SORT_BLOCK, counts_s)
    ends = jnp.sum(owner_s[:, :, None] * group_end[None, :, :], axis=1)
    g_lo = jnp.sum((ends <= lo_row[:, None]).astype(I32), axis=1)
    g_hi = jnp.sum((ends < hi_row[:, None]).astype(I32), axis=1)
    span = (g_hi - g_lo + 1) * LANES
    n_win = jnp.where(has_rows, (span + TOKEN_WINDOW - 1) // TOKEN_WINDOW, 0).astype(I32)
    start = jnp.where(has_rows, g_lo * LANES, 0).astype(I32)
    d_plan = (e_s.astype(I32), start, n_win)

    tile_lo = rank[:, ::COMBINE_TILE]
    tile_hi = jnp.concatenate([tile_lo[:, 1:], counts[:, None]], axis=1)
    lo = tile_lo + offs[:, None]
    hi = tile_hi + offs[:, None]
    wstart = jnp.minimum((lo // BF16_ROWS) * BF16_ROWS, SORTED_ROWS - COMBINE_WINDOW)
    rounds = jnp.where(hi > lo, (hi - wstart + COMBINE_WINDOW - 1) // COMBINE_WINDOW, 0)
    n_round = jnp.maximum(jnp.max(rounds, axis=0), 1).astype(I32)
    c_plan = (wstart.T.reshape(-1).astype(I32), n_round)
    return dest, (rexp, n_unit), d_plan, c_plan


def _moe(l, x1, h, gw, sel, rank, counts, w1, w3, w2, post_g):
    dest, e_plan, d_plan, c_plan = _routing_plan(sel, rank, counts)
    dest3 = dest.reshape(N_EXPERTS, 1, TOKENS)
    gw3 = gw.reshape(N_EXPERTS, 1, TOKENS)
    xs, gs = _dispatch(d_plan, dest3, gw3, h)
    ys = _experts(*e_plan, xs, gs, w1, w3, w2)
    return _combine(l, *c_plan, dest, ys, x1, post_g)


def kernel(x, pre_mix_g, post_mix_g, w_in, w_fourier, w_gmlp, w_out, w_spatial, b_spatial,
           gmlp_ln_g, gmlp_ln_b, pre_ffn_g, post_ffn_g, ffn_w1, ffn_w3, ffn_w2,
           router_w, router_b, moe_w1, moe_w3, moe_w2):
    pos = jnp.asarray(_POS_DFT).astype(BF16)
    chan = jnp.asarray(_CHAN_DFT).astype(BF16)
    tri = jnp.asarray(np.triu(np.ones((MIX_TILE, MIX_TILE), np.float32), 1)).astype(BF16)

    rows = lambda v: v.reshape(v.shape[0], 1, v.shape[1])
    pre_mix_g, post_mix_g, pre_ffn_g, post_ffn_g, gmlp_ln_g, gmlp_ln_b = map(
        rows, (pre_mix_g, post_mix_g, pre_ffn_g, post_ffn_g, gmlp_ln_g, gmlp_ln_b))
    b_full = jnp.repeat(jnp.swapaxes(b_spatial, 1, 2), GROUP_DIM, axis=2)
    router_wt = jnp.swapaxes(router_w, 1, 2)
    router_b = router_b.reshape(router_b.shape[0], N_EXPERTS, 1)

    assert DEPTH == 2
    moe_rows = lambda w: w[0].reshape(-1, w.shape[-1])
    moe_back = lambda w, like: w.reshape(like.shape[1:])
    mixer_weights = lambda l: [(w, l) for w in (w_in, w_fourier, w_gmlp, w_out)]

    xf = x.reshape(TOKENS, D_MODEL)
    fa, (win, wf, wg, wo) = _fourier(0, xf.reshape(BATCH, SEQ, D_MODEL), pre_mix_g, w_in, chan, pos,
                                     side=mixer_weights(0))
    x1, (ffn1, ffn3, ffn2, moe1) = _mixer_body(
        0, xf, fa, pre_mix_g, win, w_spatial, b_full, gmlp_ln_g, gmlp_ln_b, wf, wg, wo, post_mix_g,
        side=(ffn_w1[0], ffn_w3[0], ffn_w2[0], moe_rows(moe_w1)))
    xf, (moe3, moe2) = _dense_ffn(0, x1, pre_ffn_g, ffn1, ffn3, ffn2, post_ffn_g,
                                  side=(moe_rows(moe_w3), moe_rows(moe_w2)))

    fa, (win, wf, wg, wo) = _fourier(1, xf.reshape(BATCH, SEQ, D_MODEL), pre_mix_g, w_in, chan, pos,
                                     side=mixer_weights(1))
    x1, h, gw, sel, rank, counts = _mixer_body_router(
        1, xf, fa, pre_mix_g, win, w_spatial, b_full, gmlp_ln_g, gmlp_ln_b, wf, wg, wo, post_mix_g,
        pre_ffn_g, router_wt, router_b, tri)
    xf = _moe(1, x1, h, gw, sel, rank, counts, moe_back(moe1, moe_w1), moe_back(moe3, moe_w3),
              moe_back(moe2, moe_w2), post_ffn_g)
    return xf.reshape(BATCH, SEQ, D_MODEL)
```

```python
import functools

import numpy as np
import jax
import jax.numpy as jnp
from jax import lax
from jax.experimental import pallas as pl
from jax.experimental.pallas import tpu as pltpu

F32 = jnp.float32
BF16 = jnp.bfloat16
I32 = jnp.int32

D_MODEL = 1024
BATCH = 8
SEQ = 2048
HALF_SEQ = SEQ // 2
TOKENS = BATCH * SEQ
DEPTH = 2
N_GROUPS = 4
GROUP_DIM = 128
MIX_WIDTH = N_GROUPS * GROUP_DIM
CHUNK = 128
D_IN = 3 * MIX_WIDTH + 2 * D_MODEL
D_FF_DENSE = 2816
N_EXPERTS = 8
D_FF_EXPERT = 3584
RMS_EPS = 1e-6
LN_EPS = 1e-5
LANES = 128

ROW_TILE = 512
MIX_TILE = 512
EXPERT_BLOCK = 512
EXPERT_PAD = 256
EXPERT_REGION = 1024
SORT_BLOCK = 256
TOKEN_WINDOW = 1280
COMBINE_TILE = 256
COMBINE_WINDOW = 128
BF16_ROWS = 16
EXPERT_FF_TILE = 1792

SORTED_ROWS = 2 * TOKENS + N_EXPERTS * EXPERT_REGION
N_REGIONS = SORTED_ROWS // EXPERT_REGION
N_SORT_BLOCKS = SORTED_ROWS // SORT_BLOCK
N_COMBINE_TILES = TOKENS // COMBINE_TILE

VMEM_LIMIT = 56 * 1024 * 1024


def _params(*sem):
    return pltpu.CompilerParams(dimension_semantics=sem, vmem_limit_bytes=VMEM_LIMIT)


def _resident(shape):
    nd = len(shape)
    return pl.BlockSpec(shape, lambda *_: (0,) * nd, pipeline_mode=pl.Buffered(1))


def _layer(shape, l):
    nd = len(shape)
    return pl.BlockSpec((None,) + tuple(shape), lambda *_: (l,) + (0,) * nd,
                        pipeline_mode=pl.Buffered(1))


def _rms(x, g):
    return x * lax.rsqrt(jnp.mean(x * x, axis=-1, keepdims=True) + RMS_EPS) * g


def _dot(a, b):
    return jnp.dot(a, b, preferred_element_type=F32)


def _dot_nt(a, b):
    return lax.dot_general(a, b, (((1,), (1,)), ((), ())), preferred_element_type=F32)


def _dot_tn(a, b):
    return lax.dot_general(a, b, (((0,), (0,)), ((), ())), preferred_element_type=F32)


def _gelu(x):
    return 0.5 * x * (1.0 + lax.erf(x * np.float32(np.sqrt(0.5))))


def _cast_once(*pairs):
    chunk = 128

    @pl.when(pl.program_id(0) == 0)
    def _():
        for src_ref, dst_ref in pairs:
            def body(i, carry, src_ref=src_ref, dst_ref=dst_ref):
                rows = pl.ds(pl.multiple_of(i * chunk, chunk), chunk)
                dst_ref[rows, :] = src_ref[rows, :].astype(BF16)
                return carry
            lax.fori_loop(0, src_ref.shape[0] // chunk, body, 0)


def _split_bf16(x):
    hi = x.astype(BF16)
    return hi, (x - hi.astype(F32)).astype(BF16)


def _side_specs(arrays, steps):
    in_specs, out_specs, shapes, operands = [], [], [], []
    for entry in arrays:
        a, l = entry if isinstance(entry, tuple) else (entry, None)
        rows, cols = a.shape[-2:]
        chunk, per = rows // steps, 1
        if chunk % BF16_ROWS:
            chunk, per = 2 * chunk, 2
        assert rows % chunk == 0 and chunk % BF16_ROWS == 0, (a.shape, steps)
        out_specs.append(pl.BlockSpec((chunk, cols), lambda i, per=per: (i // per, 0)))
        if l is None:
            in_specs.append(out_specs[-1])
        else:
            in_specs.append(pl.BlockSpec((None, chunk, cols), lambda i, per=per, l=l: (l, i // per, 0)))
        shapes.append(jax.ShapeDtypeStruct((rows, cols), BF16))
        operands.append(a)
    return in_specs, out_specs, shapes, operands


def _with_side_casts(body, n_in, n_out, n_side):
    def kernel(*refs):
        ins, rest = refs[:n_in], refs[n_in:]
        srcs, rest = rest[:n_side], rest[n_side:]
        outs, rest = rest[:n_out], rest[n_out:]
        dsts, scratch = rest[:n_side], rest[n_side:]
        body(*ins, *outs, *scratch)
        for src, dst in zip(srcs, dsts):
            dst[...] = src[...].astype(BF16)
    return kernel


def _dft_tables():
    k = np.arange(HALF_SEQ, dtype=np.int64)
    scale = 1.0 / np.sqrt(SEQ)
    halves = []
    for p in (0, 1):
        n = 2 * np.arange(HALF_SEQ, dtype=np.int64) + p
        ang = 2.0 * np.pi * ((k[:, None] * n[None, :]) % SEQ).astype(np.float64) / SEQ
        halves.append(np.concatenate([np.cos(ang) * scale, -np.sin(ang) * scale], axis=1))
    pos = np.stack(halves)
    c = np.arange(GROUP_DIM, dtype=np.int64)
    angc = 2.0 * np.pi * ((c[:, None] * c[None, :]) % GROUP_DIM).astype(np.float64) / GROUP_DIM
    scalec = 1.0 / np.sqrt(GROUP_DIM)
    chan = np.concatenate([np.cos(angc) * scalec, np.sin(angc) * scalec], axis=1)
    return pos.astype(np.float32), chan.astype(np.float32)


_POS_DFT, _CHAN_DFT = _dft_tables()


def _fourier_kernel(l, x_ref, g_ref, wa32_ref, chan_ref, pos_ref, out_ref, rhs_ref, wa_ref):
    _cast_once((wa32_ref, wa_ref))
    for p in (0, 1):
        h = _rms(x_ref[0, :, p * D_MODEL:(p + 1) * D_MODEL], g_ref[l:l + 1, :]).astype(BF16)
        za = _dot(h, wa_ref[...]).astype(BF16)
        for g in range(N_GROUPS):
            cols = slice(g * GROUP_DIM, (g + 1) * GROUP_DIM)
            xcs = _dot(za[:, cols], chan_ref[...])
            rhs_ref[p, 0:HALF_SEQ, cols] = xcs[:, 0:GROUP_DIM].astype(BF16)
            rhs_ref[p, HALF_SEQ:SEQ, cols] = xcs[:, GROUP_DIM:2 * GROUP_DIM].astype(BF16)

    even = _dot(pos_ref[0], rhs_ref[0])
    odd = _dot(pos_ref[1], rhs_ref[1])
    out_ref[0, 0] = (even + odd).astype(BF16)
    out_ref[0, 1] = (even - odd).astype(BF16)


def _fourier(l, x, g, w_in, chan, pos, side=()):
    side_in, side_out, side_shapes, side_ops = _side_specs(side, BATCH)
    outs = pl.pallas_call(
        _with_side_casts(functools.partial(_fourier_kernel, l), 5, 1, len(side)),
        grid=(BATCH,),
        in_specs=[pl.BlockSpec((1, HALF_SEQ, 2 * D_MODEL), lambda b: (b, 0, 0)),
                  _resident((DEPTH, D_MODEL)), _layer((D_MODEL, MIX_WIDTH), l),
                  _resident((GROUP_DIM, 2 * GROUP_DIM)), _resident((2, HALF_SEQ, SEQ))] + side_in,
        out_specs=[pl.BlockSpec((1, 2, HALF_SEQ, MIX_WIDTH), lambda b: (b, 0, 0, 0))] + side_out,
        out_shape=[jax.ShapeDtypeStruct((BATCH, 2, HALF_SEQ, MIX_WIDTH), BF16)] + side_shapes,
        scratch_shapes=[pltpu.VMEM((2, SEQ, MIX_WIDTH), BF16),
                        pltpu.VMEM((D_MODEL, MIX_WIDTH), BF16)],
        compiler_params=_params("arbitrary"),
        name="fourier_mix",
    )(x, g, w_in, chan, pos, *side_ops)
    return outs[0].reshape(TOKENS, MIX_WIDTH), outs[1:]


def _mix_common(l, x_ref, fa_ref, prg_ref, win_ref, ws_ref, bs_ref, lng_ref, lnb_ref,
                wf_ref, wg_ref, wo_ref, pg_ref, x1_ref, sg_ref):
    x = x_ref[...]
    h = _rms(x, prg_ref[l:l + 1, :]).astype(BF16)
    v = _gelu(_dot(h, win_ref[:, 2 * MIX_WIDTH:3 * MIX_WIDTH]))
    u = _gelu(_dot(h, win_ref[:, MIX_WIDTH:2 * MIX_WIDTH]))
    gate = jax.nn.sigmoid(_dot(h, win_ref[:, 3 * MIX_WIDTH:D_IN]))
    ya = _dot(fa_ref[...], wf_ref[...])
    mu = jnp.mean(v, axis=-1, keepdims=True)
    vc = v - mu
    var = jnp.mean(vc * vc, axis=-1, keepdims=True)
    vln = (vc * lax.rsqrt(var + LN_EPS) * lng_ref[l:l + 1, :] + lnb_ref[l:l + 1, :]).astype(BF16)
    ws = [ws_ref[g].astype(BF16) for g in range(N_GROUPS)]
    for c in range(x_ref.shape[0] // CHUNK):
        rows = slice(c * CHUNK, (c + 1) * CHUNK)
        for g in range(N_GROUPS):
            cols = slice(g * GROUP_DIM, (g + 1) * GROUP_DIM)
            sv = _dot(ws[g], vln[rows, cols]) + bs_ref[:, cols]
            sg_ref[rows, cols] = (u[rows, cols] * sv).astype(BF16)
    yb = _dot(sg_ref[...], wg_ref[...])
    m = (gate[:, 0:D_MODEL] * ya + gate[:, D_MODEL:2 * D_MODEL] * yb).astype(BF16)
    half = x_ref.shape[0] // 2
    for r in range(2):
        rows = slice(r * half, (r + 1) * half)
        y = _dot(m[rows, :], wo_ref[...])
        x1_ref[rows, :] = x[rows, :] + _rms(y, pg_ref[l:l + 1, :])


def _mix_kernel(l, *refs):
    _mix_common(l, *refs)


def _mix_router_kernel(l, *refs):
    (*ins, fg_ref, rwt_ref, rb_ref, tri_ref,
     x1_ref, h_ref, gw_ref, sel_ref, rank_ref, cnt_ref, sg_ref, carry_ref) = refs

    @pl.when(pl.program_id(0) == 0)
    def _():
        carry_ref[...] = jnp.zeros_like(carry_ref)

    _mix_common(l, *ins, x1_ref, sg_ref)
    h = _rms(x1_ref[...], fg_ref[l:l + 1, :])
    h_hi, h_lo = _split_bf16(h)
    h_ref[...] = h_hi

    w_hi, w_lo = _split_bf16(rwt_ref[...])
    logits = _dot_nt(w_hi, h_hi) + _dot_nt(w_hi, h_lo) + _dot_nt(w_lo, h_hi) + rb_ref[...]
    row = lax.broadcasted_iota(I32, logits.shape, 0)
    m1 = jnp.max(logits, axis=0, keepdims=True)
    i1 = jnp.min(jnp.where(logits == m1, row, N_EXPERTS), axis=0, keepdims=True)
    rest = jnp.where(row == i1, -jnp.inf, logits)
    m2 = jnp.max(rest, axis=0, keepdims=True)
    i2 = jnp.min(jnp.where(rest == m2, row, N_EXPERTS), axis=0, keepdims=True)
    e2 = jnp.exp(m2 - m1)
    den = 1.0 + e2
    gw_ref[...] = jnp.where(row == i1, 1.0 / den, 0.0) + jnp.where(row == i2, e2 / den, 0.0)
    sel = jnp.where((row == i1) | (row == i2), 1.0, 0.0)
    sel_ref[...] = sel.astype(I32)
    before = _dot(sel.astype(BF16), tri_ref[...]) + carry_ref[...]
    rank_ref[...] = before.astype(I32)
    carry_ref[...] = carry_ref[...] + jnp.sum(sel, axis=1, keepdims=True)
    cnt_ref[...] = carry_ref[...].astype(I32)


def _mix_specs(l):
    row = lambda w: pl.BlockSpec((MIX_TILE, w), lambda i: (i, 0))
    in_specs = [row(D_MODEL), row(MIX_WIDTH), _resident((DEPTH, D_MODEL)), _resident((D_MODEL, D_IN)),
                _layer((N_GROUPS, CHUNK, CHUNK), l), _layer((CHUNK, MIX_WIDTH), l),
                _resident((DEPTH, MIX_WIDTH)), _resident((DEPTH, MIX_WIDTH)),
                _resident((MIX_WIDTH, D_MODEL)), _resident((MIX_WIDTH, D_MODEL)),
                _resident((D_MODEL, D_MODEL)), _resident((DEPTH, D_MODEL))]
    return row, in_specs, [pltpu.VMEM((MIX_TILE, MIX_WIDTH), BF16)]


def _mixer_body(l, x, fa, pre_g, w_in, w_s, b_full, ln_g, ln_b, w_f, w_g, w_o, post_g, side=()):
    row, in_specs, scratch = _mix_specs(l)
    steps = TOKENS // MIX_TILE
    side_in, side_out, side_shapes, side_ops = _side_specs(side, steps)
    outs = pl.pallas_call(
        _with_side_casts(functools.partial(_mix_kernel, l), len(in_specs), 1, len(side)),
        grid=(steps,),
        in_specs=in_specs + side_in,
        out_specs=[row(D_MODEL)] + side_out,
        out_shape=[jax.ShapeDtypeStruct((TOKENS, D_MODEL), F32)] + side_shapes,
        scratch_shapes=scratch,
        compiler_params=_params("arbitrary"),
        name="mixer_body",
    )(x, fa, pre_g, w_in, w_s, b_full, ln_g, ln_b, w_f, w_g, w_o, post_g, *side_ops)
    return outs[0], outs[1:]


def _mixer_body_router(l, x, fa, pre_g, w_in, w_s, b_full, ln_g, ln_b, w_f, w_g, w_o, post_g,
                       ffn_g, router_wt, router_b, tri):
    row, in_specs, scratch = _mix_specs(l)
    in_specs += [_resident((DEPTH, D_MODEL)), _layer((N_EXPERTS, D_MODEL), l // 2),
                 _layer((N_EXPERTS, 1), l // 2), _resident((MIX_TILE, MIX_TILE))]
    col = pl.BlockSpec((N_EXPERTS, MIX_TILE), lambda i: (0, i))
    per_token = lambda dt: jax.ShapeDtypeStruct((N_EXPERTS, TOKENS), dt)
    return pl.pallas_call(
        functools.partial(_mix_router_kernel, l),
        grid=(TOKENS // MIX_TILE,),
        in_specs=in_specs,
        out_specs=[row(D_MODEL), row(D_MODEL), col, col, col,
                   pl.BlockSpec((N_EXPERTS, 1), lambda i: (0, 0))],
        out_shape=[jax.ShapeDtypeStruct((TOKENS, D_MODEL), F32),
                   jax.ShapeDtypeStruct((TOKENS, D_MODEL), BF16),
                   per_token(F32), per_token(I32), per_token(I32),
                   jax.ShapeDtypeStruct((N_EXPERTS, 1), I32)],
        scratch_shapes=scratch + [pltpu.VMEM((N_EXPERTS, 1), F32)],
        compiler_params=_params("arbitrary"),
        name="mixer_body_router",
    )(x, fa, pre_g, w_in, w_s, b_full, ln_g, ln_b, w_f, w_g, w_o, post_g,
      ffn_g, router_wt, router_b, tri)


def _dense_ffn_kernel(l, x_ref, fg_ref, w1_ref, w3_ref, w2_ref, pg_ref, out_ref):
    half = x_ref.shape[0] // 2
    for r in range(2):
        rows = slice(r * half, (r + 1) * half)
        x = x_ref[rows, :]
        h = _rms(x, fg_ref[l:l + 1, :]).astype(BF16)
        a = _dot(h, w1_ref[...])
        b = _dot(h, w3_ref[...])
        y = _dot((jax.nn.silu(a) * b).astype(BF16), w2_ref[...])
        out_ref[rows, :] = x + _rms(y, pg_ref[l:l + 1, :])


def _dense_ffn(l, x, ffn_g, w1, w3, w2, post_g, side=()):
    row = pl.BlockSpec((ROW_TILE, D_MODEL), lambda i: (i, 0))
    steps = TOKENS // ROW_TILE
    side_in, side_out, side_shapes, side_ops = _side_specs(side, steps)
    outs = pl.pallas_call(
        _with_side_casts(functools.partial(_dense_ffn_kernel, l), 6, 1, len(side)),
        grid=(steps,),
        in_specs=[row, _resident((DEPTH, D_MODEL)), _resident((D_MODEL, D_FF_DENSE)),
                  _resident((D_MODEL, D_FF_DENSE)), _resident((D_FF_DENSE, D_MODEL)),
                  _resident((DEPTH, D_MODEL))] + side_in,
        out_specs=[row] + side_out,
        out_shape=[jax.ShapeDtypeStruct((TOKENS, D_MODEL), F32)] + side_shapes,
        compiler_params=_params("arbitrary"),
        name="dense_ffn",
    )(x, ffn_g, w1, w3, w2, post_g, *side_ops)
    return outs[0], outs[1:]


def _window_hits(k, block, start_ref, dest_ref):
    wanted = start_ref[block] + k * TOKEN_WINDOW
    begin = pl.multiple_of(jnp.minimum(wanted, TOKENS - TOKEN_WINDOW), LANES)
    tok = lax.broadcasted_iota(I32, (1, TOKEN_WINDOW), 1) + begin
    dest = jnp.where(tok >= wanted, dest_ref[:, pl.ds(begin, TOKEN_WINDOW)], -1)
    rows = lax.broadcasted_iota(I32, (SORT_BLOCK, TOKEN_WINDOW), 0) + block * SORT_BLOCK
    return begin, rows == dest


def _dispatch_kernel(exp_ref, start_ref, nwin_ref, dest_ref, gw_ref, h_ref, xs_ref, gs_ref):
    j = pl.program_id(0)

    def window(k):
        begin, hit = _window_hits(k, j, start_ref, dest_ref)
        onehot = jnp.where(hit, 1.0, 0.0).astype(BF16)
        picked = _dot(onehot, h_ref[pl.ds(begin, TOKEN_WINDOW), :]).astype(BF16)
        gw = gw_ref[:, pl.ds(begin, TOKEN_WINDOW)]
        return picked, jnp.sum(jnp.where(hit, gw, 0.0), axis=-1, keepdims=True)

    @pl.when(nwin_ref[j] == 0)
    def _():
        xs_ref[...] = jnp.zeros_like(xs_ref)
        gs_ref[...] = jnp.zeros_like(gs_ref)

    @pl.when(nwin_ref[j] > 0)
    def _():
        xs_ref[...], gs_ref[...] = window(0)

        def more(k, carry):
            picked, gsum = window(k)
            xs_ref[...] += picked
            gs_ref[...] += gsum
            return carry
        lax.fori_loop(1, nwin_ref[j], more, 0)


def _per_expert_row():
    return pl.BlockSpec((None, 1, TOKENS),
                        lambda j, e, s, n: (e[jnp.minimum(j, N_SORT_BLOCKS - 1)], 0, 0))


def _dispatch(plan, dest3, gw3, h):
    exp, start, nwin = plan
    grid_spec = pltpu.PrefetchScalarGridSpec(
        num_scalar_prefetch=3,
        grid=(N_SORT_BLOCKS,),
        in_specs=[_per_expert_row(), _per_expert_row(),
                  pl.BlockSpec((TOKENS, D_MODEL), lambda j, e, s, n: (0, 0),
                               pipeline_mode=pl.Buffered(1))],
        out_specs=[pl.BlockSpec((SORT_BLOCK, D_MODEL), lambda j, e, s, n: (j, 0)),
                   pl.BlockSpec((SORT_BLOCK, 1), lambda j, e, s, n: (j, 0))],
    )
    return pl.pallas_call(
        _dispatch_kernel,
        grid_spec=grid_spec,
        out_shape=[jax.ShapeDtypeStruct((SORTED_ROWS, D_MODEL), BF16),
                   jax.ShapeDtypeStruct((SORTED_ROWS, 1), F32)],
        compiler_params=_params("arbitrary"),
        name="moe_dispatch",
    )(exp, start, nwin, dest3, gw3, h)


def _expert_kernel(rexp_ref, nunit_ref, xs_ref, gs_ref, w1_ref, w3_ref, w2_ref, ys_ref, acc_ref):
    q = pl.program_id(0)
    f = pl.program_id(1)
    last = pl.num_programs(1) - 1
    nunit = nunit_ref[q]

    @pl.when((q == 0) & (f == 0))
    def _():
        acc_ref[...] = jnp.zeros_like(acc_ref)

    def swiglu_part(n_rows):
        for start in range(0, n_rows, EXPERT_BLOCK):
            rows = slice(start, min(start + EXPERT_BLOCK, n_rows))
            x = xs_ref[rows, :]
            a = _dot(x, w1_ref[...])
            b = _dot(x, w3_ref[...])
            part = _dot((jax.nn.silu(a) * b).astype(BF16), w2_ref[...])
            acc_ref[rows, :] = jnp.where(f == 0, 0.0, acc_ref[rows, :]) + part

        @pl.when(f == last)
        def _():
            ys_ref[0:n_rows, :] = (acc_ref[0:n_rows, :] * gs_ref[0:n_rows, :]).astype(BF16)
            if n_rows < EXPERT_REGION:
                ys_ref[n_rows:EXPERT_REGION, :] = jnp.zeros((EXPERT_REGION - n_rows, D_MODEL), BF16)

    for units in range(1, EXPERT_REGION // EXPERT_PAD + 1):
        pl.when(nunit == units)(functools.partial(swiglu_part, units * EXPERT_PAD))

    @pl.when((nunit == 0) & (f == last))
    def _():
        ys_ref[...] = jnp.zeros_like(ys_ref)


def _experts(rexp, nunit, xs, gs, w1, w3, w2):
    n_ff = D_FF_EXPERT // EXPERT_FF_TILE
    tile = lambda q, f: jnp.where(q % 2 == 0, f, n_ff - 1 - f)
    grid_spec = pltpu.PrefetchScalarGridSpec(
        num_scalar_prefetch=2,
        grid=(N_REGIONS, D_FF_EXPERT // EXPERT_FF_TILE),
        in_specs=[
            pl.BlockSpec((EXPERT_REGION, D_MODEL), lambda q, f, re, nh: (q, 0)),
            pl.BlockSpec((EXPERT_REGION, 1), lambda q, f, re, nh: (q, 0)),
            pl.BlockSpec((None, D_MODEL, EXPERT_FF_TILE), lambda q, f, re, nh: (re[q], 0, tile(q, f))),
            pl.BlockSpec((None, D_MODEL, EXPERT_FF_TILE), lambda q, f, re, nh: (re[q], 0, tile(q, f))),
            pl.BlockSpec((None, EXPERT_FF_TILE, D_MODEL), lambda q, f, re, nh: (re[q], tile(q, f), 0)),
        ],
        out_specs=pl.BlockSpec((EXPERT_REGION, D_MODEL), lambda q, f, re, nh: (q, 0)),
        scratch_shapes=[pltpu.VMEM((EXPERT_REGION, D_MODEL), F32)],
    )
    return pl.pallas_call(
        _expert_kernel,
        grid_spec=grid_spec,
        out_shape=jax.ShapeDtypeStruct((SORTED_ROWS, D_MODEL), BF16),
        compiler_params=_params("arbitrary", "arbitrary"),
        name="moe_experts",
    )(rexp, nunit, xs, gs, w1, w3, w2)


def _combine_copies(wstart_ref, ys_hbm, buf_ref, sem_ref, tile, rnd, slot):
    copies = []
    for e in range(N_EXPERTS):
        wanted = wstart_ref[tile * N_EXPERTS + e] + rnd * COMBINE_WINDOW
        begin = pl.multiple_of(jnp.minimum(wanted, SORTED_ROWS - COMBINE_WINDOW), BF16_ROWS)
        copies.append(pltpu.make_async_copy(
            ys_hbm.at[pl.ds(begin, COMBINE_WINDOW), :],
            buf_ref.at[slot, pl.ds(e * COMBINE_WINDOW, COMBINE_WINDOW), :],
            sem_ref.at[slot, e]))
    return copies


def _combine_kernel(l, wstart_ref, nround_ref, dest_ref, ys_hbm, x_ref, pg_ref, out_ref,
                    buf_ref, sem_ref):
    i = pl.program_id(0)
    n_tiles = pl.num_programs(0)
    slot = lax.rem(i, 2)

    @pl.when(i == 0)
    def _():
        for c in _combine_copies(wstart_ref, ys_hbm, buf_ref, sem_ref, 0, 0, 0):
            c.start()

    @pl.when(i + 1 < n_tiles)
    def _():
        for c in _combine_copies(wstart_ref, ys_hbm, buf_ref, sem_ref, i + 1, 0, 1 - slot):
            c.start()

    def gather(rnd, fetch_slot):
        sub = lax.broadcasted_iota(I32, (COMBINE_WINDOW, 1), 0)
        pieces = []
        for e in range(N_EXPERTS):
            wanted = wstart_ref[i * N_EXPERTS + e] + rnd * COMBINE_WINDOW
            begin = jnp.minimum(wanted, SORTED_ROWS - COMBINE_WINDOW)
            rowid = sub + begin
            rowid = jnp.where(rowid >= wanted, rowid, -2)
            pieces.append(jnp.where(rowid == dest_ref[e:e + 1, :], 1.0, 0.0).astype(BF16))
        onehot = jnp.concatenate(pieces, axis=0)
        return _dot_tn(onehot, buf_ref[fetch_slot])

    for c in _combine_copies(wstart_ref, ys_hbm, buf_ref, sem_ref, i, 0, slot):
        c.wait()
    y = gather(0, slot)

    def extra(rnd, y):
        copies = _combine_copies(wstart_ref, ys_hbm, buf_ref, sem_ref, i, rnd, 2)
        for c in copies:
            c.start()
        for c in copies:
            c.wait()
        return y + gather(rnd, 2)

    y = lax.fori_loop(1, nround_ref[i], extra, y)
    out_ref[...] = x_ref[...] + _rms(y, pg_ref[l:l + 1, :])


def _combine(l, wstart, nround, dest, ys, x, post_g):
    tile = lambda i, ws, nr: (i, 0)
    grid_spec = pltpu.PrefetchScalarGridSpec(
        num_scalar_prefetch=2,
        grid=(TOKENS // COMBINE_TILE,),
        in_specs=[
            pl.BlockSpec((N_EXPERTS, COMBINE_TILE), lambda i, ws, nr: (0, i)),
            pl.BlockSpec(memory_space=pl.ANY),
            pl.BlockSpec((COMBINE_TILE, D_MODEL), tile),
            pl.BlockSpec((DEPTH, D_MODEL), lambda i, ws, nr: (0, 0)),
        ],
        out_specs=pl.BlockSpec((COMBINE_TILE, D_MODEL), tile),
        scratch_shapes=[pltpu.VMEM((3, N_EXPERTS * COMBINE_WINDOW, D_MODEL), BF16),
                        pltpu.SemaphoreType.DMA((3, N_EXPERTS))],
    )
    return pl.pallas_call(
        functools.partial(_combine_kernel, l),
        grid_spec=grid_spec,
        out_shape=jax.ShapeDtypeStruct((TOKENS, D_MODEL), F32),
        compiler_params=_params("arbitrary"),
        name="moe_combine",
    )(wstart, nround, dest, ys, x, post_g)


def _routing_plan(sel, rank, counts):
    counts = counts.reshape(N_EXPERTS)
    padded = ((counts + EXPERT_PAD - 1) // EXPERT_PAD) * EXPERT_PAD
    region = ((counts + EXPERT_REGION - 1) // EXPERT_REGION) * EXPERT_REGION
    offs = jnp.cumsum(region) - region
    dest = jnp.where(sel == 1, rank + offs[:, None], -1)

    n_reg = region // EXPERT_REGION
    cum_reg = jnp.cumsum(n_reg)
    qs = jnp.arange(N_REGIONS, dtype=I32)
    qc = jnp.minimum(qs, cum_reg[-1] - 1)[:, None]
    owner = ((qc >= (cum_reg - n_reg)[None, :]) & (qc < cum_reg[None, :])).astype(I32)
    pick = lambda own, v: jnp.sum(own * v[None, :], axis=1)
    experts = jnp.arange(N_EXPERTS, dtype=I32)
    rexp = pick(owner, experts).astype(I32)
    rows_left = pick(owner, offs + padded) - qs * EXPERT_REGION
    n_unit = jnp.clip(rows_left // EXPERT_PAD, 0, EXPERT_REGION // EXPERT_PAD).astype(I32)

    group_end = (rank + sel)[:, LANES - 1::LANES]
    js = jnp.arange(N_SORT_BLOCKS, dtype=I32)
    owner_s = jnp.repeat(owner, EXPERT_REGION // SORT_BLOCK, axis=0)
    e_s = pick(owner_s, experts)
    counts_s = pick(owner_s, counts)
    lo_row = js * SORT_BLOCK - pick(owner_s, offs)
    has_rows = lo_row < counts_s
    hi_row = jnp.minimum(lo_row + SORT_BLOCK, counts_s)
    ends = jnp.sum(owner_s[:, :, None] * group_end[None, :, :], axis=1)
    g_lo = jnp.sum((ends <= lo_row[:, None]).astype(I32), axis=1)
    g_hi = jnp.sum((ends < hi_row[:, None]).astype(I32), axis=1)
    span = (g_hi - g_lo + 1) * LANES
    n_win = jnp.where(has_rows, (span + TOKEN_WINDOW - 1) // TOKEN_WINDOW, 0).astype(I32)
    start = jnp.where(has_rows, g_lo * LANES, 0).astype(I32)
    d_plan = (e_s.astype(I32), start, n_win)

    tile_lo = rank[:, ::COMBINE_TILE]
    tile_hi = jnp.concatenate([tile_lo[:, 1:], counts[:, None]], axis=1)
    lo = tile_lo + offs[:, None]
    hi = tile_hi + offs[:, None]
    wstart = jnp.minimum((lo // BF16_ROWS) * BF16_ROWS, SORTED_ROWS - COMBINE_WINDOW)
    rounds = jnp.where(hi > lo, (hi - wstart + COMBINE_WINDOW - 1) // COMBINE_WINDOW, 0)
    n_round = jnp.maximum(jnp.max(rounds, axis=0), 1).astype(I32)
    c_plan = (wstart.T.reshape(-1).astype(I32), n_round)
    return dest, (rexp, n_unit), d_plan, c_plan


def _moe(l, x1, h, gw, sel, rank, counts, w1, w3, w2, post_g):
    dest, e_plan, d_plan, c_plan = _routing_plan(sel, rank, counts)
    dest3 = dest.reshape(N_EXPERTS, 1, TOKENS)
    gw3 = gw.reshape(N_EXPERTS, 1, TOKENS)
    xs, gs = _dispatch(d_plan, dest3, gw3, h)
    ys = _experts(*e_plan, xs, gs, w1, w3, w2)
    return _combine(l, *c_plan, dest, ys, x1, post_g)


def kernel(x, pre_mix_g, post_mix_g, w_in, w_fourier, w_gmlp, w_out, w_spatial, b_spatial,
           gmlp_ln_g, gmlp_ln_b, pre_ffn_g, post_ffn_g, ffn_w1, ffn_w3, ffn_w2,
           router_w, router_b, moe_w1, moe_w3, moe_w2):
    pos = jnp.asarray(_POS_DFT).astype(BF16)
    chan = jnp.asarray(_CHAN_DFT).astype(BF16)
    tri = jnp.asarray(np.triu(np.ones((MIX_TILE, MIX_TILE), np.float32), 1)).astype(BF16)

    b_full = jnp.repeat(jnp.swapaxes(b_spatial, 1, 2), GROUP_DIM, axis=2)
    router_wt = jnp.swapaxes(router_w, 1, 2)
    router_b = router_b.reshape(router_b.shape[0], N_EXPERTS, 1)

    assert DEPTH == 2
    moe_rows = lambda w: w[0].reshape(-1, w.shape[-1])
    moe_back = lambda w, like: w.reshape(like.shape[1:])
    mixer_weights = lambda l: [(w, l) for w in (w_in, w_fourier, w_gmlp, w_out)]

    xf = x.reshape(TOKENS, D_MODEL)
    pairs = lambda v: v.reshape(BATCH, HALF_SEQ, 2 * D_MODEL)
    fa, (win, wf, wg, wo) = _fourier(0, pairs(xf), pre_mix_g, w_in, chan, pos,
                                     side=mixer_weights(0))
    x1, (ffn1, ffn3, ffn2, moe1) = _mixer_body(
        0, xf, fa, pre_mix_g, win, w_spatial, b_full, gmlp_ln_g, gmlp_ln_b, wf, wg, wo, post_mix_g,
        side=(ffn_w1[0], ffn_w3[0], ffn_w2[0], moe_rows(moe_w1)))
    xf, (moe3, moe2) = _dense_ffn(0, x1, pre_ffn_g, ffn1, ffn3, ffn2, post_ffn_g,
                                  side=(moe_rows(moe_w3), moe_rows(moe_w2)))

    fa, (win, wf, wg, wo) = _fourier(1, pairs(xf), pre_mix_g, w_in, chan, pos,
                                     side=mixer_weights(1))
    x1, h, gw, sel, rank, counts = _mixer_body_router(
        1, xf, fa, pre_mix_g, win, w_spatial, b_full, gmlp_ln_g, gmlp_ln_b, wf, wg, wo, post_mix_g,
        pre_ffn_g, router_wt, router_b, tri)
    xf = _moe(1, x1, h, gw, sel, rank, counts, moe_back(moe1, moe_w1), moe_back(moe3, moe_w3),
              moe_back(moe2, moe_w2), post_ffn_g)
    return xf.reshape(BATCH, SEQ, D_MODEL)
```

```python
import functools

import numpy as np
import jax
import jax.numpy as jnp
from jax import lax
from jax.experimental import pallas as pl
from jax.experimental.pallas import tpu as pltpu

F32 = jnp.float32
BF16 = jnp.bfloat16
I32 = jnp.int32

D_MODEL = 1024
BATCH = 8
SEQ = 2048
HALF_SEQ = SEQ // 2
TOKENS = BATCH * SEQ
DEPTH = 2
N_GROUPS = 4
GROUP_DIM = 128
MIX_WIDTH = N_GROUPS * GROUP_DIM
CHUNK = 128
D_IN = 3 * MIX_WIDTH + 2 * D_MODEL
D_FF_DENSE = 2816
N_EXPERTS = 8
D_FF_EXPERT = 3584
RMS_EPS = 1e-6
LN_EPS = 1e-5
LANES = 128

ROW_TILE = 512
MIX_TILE = 512
EXPERT_BLOCK = 512
EXPERT_PAD = 256
EXPERT_REGION = 1024
SORT_BLOCK = 256
TOKEN_WINDOW = 1280
COMBINE_TILE = 256
COMBINE_WINDOW = 128
BF16_ROWS = 16
EXPERT_FF_TILE = 1792

SORTED_ROWS = 2 * TOKENS + N_EXPERTS * EXPERT_REGION
N_REGIONS = SORTED_ROWS // EXPERT_REGION
N_SORT_BLOCKS = SORTED_ROWS // SORT_BLOCK
N_COMBINE_TILES = TOKENS // COMBINE_TILE

VMEM_LIMIT = 56 * 1024 * 1024


def _params(*sem):
    return pltpu.CompilerParams(dimension_semantics=sem, vmem_limit_bytes=VMEM_LIMIT)


def _resident(shape):
    nd = len(shape)
    return pl.BlockSpec(shape, lambda *_: (0,) * nd, pipeline_mode=pl.Buffered(1))


def _layer(shape, l):
    nd = len(shape)
    return pl.BlockSpec((None,) + tuple(shape), lambda *_: (l,) + (0,) * nd,
                        pipeline_mode=pl.Buffered(1))


def _rms(x, g):
    return x * lax.rsqrt(jnp.mean(x * x, axis=-1, keepdims=True) + RMS_EPS) * g


def _dot(a, b):
    return jnp.dot(a, b, preferred_element_type=F32)


def _dot_nt(a, b):
    return lax.dot_general(a, b, (((1,), (1,)), ((), ())), preferred_element_type=F32)


def _dot_tn(a, b):
    return lax.dot_general(a, b, (((0,), (0,)), ((), ())), preferred_element_type=F32)


def _gelu(x):
    return 0.5 * x * (1.0 + lax.erf(x * np.float32(np.sqrt(0.5))))


def _cast_once(*pairs):
    chunk = 128

    @pl.when(pl.program_id(0) == 0)
    def _():
        for src_ref, dst_ref in pairs:
            def body(i, carry, src_ref=src_ref, dst_ref=dst_ref):
                rows = pl.ds(pl.multiple_of(i * chunk, chunk), chunk)
                dst_ref[rows, :] = src_ref[rows, :].astype(BF16)
                return carry
            lax.fori_loop(0, src_ref.shape[0] // chunk, body, 0)


def _split_bf16(x):
    hi = x.astype(BF16)
    return hi, (x - hi.astype(F32)).astype(BF16)


def _side_specs(arrays, steps):
    in_specs, out_specs, shapes, operands = [], [], [], []
    for entry in arrays:
        a, l = entry if isinstance(entry, tuple) else (entry, None)
        rows, cols = a.shape[-2:]
        chunk, per = rows // steps, 1
        if chunk % BF16_ROWS:
            chunk, per = 2 * chunk, 2
        assert rows % chunk == 0 and chunk % BF16_ROWS == 0, (a.shape, steps)
        out_specs.append(pl.BlockSpec((chunk, cols), lambda i, per=per: (i // per, 0)))
        if l is None:
            in_specs.append(out_specs[-1])
        else:
            in_specs.append(pl.BlockSpec((None, chunk, cols), lambda i, per=per, l=l: (l, i // per, 0)))
        shapes.append(jax.ShapeDtypeStruct((rows, cols), BF16))
        operands.append(a)
    return in_specs, out_specs, shapes, operands


def _with_side_casts(body, n_in, n_out, n_side):
    def kernel(*refs):
        ins, rest = refs[:n_in], refs[n_in:]
        srcs, rest = rest[:n_side], rest[n_side:]
        outs, rest = rest[:n_out], rest[n_out:]
        dsts, scratch = rest[:n_side], rest[n_side:]
        body(*ins, *outs, *scratch)
        for src, dst in zip(srcs, dsts):
            dst[...] = src[...].astype(BF16)
    return kernel


def _dft_tables():
    k = np.arange(HALF_SEQ, dtype=np.int64)
    scale = 1.0 / np.sqrt(SEQ)
    halves = []
    for p in (0, 1):
        n = 2 * np.arange(HALF_SEQ, dtype=np.int64) + p
        ang = 2.0 * np.pi * ((k[:, None] * n[None, :]) % SEQ).astype(np.float64) / SEQ
        halves.append(np.concatenate([np.cos(ang) * scale, -np.sin(ang) * scale], axis=1))
    pos = np.stack(halves)
    c = np.arange(GROUP_DIM, dtype=np.int64)
    angc = 2.0 * np.pi * ((c[:, None] * c[None, :]) % GROUP_DIM).astype(np.float64) / GROUP_DIM
    scalec = 1.0 / np.sqrt(GROUP_DIM)
    chan = np.concatenate([np.cos(angc) * scalec, np.sin(angc) * scalec], axis=1)
    return pos.astype(np.float32), chan.astype(np.float32)


_POS_DFT, _CHAN_DFT = _dft_tables()


def _fourier_kernel(l, x_ref, g_ref, wa32_ref, chan_ref, pos_ref, out_ref, stage_ref, rhs_ref, wa_ref):
    _cast_once((wa32_ref, wa_ref))
    for half in range(SEQ // HALF_SEQ):
        rows = slice(half * HALF_SEQ, (half + 1) * HALF_SEQ)
        h = _rms(x_ref[0, rows, :], g_ref[l:l + 1, :]).astype(BF16)
        za = _dot(h, wa_ref[...]).astype(BF16)
        for g in range(N_GROUPS):
            xcs = _dot(za[:, g * GROUP_DIM:(g + 1) * GROUP_DIM], chan_ref[...])
            stage_ref[g, rows, :] = xcs[:, 0:GROUP_DIM]
            stage_ref[N_GROUPS + g, rows, :] = xcs[:, GROUP_DIM:2 * GROUP_DIM]
    for g in range(N_GROUPS):
        cols = slice(g * GROUP_DIM, (g + 1) * GROUP_DIM)
        for p in (0, 1):
            rows = pl.ds(p, HALF_SEQ, stride=2)
            rhs_ref[p, 0:HALF_SEQ, cols] = stage_ref[g, rows, :].astype(BF16)
            rhs_ref[p, HALF_SEQ:SEQ, cols] = stage_ref[N_GROUPS + g, rows, :].astype(BF16)

    even = _dot(pos_ref[0], rhs_ref[0])
    odd = _dot(pos_ref[1], rhs_ref[1])
    out_ref[0, 0] = (even + odd).astype(BF16)
    out_ref[0, 1] = (even - odd).astype(BF16)


def _fourier(l, x, g, w_in, chan, pos, side=()):
    side_in, side_out, side_shapes, side_ops = _side_specs(side, BATCH)
    outs = pl.pallas_call(
        _with_side_casts(functools.partial(_fourier_kernel, l), 5, 1, len(side)),
        grid=(BATCH,),
        in_specs=[pl.BlockSpec((1, SEQ, D_MODEL), lambda b: (b, 0, 0)),
                  _resident((DEPTH, D_MODEL)), _layer((D_MODEL, MIX_WIDTH), l),
                  _resident((GROUP_DIM, 2 * GROUP_DIM)), _resident((2, HALF_SEQ, SEQ))] + side_in,
        out_specs=[pl.BlockSpec((1, 2, HALF_SEQ, MIX_WIDTH), lambda b: (b, 0, 0, 0))] + side_out,
        out_shape=[jax.ShapeDtypeStruct((BATCH, 2, HALF_SEQ, MIX_WIDTH), BF16)] + side_shapes,
        scratch_shapes=[pltpu.VMEM((2 * N_GROUPS, SEQ, GROUP_DIM), F32),
                        pltpu.VMEM((2, SEQ, MIX_WIDTH), BF16),
                        pltpu.VMEM((D_MODEL, MIX_WIDTH), BF16)],
        compiler_params=_params("arbitrary"),
        name="fourier_mix",
    )(x, g, w_in, chan, pos, *side_ops)
    return outs[0].reshape(TOKENS, MIX_WIDTH), outs[1:]


def _mix_common(l, x_ref, fa_ref, prg_ref, win_ref, ws_ref, bs_ref, lng_ref, lnb_ref,
                wf_ref, wg_ref, wo_ref, pg_ref, x1_ref, sg_ref):
    x = x_ref[...]
    h = _rms(x, prg_ref[l:l + 1, :]).astype(BF16)
    v = _gelu(_dot(h, win_ref[:, 2 * MIX_WIDTH:3 * MIX_WIDTH]))
    u = _gelu(_dot(h, win_ref[:, MIX_WIDTH:2 * MIX_WIDTH]))
    gate = jax.nn.sigmoid(_dot(h, win_ref[:, 3 * MIX_WIDTH:D_IN]))
    ya = _dot(fa_ref[...], wf_ref[...])
    mu = jnp.mean(v, axis=-1, keepdims=True)
    vc = v - mu
    var = jnp.mean(vc * vc, axis=-1, keepdims=True)
    vln = (vc * lax.rsqrt(var + LN_EPS) * lng_ref[l:l + 1, :] + lnb_ref[l:l + 1, :]).astype(BF16)
    ws = [ws_ref[g].astype(BF16) for g in range(N_GROUPS)]
    for c in range(x_ref.shape[0] // CHUNK):
        rows = slice(c * CHUNK, (c + 1) * CHUNK)
        for g in range(N_GROUPS):
            cols = slice(g * GROUP_DIM, (g + 1) * GROUP_DIM)
            sv = _dot(ws[g], vln[rows, cols]) + bs_ref[:, cols]
            sg_ref[rows, cols] = (u[rows, cols] * sv).astype(BF16)
    yb = _dot(sg_ref[...], wg_ref[...])
    m = (gate[:, 0:D_MODEL] * ya + gate[:, D_MODEL:2 * D_MODEL] * yb).astype(BF16)
    half = x_ref.shape[0] // 2
    for r in range(2):
        rows = slice(r * half, (r + 1) * half)
        y = _dot(m[rows, :], wo_ref[...])
        x1_ref[rows, :] = x[rows, :] + _rms(y, pg_ref[l:l + 1, :])


def _mix_kernel(l, *refs):
    _mix_common(l, *refs)


def _mix_router_kernel(l, *refs):
    (*ins, fg_ref, rwt_ref, rb_ref, tri_ref,
     x1_ref, h_ref, gw_ref, sel_ref, rank_ref, cnt_ref, sg_ref, carry_ref) = refs

    @pl.when(pl.program_id(0) == 0)
    def _():
        carry_ref[...] = jnp.zeros_like(carry_ref)

    _mix_common(l, *ins, x1_ref, sg_ref)
    h = _rms(x1_ref[...], fg_ref[l:l + 1, :])
    h_hi, h_lo = _split_bf16(h)
    h_ref[...] = h_hi

    w_hi, w_lo = _split_bf16(rwt_ref[...])
    logits = _dot_nt(w_hi, h_hi) + _dot_nt(w_hi, h_lo) + _dot_nt(w_lo, h_hi) + rb_ref[...]
    row = lax.broadcasted_iota(I32, logits.shape, 0)
    m1 = jnp.max(logits, axis=0, keepdims=True)
    i1 = jnp.min(jnp.where(logits == m1, row, N_EXPERTS), axis=0, keepdims=True)
    rest = jnp.where(row == i1, -jnp.inf, logits)
    m2 = jnp.max(rest, axis=0, keepdims=True)
    i2 = jnp.min(jnp.where(rest == m2, row, N_EXPERTS), axis=0, keepdims=True)
    e2 = jnp.exp(m2 - m1)
    den = 1.0 + e2
    gw_ref[...] = jnp.where(row == i1, 1.0 / den, 0.0) + jnp.where(row == i2, e2 / den, 0.0)
    sel = jnp.where((row == i1) | (row == i2), 1.0, 0.0)
    sel_ref[...] = sel.astype(I32)
    before = _dot(sel.astype(BF16), tri_ref[...]) + carry_ref[...]
    rank_ref[...] = before.astype(I32)
    carry_ref[...] = carry_ref[...] + jnp.sum(sel, axis=1, keepdims=True)
    cnt_ref[...] = carry_ref[...].astype(I32)


def _mix_specs(l):
    row = lambda w: pl.BlockSpec((MIX_TILE, w), lambda i: (i, 0))
    in_specs = [row(D_MODEL), row(MIX_WIDTH), _resident((DEPTH, D_MODEL)), _resident((D_MODEL, D_IN)),
                _layer((N_GROUPS, CHUNK, CHUNK), l), _layer((CHUNK, MIX_WIDTH), l),
                _resident((DEPTH, MIX_WIDTH)), _resident((DEPTH, MIX_WIDTH)),
                _resident((MIX_WIDTH, D_MODEL)), _resident((MIX_WIDTH, D_MODEL)),
                _resident((D_MODEL, D_MODEL)), _resident((DEPTH, D_MODEL))]
    return row, in_specs, [pltpu.VMEM((MIX_TILE, MIX_WIDTH), BF16)]


def _mixer_body(l, x, fa, pre_g, w_in, w_s, b_full, ln_g, ln_b, w_f, w_g, w_o, post_g, side=()):
    row, in_specs, scratch = _mix_specs(l)
    steps = TOKENS // MIX_TILE
    side_in, side_out, side_shapes, side_ops = _side_specs(side, steps)
    outs = pl.pallas_call(
        _with_side_casts(functools.partial(_mix_kernel, l), len(in_specs), 1, len(side)),
        grid=(steps,),
        in_specs=in_specs + side_in,
        out_specs=[row(D_MODEL)] + side_out,
        out_shape=[jax.ShapeDtypeStruct((TOKENS, D_MODEL), F32)] + side_shapes,
        scratch_shapes=scratch,
        compiler_params=_params("arbitrary"),
        name="mixer_body",
    )(x, fa, pre_g, w_in, w_s, b_full, ln_g, ln_b, w_f, w_g, w_o, post_g, *side_ops)
    return outs[0], outs[1:]


def _mixer_body_router(l, x, fa, pre_g, w_in, w_s, b_full, ln_g, ln_b, w_f, w_g, w_o, post_g,
                       ffn_g, router_wt, router_b, tri):
    row, in_specs, scratch = _mix_specs(l)
    in_specs += [_resident((DEPTH, D_MODEL)), _layer((N_EXPERTS, D_MODEL), l // 2),
                 _layer((N_EXPERTS, 1), l // 2), _resident((MIX_TILE, MIX_TILE))]
    col = pl.BlockSpec((N_EXPERTS, MIX_TILE), lambda i: (0, i))
    per_token = lambda dt: jax.ShapeDtypeStruct((N_EXPERTS, TOKENS), dt)
    return pl.pallas_call(
        functools.partial(_mix_router_kernel, l),
        grid=(TOKENS // MIX_TILE,),
        in_specs=in_specs,
        out_specs=[row(D_MODEL), row(D_MODEL), col, col, col,
                   pl.BlockSpec((N_EXPERTS, 1), lambda i: (0, 0))],
        out_shape=[jax.ShapeDtypeStruct((TOKENS, D_MODEL), F32),
                   jax.ShapeDtypeStruct((TOKENS, D_MODEL), BF16),
                   per_token(F32), per_token(I32), per_token(I32),
                   jax.ShapeDtypeStruct((N_EXPERTS, 1), I32)],
        scratch_shapes=scratch + [pltpu.VMEM((N_EXPERTS, 1), F32)],
        compiler_params=_params("arbitrary"),
        name="mixer_body_router",
    )(x, fa, pre_g, w_in, w_s, b_full, ln_g, ln_b, w_f, w_g, w_o, post_g,
      ffn_g, router_wt, router_b, tri)


def _dense_ffn_kernel(l, x_ref, fg_ref, w1_ref, w3_ref, w2_ref, pg_ref, out_ref):
    half = x_ref.shape[0] // 2
    for r in range(2):
        rows = slice(r * half, (r + 1) * half)
        x = x_ref[rows, :]
        h = _rms(x, fg_ref[l:l + 1, :]).astype(BF16)
        a = _dot(h, w1_ref[...])
        b = _dot(h, w3_ref[...])
        y = _dot((jax.nn.silu(a) * b).astype(BF16), w2_ref[...])
        out_ref[rows, :] = x + _rms(y, pg_ref[l:l + 1, :])


def _dense_ffn(l, x, ffn_g, w1, w3, w2, post_g, side=()):
    row = pl.BlockSpec((ROW_TILE, D_MODEL), lambda i: (i, 0))
    steps = TOKENS // ROW_TILE
    side_in, side_out, side_shapes, side_ops = _side_specs(side, steps)
    outs = pl.pallas_call(
        _with_side_casts(functools.partial(_dense_ffn_kernel, l), 6, 1, len(side)),
        grid=(steps,),
        in_specs=[row, _resident((DEPTH, D_MODEL)), _resident((D_MODEL, D_FF_DENSE)),
                  _resident((D_MODEL, D_FF_DENSE)), _resident((D_FF_DENSE, D_MODEL)),
                  _resident((DEPTH, D_MODEL))] + side_in,
        out_specs=[row] + side_out,
        out_shape=[jax.ShapeDtypeStruct((TOKENS, D_MODEL), F32)] + side_shapes,
        compiler_params=_params("arbitrary"),
        name="dense_ffn",
    )(x, ffn_g, w1, w3, w2, post_g, *side_ops)
    return outs[0], outs[1:]


def _window_hits(k, block, start_ref, dest_ref):
    wanted = start_ref[block] + k * TOKEN_WINDOW
    begin = pl.multiple_of(jnp.minimum(wanted, TOKENS - TOKEN_WINDOW), LANES)
    tok = lax.broadcasted_iota(I32, (1, TOKEN_WINDOW), 1) + begin
    dest = jnp.where(tok >= wanted, dest_ref[:, pl.ds(begin, TOKEN_WINDOW)], -1)
    rows = lax.broadcasted_iota(I32, (SORT_BLOCK, TOKEN_WINDOW), 0) + block * SORT_BLOCK
    return begin, rows == dest


def _dispatch_kernel(exp_ref, start_ref, nwin_ref, dest_ref, gw_ref, h_ref, xs_ref, gs_ref):
    j = pl.program_id(0)

    def window(k):
        begin, hit = _window_hits(k, j, start_ref, dest_ref)
        onehot = jnp.where(hit, 1.0, 0.0).astype(BF16)
        picked = _dot(onehot, h_ref[pl.ds(begin, TOKEN_WINDOW), :]).astype(BF16)
        gw = gw_ref[:, pl.ds(begin, TOKEN_WINDOW)]
        return picked, jnp.sum(jnp.where(hit, gw, 0.0), axis=-1, keepdims=True)

    @pl.when(nwin_ref[j] == 0)
    def _():
        xs_ref[...] = jnp.zeros_like(xs_ref)
        gs_ref[...] = jnp.zeros_like(gs_ref)

    @pl.when(nwin_ref[j] > 0)
    def _():
        xs_ref[...], gs_ref[...] = window(0)

        def more(k, carry):
            picked, gsum = window(k)
            xs_ref[...] += picked
            gs_ref[...] += gsum
            return carry
        lax.fori_loop(1, nwin_ref[j], more, 0)


def _per_expert_row():
    return pl.BlockSpec((None, 1, TOKENS),
                        lambda j, e, s, n: (e[jnp.minimum(j, N_SORT_BLOCKS - 1)], 0, 0))


def _dispatch(plan, dest3, gw3, h):
    exp, start, nwin = plan
    grid_spec = pltpu.PrefetchScalarGridSpec(
        num_scalar_prefetch=3,
        grid=(N_SORT_BLOCKS,),
        in_specs=[_per_expert_row(), _per_expert_row(),
                  pl.BlockSpec((TOKENS, D_MODEL), lambda j, e, s, n: (0, 0),
                               pipeline_mode=pl.Buffered(1))],
        out_specs=[pl.BlockSpec((SORT_BLOCK, D_MODEL), lambda j, e, s, n: (j, 0)),
                   pl.BlockSpec((SORT_BLOCK, 1), lambda j, e, s, n: (j, 0))],
    )
    return pl.pallas_call(
        _dispatch_kernel,
        grid_spec=grid_spec,
        out_shape=[jax.ShapeDtypeStruct((SORTED_ROWS, D_MODEL), BF16),
                   jax.ShapeDtypeStruct((SORTED_ROWS, 1), F32)],
        compiler_params=_params("arbitrary"),
        name="moe_dispatch",
    )(exp, start, nwin, dest3, gw3, h)


def _expert_kernel(rexp_ref, nunit_ref, xs_ref, gs_ref, w1_ref, w3_ref, w2_ref, ys_ref, acc_ref):
    q = pl.program_id(0)
    f = pl.program_id(1)
    last = pl.num_programs(1) - 1
    nunit = nunit_ref[q]

    @pl.when((q == 0) & (f == 0))
    def _():
        acc_ref[...] = jnp.zeros_like(acc_ref)

    def swiglu_part(n_rows):
        for start in range(0, n_rows, EXPERT_BLOCK):
            rows = slice(start, min(start + EXPERT_BLOCK, n_rows))
            x = xs_ref[rows, :]
            a = _dot(x, w1_ref[...])
            b = _dot(x, w3_ref[...])
            part = _dot((jax.nn.silu(a) * b).astype(BF16), w2_ref[...])
            acc_ref[rows, :] = jnp.where(f == 0, 0.0, acc_ref[rows, :]) + part

        @pl.when(f == last)
        def _():
            ys_ref[0:n_rows, :] = (acc_ref[0:n_rows, :] * gs_ref[0:n_rows, :]).astype(BF16)
            if n_rows < EXPERT_REGION:
                ys_ref[n_rows:EXPERT_REGION, :] = jnp.zeros((EXPERT_REGION - n_rows, D_MODEL), BF16)

    for units in range(1, EXPERT_REGION // EXPERT_PAD + 1):
        pl.when(nunit == units)(functools.partial(swiglu_part, units * EXPERT_PAD))

    @pl.when((nunit == 0) & (f == last))
    def _():
        ys_ref[...] = jnp.zeros_like(ys_ref)


def _experts(rexp, nunit, xs, gs, w1, w3, w2):
    n_ff = D_FF_EXPERT // EXPERT_FF_TILE
    tile = lambda q, f: jnp.where(q % 2 == 0, f, n_ff - 1 - f)
    grid_spec = pltpu.PrefetchScalarGridSpec(
        num_scalar_prefetch=2,
        grid=(N_REGIONS, D_FF_EXPERT // EXPERT_FF_TILE),
        in_specs=[
            pl.BlockSpec((EXPERT_REGION, D_MODEL), lambda q, f, re, nh: (q, 0)),
            pl.BlockSpec((EXPERT_REGION, 1), lambda q, f, re, nh: (q, 0)),
            pl.BlockSpec((None, D_MODEL, EXPERT_FF_TILE), lambda q, f, re, nh: (re[q], 0, tile(q, f))),
            pl.BlockSpec((None, D_MODEL, EXPERT_FF_TILE), lambda q, f, re, nh: (re[q], 0, tile(q, f))),
            pl.BlockSpec((None, EXPERT_FF_TILE, D_MODEL), lambda q, f, re, nh: (re[q], tile(q, f), 0)),
        ],
        out_specs=pl.BlockSpec((EXPERT_REGION, D_MODEL), lambda q, f, re, nh: (q, 0)),
        scratch_shapes=[pltpu.VMEM((EXPERT_REGION, D_MODEL), F32)],
    )
    return pl.pallas_call(
        _expert_kernel,
        grid_spec=grid_spec,
        out_shape=jax.ShapeDtypeStruct((SORTED_ROWS, D_MODEL), BF16),
        compiler_params=_params("arbitrary", "arbitrary"),
        name="moe_experts",
    )(rexp, nunit, xs, gs, w1, w3, w2)


def _combine_copies(wstart_ref, ys_hbm, buf_ref, sem_ref, tile, rnd, slot):
    copies = []
    for e in range(N_EXPERTS):
        wanted = wstart_ref[tile * N_EXPERTS + e] + rnd * COMBINE_WINDOW
        begin = pl.multiple_of(jnp.minimum(wanted, SORTED_ROWS - COMBINE_WINDOW), BF16_ROWS)
        copies.append(pltpu.make_async_copy(
            ys_hbm.at[pl.ds(begin, COMBINE_WINDOW), :],
            buf_ref.at[slot, pl.ds(e * COMBINE_WINDOW, COMBINE_WINDOW), :],
            sem_ref.at[slot, e]))
    return copies


def _combine_kernel(l, wstart_ref, nround_ref, dest_ref, ys_hbm, x_ref, pg_ref, out_ref,
                    buf_ref, sem_ref):
    i = pl.program_id(0)
    n_tiles = pl.num_programs(0)
    slot = lax.rem(i, 2)

    @pl.when(i == 0)
    def _():
        for c in _combine_copies(wstart_ref, ys_hbm, buf_ref, sem_ref, 0, 0, 0):
            c.start()

    @pl.when(i + 1 < n_tiles)
    def _():
        for c in _combine_copies(wstart_ref, ys_hbm, buf_ref, sem_ref, i + 1, 0, 1 - slot):
            c.start()

    def gather(rnd, fetch_slot):
        sub = lax.broadcasted_iota(I32, (COMBINE_WINDOW, 1), 0)
        pieces = []
        for e in range(N_EXPERTS):
            wanted = wstart_ref[i * N_EXPERTS + e] + rnd * COMBINE_WINDOW
            begin = jnp.minimum(wanted, SORTED_ROWS - COMBINE_WINDOW)
            rowid = sub + begin
            rowid = jnp.where(rowid >= wanted, rowid, -2)
            pieces.append(jnp.where(rowid == dest_ref[e:e + 1, :], 1.0, 0.0).astype(BF16))
        onehot = jnp.concatenate(pieces, axis=0)
        return _dot_tn(onehot, buf_ref[fetch_slot])

    for c in _combine_copies(wstart_ref, ys_hbm, buf_ref, sem_ref, i, 0, slot):
        c.wait()
    y = gather(0, slot)

    def extra(rnd, y):
        copies = _combine_copies(wstart_ref, ys_hbm, buf_ref, sem_ref, i, rnd, 2)
        for c in copies:
            c.start()
        for c in copies:
            c.wait()
        return y + gather(rnd, 2)

    y = lax.fori_loop(1, nround_ref[i], extra, y)
    out_ref[...] = x_ref[...] + _rms(y, pg_ref[l:l + 1, :])


def _combine(l, wstart, nround, dest, ys, x, post_g):
    tile = lambda i, ws, nr: (i, 0)
    grid_spec = pltpu.PrefetchScalarGridSpec(
        num_scalar_prefetch=2,
        grid=(TOKENS // COMBINE_TILE,),
        in_specs=[
            pl.BlockSpec((N_EXPERTS, COMBINE_TILE), lambda i, ws, nr: (0, i)),
            pl.BlockSpec(memory_space=pl.ANY),
            pl.BlockSpec((COMBINE_TILE, D_MODEL), tile),
            pl.BlockSpec((DEPTH, D_MODEL), lambda i, ws, nr: (0, 0)),
        ],
        out_specs=pl.BlockSpec((COMBINE_TILE, D_MODEL), tile),
        scratch_shapes=[pltpu.VMEM((3, N_EXPERTS * COMBINE_WINDOW, D_MODEL), BF16),
                        pltpu.SemaphoreType.DMA((3, N_EXPERTS))],
    )
    return pl.pallas_call(
        functools.partial(_combine_kernel, l),
        grid_spec=grid_spec,
        out_shape=jax.ShapeDtypeStruct((TOKENS, D_MODEL), F32),
        compiler_params=_params("arbitrary"),
        name="moe_combine",
    )(wstart, nround, dest, ys, x, post_g)


def _routing_plan(sel, rank, counts):
    counts = counts.reshape(N_EXPERTS)
    padded = ((counts + EXPERT_PAD - 1) // EXPERT_PAD) * EXPERT_PAD
    region = ((counts + EXPERT_REGION - 1) // EXPERT_REGION) * EXPERT_REGION
    offs = jnp.cumsum(region) - region
    dest = jnp.where(sel == 1, rank + offs[:, None], -1)

    n_reg = region // EXPERT_REGION
    cum_reg = jnp.cumsum(n_reg)
    qs = jnp.arange(N_REGIONS, dtype=I32)
    qc = jnp.minimum(qs, cum_reg[-1] - 1)[:, None]
    owner = ((qc >= (cum_reg - n_reg)[None, :]) & (qc < cum_reg[None, :])).astype(I32)
    pick = lambda own, v: jnp.sum(own * v[None, :], axis=1)
    experts = jnp.arange(N_EXPERTS, dtype=I32)
    rexp = pick(owner, experts).astype(I32)
    rows_left = pick(owner, offs + padded) - qs * EXPERT_REGION
    n_unit = jnp.clip(rows_left // EXPERT_PAD, 0, EXPERT_REGION // EXPERT_PAD).astype(I32)

    group_end = (rank + sel)[:, LANES - 1::LANES]
    js = jnp.arange(N_SORT_BLOCKS, dtype=I32)
    owner_s = jnp.repeat(owner, EXPERT_REGION // SORT_BLOCK, axis=0)
    e_s = pick(owner_s, experts)
    counts_s = pick(owner_s, counts)
    lo_row = js * SORT_BLOCK - pick(owner_s, offs)
    has_rows = lo_row < counts_s
    hi_row = jnp.minimum(lo_row + SORT_BLOCK, counts_s)
    ends = jnp.sum(owner_s[:, :, None] * group_end[None, :, :], axis=1)
    g_lo = jnp.sum((ends <= lo_row[:, None]).astype(I32), axis=1)
    g_hi = jnp.sum((ends < hi_row[:, None]).astype(I32), axis=1)
    span = (g_hi - g_lo + 1) * LANES
    n_win = jnp.where(has_rows, (span + TOKEN_WINDOW - 1) // TOKEN_WINDOW, 0).astype(I32)
    start = jnp.where(has_rows, g_lo * LANES, 0).astype(I32)
    d_plan = (e_s.astype(I32), start, n_win)

    tile_lo = rank[:, ::COMBINE_TILE]
    tile_hi = jnp.concatenate([tile_lo[:, 1:], counts[:, None]], axis=1)
    lo = tile_lo + offs[:, None]
    hi = tile_hi + offs[:, None]
    wstart = jnp.minimum((lo // BF16_ROWS) * BF16_ROWS, SORTED_ROWS - COMBINE_WINDOW)
    rounds = jnp.where(hi > lo, (hi - wstart + COMBINE_WINDOW - 1) // COMBINE_WINDOW, 0)
    n_round = jnp.maximum(jnp.max(rounds, axis=0), 1).astype(I32)
    c_plan = (wstart.T.reshape(-1).astype(I32), n_round)
    return dest, (rexp, n_unit), d_plan, c_plan


def _moe(l, x1, h, gw, sel, rank, counts, w1, w3, w2, post_g):
    dest, e_plan, d_plan, c_plan = _routing_plan(sel, rank, counts)
    dest3 = dest.reshape(N_EXPERTS, 1, TOKENS)
    gw3 = gw.reshape(N_EXPERTS, 1, TOKENS)
    xs, gs = _dispatch(d_plan, dest3, gw3, h)
    ys = _experts(*e_plan, xs, gs, w1, w3, w2)
    return _combine(l, *c_plan, dest, ys, x1, post_g)


def kernel(x, pre_mix_g, post_mix_g, w_in, w_fourier, w_gmlp, w_out, w_spatial, b_spatial,
           gmlp_ln_g, gmlp_ln_b, pre_ffn_g, post_ffn_g, ffn_w1, ffn_w3, ffn_w2,
           router_w, router_b, moe_w1, moe_w3, moe_w2):
    pos = jnp.asarray(_POS_DFT).astype(BF16)
    chan = jnp.asarray(_CHAN_DFT).astype(BF16)
    tri = jnp.asarray(np.triu(np.ones((MIX_TILE, MIX_TILE), np.float32), 1)).astype(BF16)

    b_full = jnp.repeat(jnp.swapaxes(b_spatial, 1, 2), GROUP_DIM, axis=2)
    router_wt = jnp.swapaxes(router_w, 1, 2)
    router_b = router_b.reshape(router_b.shape[0], N_EXPERTS, 1)

    assert DEPTH == 2
    moe_rows = lambda w: w[0].reshape(-1, w.shape[-1])
    moe_back = lambda w, like: w.reshape(like.shape[1:])
    mixer_weights = lambda l: [(w, l) for w in (w_in, w_fourier, w_gmlp, w_out)]

    xf = x.reshape(TOKENS, D_MODEL)
    per_batch = lambda v: v.reshape(BATCH, SEQ, D_MODEL)
    fa, (win, wf, wg, wo) = _fourier(0, per_batch(xf), pre_mix_g, w_in, chan, pos,
                                     side=mixer_weights(0))
    x1, (ffn1, ffn3, ffn2, moe1) = _mixer_body(
        0, xf, fa, pre_mix_g, win, w_spatial, b_full, gmlp_ln_g, gmlp_ln_b, wf, wg, wo, post_mix_g,
        side=(ffn_w1[0], ffn_w3[0], ffn_w2[0], moe_rows(moe_w1)))
    xf, (moe3, moe2) = _dense_ffn(0, x1, pre_ffn_g, ffn1, ffn3, ffn2, post_ffn_g,
                                  side=(moe_rows(moe_w3), moe_rows(moe_w2)))

    fa, (win, wf, wg, wo) = _fourier(1, per_batch(xf), pre_mix_g, w_in, chan, pos,
                                     side=mixer_weights(1))
    x1, h, gw, sel, rank, counts = _mixer_body_router(
        1, xf, fa, pre_mix_g, win, w_spatial, b_full, gmlp_ln_g, gmlp_ln_b, wf, wg, wo, post_mix_g,
        pre_ffn_g, router_wt, router_b, tri)
    xf = _moe(1, x1, h, gw, sel, rank, counts, moe_back(moe1, moe_w1), moe_back(moe3, moe_w3),
              moe_back(moe2, moe_w2), post_ffn_g)
    return xf.reshape(BATCH, SEQ, D_MODEL)
```

```python
import functools

import numpy as np
import jax
import jax.numpy as jnp
from jax import lax
from jax.experimental import pallas as pl
from jax.experimental.pallas import tpu as pltpu

F32 = jnp.float32
BF16 = jnp.bfloat16
I32 = jnp.int32

D_MODEL = 1024
BATCH = 8
SEQ = 2048
HALF_SEQ = SEQ // 2
TOKENS = BATCH * SEQ
DEPTH = 2
N_GROUPS = 4
GROUP_DIM = 128
MIX_WIDTH = N_GROUPS * GROUP_DIM
CHUNK = 128
D_IN = 3 * MIX_WIDTH + 2 * D_MODEL
D_FF_DENSE = 2816
N_EXPERTS = 8
D_FF_EXPERT = 3584
RMS_EPS = 1e-6
LN_EPS = 1e-5
LANES = 128

ROW_TILE = 512
MIX_TILE = 512
ROUTER_TILE = 1024
EXPERT_BLOCK = 512
EXPERT_PAD = 256
EXPERT_REGION = 1024
SORT_BLOCK = 256
TOKEN_WINDOW = 1280
COMBINE_TILE = 256
COMBINE_WINDOW = 128
BF16_ROWS = 16
EXPERT_FF_TILE = 1792

SORTED_ROWS = 2 * TOKENS + N_EXPERTS * EXPERT_REGION
N_REGIONS = SORTED_ROWS // EXPERT_REGION
N_SORT_BLOCKS = SORTED_ROWS // SORT_BLOCK
N_COMBINE_TILES = TOKENS // COMBINE_TILE

VMEM_LIMIT = 56 * 1024 * 1024


def _params(*sem):
    return pltpu.CompilerParams(dimension_semantics=sem, vmem_limit_bytes=VMEM_LIMIT)


def _resident(shape):
    nd = len(shape)
    return pl.BlockSpec(shape, lambda *_: (0,) * nd, pipeline_mode=pl.Buffered(1))


def _layer(shape, l):
    nd = len(shape)
    return pl.BlockSpec((None,) + tuple(shape), lambda *_: (l,) + (0,) * nd,
                        pipeline_mode=pl.Buffered(1))


def _rms(x, g):
    return x * lax.rsqrt(jnp.mean(x * x, axis=-1, keepdims=True) + RMS_EPS) * g


def _dot(a, b):
    return jnp.dot(a, b, preferred_element_type=F32)


def _dot_nt(a, b):
    return lax.dot_general(a, b, (((1,), (1,)), ((), ())), preferred_element_type=F32)


def _dot_tn(a, b):
    return lax.dot_general(a, b, (((0,), (0,)), ((), ())), preferred_element_type=F32)


def _gelu(x):
    return 0.5 * x * (1.0 + lax.erf(x * np.float32(np.sqrt(0.5))))


def _cast_once(*pairs):
    chunk = 128

    @pl.when(pl.program_id(0) == 0)
    def _():
        for src_ref, dst_ref in pairs:
            def body(i, carry, src_ref=src_ref, dst_ref=dst_ref):
                rows = pl.ds(pl.multiple_of(i * chunk, chunk), chunk)
                dst_ref[rows, :] = src_ref[rows, :].astype(BF16)
                return carry
            lax.fori_loop(0, src_ref.shape[0] // chunk, body, 0)


def _split_bf16(x):
    hi = x.astype(BF16)
    return hi, (x - hi.astype(F32)).astype(BF16)


def _side_specs(arrays, steps):
    in_specs, out_specs, shapes, operands = [], [], [], []
    for entry in arrays:
        a, l = entry if isinstance(entry, tuple) else (entry, None)
        rows, cols = a.shape[-2:]
        chunk, per = rows // steps, 1
        if chunk % BF16_ROWS:
            chunk, per = 2 * chunk, 2
        assert rows % chunk == 0 and chunk % BF16_ROWS == 0, (a.shape, steps)
        out_specs.append(pl.BlockSpec((chunk, cols), lambda i, per=per: (i // per, 0)))
        if l is None:
            in_specs.append(out_specs[-1])
        else:
            in_specs.append(pl.BlockSpec((None, chunk, cols), lambda i, per=per, l=l: (l, i // per, 0)))
        shapes.append(jax.ShapeDtypeStruct((rows, cols), BF16))
        operands.append(a)
    return in_specs, out_specs, shapes, operands


def _with_side_casts(body, n_in, n_out, n_side):
    def kernel(*refs):
        ins, rest = refs[:n_in], refs[n_in:]
        srcs, rest = rest[:n_side], rest[n_side:]
        outs, rest = rest[:n_out], rest[n_out:]
        dsts, scratch = rest[:n_side], rest[n_side:]
        body(*ins, *outs, *scratch)
        for src, dst in zip(srcs, dsts):
            dst[...] = src[...].astype(BF16)
    return kernel


def _dft_tables():
    k = np.arange(HALF_SEQ, dtype=np.int64)
    scale = 1.0 / np.sqrt(SEQ)
    halves = []
    for p in (0, 1):
        n = 2 * np.arange(HALF_SEQ, dtype=np.int64) + p
        ang = 2.0 * np.pi * ((k[:, None] * n[None, :]) % SEQ).astype(np.float64) / SEQ
        halves.append(np.concatenate([np.cos(ang) * scale, -np.sin(ang) * scale], axis=1))
    pos = np.stack(halves)
    c = np.arange(GROUP_DIM, dtype=np.int64)
    angc = 2.0 * np.pi * ((c[:, None] * c[None, :]) % GROUP_DIM).astype(np.float64) / GROUP_DIM
    scalec = 1.0 / np.sqrt(GROUP_DIM)
    chan = np.concatenate([np.cos(angc) * scalec, np.sin(angc) * scalec], axis=1)
    return pos.astype(np.float32), chan.astype(np.float32)


_POS_DFT, _CHAN_DFT = _dft_tables()


def _fourier_kernel(l, x_ref, g_ref, wa32_ref, chan_ref, pos_ref, out_ref, stage_ref, rhs_ref, wa_ref):
    _cast_once((wa32_ref, wa_ref))
    for half in range(SEQ // HALF_SEQ):
        rows = slice(half * HALF_SEQ, (half + 1) * HALF_SEQ)
        h = _rms(x_ref[0, rows, :], g_ref[l:l + 1, :]).astype(BF16)
        za = _dot(h, wa_ref[...]).astype(BF16)
        for g in range(N_GROUPS):
            xcs = _dot(za[:, g * GROUP_DIM:(g + 1) * GROUP_DIM], chan_ref[...])
            stage_ref[g, rows, :] = xcs[:, 0:GROUP_DIM]
            stage_ref[N_GROUPS + g, rows, :] = xcs[:, GROUP_DIM:2 * GROUP_DIM]
    for g in range(N_GROUPS):
        cols = slice(g * GROUP_DIM, (g + 1) * GROUP_DIM)
        for p in (0, 1):
            rows = pl.ds(p, HALF_SEQ, stride=2)
            rhs_ref[p, 0:HALF_SEQ, cols] = stage_ref[g, rows, :].astype(BF16)
            rhs_ref[p, HALF_SEQ:SEQ, cols] = stage_ref[N_GROUPS + g, rows, :].astype(BF16)

    even = _dot(pos_ref[0], rhs_ref[0])
    odd = _dot(pos_ref[1], rhs_ref[1])
    out_ref[0, 0] = (even + odd).astype(BF16)
    out_ref[0, 1] = (even - odd).astype(BF16)


def _fourier(l, x, g, w_in, chan, pos, side=()):
    side_in, side_out, side_shapes, side_ops = _side_specs(side, BATCH)
    outs = pl.pallas_call(
        _with_side_casts(functools.partial(_fourier_kernel, l), 5, 1, len(side)),
        grid=(BATCH,),
        in_specs=[pl.BlockSpec((1, SEQ, D_MODEL), lambda b: (b, 0, 0)),
                  _resident((DEPTH, D_MODEL)), _layer((D_MODEL, MIX_WIDTH), l),
                  _resident((GROUP_DIM, 2 * GROUP_DIM)), _resident((2, HALF_SEQ, SEQ))] + side_in,
        out_specs=[pl.BlockSpec((1, 2, HALF_SEQ, MIX_WIDTH), lambda b: (b, 0, 0, 0))] + side_out,
        out_shape=[jax.ShapeDtypeStruct((BATCH, 2, HALF_SEQ, MIX_WIDTH), BF16)] + side_shapes,
        scratch_shapes=[pltpu.VMEM((2 * N_GROUPS, SEQ, GROUP_DIM), F32),
                        pltpu.VMEM((2, SEQ, MIX_WIDTH), BF16),
                        pltpu.VMEM((D_MODEL, MIX_WIDTH), BF16)],
        compiler_params=_params("arbitrary"),
        name="fourier_mix",
    )(x, g, w_in, chan, pos, *side_ops)
    return outs[0].reshape(TOKENS, MIX_WIDTH), outs[1:]


def _mix_common(l, x_ref, fa_ref, prg_ref, win_ref, ws_ref, bs_ref, lng_ref, lnb_ref,
                wf_ref, wg_ref, wo_ref, pg_ref, x1_ref, sg_ref):
    x = x_ref[...]
    h = _rms(x, prg_ref[l:l + 1, :]).astype(BF16)
    v = _gelu(_dot(h, win_ref[:, 2 * MIX_WIDTH:3 * MIX_WIDTH]))
    u = _gelu(_dot(h, win_ref[:, MIX_WIDTH:2 * MIX_WIDTH]))
    gate = jax.nn.sigmoid(_dot(h, win_ref[:, 3 * MIX_WIDTH:D_IN]))
    ya = _dot(fa_ref[...], wf_ref[...])
    mu = jnp.mean(v, axis=-1, keepdims=True)
    vc = v - mu
    var = jnp.mean(vc * vc, axis=-1, keepdims=True)
    vln = (vc * lax.rsqrt(var + LN_EPS) * lng_ref[l:l + 1, :] + lnb_ref[l:l + 1, :]).astype(BF16)
    ws = [ws_ref[g].astype(BF16) for g in range(N_GROUPS)]
    for c in range(x_ref.shape[0] // CHUNK):
        rows = slice(c * CHUNK, (c + 1) * CHUNK)
        for g in range(N_GROUPS):
            cols = slice(g * GROUP_DIM, (g + 1) * GROUP_DIM)
            sv = _dot(ws[g], vln[rows, cols]) + bs_ref[:, cols]
            sg_ref[rows, cols] = (u[rows, cols] * sv).astype(BF16)
    yb = _dot(sg_ref[...], wg_ref[...])
    m = (gate[:, 0:D_MODEL] * ya + gate[:, D_MODEL:2 * D_MODEL] * yb).astype(BF16)
    half = x_ref.shape[0] // 2
    for r in range(2):
        rows = slice(r * half, (r + 1) * half)
        y = _dot(m[rows, :], wo_ref[...])
        x1_ref[rows, :] = x[rows, :] + _rms(y, pg_ref[l:l + 1, :])


def _mix_kernel(l, *refs):
    _mix_common(l, *refs)


def _mix_router_kernel(l, *refs):
    (*ins, fg_ref, rwt_ref, rb_ref, tri_ref,
     x1_ref, h_ref, gw_ref, sel_ref, rank_ref, cnt_ref, sg_ref, carry_ref) = refs

    @pl.when(pl.program_id(0) == 0)
    def _():
        carry_ref[...] = jnp.zeros_like(carry_ref)

    _mix_common(l, *ins, x1_ref, sg_ref)
    h = _rms(x1_ref[...], fg_ref[l:l + 1, :])
    h_hi, h_lo = _split_bf16(h)
    h_ref[...] = h_hi

    w_hi, w_lo = _split_bf16(rwt_ref[...])
    logits = _dot_nt(w_hi, h_hi) + _dot_nt(w_hi, h_lo) + _dot_nt(w_lo, h_hi) + rb_ref[...]
    row = lax.broadcasted_iota(I32, logits.shape, 0)
    m1 = jnp.max(logits, axis=0, keepdims=True)
    i1 = jnp.min(jnp.where(logits == m1, row, N_EXPERTS), axis=0, keepdims=True)
    rest = jnp.where(row == i1, -jnp.inf, logits)
    m2 = jnp.max(rest, axis=0, keepdims=True)
    i2 = jnp.min(jnp.where(rest == m2, row, N_EXPERTS), axis=0, keepdims=True)
    e2 = jnp.exp(m2 - m1)
    den = 1.0 + e2
    gw_ref[...] = jnp.where(row == i1, 1.0 / den, 0.0) + jnp.where(row == i2, e2 / den, 0.0)
    sel = jnp.where((row == i1) | (row == i2), 1.0, 0.0)
    sel_ref[...] = sel.astype(I32)
    before = _dot(sel.astype(BF16), tri_ref[...]) + carry_ref[...]
    rank_ref[...] = before.astype(I32)
    carry_ref[...] = carry_ref[...] + jnp.sum(sel, axis=1, keepdims=True)
    cnt_ref[...] = carry_ref[...].astype(I32)


def _mix_specs(l, tile):
    row = lambda w: pl.BlockSpec((tile, w), lambda i: (i, 0))
    in_specs = [row(D_MODEL), row(MIX_WIDTH), _resident((DEPTH, D_MODEL)), _resident((D_MODEL, D_IN)),
                _layer((N_GROUPS, CHUNK, CHUNK), l), _layer((CHUNK, MIX_WIDTH), l),
                _resident((DEPTH, MIX_WIDTH)), _resident((DEPTH, MIX_WIDTH)),
                _resident((MIX_WIDTH, D_MODEL)), _resident((MIX_WIDTH, D_MODEL)),
                _resident((D_MODEL, D_MODEL)), _resident((DEPTH, D_MODEL))]
    return row, in_specs, [pltpu.VMEM((tile, MIX_WIDTH), BF16)]


def _mixer_body(l, x, fa, pre_g, w_in, w_s, b_full, ln_g, ln_b, w_f, w_g, w_o, post_g, side=()):
    row, in_specs, scratch = _mix_specs(l, MIX_TILE)
    steps = TOKENS // MIX_TILE
    side_in, side_out, side_shapes, side_ops = _side_specs(side, steps)
    outs = pl.pallas_call(
        _with_side_casts(functools.partial(_mix_kernel, l), len(in_specs), 1, len(side)),
        grid=(steps,),
        in_specs=in_specs + side_in,
        out_specs=[row(D_MODEL)] + side_out,
        out_shape=[jax.ShapeDtypeStruct((TOKENS, D_MODEL), F32)] + side_shapes,
        scratch_shapes=scratch,
        compiler_params=_params("arbitrary"),
        name="mixer_body",
    )(x, fa, pre_g, w_in, w_s, b_full, ln_g, ln_b, w_f, w_g, w_o, post_g, *side_ops)
    return outs[0], outs[1:]


def _mixer_body_router(l, x, fa, pre_g, w_in, w_s, b_full, ln_g, ln_b, w_f, w_g, w_o, post_g,
                       ffn_g, router_wt, router_b, tri):
    row, in_specs, scratch = _mix_specs(l, ROUTER_TILE)
    in_specs += [_resident((DEPTH, D_MODEL)), _layer((N_EXPERTS, D_MODEL), l // 2),
                 _layer((N_EXPERTS, 1), l // 2), _resident((ROUTER_TILE, ROUTER_TILE))]
    col = pl.BlockSpec((N_EXPERTS, ROUTER_TILE), lambda i: (0, i))
    per_token = lambda dt: jax.ShapeDtypeStruct((N_EXPERTS, TOKENS), dt)
    return pl.pallas_call(
        functools.partial(_mix_router_kernel, l),
        grid=(TOKENS // ROUTER_TILE,),
        in_specs=in_specs,
        out_specs=[row(D_MODEL), row(D_MODEL), col, col, col,
                   pl.BlockSpec((N_EXPERTS, 1), lambda i: (0, 0))],
        out_shape=[jax.ShapeDtypeStruct((TOKENS, D_MODEL), F32),
                   jax.ShapeDtypeStruct((TOKENS, D_MODEL), BF16),
                   per_token(F32), per_token(I32), per_token(I32),
                   jax.ShapeDtypeStruct((N_EXPERTS, 1), I32)],
        scratch_shapes=scratch + [pltpu.VMEM((N_EXPERTS, 1), F32)],
        compiler_params=_params("arbitrary"),
        name="mixer_body_router",
    )(x, fa, pre_g, w_in, w_s, b_full, ln_g, ln_b, w_f, w_g, w_o, post_g,
      ffn_g, router_wt, router_b, tri)


def _dense_ffn_kernel(l, x_ref, fg_ref, w1_ref, w3_ref, w2_ref, pg_ref, out_ref):
    half = x_ref.shape[0] // 2
    for r in range(2):
        rows = slice(r * half, (r + 1) * half)
        x = x_ref[rows, :]
        h = _rms(x, fg_ref[l:l + 1, :]).astype(BF16)
        a = _dot(h, w1_ref[...])
        b = _dot(h, w3_ref[...])
        y = _dot((jax.nn.silu(a) * b).astype(BF16), w2_ref[...])
        out_ref[rows, :] = x + _rms(y, pg_ref[l:l + 1, :])


def _dense_ffn(l, x, ffn_g, w1, w3, w2, post_g, side=()):
    row = pl.BlockSpec((ROW_TILE, D_MODEL), lambda i: (i, 0))
    steps = TOKENS // ROW_TILE
    side_in, side_out, side_shapes, side_ops = _side_specs(side, steps)
    outs = pl.pallas_call(
        _with_side_casts(functools.partial(_dense_ffn_kernel, l), 6, 1, len(side)),
        grid=(steps,),
        in_specs=[row, _resident((DEPTH, D_MODEL)), _resident((D_MODEL, D_FF_DENSE)),
                  _resident((D_MODEL, D_FF_DENSE)), _resident((D_FF_DENSE, D_MODEL)),
                  _resident((DEPTH, D_MODEL))] + side_in,
        out_specs=[row] + side_out,
        out_shape=[jax.ShapeDtypeStruct((TOKENS, D_MODEL), F32)] + side_shapes,
        compiler_params=_params("arbitrary"),
        name="dense_ffn",
    )(x, ffn_g, w1, w3, w2, post_g, *side_ops)
    return outs[0], outs[1:]


def _window_hits(k, block, start_ref, dest_ref):
    wanted = start_ref[block] + k * TOKEN_WINDOW
    begin = pl.multiple_of(jnp.minimum(wanted, TOKENS - TOKEN_WINDOW), LANES)
    tok = lax.broadcasted_iota(I32, (1, TOKEN_WINDOW), 1) + begin
    dest = jnp.where(tok >= wanted, dest_ref[:, pl.ds(begin, TOKEN_WINDOW)], -1)
    rows = lax.broadcasted_iota(I32, (SORT_BLOCK, TOKEN_WINDOW), 0) + block * SORT_BLOCK
    return begin, rows == dest


def _dispatch_kernel(exp_ref, start_ref, nwin_ref, dest_ref, gw_ref, h_ref, xs_ref, gs_ref):
    j = pl.program_id(0)

    def window(k):
        begin, hit = _window_hits(k, j, start_ref, dest_ref)
        onehot = jnp.where(hit, 1.0, 0.0).astype(BF16)
        picked = _dot(onehot, h_ref[pl.ds(begin, TOKEN_WINDOW), :]).astype(BF16)
        gw = gw_ref[:, pl.ds(begin, TOKEN_WINDOW)]
        return picked, jnp.sum(jnp.where(hit, gw, 0.0), axis=-1, keepdims=True)

    @pl.when(nwin_ref[j] == 0)
    def _():
        xs_ref[...] = jnp.zeros_like(xs_ref)
        gs_ref[...] = jnp.zeros_like(gs_ref)

    @pl.when(nwin_ref[j] > 0)
    def _():
        xs_ref[...], gs_ref[...] = window(0)

        def more(k, carry):
            picked, gsum = window(k)
            xs_ref[...] += picked
            gs_ref[...] += gsum
            return carry
        lax.fori_loop(1, nwin_ref[j], more, 0)


def _per_expert_row():
    return pl.BlockSpec((None, 1, TOKENS),
                        lambda j, e, s, n: (e[jnp.minimum(j, N_SORT_BLOCKS - 1)], 0, 0))


def _dispatch(plan, dest3, gw3, h):
    exp, start, nwin = plan
    grid_spec = pltpu.PrefetchScalarGridSpec(
        num_scalar_prefetch=3,
        grid=(N_SORT_BLOCKS,),
        in_specs=[_per_expert_row(), _per_expert_row(),
                  pl.BlockSpec((TOKENS, D_MODEL), lambda j, e, s, n: (0, 0),
                               pipeline_mode=pl.Buffered(1))],
        out_specs=[pl.BlockSpec((SORT_BLOCK, D_MODEL), lambda j, e, s, n: (j, 0)),
                   pl.BlockSpec((SORT_BLOCK, 1), lambda j, e, s, n: (j, 0))],
    )
    return pl.pallas_call(
        _dispatch_kernel,
        grid_spec=grid_spec,
        out_shape=[jax.ShapeDtypeStruct((SORTED_ROWS, D_MODEL), BF16),
                   jax.ShapeDtypeStruct((SORTED_ROWS, 1), F32)],
        compiler_params=_params("arbitrary"),
        name="moe_dispatch",
    )(exp, start, nwin, dest3, gw3, h)


def _expert_kernel(rexp_ref, nunit_ref, xs_ref, gs_ref, w1_ref, w3_ref, w2_ref, ys_ref, acc_ref):
    q = pl.program_id(0)
    f = pl.program_id(1)
    last = pl.num_programs(1) - 1
    nunit = nunit_ref[q]

    @pl.when((q == 0) & (f == 0))
    def _():
        acc_ref[...] = jnp.zeros_like(acc_ref)

    def swiglu_part(n_rows):
        for start in range(0, n_rows, EXPERT_BLOCK):
            rows = slice(start, min(start + EXPERT_BLOCK, n_rows))
            x = xs_ref[rows, :]
            a = _dot(x, w1_ref[...])
            b = _dot(x, w3_ref[...])
            part = _dot((jax.nn.silu(a) * b).astype(BF16), w2_ref[...])
            acc_ref[rows, :] = jnp.where(f == 0, 0.0, acc_ref[rows, :]) + part

        @pl.when(f == last)
        def _():
            ys_ref[0:n_rows, :] = (acc_ref[0:n_rows, :] * gs_ref[0:n_rows, :]).astype(BF16)
            if n_rows < EXPERT_REGION:
                ys_ref[n_rows:EXPERT_REGION, :] = jnp.zeros((EXPERT_REGION - n_rows, D_MODEL), BF16)

    for units in range(1, EXPERT_REGION // EXPERT_PAD + 1):
        pl.when(nunit == units)(functools.partial(swiglu_part, units * EXPERT_PAD))

    @pl.when((nunit == 0) & (f == last))
    def _():
        ys_ref[...] = jnp.zeros_like(ys_ref)


def _experts(rexp, nunit, xs, gs, w1, w3, w2):
    n_ff = D_FF_EXPERT // EXPERT_FF_TILE
    tile = lambda q, f: jnp.where(q % 2 == 0, f, n_ff - 1 - f)
    grid_spec = pltpu.PrefetchScalarGridSpec(
        num_scalar_prefetch=2,
        grid=(N_REGIONS, D_FF_EXPERT // EXPERT_FF_TILE),
        in_specs=[
            pl.BlockSpec((EXPERT_REGION, D_MODEL), lambda q, f, re, nh: (q, 0)),
            pl.BlockSpec((EXPERT_REGION, 1), lambda q, f, re, nh: (q, 0)),
            pl.BlockSpec((None, D_MODEL, EXPERT_FF_TILE), lambda q, f, re, nh: (re[q], 0, tile(q, f))),
            pl.BlockSpec((None, D_MODEL, EXPERT_FF_TILE), lambda q, f, re, nh: (re[q], 0, tile(q, f))),
            pl.BlockSpec((None, EXPERT_FF_TILE, D_MODEL), lambda q, f, re, nh: (re[q], tile(q, f), 0)),
        ],
        out_specs=pl.BlockSpec((EXPERT_REGION, D_MODEL), lambda q, f, re, nh: (q, 0)),
        scratch_shapes=[pltpu.VMEM((EXPERT_REGION, D_MODEL), F32)],
    )
    return pl.pallas_call(
        _expert_kernel,
        grid_spec=grid_spec,
        out_shape=jax.ShapeDtypeStruct((SORTED_ROWS, D_MODEL), BF16),
        compiler_params=_params("arbitrary", "arbitrary"),
        name="moe_experts",
    )(rexp, nunit, xs, gs, w1, w3, w2)


def _combine_copies(wstart_ref, ys_hbm, buf_ref, sem_ref, tile, rnd, slot):
    copies = []
    for e in range(N_EXPERTS):
        wanted = wstart_ref[tile * N_EXPERTS + e] + rnd * COMBINE_WINDOW
        begin = pl.multiple_of(jnp.minimum(wanted, SORTED_ROWS - COMBINE_WINDOW), BF16_ROWS)
        copies.append(pltpu.make_async_copy(
            ys_hbm.at[pl.ds(begin, COMBINE_WINDOW), :],
            buf_ref.at[slot, pl.ds(e * COMBINE_WINDOW, COMBINE_WINDOW), :],
            sem_ref.at[slot, e]))
    return copies


def _combine_kernel(l, wstart_ref, nround_ref, dest_ref, ys_hbm, x_ref, pg_ref, out_ref,
                    buf_ref, sem_ref):
    i = pl.program_id(0)
    n_tiles = pl.num_programs(0)
    slot = lax.rem(i, 2)

    @pl.when(i == 0)
    def _():
        for c in _combine_copies(wstart_ref, ys_hbm, buf_ref, sem_ref, 0, 0, 0):
            c.start()

    @pl.when(i + 1 < n_tiles)
    def _():
        for c in _combine_copies(wstart_ref, ys_hbm, buf_ref, sem_ref, i + 1, 0, 1 - slot):
            c.start()

    def gather(rnd, fetch_slot):
        sub = lax.broadcasted_iota(I32, (COMBINE_WINDOW, 1), 0)
        pieces = []
        for e in range(N_EXPERTS):
            wanted = wstart_ref[i * N_EXPERTS + e] + rnd * COMBINE_WINDOW
            begin = jnp.minimum(wanted, SORTED_ROWS - COMBINE_WINDOW)
            rowid = sub + begin
            rowid = jnp.where(rowid >= wanted, rowid, -2)
            pieces.append(jnp.where(rowid == dest_ref[e:e + 1, :], 1.0, 0.0).astype(BF16))
        onehot = jnp.concatenate(pieces, axis=0)
        return _dot_tn(onehot, buf_ref[fetch_slot])

    for c in _combine_copies(wstart_ref, ys_hbm, buf_ref, sem_ref, i, 0, slot):
        c.wait()
    y = gather(0, slot)

    def extra(rnd, y):
        copies = _combine_copies(wstart_ref, ys_hbm, buf_ref, sem_ref, i, rnd, 2)
        for c in copies:
            c.start()
        for c in copies:
            c.wait()
        return y + gather(rnd, 2)

    y = lax.fori_loop(1, nround_ref[i], extra, y)
    out_ref[...] = x_ref[...] + _rms(y, pg_ref[l:l + 1, :])


def _combine(l, wstart, nround, dest, ys, x, post_g):
    tile = lambda i, ws, nr: (i, 0)
    grid_spec = pltpu.PrefetchScalarGridSpec(
        num_scalar_prefetch=2,
        grid=(TOKENS // COMBINE_TILE,),
        in_specs=[
            pl.BlockSpec((N_EXPERTS, COMBINE_TILE), lambda i, ws, nr: (0, i)),
            pl.BlockSpec(memory_space=pl.ANY),
            pl.BlockSpec((COMBINE_TILE, D_MODEL), tile),
            pl.BlockSpec((DEPTH, D_MODEL), lambda i, ws, nr: (0, 0)),
        ],
        out_specs=pl.BlockSpec((COMBINE_TILE, D_MODEL), tile),
        scratch_shapes=[pltpu.VMEM((3, N_EXPERTS * COMBINE_WINDOW, D_MODEL), BF16),
                        pltpu.SemaphoreType.DMA((3, N_EXPERTS))],
    )
    return pl.pallas_call(
        functools.partial(_combine_kernel, l),
        grid_spec=grid_spec,
        out_shape=jax.ShapeDtypeStruct((TOKENS, D_MODEL), F32),
        compiler_params=_params("arbitrary"),
        name="moe_combine",
    )(wstart, nround, dest, ys, x, post_g)


def _routing_plan(sel, rank, counts):
    counts = counts.reshape(N_EXPERTS)
    padded = ((counts + EXPERT_PAD - 1) // EXPERT_PAD) * EXPERT_PAD
    region = ((counts + EXPERT_REGION - 1) // EXPERT_REGION) * EXPERT_REGION
    offs = jnp.cumsum(region) - region
    dest = jnp.where(sel == 1, rank + offs[:, None], -1)

    n_reg = region // EXPERT_REGION
    cum_reg = jnp.cumsum(n_reg)
    qs = jnp.arange(N_REGIONS, dtype=I32)
    qc = jnp.minimum(qs, cum_reg[-1] - 1)[:, None]
    owner = ((qc >= (cum_reg - n_reg)[None, :]) & (qc < cum_reg[None, :])).astype(I32)
    pick = lambda own, v: jnp.sum(own * v[None, :], axis=1)
    experts = jnp.arange(N_EXPERTS, dtype=I32)
    rexp = pick(owner, experts).astype(I32)
    rows_left = pick(owner, offs + padded) - qs * EXPERT_REGION
    n_unit = jnp.clip(rows_left // EXPERT_PAD, 0, EXPERT_REGION // EXPERT_PAD).astype(I32)

    group_end = (rank + sel)[:, LANES - 1::LANES]
    js = jnp.arange(N_SORT_BLOCKS, dtype=I32)
    owner_s = jnp.repeat(owner, EXPERT_REGION // SORT_BLOCK, axis=0)
    e_s = pick(owner_s, experts)
    counts_s = pick(owner_s, counts)
    lo_row = js * SORT_BLOCK - pick(owner_s, offs)
    has_rows = lo_row < counts_s
    hi_row = jnp.minimum(lo_row + SORT_BLOCK, counts_s)
    ends = jnp.sum(owner_s[:, :, None] * group_end[None, :, :], axis=1)
    g_lo = jnp.sum((ends <= lo_row[:, None]).astype(I32), axis=1)
    g_hi = jnp.sum((ends < hi_row[:, None]).astype(I32), axis=1)
    span = (g_hi - g_lo + 1) * LANES
    n_win = jnp.where(has_rows, (span + TOKEN_WINDOW - 1) // TOKEN_WINDOW, 0).astype(I32)
    start = jnp.where(has_rows, g_lo * LANES, 0).astype(I32)
    d_plan = (e_s.astype(I32), start, n_win)

    tile_lo = rank[:, ::COMBINE_TILE]
    tile_hi = jnp.concatenate([tile_lo[:, 1:], counts[:, None]], axis=1)
    lo = tile_lo + offs[:, None]
    hi = tile_hi + offs[:, None]
    wstart = jnp.minimum((lo // BF16_ROWS) * BF16_ROWS, SORTED_ROWS - COMBINE_WINDOW)
    rounds = jnp.where(hi > lo, (hi - wstart + COMBINE_WINDOW - 1) // COMBINE_WINDOW, 0)
    n_round = jnp.maximum(jnp.max(rounds, axis=0), 1).astype(I32)
    c_plan = (wstart.T.reshape(-1).astype(I32), n_round)
    return dest, (rexp, n_unit), d_plan, c_plan


def _moe(l, x1, h, gw, sel, rank, counts, w1, w3, w2, post_g):
    dest, e_plan, d_plan, c_plan = _routing_plan(sel, rank, counts)
    dest3 = dest.reshape(N_EXPERTS, 1, TOKENS)
    gw3 = gw.reshape(N_EXPERTS, 1, TOKENS)
    xs, gs = _dispatch(d_plan, dest3, gw3, h)
    ys = _experts(*e_plan, xs, gs, w1, w3, w2)
    return _combine(l, *c_plan, dest, ys, x1, post_g)


def kernel(x, pre_mix_g, post_mix_g, w_in, w_fourier, w_gmlp, w_out, w_spatial, b_spatial,
           gmlp_ln_g, gmlp_ln_b, pre_ffn_g, post_ffn_g, ffn_w1, ffn_w3, ffn_w2,
           router_w, router_b, moe_w1, moe_w3, moe_w2):
    pos = jnp.asarray(_POS_DFT).astype(BF16)
    chan = jnp.asarray(_CHAN_DFT).astype(BF16)
    tri = jnp.asarray(np.triu(np.ones((ROUTER_TILE, ROUTER_TILE), np.float32), 1)).astype(BF16)

    b_full = jnp.repeat(jnp.swapaxes(b_spatial, 1, 2), GROUP_DIM, axis=2)
    router_wt = jnp.swapaxes(router_w, 1, 2)
    router_b = router_b.reshape(router_b.shape[0], N_EXPERTS, 1)

    assert DEPTH == 2
    moe_rows = lambda w: w[0].reshape(-1, w.shape[-1])
    moe_back = lambda w, like: w.reshape(like.shape[1:])
    mixer_weights = lambda l: [(w, l) for w in (w_in, w_fourier, w_gmlp, w_out)]

    xf = x.reshape(TOKENS, D_MODEL)
    per_batch = lambda v: v.reshape(BATCH, SEQ, D_MODEL)
    fa, (win, wf, wg, wo) = _fourier(0, per_batch(xf), pre_mix_g, w_in, chan, pos,
                                     side=mixer_weights(0))
    x1, (ffn1, ffn3, ffn2, moe1) = _mixer_body(
        0, xf, fa, pre_mix_g, win, w_spatial, b_full, gmlp_ln_g, gmlp_ln_b, wf, wg, wo, post_mix_g,
        side=(ffn_w1[0], ffn_w3[0], ffn_w2[0], moe_rows(moe_w1)))
    xf, (moe3, moe2) = _dense_ffn(0, x1, pre_ffn_g, ffn1, ffn3, ffn2, post_ffn_g,
                                  side=(moe_rows(moe_w3), moe_rows(moe_w2)))

    fa, (win, wf, wg, wo) = _fourier(1, per_batch(xf), pre_mix_g, w_in, chan, pos,
                                     side=mixer_weights(1))
    x1, h, gw, sel, rank, counts = _mixer_body_router(
        1, xf, fa, pre_mix_g, win, w_spatial, b_full, gmlp_ln_g, gmlp_ln_b, wf, wg, wo, post_mix_g,
        pre_ffn_g, router_wt, router_b, tri)
    xf = _moe(1, x1, h, gw, sel, rank, counts, moe_back(moe1, moe_w1), moe_back(moe3, moe_w3),
              moe_back(moe2, moe_w2), post_ffn_g)
    return xf.reshape(BATCH, SEQ, D_MODEL)
```

```python
import functools

import numpy as np
import jax
import jax.numpy as jnp
from jax import lax
from jax.experimental import pallas as pl
from jax.experimental.pallas import tpu as pltpu

F32 = jnp.float32
BF16 = jnp.bfloat16
I32 = jnp.int32

D_MODEL = 1024
BATCH = 8
SEQ = 2048
HALF_SEQ = SEQ // 2
TOKENS = BATCH * SEQ
DEPTH = 2
N_GROUPS = 4
GROUP_DIM = 128
MIX_WIDTH = N_GROUPS * GROUP_DIM
CHUNK = 128
D_IN = 3 * MIX_WIDTH + 2 * D_MODEL
D_FF_DENSE = 2816
N_EXPERTS = 8
D_FF_EXPERT = 3584
RMS_EPS = 1e-6
LN_EPS = 1e-5
LANES = 128

ROW_TILE = 512
MIX_TILE = 512
ROUTER_TILE = 1024
EXPERT_BLOCK = 512
EXPERT_PAD = 256
EXPERT_REGION = 1024
SORT_BLOCK = 256
TOKEN_WINDOW = 1280
COMBINE_TILE = 256
COMBINE_WINDOW = 112
BF16_ROWS = 16
EXPERT_FF_TILE = 1792

SORTED_ROWS = 2 * TOKENS + N_EXPERTS * EXPERT_REGION
N_REGIONS = SORTED_ROWS // EXPERT_REGION
N_SORT_BLOCKS = SORTED_ROWS // SORT_BLOCK
N_COMBINE_TILES = TOKENS // COMBINE_TILE

VMEM_LIMIT = 56 * 1024 * 1024


def _params(*sem):
    return pltpu.CompilerParams(dimension_semantics=sem, vmem_limit_bytes=VMEM_LIMIT)


def _resident(shape):
    nd = len(shape)
    return pl.BlockSpec(shape, lambda *_: (0,) * nd, pipeline_mode=pl.Buffered(1))


def _layer(shape, l):
    nd = len(shape)
    return pl.BlockSpec((None,) + tuple(shape), lambda *_: (l,) + (0,) * nd,
                        pipeline_mode=pl.Buffered(1))


def _rms(x, g):
    return x * lax.rsqrt(jnp.mean(x * x, axis=-1, keepdims=True) + RMS_EPS) * g


def _dot(a, b):
    return jnp.dot(a, b, preferred_element_type=F32)


def _dot_nt(a, b):
    return lax.dot_general(a, b, (((1,), (1,)), ((), ())), preferred_element_type=F32)


def _dot_tn(a, b):
    return lax.dot_general(a, b, (((0,), (0,)), ((), ())), preferred_element_type=F32)


def _gelu(x):
    return 0.5 * x * (1.0 + lax.erf(x * np.float32(np.sqrt(0.5))))


def _cast_once(*pairs):
    chunk = 128

    @pl.when(pl.program_id(0) == 0)
    def _():
        for src_ref, dst_ref in pairs:
            def body(i, carry, src_ref=src_ref, dst_ref=dst_ref):
                rows = pl.ds(pl.multiple_of(i * chunk, chunk), chunk)
                dst_ref[rows, :] = src_ref[rows, :].astype(BF16)
                return carry
            lax.fori_loop(0, src_ref.shape[0] // chunk, body, 0)


def _split_bf16(x):
    hi = x.astype(BF16)
    return hi, (x - hi.astype(F32)).astype(BF16)


def _side_specs(arrays, steps):
    in_specs, out_specs, shapes, operands = [], [], [], []
    for entry in arrays:
        a, l = entry if isinstance(entry, tuple) else (entry, None)
        rows, cols = a.shape[-2:]
        chunk, per = rows // steps, 1
        if chunk % BF16_ROWS:
            chunk, per = 2 * chunk, 2
        assert rows % chunk == 0 and chunk % BF16_ROWS == 0, (a.shape, steps)
        out_specs.append(pl.BlockSpec((chunk, cols), lambda i, per=per: (i // per, 0)))
        if l is None:
            in_specs.append(out_specs[-1])
        else:
            in_specs.append(pl.BlockSpec((None, chunk, cols), lambda i, per=per, l=l: (l, i // per, 0)))
        shapes.append(jax.ShapeDtypeStruct((rows, cols), BF16))
        operands.append(a)
    return in_specs, out_specs, shapes, operands


def _with_side_casts(body, n_in, n_out, n_side):
    def kernel(*refs):
        ins, rest = refs[:n_in], refs[n_in:]
        srcs, rest = rest[:n_side], rest[n_side:]
        outs, rest = rest[:n_out], rest[n_out:]
        dsts, scratch = rest[:n_side], rest[n_side:]
        body(*ins, *outs, *scratch)
        for src, dst in zip(srcs, dsts):
            dst[...] = src[...].astype(BF16)
    return kernel


def _dft_tables():
    k = np.arange(HALF_SEQ, dtype=np.int64)
    scale = 1.0 / np.sqrt(SEQ)
    halves = []
    for p in (0, 1):
        n = 2 * np.arange(HALF_SEQ, dtype=np.int64) + p
        ang = 2.0 * np.pi * ((k[:, None] * n[None, :]) % SEQ).astype(np.float64) / SEQ
        halves.append(np.concatenate([np.cos(ang) * scale, -np.sin(ang) * scale], axis=1))
    pos = np.stack(halves)
    c = np.arange(GROUP_DIM, dtype=np.int64)
    angc = 2.0 * np.pi * ((c[:, None] * c[None, :]) % GROUP_DIM).astype(np.float64) / GROUP_DIM
    scalec = 1.0 / np.sqrt(GROUP_DIM)
    chan = np.concatenate([np.cos(angc) * scalec, np.sin(angc) * scalec], axis=1)
    return pos.astype(np.float32), chan.astype(np.float32)


_POS_DFT, _CHAN_DFT = _dft_tables()


def _fourier_kernel(l, x_ref, g_ref, wa32_ref, chan_ref, pos_ref, out_ref, stage_ref, rhs_ref, wa_ref):
    _cast_once((wa32_ref, wa_ref))
    for half in range(SEQ // HALF_SEQ):
        rows = slice(half * HALF_SEQ, (half + 1) * HALF_SEQ)
        h = _rms(x_ref[0, rows, :], g_ref[l:l + 1, :]).astype(BF16)
        za = _dot(h, wa_ref[...]).astype(BF16)
        for g in range(N_GROUPS):
            xcs = _dot(za[:, g * GROUP_DIM:(g + 1) * GROUP_DIM], chan_ref[...])
            stage_ref[g, rows, :] = xcs[:, 0:GROUP_DIM]
            stage_ref[N_GROUPS + g, rows, :] = xcs[:, GROUP_DIM:2 * GROUP_DIM]
    for g in range(N_GROUPS):
        cols = slice(g * GROUP_DIM, (g + 1) * GROUP_DIM)
        for p in (0, 1):
            rows = pl.ds(p, HALF_SEQ, stride=2)
            rhs_ref[p, 0:HALF_SEQ, cols] = stage_ref[g, rows, :].astype(BF16)
            rhs_ref[p, HALF_SEQ:SEQ, cols] = stage_ref[N_GROUPS + g, rows, :].astype(BF16)

    even = _dot(pos_ref[0], rhs_ref[0])
    odd = _dot(pos_ref[1], rhs_ref[1])
    out_ref[0, 0] = (even + odd).astype(BF16)
    out_ref[0, 1] = (even - odd).astype(BF16)


def _fourier(l, x, g, w_in, chan, pos, side=()):
    side_in, side_out, side_shapes, side_ops = _side_specs(side, BATCH)
    outs = pl.pallas_call(
        _with_side_casts(functools.partial(_fourier_kernel, l), 5, 1, len(side)),
        grid=(BATCH,),
        in_specs=[pl.BlockSpec((1, SEQ, D_MODEL), lambda b: (b, 0, 0)),
                  _resident((DEPTH, D_MODEL)), _layer((D_MODEL, MIX_WIDTH), l),
                  _resident((GROUP_DIM, 2 * GROUP_DIM)), _resident((2, HALF_SEQ, SEQ))] + side_in,
        out_specs=[pl.BlockSpec((1, 2, HALF_SEQ, MIX_WIDTH), lambda b: (b, 0, 0, 0))] + side_out,
        out_shape=[jax.ShapeDtypeStruct((BATCH, 2, HALF_SEQ, MIX_WIDTH), BF16)] + side_shapes,
        scratch_shapes=[pltpu.VMEM((2 * N_GROUPS, SEQ, GROUP_DIM), F32),
                        pltpu.VMEM((2, SEQ, MIX_WIDTH), BF16),
                        pltpu.VMEM((D_MODEL, MIX_WIDTH), BF16)],
        compiler_params=_params("arbitrary"),
        name="fourier_mix",
    )(x, g, w_in, chan, pos, *side_ops)
    return outs[0].reshape(TOKENS, MIX_WIDTH), outs[1:]


def _mix_common(l, x_ref, fa_ref, prg_ref, win_ref, ws_ref, bs_ref, lng_ref, lnb_ref,
                wf_ref, wg_ref, wo_ref, pg_ref, x1_ref, sg_ref):
    x = x_ref[...]
    h = _rms(x, prg_ref[l:l + 1, :]).astype(BF16)
    v = _gelu(_dot(h, win_ref[:, 2 * MIX_WIDTH:3 * MIX_WIDTH]))
    u = _gelu(_dot(h, win_ref[:, MIX_WIDTH:2 * MIX_WIDTH]))
    gate = jax.nn.sigmoid(_dot(h, win_ref[:, 3 * MIX_WIDTH:D_IN]))
    ya = _dot(fa_ref[...], wf_ref[...])
    mu = jnp.mean(v, axis=-1, keepdims=True)
    vc = v - mu
    var = jnp.mean(vc * vc, axis=-1, keepdims=True)
    vln = (vc * lax.rsqrt(var + LN_EPS) * lng_ref[l:l + 1, :] + lnb_ref[l:l + 1, :]).astype(BF16)
    ws = [ws_ref[g].astype(BF16) for g in range(N_GROUPS)]
    for c in range(x_ref.shape[0] // CHUNK):
        rows = slice(c * CHUNK, (c + 1) * CHUNK)
        for g in range(N_GROUPS):
            cols = slice(g * GROUP_DIM, (g + 1) * GROUP_DIM)
            sv = _dot(ws[g], vln[rows, cols]) + bs_ref[:, cols]
            sg_ref[rows, cols] = (u[rows, cols] * sv).astype(BF16)
    yb = _dot(sg_ref[...], wg_ref[...])
    m = (gate[:, 0:D_MODEL] * ya + gate[:, D_MODEL:2 * D_MODEL] * yb).astype(BF16)
    half = x_ref.shape[0] // 2
    for r in range(2):
        rows = slice(r * half, (r + 1) * half)
        y = _dot(m[rows, :], wo_ref[...])
        x1_ref[rows, :] = x[rows, :] + _rms(y, pg_ref[l:l + 1, :])


def _mix_kernel(l, *refs):
    _mix_common(l, *refs)


def _mix_router_kernel(l, *refs):
    (*ins, fg_ref, rwt_ref, rb_ref, tri_ref,
     x1_ref, h_ref, gw_ref, sel_ref, rank_ref, cnt_ref, sg_ref, carry_ref) = refs

    @pl.when(pl.program_id(0) == 0)
    def _():
        carry_ref[...] = jnp.zeros_like(carry_ref)

    _mix_common(l, *ins, x1_ref, sg_ref)
    h = _rms(x1_ref[...], fg_ref[l:l + 1, :])
    h_hi, h_lo = _split_bf16(h)
    h_ref[...] = h_hi

    w_hi, w_lo = _split_bf16(rwt_ref[...])
    logits = _dot_nt(w_hi, h_hi) + _dot_nt(w_hi, h_lo) + _dot_nt(w_lo, h_hi) + rb_ref[...]
    row = lax.broadcasted_iota(I32, logits.shape, 0)
    m1 = jnp.max(logits, axis=0, keepdims=True)
    i1 = jnp.min(jnp.where(logits == m1, row, N_EXPERTS), axis=0, keepdims=True)
    rest = jnp.where(row == i1, -jnp.inf, logits)
    m2 = jnp.max(rest, axis=0, keepdims=True)
    i2 = jnp.min(jnp.where(rest == m2, row, N_EXPERTS), axis=0, keepdims=True)
    e2 = jnp.exp(m2 - m1)
    den = 1.0 + e2
    gw_ref[...] = jnp.where(row == i1, 1.0 / den, 0.0) + jnp.where(row == i2, e2 / den, 0.0)
    sel = jnp.where((row == i1) | (row == i2), 1.0, 0.0)
    sel_ref[...] = sel.astype(I32)
    before = _dot(sel.astype(BF16), tri_ref[...]) + carry_ref[...]
    rank_ref[...] = before.astype(I32)
    carry_ref[...] = carry_ref[...] + jnp.sum(sel, axis=1, keepdims=True)
    cnt_ref[...] = carry_ref[...].astype(I32)


def _mix_specs(l, tile):
    row = lambda w: pl.BlockSpec((tile, w), lambda i: (i, 0))
    in_specs = [row(D_MODEL), row(MIX_WIDTH), _resident((DEPTH, D_MODEL)), _resident((D_MODEL, D_IN)),
                _layer((N_GROUPS, CHUNK, CHUNK), l), _layer((CHUNK, MIX_WIDTH), l),
                _resident((DEPTH, MIX_WIDTH)), _resident((DEPTH, MIX_WIDTH)),
                _resident((MIX_WIDTH, D_MODEL)), _resident((MIX_WIDTH, D_MODEL)),
                _resident((D_MODEL, D_MODEL)), _resident((DEPTH, D_MODEL))]
    return row, in_specs, [pltpu.VMEM((tile, MIX_WIDTH), BF16)]


def _mixer_body(l, x, fa, pre_g, w_in, w_s, b_full, ln_g, ln_b, w_f, w_g, w_o, post_g, side=()):
    row, in_specs, scratch = _mix_specs(l, MIX_TILE)
    steps = TOKENS // MIX_TILE
    side_in, side_out, side_shapes, side_ops = _side_specs(side, steps)
    outs = pl.pallas_call(
        _with_side_casts(functools.partial(_mix_kernel, l), len(in_specs), 1, len(side)),
        grid=(steps,),
        in_specs=in_specs + side_in,
        out_specs=[row(D_MODEL)] + side_out,
        out_shape=[jax.ShapeDtypeStruct((TOKENS, D_MODEL), F32)] + side_shapes,
        scratch_shapes=scratch,
        compiler_params=_params("arbitrary"),
        name="mixer_body",
    )(x, fa, pre_g, w_in, w_s, b_full, ln_g, ln_b, w_f, w_g, w_o, post_g, *side_ops)
    return outs[0], outs[1:]


def _mixer_body_router(l, x, fa, pre_g, w_in, w_s, b_full, ln_g, ln_b, w_f, w_g, w_o, post_g,
                       ffn_g, router_wt, router_b, tri):
    row, in_specs, scratch = _mix_specs(l, ROUTER_TILE)
    in_specs += [_resident((DEPTH, D_MODEL)), _layer((N_EXPERTS, D_MODEL), l // 2),
                 _layer((N_EXPERTS, 1), l // 2), _resident((ROUTER_TILE, ROUTER_TILE))]
    col = pl.BlockSpec((N_EXPERTS, ROUTER_TILE), lambda i: (0, i))
    per_token = lambda dt: jax.ShapeDtypeStruct((N_EXPERTS, TOKENS), dt)
    return pl.pallas_call(
        functools.partial(_mix_router_kernel, l),
        grid=(TOKENS // ROUTER_TILE,),
        in_specs=in_specs,
        out_specs=[row(D_MODEL), row(D_MODEL), col, col, col,
                   pl.BlockSpec((N_EXPERTS, 1), lambda i: (0, 0))],
        out_shape=[jax.ShapeDtypeStruct((TOKENS, D_MODEL), F32),
                   jax.ShapeDtypeStruct((TOKENS, D_MODEL), BF16),
                   per_token(F32), per_token(I32), per_token(I32),
                   jax.ShapeDtypeStruct((N_EXPERTS, 1), I32)],
        scratch_shapes=scratch + [pltpu.VMEM((N_EXPERTS, 1), F32)],
        compiler_params=_params("arbitrary"),
        name="mixer_body_router",
    )(x, fa, pre_g, w_in, w_s, b_full, ln_g, ln_b, w_f, w_g, w_o, post_g,
      ffn_g, router_wt, router_b, tri)


def _dense_ffn_kernel(l, x_ref, fg_ref, w1_ref, w3_ref, w2_ref, pg_ref, out_ref):
    half = x_ref.shape[0] // 2
    for r in range(2):
        rows = slice(r * half, (r + 1) * half)
        x = x_ref[rows, :]
        h = _rms(x, fg_ref[l:l + 1, :]).astype(BF16)
        a = _dot(h, w1_ref[...])
        b = _dot(h, w3_ref[...])
        y = _dot((jax.nn.silu(a) * b).astype(BF16), w2_ref[...])
        out_ref[rows, :] = x + _rms(y, pg_ref[l:l + 1, :])


def _dense_ffn(l, x, ffn_g, w1, w3, w2, post_g, side=()):
    row = pl.BlockSpec((ROW_TILE, D_MODEL), lambda i: (i, 0))
    steps = TOKENS // ROW_TILE
    side_in, side_out, side_shapes, side_ops = _side_specs(side, steps)
    outs = pl.pallas_call(
        _with_side_casts(functools.partial(_dense_ffn_kernel, l), 6, 1, len(side)),
        grid=(steps,),
        in_specs=[row, _resident((DEPTH, D_MODEL)), _resident((D_MODEL, D_FF_DENSE)),
                  _resident((D_MODEL, D_FF_DENSE)), _resident((D_FF_DENSE, D_MODEL)),
                  _resident((DEPTH, D_MODEL))] + side_in,
        out_specs=[row] + side_out,
        out_shape=[jax.ShapeDtypeStruct((TOKENS, D_MODEL), F32)] + side_shapes,
        compiler_params=_params("arbitrary"),
        name="dense_ffn",
    )(x, ffn_g, w1, w3, w2, post_g, *side_ops)
    return outs[0], outs[1:]


def _window_hits(k, block, start_ref, dest_ref):
    wanted = start_ref[block] + k * TOKEN_WINDOW
    begin = pl.multiple_of(jnp.minimum(wanted, TOKENS - TOKEN_WINDOW), LANES)
    tok = lax.broadcasted_iota(I32, (1, TOKEN_WINDOW), 1) + begin
    dest = jnp.where(tok >= wanted, dest_ref[:, pl.ds(begin, TOKEN_WINDOW)], -1)
    rows = lax.broadcasted_iota(I32, (SORT_BLOCK, TOKEN_WINDOW), 0) + block * SORT_BLOCK
    return begin, rows == dest


def _dispatch_kernel(exp_ref, start_ref, nwin_ref, dest_ref, gw_ref, h_ref, xs_ref, gs_ref):
    j = pl.program_id(0)

    def window(k):
        begin, hit = _window_hits(k, j, start_ref, dest_ref)
        onehot = jnp.where(hit, 1.0, 0.0).astype(BF16)
        picked = _dot(onehot, h_ref[pl.ds(begin, TOKEN_WINDOW), :]).astype(BF16)
        gw = gw_ref[:, pl.ds(begin, TOKEN_WINDOW)]
        return picked, jnp.sum(jnp.where(hit, gw, 0.0), axis=-1, keepdims=True)

    @pl.when(nwin_ref[j] == 0)
    def _():
        xs_ref[...] = jnp.zeros_like(xs_ref)
        gs_ref[...] = jnp.zeros_like(gs_ref)

    @pl.when(nwin_ref[j] > 0)
    def _():
        xs_ref[...], gs_ref[...] = window(0)

        def more(k, carry):
            picked, gsum = window(k)
            xs_ref[...] += picked
            gs_ref[...] += gsum
            return carry
        lax.fori_loop(1, nwin_ref[j], more, 0)


def _per_expert_row():
    return pl.BlockSpec((None, 1, TOKENS),
                        lambda j, e, s, n: (e[jnp.minimum(j, N_SORT_BLOCKS - 1)], 0, 0))


def _dispatch(plan, dest3, gw3, h):
    exp, start, nwin = plan
    grid_spec = pltpu.PrefetchScalarGridSpec(
        num_scalar_prefetch=3,
        grid=(N_SORT_BLOCKS,),
        in_specs=[_per_expert_row(), _per_expert_row(),
                  pl.BlockSpec((TOKENS, D_MODEL), lambda j, e, s, n: (0, 0),
                               pipeline_mode=pl.Buffered(1))],
        out_specs=[pl.BlockSpec((SORT_BLOCK, D_MODEL), lambda j, e, s, n: (j, 0)),
                   pl.BlockSpec((SORT_BLOCK, 1), lambda j, e, s, n: (j, 0))],
    )
    return pl.pallas_call(
        _dispatch_kernel,
        grid_spec=grid_spec,
        out_shape=[jax.ShapeDtypeStruct((SORTED_ROWS, D_MODEL), BF16),
                   jax.ShapeDtypeStruct((SORTED_ROWS, 1), F32)],
        compiler_params=_params("arbitrary"),
        name="moe_dispatch",
    )(exp, start, nwin, dest3, gw3, h)


def _expert_kernel(rexp_ref, nunit_ref, xs_ref, gs_ref, w1_ref, w3_ref, w2_ref, ys_ref, acc_ref):
    q = pl.program_id(0)
    f = pl.program_id(1)
    last = pl.num_programs(1) - 1
    nunit = nunit_ref[q]

    @pl.when((q == 0) & (f == 0))
    def _():
        acc_ref[...] = jnp.zeros_like(acc_ref)

    def swiglu_part(n_rows):
        for start in range(0, n_rows, EXPERT_BLOCK):
            rows = slice(start, min(start + EXPERT_BLOCK, n_rows))
            x = xs_ref[rows, :]
            a = _dot(x, w1_ref[...])
            b = _dot(x, w3_ref[...])
            part = _dot((jax.nn.silu(a) * b).astype(BF16), w2_ref[...])
            acc_ref[rows, :] = jnp.where(f == 0, 0.0, acc_ref[rows, :]) + part

        @pl.when(f == last)
        def _():
            ys_ref[0:n_rows, :] = (acc_ref[0:n_rows, :] * gs_ref[0:n_rows, :]).astype(BF16)
            if n_rows < EXPERT_REGION:
                ys_ref[n_rows:EXPERT_REGION, :] = jnp.zeros((EXPERT_REGION - n_rows, D_MODEL), BF16)

    for units in range(1, EXPERT_REGION // EXPERT_PAD + 1):
        pl.when(nunit == units)(functools.partial(swiglu_part, units * EXPERT_PAD))

    @pl.when((nunit == 0) & (f == last))
    def _():
        ys_ref[...] = jnp.zeros_like(ys_ref)


def _experts(rexp, nunit, xs, gs, w1, w3, w2):
    n_ff = D_FF_EXPERT // EXPERT_FF_TILE
    tile = lambda q, f: jnp.where(q % 2 == 0, f, n_ff - 1 - f)
    grid_spec = pltpu.PrefetchScalarGridSpec(
        num_scalar_prefetch=2,
        grid=(N_REGIONS, D_FF_EXPERT // EXPERT_FF_TILE),
        in_specs=[
            pl.BlockSpec((EXPERT_REGION, D_MODEL), lambda q, f, re, nh: (q, 0)),
            pl.BlockSpec((EXPERT_REGION, 1), lambda q, f, re, nh: (q, 0)),
            pl.BlockSpec((None, D_MODEL, EXPERT_FF_TILE), lambda q, f, re, nh: (re[q], 0, tile(q, f))),
            pl.BlockSpec((None, D_MODEL, EXPERT_FF_TILE), lambda q, f, re, nh: (re[q], 0, tile(q, f))),
            pl.BlockSpec((None, EXPERT_FF_TILE, D_MODEL), lambda q, f, re, nh: (re[q], tile(q, f), 0)),
        ],
        out_specs=pl.BlockSpec((EXPERT_REGION, D_MODEL), lambda q, f, re, nh: (q, 0)),
        scratch_shapes=[pltpu.VMEM((EXPERT_REGION, D_MODEL), F32)],
    )
    return pl.pallas_call(
        _expert_kernel,
        grid_spec=grid_spec,
        out_shape=jax.ShapeDtypeStruct((SORTED_ROWS, D_MODEL), BF16),
        compiler_params=_params("arbitrary", "arbitrary"),
        name="moe_experts",
    )(rexp, nunit, xs, gs, w1, w3, w2)


def _combine_copies(wstart_ref, ys_hbm, buf_ref, sem_ref, tile, rnd, slot):
    copies = []
    for e in range(N_EXPERTS):
        wanted = wstart_ref[tile * N_EXPERTS + e] + rnd * COMBINE_WINDOW
        begin = pl.multiple_of(jnp.minimum(wanted, SORTED_ROWS - COMBINE_WINDOW), BF16_ROWS)
        copies.append(pltpu.make_async_copy(
            ys_hbm.at[pl.ds(begin, COMBINE_WINDOW), :],
            buf_ref.at[slot, pl.ds(e * COMBINE_WINDOW, COMBINE_WINDOW), :],
            sem_ref.at[slot, e]))
    return copies


def _combine_kernel(l, wstart_ref, nround_ref, dest_ref, ys_hbm, x_ref, pg_ref, out_ref,
                    buf_ref, sem_ref):
    i = pl.program_id(0)
    n_tiles = pl.num_programs(0)
    slot = lax.rem(i, 2)

    @pl.when(i == 0)
    def _():
        for c in _combine_copies(wstart_ref, ys_hbm, buf_ref, sem_ref, 0, 0, 0):
            c.start()

    @pl.when(i + 1 < n_tiles)
    def _():
        for c in _combine_copies(wstart_ref, ys_hbm, buf_ref, sem_ref, i + 1, 0, 1 - slot):
            c.start()

    def gather(rnd, fetch_slot):
        sub = lax.broadcasted_iota(I32, (COMBINE_WINDOW, 1), 0)
        pieces = []
        for e in range(N_EXPERTS):
            wanted = wstart_ref[i * N_EXPERTS + e] + rnd * COMBINE_WINDOW
            begin = jnp.minimum(wanted, SORTED_ROWS - COMBINE_WINDOW)
            rowid = sub + begin
            rowid = jnp.where(rowid >= wanted, rowid, -2)
            pieces.append(jnp.where(rowid == dest_ref[e:e + 1, :], 1.0, 0.0).astype(BF16))
        onehot = jnp.concatenate(pieces, axis=0)
        return _dot_tn(onehot, buf_ref[fetch_slot])

    for c in _combine_copies(wstart_ref, ys_hbm, buf_ref, sem_ref, i, 0, slot):
        c.wait()
    y = gather(0, slot)

    def extra(rnd, y):
        copies = _combine_copies(wstart_ref, ys_hbm, buf_ref, sem_ref, i, rnd, 2)
        for c in copies:
            c.start()
        for c in copies:
            c.wait()
        return y + gather(rnd, 2)

    y = lax.fori_loop(1, nround_ref[i], extra, y)
    out_ref[...] = x_ref[...] + _rms(y, pg_ref[l:l + 1, :])


def _combine(l, wstart, nround, dest, ys, x, post_g):
    tile = lambda i, ws, nr: (i, 0)
    grid_spec = pltpu.PrefetchScalarGridSpec(
        num_scalar_prefetch=2,
        grid=(TOKENS // COMBINE_TILE,),
        in_specs=[
            pl.BlockSpec((N_EXPERTS, COMBINE_TILE), lambda i, ws, nr: (0, i)),
            pl.BlockSpec(memory_space=pl.ANY),
            pl.BlockSpec((COMBINE_TILE, D_MODEL), tile),
            pl.BlockSpec((DEPTH, D_MODEL), lambda i, ws, nr: (0, 0)),
        ],
        out_specs=pl.BlockSpec((COMBINE_TILE, D_MODEL), tile),
        scratch_shapes=[pltpu.VMEM((3, N_EXPERTS * COMBINE_WINDOW, D_MODEL), BF16),
                        pltpu.SemaphoreType.DMA((3, N_EXPERTS))],
    )
    return pl.pallas_call(
        functools.partial(_combine_kernel, l),
        grid_spec=grid_spec,
        out_shape=jax.ShapeDtypeStruct((TOKENS, D_MODEL), F32),
        compiler_params=_params("arbitrary"),
        name="moe_combine",
    )(wstart, nround, dest, ys, x, post_g)


def _routing_plan(sel, rank, counts):
    counts = counts.reshape(N_EXPERTS)
    padded = ((counts + EXPERT_PAD - 1) // EXPERT_PAD) * EXPERT_PAD
    region = ((counts + EXPERT_REGION - 1) // EXPERT_REGION) * EXPERT_REGION
    offs = jnp.cumsum(region) - region
    dest = jnp.where(sel == 1, rank + offs[:, None], -1)

    n_reg = region // EXPERT_REGION
    cum_reg = jnp.cumsum(n_reg)
    qs = jnp.arange(N_REGIONS, dtype=I32)
    qc = jnp.minimum(qs, cum_reg[-1] - 1)[:, None]
    owner = ((qc >= (cum_reg - n_reg)[None, :]) & (qc < cum_reg[None, :])).astype(I32)
    pick = lambda own, v: jnp.sum(own * v[None, :], axis=1)
    experts = jnp.arange(N_EXPERTS, dtype=I32)
    rexp = pick(owner, experts).astype(I32)
    rows_left = pick(owner, offs + padded) - qs * EXPERT_REGION
    n_unit = jnp.clip(rows_left // EXPERT_PAD, 0, EXPERT_REGION // EXPERT_PAD).astype(I32)

    group_end = (rank + sel)[:, LANES - 1::LANES]
    js = jnp.arange(N_SORT_BLOCKS, dtype=I32)
    owner_s = jnp.repeat(owner, EXPERT_REGION // SORT_BLOCK, axis=0)
    e_s = pick(owner_s, experts)
    counts_s = pick(owner_s, counts)
    lo_row = js * SORT_BLOCK - pick(owner_s, offs)
    has_rows = lo_row < counts_s
    hi_row = jnp.minimum(lo_row + SORT_BLOCK, counts_s)
    ends = jnp.sum(owner_s[:, :, None] * group_end[None, :, :], axis=1)
    g_lo = jnp.sum((ends <= lo_row[:, None]).astype(I32), axis=1)
    g_hi = jnp.sum((ends < hi_row[:, None]).astype(I32), axis=1)
    span = (g_hi - g_lo + 1) * LANES
    n_win = jnp.where(has_rows, (span + TOKEN_WINDOW - 1) // TOKEN_WINDOW, 0).astype(I32)
    start = jnp.where(has_rows, g_lo * LANES, 0).astype(I32)
    d_plan = (e_s.astype(I32), start, n_win)

    tile_lo = rank[:, ::COMBINE_TILE]
    tile_hi = jnp.concatenate([tile_lo[:, 1:], counts[:, None]], axis=1)
    lo = tile_lo + offs[:, None]
    hi = tile_hi + offs[:, None]
    wstart = jnp.minimum((lo // BF16_ROWS) * BF16_ROWS, SORTED_ROWS - COMBINE_WINDOW)
    rounds = jnp.where(hi > lo, (hi - wstart + COMBINE_WINDOW - 1) // COMBINE_WINDOW, 0)
    n_round = jnp.maximum(jnp.max(rounds, axis=0), 1).astype(I32)
    c_plan = (wstart.T.reshape(-1).astype(I32), n_round)
    return dest, (rexp, n_unit), d_plan, c_plan


def _moe(l, x1, h, gw, sel, rank, counts, w1, w3, w2, post_g):
    dest, e_plan, d_plan, c_plan = _routing_plan(sel, rank, counts)
    dest3 = dest.reshape(N_EXPERTS, 1, TOKENS)
    gw3 = gw.reshape(N_EXPERTS, 1, TOKENS)
    xs, gs = _dispatch(d_plan, dest3, gw3, h)
    ys = _experts(*e_plan, xs, gs, w1, w3, w2)
    return _combine(l, *c_plan, dest, ys, x1, post_g)


def kernel(x, pre_mix_g, post_mix_g, w_in, w_fourier, w_gmlp, w_out, w_spatial, b_spatial,
           gmlp_ln_g, gmlp_ln_b, pre_ffn_g, post_ffn_g, ffn_w1, ffn_w3, ffn_w2,
           router_w, router_b, moe_w1, moe_w3, moe_w2):
    pos = jnp.asarray(_POS_DFT).astype(BF16)
    chan = jnp.asarray(_CHAN_DFT).astype(BF16)
    tri = jnp.asarray(np.triu(np.ones((ROUTER_TILE, ROUTER_TILE), np.float32), 1)).astype(BF16)

    b_full = jnp.repeat(jnp.swapaxes(b_spatial, 1, 2), GROUP_DIM, axis=2)
    router_wt = jnp.swapaxes(router_w, 1, 2)
    router_b = router_b.reshape(router_b.shape[0], N_EXPERTS, 1)

    assert DEPTH == 2
    moe_rows = lambda w: w[0].reshape(-1, w.shape[-1])
    moe_back = lambda w, like: w.reshape(like.shape[1:])
    mixer_weights = lambda l: [(w, l) for w in (w_in, w_fourier, w_gmlp, w_out)]

    xf = x.reshape(TOKENS, D_MODEL)
    per_batch = lambda v: v.reshape(BATCH, SEQ, D_MODEL)
    fa, (win, wf, wg, wo) = _fourier(0, per_batch(xf), pre_mix_g, w_in, chan, pos,
                                     side=mixer_weights(0))
    x1, (ffn1, ffn3, ffn2, moe1) = _mixer_body(
        0, xf, fa, pre_mix_g, win, w_spatial, b_full, gmlp_ln_g, gmlp_ln_b, wf, wg, wo, post_mix_g,
        side=(ffn_w1[0], ffn_w3[0], ffn_w2[0], moe_rows(moe_w1)))
    xf, (moe3, moe2) = _dense_ffn(0, x1, pre_ffn_g, ffn1, ffn3, ffn2, post_ffn_g,
                                  side=(moe_rows(moe_w3), moe_rows(moe_w2)))

    fa, (win, wf, wg, wo) = _fourier(1, per_batch(xf), pre_mix_g, w_in, chan, pos,
                                     side=mixer_weights(1))
    x1, h, gw, sel, rank, counts = _mixer_body_router(
        1, xf, fa, pre_mix_g, win, w_spatial, b_full, gmlp_ln_g, gmlp_ln_b, wf, wg, wo, post_mix_g,
        pre_ffn_g, router_wt, router_b, tri)
    xf = _moe(1, x1, h, gw, sel, rank, counts, moe_back(moe1, moe_w1), moe_back(moe3, moe_w3),
              moe_back(moe2, moe_w2), post_ffn_g)
    return xf.reshape(BATCH, SEQ, D_MODEL)
```

```python
import functools

import numpy as np
import jax
import jax.numpy as jnp
from jax import lax
from jax.experimental import pallas as pl
from jax.experimental.pallas import tpu as pltpu

F32 = jnp.float32
BF16 = jnp.bfloat16
I32 = jnp.int32

D_MODEL = 1024
BATCH = 8
SEQ = 2048
HALF_SEQ = SEQ // 2
TOKENS = BATCH * SEQ
DEPTH = 2
N_GROUPS = 4
GROUP_DIM = 128
MIX_WIDTH = N_GROUPS * GROUP_DIM
CHUNK = 128
D_IN = 3 * MIX_WIDTH + 2 * D_MODEL
D_FF_DENSE = 2816
N_EXPERTS = 8
D_FF_EXPERT = 3584
RMS_EPS = 1e-6
LN_EPS = 1e-5
LANES = 128

ROW_TILE = 512
MIX_TILE = 512
ROUTER_TILE = 1024
EXPERT_BLOCK = 512
EXPERT_PAD = 256
EXPERT_REGION = 1024
SORT_BLOCK = 256
TOKEN_WINDOW = 1280
COMBINE_TILE = 256
COMBINE_WINDOW = 128
BF16_ROWS = 16
EXPERT_FF_TILE = 1792

SORTED_ROWS = 2 * TOKENS + N_EXPERTS * EXPERT_REGION
N_REGIONS = SORTED_ROWS // EXPERT_REGION
N_SORT_BLOCKS = SORTED_ROWS // SORT_BLOCK
N_COMBINE_TILES = TOKENS // COMBINE_TILE

VMEM_LIMIT = 56 * 1024 * 1024


def _params(*sem):
    return pltpu.CompilerParams(dimension_semantics=sem, vmem_limit_bytes=VMEM_LIMIT)


def _resident(shape):
    nd = len(shape)
    return pl.BlockSpec(shape, lambda *_: (0,) * nd, pipeline_mode=pl.Buffered(1))


def _layer(shape, l):
    nd = len(shape)
    return pl.BlockSpec((None,) + tuple(shape), lambda *_: (l,) + (0,) * nd,
                        pipeline_mode=pl.Buffered(1))


def _rms(x, g):
    return x * lax.rsqrt(jnp.mean(x * x, axis=-1, keepdims=True) + RMS_EPS) * g


def _dot(a, b):
    return jnp.dot(a, b, preferred_element_type=F32)


def _dot_nt(a, b):
    return lax.dot_general(a, b, (((1,), (1,)), ((), ())), preferred_element_type=F32)


def _dot_tn(a, b):
    return lax.dot_general(a, b, (((0,), (0,)), ((), ())), preferred_element_type=F32)


def _gelu(x):
    return 0.5 * x * (1.0 + lax.erf(x * np.float32(np.sqrt(0.5))))


def _cast_once(*pairs):
    chunk = 128

    @pl.when(pl.program_id(0) == 0)
    def _():
        for src_ref, dst_ref in pairs:
            def body(i, carry, src_ref=src_ref, dst_ref=dst_ref):
                rows = pl.ds(pl.multiple_of(i * chunk, chunk), chunk)
                dst_ref[rows, :] = src_ref[rows, :].astype(BF16)
                return carry
            lax.fori_loop(0, src_ref.shape[0] // chunk, body, 0)


def _split_bf16(x):
    hi = x.astype(BF16)
    return hi, (x - hi.astype(F32)).astype(BF16)


def _side_specs(arrays, steps):
    in_specs, out_specs, shapes, operands = [], [], [], []
    for entry in arrays:
        a, l = entry if isinstance(entry, tuple) else (entry, None)
        rows, cols = a.shape[-2:]
        chunk, per = rows // steps, 1
        if chunk % BF16_ROWS:
            chunk, per = 2 * chunk, 2
        assert rows % chunk == 0 and chunk % BF16_ROWS == 0, (a.shape, steps)
        out_specs.append(pl.BlockSpec((chunk, cols), lambda i, per=per: (i // per, 0)))
        if l is None:
            in_specs.append(out_specs[-1])
        else:
            in_specs.append(pl.BlockSpec((None, chunk, cols), lambda i, per=per, l=l: (l, i // per, 0)))
        shapes.append(jax.ShapeDtypeStruct((rows, cols), BF16))
        operands.append(a)
    return in_specs, out_specs, shapes, operands


def _with_side_casts(body, n_in, n_out, n_side):
    def kernel(*refs):
        ins, rest = refs[:n_in], refs[n_in:]
        srcs, rest = rest[:n_side], rest[n_side:]
        outs, rest = rest[:n_out], rest[n_out:]
        dsts, scratch = rest[:n_side], rest[n_side:]
        body(*ins, *outs, *scratch)
        for src, dst in zip(srcs, dsts):
            dst[...] = src[...].astype(BF16)
    return kernel


def _dft_tables():
    k = np.arange(HALF_SEQ, dtype=np.int64)
    scale = 1.0 / np.sqrt(SEQ)
    halves = []
    for p in (0, 1):
        n = 2 * np.arange(HALF_SEQ, dtype=np.int64) + p
        ang = 2.0 * np.pi * ((k[:, None] * n[None, :]) % SEQ).astype(np.float64) / SEQ
        halves.append(np.concatenate([np.cos(ang) * scale, -np.sin(ang) * scale], axis=1))
    pos = np.stack(halves)
    c = np.arange(GROUP_DIM, dtype=np.int64)
    angc = 2.0 * np.pi * ((c[:, None] * c[None, :]) % GROUP_DIM).astype(np.float64) / GROUP_DIM
    scalec = 1.0 / np.sqrt(GROUP_DIM)
    chan = np.concatenate([np.cos(angc) * scalec, np.sin(angc) * scalec], axis=1)
    return pos.astype(np.float32), chan.astype(np.float32)


_POS_DFT, _CHAN_DFT = _dft_tables()


def _fourier_kernel(l, x_ref, g_ref, wa32_ref, chan_ref, pos_ref, out_ref, stage_ref, rhs_ref, wa_ref):
    _cast_once((wa32_ref, wa_ref))
    for half in range(SEQ // HALF_SEQ):
        rows = slice(half * HALF_SEQ, (half + 1) * HALF_SEQ)
        h = _rms(x_ref[0, rows, :], g_ref[l:l + 1, :]).astype(BF16)
        za = _dot(h, wa_ref[...]).astype(BF16)
        for g in range(N_GROUPS):
            xcs = _dot(za[:, g * GROUP_DIM:(g + 1) * GROUP_DIM], chan_ref[...])
            stage_ref[g, rows, :] = xcs[:, 0:GROUP_DIM]
            stage_ref[N_GROUPS + g, rows, :] = xcs[:, GROUP_DIM:2 * GROUP_DIM]
    for g in range(N_GROUPS):
        cols = slice(g * GROUP_DIM, (g + 1) * GROUP_DIM)
        for p in (0, 1):
            rows = pl.ds(p, HALF_SEQ, stride=2)
            rhs_ref[p, 0:HALF_SEQ, cols] = stage_ref[g, rows, :].astype(BF16)
            rhs_ref[p, HALF_SEQ:SEQ, cols] = stage_ref[N_GROUPS + g, rows, :].astype(BF16)

    even = _dot(pos_ref[0], rhs_ref[0])
    odd = _dot(pos_ref[1], rhs_ref[1])
    out_ref[0, 0] = (even + odd).astype(BF16)
    out_ref[0, 1] = (even - odd).astype(BF16)


def _fourier(l, x, g, w_in, chan, pos, side=()):
    side_in, side_out, side_shapes, side_ops = _side_specs(side, BATCH)
    outs = pl.pallas_call(
        _with_side_casts(functools.partial(_fourier_kernel, l), 5, 1, len(side)),
        grid=(BATCH,),
        in_specs=[pl.BlockSpec((1, SEQ, D_MODEL), lambda b: (b, 0, 0)),
                  _resident((DEPTH, D_MODEL)), _layer((D_MODEL, MIX_WIDTH), l),
                  _resident((GROUP_DIM, 2 * GROUP_DIM)), _resident((2, HALF_SEQ, SEQ))] + side_in,
        out_specs=[pl.BlockSpec((1, 2, HALF_SEQ, MIX_WIDTH), lambda b: (b, 0, 0, 0))] + side_out,
        out_shape=[jax.ShapeDtypeStruct((BATCH, 2, HALF_SEQ, MIX_WIDTH), BF16)] + side_shapes,
        scratch_shapes=[pltpu.VMEM((2 * N_GROUPS, SEQ, GROUP_DIM), F32),
                        pltpu.VMEM((2, SEQ, MIX_WIDTH), BF16),
                        pltpu.VMEM((D_MODEL, MIX_WIDTH), BF16)],
        compiler_params=_params("arbitrary"),
        name="fourier_mix",
    )(x, g, w_in, chan, pos, *side_ops)
    return outs[0].reshape(TOKENS, MIX_WIDTH), outs[1:]


def _mix_common(l, x_ref, fa_ref, prg_ref, win_ref, ws_ref, bs_ref, lng_ref, lnb_ref,
                wf_ref, wg_ref, wo_ref, pg_ref, x1_ref, sg_ref):
    x = x_ref[...]
    h = _rms(x, prg_ref[l:l + 1, :]).astype(BF16)
    v = _gelu(_dot(h, win_ref[:, 2 * MIX_WIDTH:3 * MIX_WIDTH]))
    u = _gelu(_dot(h, win_ref[:, MIX_WIDTH:2 * MIX_WIDTH]))
    gate = jax.nn.sigmoid(_dot(h, win_ref[:, 3 * MIX_WIDTH:D_IN]))
    ya = _dot(fa_ref[...], wf_ref[...])
    mu = jnp.mean(v, axis=-1, keepdims=True)
    vc = v - mu
    var = jnp.mean(vc * vc, axis=-1, keepdims=True)
    vln = (vc * lax.rsqrt(var + LN_EPS) * lng_ref[l:l + 1, :] + lnb_ref[l:l + 1, :]).astype(BF16)
    ws = [ws_ref[g].astype(BF16) for g in range(N_GROUPS)]
    for c in range(x_ref.shape[0] // CHUNK):
        rows = slice(c * CHUNK, (c + 1) * CHUNK)
        for g in range(N_GROUPS):
            cols = slice(g * GROUP_DIM, (g + 1) * GROUP_DIM)
            sv = _dot(ws[g], vln[rows, cols]) + bs_ref[:, cols]
            sg_ref[rows, cols] = (u[rows, cols] * sv).astype(BF16)
    yb = _dot(sg_ref[...], wg_ref[...])
    m = (gate[:, 0:D_MODEL] * ya + gate[:, D_MODEL:2 * D_MODEL] * yb).astype(BF16)
    half = x_ref.shape[0] // 2
    for r in range(2):
        rows = slice(r * half, (r + 1) * half)
        y = _dot(m[rows, :], wo_ref[...])
        x1_ref[rows, :] = x[rows, :] + _rms(y, pg_ref[l:l + 1, :])


def _mix_kernel(l, *refs):
    _mix_common(l, *refs)


def _mix_router_kernel(l, *refs):
    (*ins, fg_ref, rwt_ref, rb_ref, tri_ref,
     x1_ref, h_ref, gw_ref, sel_ref, rank_ref, cnt_ref, sg_ref, carry_ref) = refs

    @pl.when(pl.program_id(0) == 0)
    def _():
        carry_ref[...] = jnp.zeros_like(carry_ref)

    _mix_common(l, *ins, x1_ref, sg_ref)
    h = _rms(x1_ref[...], fg_ref[l:l + 1, :])
    h_hi, h_lo = _split_bf16(h)
    h_ref[...] = h_hi

    w_hi, w_lo = _split_bf16(rwt_ref[...])
    logits = _dot_nt(w_hi, h_hi) + _dot_nt(w_hi, h_lo) + _dot_nt(w_lo, h_hi) + rb_ref[...]
    row = lax.broadcasted_iota(I32, logits.shape, 0)
    m1 = jnp.max(logits, axis=0, keepdims=True)
    i1 = jnp.min(jnp.where(logits == m1, row, N_EXPERTS), axis=0, keepdims=True)
    rest = jnp.where(row == i1, -jnp.inf, logits)
    m2 = jnp.max(rest, axis=0, keepdims=True)
    i2 = jnp.min(jnp.where(rest == m2, row, N_EXPERTS), axis=0, keepdims=True)
    e2 = jnp.exp(m2 - m1)
    den = 1.0 + e2
    gw_ref[...] = jnp.where(row == i1, 1.0 / den, 0.0) + jnp.where(row == i2, e2 / den, 0.0)
    sel = jnp.where((row == i1) | (row == i2), 1.0, 0.0)
    sel_ref[...] = sel.astype(I32)
    before = _dot(sel.astype(BF16), tri_ref[...]) + carry_ref[...]
    rank_ref[...] = before.astype(I32)
    carry_ref[...] = carry_ref[...] + jnp.sum(sel, axis=1, keepdims=True)
    cnt_ref[...] = carry_ref[...].astype(I32)


def _mix_specs(l, tile):
    row = lambda w: pl.BlockSpec((tile, w), lambda i: (i, 0))
    in_specs = [row(D_MODEL), row(MIX_WIDTH), _resident((DEPTH, D_MODEL)), _resident((D_MODEL, D_IN)),
                _layer((N_GROUPS, CHUNK, CHUNK), l), _layer((CHUNK, MIX_WIDTH), l),
                _resident((DEPTH, MIX_WIDTH)), _resident((DEPTH, MIX_WIDTH)),
                _resident((MIX_WIDTH, D_MODEL)), _resident((MIX_WIDTH, D_MODEL)),
                _resident((D_MODEL, D_MODEL)), _resident((DEPTH, D_MODEL))]
    return row, in_specs, [pltpu.VMEM((tile, MIX_WIDTH), BF16)]


def _mixer_body(l, x, fa, pre_g, w_in, w_s, b_full, ln_g, ln_b, w_f, w_g, w_o, post_g, side=()):
    row, in_specs, scratch = _mix_specs(l, MIX_TILE)
    steps = TOKENS // MIX_TILE
    side_in, side_out, side_shapes, side_ops = _side_specs(side, steps)
    outs = pl.pallas_call(
        _with_side_casts(functools.partial(_mix_kernel, l), len(in_specs), 1, len(side)),
        grid=(steps,),
        in_specs=in_specs + side_in,
        out_specs=[row(D_MODEL)] + side_out,
        out_shape=[jax.ShapeDtypeStruct((TOKENS, D_MODEL), F32)] + side_shapes,
        scratch_shapes=scratch,
        compiler_params=_params("arbitrary"),
        name="mixer_body",
    )(x, fa, pre_g, w_in, w_s, b_full, ln_g, ln_b, w_f, w_g, w_o, post_g, *side_ops)
    return outs[0], outs[1:]


def _mixer_body_router(l, x, fa, pre_g, w_in, w_s, b_full, ln_g, ln_b, w_f, w_g, w_o, post_g,
                       ffn_g, router_wt, router_b, tri):
    row, in_specs, scratch = _mix_specs(l, ROUTER_TILE)
    in_specs += [_resident((DEPTH, D_MODEL)), _layer((N_EXPERTS, D_MODEL), l // 2),
                 _layer((N_EXPERTS, 1), l // 2), _resident((ROUTER_TILE, ROUTER_TILE))]
    col = pl.BlockSpec((N_EXPERTS, ROUTER_TILE), lambda i: (0, i))
    per_token = lambda dt: jax.ShapeDtypeStruct((N_EXPERTS, TOKENS), dt)
    return pl.pallas_call(
        functools.partial(_mix_router_kernel, l),
        grid=(TOKENS // ROUTER_TILE,),
        in_specs=in_specs,
        out_specs=[row(D_MODEL), row(D_MODEL), col, col, col,
                   pl.BlockSpec((N_EXPERTS, 1), lambda i: (0, 0))],
        out_shape=[jax.ShapeDtypeStruct((TOKENS, D_MODEL), F32),
                   jax.ShapeDtypeStruct((TOKENS, D_MODEL), BF16),
                   per_token(F32), per_token(I32), per_token(I32),
                   jax.ShapeDtypeStruct((N_EXPERTS, 1), I32)],
        scratch_shapes=scratch + [pltpu.VMEM((N_EXPERTS, 1), F32)],
        compiler_params=_params("arbitrary"),
        name="mixer_body_router",
    )(x, fa, pre_g, w_in, w_s, b_full, ln_g, ln_b, w_f, w_g, w_o, post_g,
      ffn_g, router_wt, router_b, tri)


def _dense_ffn_kernel(l, x_ref, fg_ref, w1_ref, w3_ref, w2_ref, pg_ref, out_ref):
    half = x_ref.shape[0] // 2
    for r in range(2):
        rows = slice(r * half, (r + 1) * half)
        x = x_ref[rows, :]
        h = _rms(x, fg_ref[l:l + 1, :]).astype(BF16)
        a = _dot(h, w1_ref[...])
        b = _dot(h, w3_ref[...])
        y = _dot((jax.nn.silu(a) * b).astype(BF16), w2_ref[...])
        out_ref[rows, :] = x + _rms(y, pg_ref[l:l + 1, :])


def _dense_ffn(l, x, ffn_g, w1, w3, w2, post_g, side=()):
    row = pl.BlockSpec((ROW_TILE, D_MODEL), lambda i: (i, 0))
    steps = TOKENS // ROW_TILE
    side_in, side_out, side_shapes, side_ops = _side_specs(side, steps)
    outs = pl.pallas_call(
        _with_side_casts(functools.partial(_dense_ffn_kernel, l), 6, 1, len(side)),
        grid=(steps,),
        in_specs=[row, _resident((DEPTH, D_MODEL)), _resident((D_MODEL, D_FF_DENSE)),
                  _resident((D_MODEL, D_FF_DENSE)), _resident((D_FF_DENSE, D_MODEL)),
                  _resident((DEPTH, D_MODEL))] + side_in,
        out_specs=[row] + side_out,
        out_shape=[jax.ShapeDtypeStruct((TOKENS, D_MODEL), F32)] + side_shapes,
        compiler_params=_params("arbitrary"),
        name="dense_ffn",
    )(x, ffn_g, w1, w3, w2, post_g, *side_ops)
    return outs[0], outs[1:]


def _window_hits(k, block, start_ref, dest_ref):
    wanted = start_ref[block] + k * TOKEN_WINDOW
    begin = pl.multiple_of(jnp.minimum(wanted, TOKENS - TOKEN_WINDOW), LANES)
    tok = lax.broadcasted_iota(I32, (1, TOKEN_WINDOW), 1) + begin
    dest = jnp.where(tok >= wanted, dest_ref[:, pl.ds(begin, TOKEN_WINDOW)], -1)
    rows = lax.broadcasted_iota(I32, (SORT_BLOCK, TOKEN_WINDOW), 0) + block * SORT_BLOCK
    return begin, rows == dest


def _dispatch_kernel(exp_ref, start_ref, nwin_ref, dest_ref, gw_ref, h_ref, xs_ref, gs_ref):
    j = pl.program_id(0)

    def window(k):
        begin, hit = _window_hits(k, j, start_ref, dest_ref)
        onehot = jnp.where(hit, 1.0, 0.0).astype(BF16)
        picked = _dot(onehot, h_ref[pl.ds(begin, TOKEN_WINDOW), :]).astype(BF16)
        gw = gw_ref[:, pl.ds(begin, TOKEN_WINDOW)]
        return picked, jnp.sum(jnp.where(hit, gw, 0.0), axis=-1, keepdims=True)

    @pl.when(nwin_ref[j] == 0)
    def _():
        xs_ref[...] = jnp.zeros_like(xs_ref)
        gs_ref[...] = jnp.zeros_like(gs_ref)

    @pl.when(nwin_ref[j] > 0)
    def _():
        xs_ref[...], gs_ref[...] = window(0)

        def more(k, carry):
            picked, gsum = window(k)
            xs_ref[...] += picked
            gs_ref[...] += gsum
            return carry
        lax.fori_loop(1, nwin_ref[j], more, 0)


def _per_expert_row():
    return pl.BlockSpec((None, 1, TOKENS),
                        lambda j, e, s, n: (e[jnp.minimum(j, N_SORT_BLOCKS - 1)], 0, 0))


def _dispatch(plan, dest3, gw3, h):
    exp, start, nwin = plan
    grid_spec = pltpu.PrefetchScalarGridSpec(
        num_scalar_prefetch=3,
        grid=(N_SORT_BLOCKS,),
        in_specs=[_per_expert_row(), _per_expert_row(),
                  pl.BlockSpec((TOKENS, D_MODEL), lambda j, e, s, n: (0, 0),
                               pipeline_mode=pl.Buffered(1))],
        out_specs=[pl.BlockSpec((SORT_BLOCK, D_MODEL), lambda j, e, s, n: (j, 0)),
                   pl.BlockSpec((SORT_BLOCK, 1), lambda j, e, s, n: (j, 0))],
    )
    return pl.pallas_call(
        _dispatch_kernel,
        grid_spec=grid_spec,
        out_shape=[jax.ShapeDtypeStruct((SORTED_ROWS, D_MODEL), BF16),
                   jax.ShapeDtypeStruct((SORTED_ROWS, 1), F32)],
        compiler_params=_params("arbitrary"),
        name="moe_dispatch",
    )(exp, start, nwin, dest3, gw3, h)


def _expert_kernel(rexp_ref, nunit_ref, xs_ref, gs_ref, w1_ref, w3_ref, w2_ref, ys_ref, acc_ref):
    q = pl.program_id(0)
    f = pl.program_id(1)
    last = pl.num_programs(1) - 1
    nunit = nunit_ref[q]

    @pl.when((q == 0) & (f == 0))
    def _():
        acc_ref[...] = jnp.zeros_like(acc_ref)

    def swiglu_part(n_rows):
        for start in range(0, n_rows, EXPERT_BLOCK):
            rows = slice(start, min(start + EXPERT_BLOCK, n_rows))
            x = xs_ref[rows, :]
            a = _dot(x, w1_ref[...])
            b = _dot(x, w3_ref[...])
            part = _dot((jax.nn.silu(a) * b).astype(BF16), w2_ref[...])
            acc_ref[rows, :] = jnp.where(f == 0, 0.0, acc_ref[rows, :]) + part

        @pl.when(f == last)
        def _():
            ys_ref[0:n_rows, :] = (acc_ref[0:n_rows, :] * gs_ref[0:n_rows, :]).astype(BF16)
            if n_rows < EXPERT_REGION:
                ys_ref[n_rows:EXPERT_REGION, :] = jnp.zeros((EXPERT_REGION - n_rows, D_MODEL), BF16)

    for units in range(1, EXPERT_REGION // EXPERT_PAD + 1):
        pl.when(nunit == units)(functools.partial(swiglu_part, units * EXPERT_PAD))

    @pl.when((nunit == 0) & (f == last))
    def _():
        ys_ref[...] = jnp.zeros_like(ys_ref)


def _experts(rexp, nunit, xs, gs, w1, w3, w2):
    n_ff = D_FF_EXPERT // EXPERT_FF_TILE
    tile = lambda q, f: jnp.where(q % 2 == 0, f, n_ff - 1 - f)
    grid_spec = pltpu.PrefetchScalarGridSpec(
        num_scalar_prefetch=2,
        grid=(N_REGIONS, D_FF_EXPERT // EXPERT_FF_TILE),
        in_specs=[
            pl.BlockSpec((EXPERT_REGION, D_MODEL), lambda q, f, re, nh: (q, 0)),
            pl.BlockSpec((EXPERT_REGION, 1), lambda q, f, re, nh: (q, 0)),
            pl.BlockSpec((None, D_MODEL, EXPERT_FF_TILE), lambda q, f, re, nh: (re[q], 0, tile(q, f))),
            pl.BlockSpec((None, D_MODEL, EXPERT_FF_TILE), lambda q, f, re, nh: (re[q], 0, tile(q, f))),
            pl.BlockSpec((None, EXPERT_FF_TILE, D_MODEL), lambda q, f, re, nh: (re[q], tile(q, f), 0)),
        ],
        out_specs=pl.BlockSpec((EXPERT_REGION, D_MODEL), lambda q, f, re, nh: (q, 0)),
        scratch_shapes=[pltpu.VMEM((EXPERT_REGION, D_MODEL), F32)],
    )
    return pl.pallas_call(
        _expert_kernel,
        grid_spec=grid_spec,
        out_shape=jax.ShapeDtypeStruct((SORTED_ROWS, D_MODEL), BF16),
        compiler_params=_params("arbitrary", "arbitrary"),
        name="moe_experts",
    )(rexp, nunit, xs, gs, w1, w3, w2)


def _combine_copies(wstart_ref, ys_hbm, buf_ref, sem_ref, tile, rnd, slot):
    copies = []
    for e in range(N_EXPERTS):
        wanted = wstart_ref[tile * N_EXPERTS + e] + rnd * COMBINE_WINDOW
        begin = pl.multiple_of(jnp.minimum(wanted, SORTED_ROWS - COMBINE_WINDOW), BF16_ROWS)
        copies.append(pltpu.make_async_copy(
            ys_hbm.at[pl.ds(begin, COMBINE_WINDOW), :],
            buf_ref.at[slot, pl.ds(e * COMBINE_WINDOW, COMBINE_WINDOW), :],
            sem_ref.at[slot, e]))
    return copies


def _combine_kernel(l, wstart_ref, nround_ref, dest_ref, ys_hbm, x_ref, pg_ref, out_ref,
                    buf_ref, sem_ref):
    i = pl.program_id(0)
    n_tiles = pl.num_programs(0)
    slot = lax.rem(i, 2)

    @pl.when(i == 0)
    def _():
        for c in _combine_copies(wstart_ref, ys_hbm, buf_ref, sem_ref, 0, 0, 0):
            c.start()

    @pl.when(i + 1 < n_tiles)
    def _():
        for c in _combine_copies(wstart_ref, ys_hbm, buf_ref, sem_ref, i + 1, 0, 1 - slot):
            c.start()

    def gather(rnd, fetch_slot):
        sub = lax.broadcasted_iota(I32, (COMBINE_WINDOW, 1), 0)
        pieces = []
        for e in range(N_EXPERTS):
            wanted = wstart_ref[i * N_EXPERTS + e] + rnd * COMBINE_WINDOW
            begin = jnp.minimum(wanted, SORTED_ROWS - COMBINE_WINDOW)
            rowid = sub + begin
            rowid = jnp.where(rowid >= wanted, rowid, -2)
            pieces.append(jnp.where(rowid == dest_ref[e:e + 1, :], 1.0, 0.0).astype(BF16))
        onehot = jnp.concatenate(pieces, axis=0)
        return _dot_tn(onehot, buf_ref[fetch_slot])

    for c in _combine_copies(wstart_ref, ys_hbm, buf_ref, sem_ref, i, 0, slot):
        c.wait()
    y = gather(0, slot)

    def extra(rnd, y):
        copies = _combine_copies(wstart_ref, ys_hbm, buf_ref, sem_ref, i, rnd, 2)
        for c in copies:
            c.start()
        for c in copies:
            c.wait()
        return y + gather(rnd, 2)

    y = lax.fori_loop(1, nround_ref[i], extra, y)
    out_ref[...] = x_ref[...] + _rms(y, pg_ref[l:l + 1, :])


def _combine(l, wstart, nround, dest, ys, x, post_g):
    tile = lambda i, ws, nr: (i, 0)
    grid_spec = pltpu.PrefetchScalarGridSpec(
        num_scalar_prefetch=2,
        grid=(TOKENS // COMBINE_TILE,),
        in_specs=[
            pl.BlockSpec((N_EXPERTS, COMBINE_TILE), lambda i, ws, nr: (0, i)),
            pl.BlockSpec(memory_space=pl.ANY),
            pl.BlockSpec((COMBINE_TILE, D_MODEL), tile),
            pl.BlockSpec((DEPTH, D_MODEL), lambda i, ws, nr: (0, 0)),
        ],
        out_specs=pl.BlockSpec((COMBINE_TILE, D_MODEL), tile),
        scratch_shapes=[pltpu.VMEM((3, N_EXPERTS * COMBINE_WINDOW, D_MODEL), BF16),
                        pltpu.SemaphoreType.DMA((3, N_EXPERTS))],
    )
    return pl.pallas_call(
        functools.partial(_combine_kernel, l),
        grid_spec=grid_spec,
        out_shape=jax.ShapeDtypeStruct((TOKENS, D_MODEL), F32),
        compiler_params=_params("arbitrary"),
        name="moe_combine",
    )(wstart, nround, dest, ys, x, post_g)


def _routing_plan(sel, rank, counts):
    counts = counts.reshape(N_EXPERTS)
    padded = ((counts + EXPERT_PAD - 1) // EXPERT_PAD) * EXPERT_PAD
    region = ((counts + EXPERT_REGION - 1) // EXPERT_REGION) * EXPERT_REGION
    offs = jnp.cumsum(region) - region
    dest = jnp.where(sel == 1, rank + offs[:, None], -1)

    n_reg = region // EXPERT_REGION
    cum_reg = jnp.cumsum(n_reg)
    qs = jnp.arange(N_REGIONS, dtype=I32)
    qc = jnp.minimum(qs, cum_reg[-1] - 1)[:, None]
    owner = ((qc >= (cum_reg - n_reg)[None, :]) & (qc < cum_reg[None, :])).astype(I32)
    pick = lambda own, v: jnp.sum(own * v[None, :], axis=1)
    experts = jnp.arange(N_EXPERTS, dtype=I32)
    rexp = pick(owner, experts).astype(I32)
    rows_left = pick(owner, offs + padded) - qs * EXPERT_REGION
    n_unit = jnp.clip(rows_left // EXPERT_PAD, 0, EXPERT_REGION // EXPERT_PAD).astype(I32)

    group_end = (rank + sel)[:, LANES - 1::LANES]
    js = jnp.arange(N_SORT_BLOCKS, dtype=I32)
    owner_s = jnp.repeat(owner, EXPERT_REGION // SORT_BLOCK, axis=0)
    e_s = pick(owner_s, experts)
    counts_s = pick(owner_s, counts)
    lo_row = js * SORT_BLOCK - pick(owner_s, offs)
    has_rows = lo_row < counts_s
    hi_row = jnp.minimum(lo_row + SORT_BLOCK, counts_s)
    ends = jnp.sum(owner_s[:, :, None] * group_end[None, :, :], axis=1)
    g_lo = jnp.sum((ends <= lo_row[:, None]).astype(I32), axis=1)
    g_hi = jnp.sum((ends < hi_row[:, None]).astype(I32), axis=1)
    span = (g_hi - g_lo + 1) * LANES
    n_win = jnp.where(has_rows, (span + TOKEN_WINDOW - 1) // TOKEN_WINDOW, 0).astype(I32)
    start = jnp.where(has_rows, g_lo * LANES, 0).astype(I32)
    d_plan = (e_s.astype(I32), start, n_win)

    tile_lo = rank[:, ::COMBINE_TILE]
    tile_hi = jnp.concatenate([tile_lo[:, 1:], counts[:, None]], axis=1)
    lo = tile_lo + offs[:, None]
    hi = tile_hi + offs[:, None]
    wstart = jnp.minimum((lo // BF16_ROWS) * BF16_ROWS, SORTED_ROWS - COMBINE_WINDOW)
    rounds = jnp.where(hi > lo, (hi - wstart + COMBINE_WINDOW - 1) // COMBINE_WINDOW, 0)
    n_round = jnp.maximum(jnp.max(rounds, axis=0), 1).astype(I32)
    c_plan = (wstart.T.reshape(-1).astype(I32), n_round)
    return dest, (rexp, n_unit), d_plan, c_plan


def _moe(l, x1, h, gw, sel, rank, counts, w1, w3, w2, post_g):
    dest, e_plan, d_plan, c_plan = _routing_plan(sel, rank, counts)
    dest3 = dest.reshape(N_EXPERTS, 1, TOKENS)
    gw3 = gw.reshape(N_EXPERTS, 1, TOKENS)
    xs, gs = _dispatch(d_plan, dest3, gw3, h)
    ys = _experts(*e_plan, xs, gs, w1, w3, w2)
    return _combine(l, *c_plan, dest, ys, x1, post_g)


def kernel(x, pre_mix_g, post_mix_g, w_in, w_fourier, w_gmlp, w_out, w_spatial, b_spatial,
           gmlp_ln_g, gmlp_ln_b, pre_ffn_g, post_ffn_g, ffn_w1, ffn_w3, ffn_w2,
           router_w, router_b, moe_w1, moe_w3, moe_w2):
    pos = jnp.asarray(_POS_DFT).astype(BF16)
    chan = jnp.asarray(_CHAN_DFT).astype(BF16)
    tri = jnp.asarray(np.triu(np.ones((ROUTER_TILE, ROUTER_TILE), np.float32), 1)).astype(BF16)

    b_full = jnp.repeat(jnp.swapaxes(b_spatial, 1, 2), GROUP_DIM, axis=2)
    router_wt = jnp.swapaxes(router_w, 1, 2)
    router_b = router_b.reshape(router_b.shape[0], N_EXPERTS, 1)

    assert DEPTH == 2
    moe_rows = lambda w: w[0].reshape(-1, w.shape[-1])
    moe_back = lambda w, like: w.reshape(like.shape[1:])
    mixer_weights = lambda l: [(w, l) for w in (w_in, w_fourier, w_gmlp, w_out)]

    xf = x.reshape(TOKENS, D_MODEL)
    per_batch = lambda v: v.reshape(BATCH, SEQ, D_MODEL)
    fa, (win, wf, wg, wo) = _fourier(0, per_batch(xf), pre_mix_g, w_in, chan, pos,
                                     side=mixer_weights(0))
    x1, (ffn1, ffn3, ffn2, moe1) = _mixer_body(
        0, xf, fa, pre_mix_g, win, w_spatial, b_full, gmlp_ln_g, gmlp_ln_b, wf, wg, wo, post_mix_g,
        side=(ffn_w1[0], ffn_w3[0], ffn_w2[0], moe_rows(moe_w1)))
    xf, (moe3, moe2) = _dense_ffn(0, x1, pre_ffn_g, ffn1, ffn3, ffn2, post_ffn_g,
                                  side=(moe_rows(moe_w3), moe_rows(moe_w2)))

    fa, (win, wf, wg, wo) = _fourier(1, per_batch(xf), pre_mix_g, w_in, chan, pos,
                                     side=mixer_weights(1))
    x1, h, gw, sel, rank, counts = _mixer_body_router(
        1, xf, fa, pre_mix_g, win, w_spatial, b_full, gmlp_ln_g, gmlp_ln_b, wf, wg, wo, post_mix_g,
        pre_ffn_g, router_wt, router_b, tri)
    xf = _moe(1, x1, h, gw, sel, rank, counts, moe_back(moe1, moe_w1), moe_back(moe3, moe_w3),
              moe_back(moe2, moe_w2), post_ffn_g)
    return xf.reshape(BATCH, SEQ, D_MODEL)
```

```python
import functools

import numpy as np
import jax
import jax.numpy as jnp
from jax import lax
from jax.experimental import pallas as pl
from jax.experimental.pallas import tpu as pltpu

F32 = jnp.float32
BF16 = jnp.bfloat16
I32 = jnp.int32

D_MODEL = 1024
BATCH = 8
SEQ = 2048
HALF_SEQ = SEQ // 2
TOKENS = BATCH * SEQ
DEPTH = 2
N_GROUPS = 4
GROUP_DIM = 128
MIX_WIDTH = N_GROUPS * GROUP_DIM
CHUNK = 128
D_IN = 3 * MIX_WIDTH + 2 * D_MODEL
D_FF_DENSE = 2816
N_EXPERTS = 8
D_FF_EXPERT = 3584
RMS_EPS = 1e-6
LN_EPS = 1e-5
LANES = 128

ROW_TILE = 512
MIX_TILE = 512
ROUTER_TILE = 1024
EXPERT_BLOCK = 512
EXPERT_PAD = 256
EXPERT_REGION = 1024
SORT_BLOCK = 256
TOKEN_WINDOW = 1280
COMBINE_TILE = 256
COMBINE_WINDOW = 128
BF16_ROWS = 16
EXPERT_FF_TILE = 1792

SORTED_ROWS = 2 * TOKENS + N_EXPERTS * EXPERT_REGION
N_REGIONS = SORTED_ROWS // EXPERT_REGION
N_SORT_BLOCKS = SORTED_ROWS // SORT_BLOCK
N_COMBINE_TILES = TOKENS // COMBINE_TILE

VMEM_LIMIT = 56 * 1024 * 1024


def _params(*sem):
    return pltpu.CompilerParams(dimension_semantics=sem, vmem_limit_bytes=VMEM_LIMIT)


def _resident(shape):
    nd = len(shape)
    return pl.BlockSpec(shape, lambda *_: (0,) * nd, pipeline_mode=pl.Buffered(1))


def _layer(shape, l):
    nd = len(shape)
    return pl.BlockSpec((None,) + tuple(shape), lambda *_: (l,) + (0,) * nd,
                        pipeline_mode=pl.Buffered(1))


def _rms(x, g):
    return x * lax.rsqrt(jnp.mean(x * x, axis=-1, keepdims=True) + RMS_EPS) * g


def _dot(a, b):
    return jnp.dot(a, b, preferred_element_type=F32)


def _dot_nt(a, b):
    return lax.dot_general(a, b, (((1,), (1,)), ((), ())), preferred_element_type=F32)


def _dot_tn(a, b):
    return lax.dot_general(a, b, (((0,), (0,)), ((), ())), preferred_element_type=F32)


def _gelu(x):
    return 0.5 * x * (1.0 + lax.erf(x * np.float32(np.sqrt(0.5))))


def _cast_once(*pairs):
    chunk = 128

    @pl.when(pl.program_id(0) == 0)
    def _():
        for src_ref, dst_ref in pairs:
            def body(i, carry, src_ref=src_ref, dst_ref=dst_ref):
                rows = pl.ds(pl.multiple_of(i * chunk, chunk), chunk)
                dst_ref[rows, :] = src_ref[rows, :].astype(BF16)
                return carry
            lax.fori_loop(0, src_ref.shape[0] // chunk, body, 0)


def _split_bf16(x):
    hi = x.astype(BF16)
    return hi, (x - hi.astype(F32)).astype(BF16)


def _side_specs(arrays, steps):
    in_specs, out_specs, shapes, operands = [], [], [], []
    for entry in arrays:
        a, l = entry if isinstance(entry, tuple) else (entry, None)
        rows, cols = a.shape[-2:]
        chunk, per = rows // steps, 1
        if chunk % BF16_ROWS:
            chunk, per = 2 * chunk, 2
        assert rows % chunk == 0 and chunk % BF16_ROWS == 0, (a.shape, steps)
        out_specs.append(pl.BlockSpec((chunk, cols), lambda i, per=per: (i // per, 0)))
        if l is None:
            in_specs.append(out_specs[-1])
        else:
            in_specs.append(pl.BlockSpec((None, chunk, cols), lambda i, per=per, l=l: (l, i // per, 0)))
        shapes.append(jax.ShapeDtypeStruct((rows, cols), BF16))
        operands.append(a)
    return in_specs, out_specs, shapes, operands


def _with_side_casts(body, n_in, n_out, n_side):
    def kernel(*refs):
        ins, rest = refs[:n_in], refs[n_in:]
        srcs, rest = rest[:n_side], rest[n_side:]
        outs, rest = rest[:n_out], rest[n_out:]
        dsts, scratch = rest[:n_side], rest[n_side:]
        body(*ins, *outs, *scratch)
        for src, dst in zip(srcs, dsts):
            dst[...] = src[...].astype(BF16)
    return kernel


def _dft_tables():
    k = np.arange(HALF_SEQ, dtype=np.int64)
    scale = 1.0 / np.sqrt(SEQ)
    halves = []
    for p in (0, 1):
        n = 2 * np.arange(HALF_SEQ, dtype=np.int64) + p
        ang = 2.0 * np.pi * ((k[:, None] * n[None, :]) % SEQ).astype(np.float64) / SEQ
        halves.append(np.concatenate([np.cos(ang) * scale, -np.sin(ang) * scale], axis=1))
    pos = np.stack(halves)
    c = np.arange(GROUP_DIM, dtype=np.int64)
    angc = 2.0 * np.pi * ((c[:, None] * c[None, :]) % GROUP_DIM).astype(np.float64) / GROUP_DIM
    scalec = 1.0 / np.sqrt(GROUP_DIM)
    chan = np.concatenate([np.cos(angc) * scalec, np.sin(angc) * scalec], axis=1)
    return pos.astype(np.float32), chan.astype(np.float32)


_POS_DFT, _CHAN_DFT = _dft_tables()


def _fourier_kernel(l, x_ref, g_ref, wa32_ref, chan_ref, pos_ref, out_ref, stage_ref, rhs_ref, wa_ref):
    _cast_once((wa32_ref, wa_ref))
    for half in range(SEQ // HALF_SEQ):
        rows = slice(half * HALF_SEQ, (half + 1) * HALF_SEQ)
        h = _rms(x_ref[0, rows, :], g_ref[l:l + 1, :]).astype(BF16)
        za = _dot(h, wa_ref[...]).astype(BF16)
        for g in range(N_GROUPS):
            xcs = _dot(za[:, g * GROUP_DIM:(g + 1) * GROUP_DIM], chan_ref[...])
            stage_ref[g, rows, :] = xcs[:, 0:GROUP_DIM]
            stage_ref[N_GROUPS + g, rows, :] = xcs[:, GROUP_DIM:2 * GROUP_DIM]
    for g in range(N_GROUPS):
        cols = slice(g * GROUP_DIM, (g + 1) * GROUP_DIM)
        for p in (0, 1):
            rows = pl.ds(p, HALF_SEQ, stride=2)
            rhs_ref[p, 0:HALF_SEQ, cols] = stage_ref[g, rows, :].astype(BF16)
            rhs_ref[p, HALF_SEQ:SEQ, cols] = stage_ref[N_GROUPS + g, rows, :].astype(BF16)

    even = _dot(pos_ref[0], rhs_ref[0])
    odd = _dot(pos_ref[1], rhs_ref[1])
    out_ref[0, 0] = (even + odd).astype(BF16)
    out_ref[0, 1] = (even - odd).astype(BF16)


def _fourier(l, x, g, w_in, chan, pos, side=()):
    side_in, side_out, side_shapes, side_ops = _side_specs(side, BATCH)
    outs = pl.pallas_call(
        _with_side_casts(functools.partial(_fourier_kernel, l), 5, 1, len(side)),
        grid=(BATCH,),
        in_specs=[pl.BlockSpec((1, SEQ, D_MODEL), lambda b: (b, 0, 0)),
                  _resident((DEPTH, D_MODEL)), _layer((D_MODEL, MIX_WIDTH), l),
                  _resident((GROUP_DIM, 2 * GROUP_DIM)), _resident((2, HALF_SEQ, SEQ))] + side_in,
        out_specs=[pl.BlockSpec((1, 2, HALF_SEQ, MIX_WIDTH), lambda b: (b, 0, 0, 0))] + side_out,
        out_shape=[jax.ShapeDtypeStruct((BATCH, 2, HALF_SEQ, MIX_WIDTH), BF16)] + side_shapes,
        scratch_shapes=[pltpu.VMEM((2 * N_GROUPS, SEQ, GROUP_DIM), F32),
                        pltpu.VMEM((2, SEQ, MIX_WIDTH), BF16),
                        pltpu.VMEM((D_MODEL, MIX_WIDTH), BF16)],
        compiler_params=_params("arbitrary"),
        name="fourier_mix",
    )(x, g, w_in, chan, pos, *side_ops)
    return outs[0].reshape(TOKENS, MIX_WIDTH), outs[1:]


def _mix_common(l, x_ref, fa_ref, prg_ref, win_ref, ws_ref, bs_ref, lng_ref, lnb_ref,
                wf_ref, wg_ref, wo_ref, pg_ref, x1_ref, sg_ref):
    x = x_ref[...]
    h = _rms(x, prg_ref[l:l + 1, :]).astype(BF16)
    v = _gelu(_dot(h, win_ref[:, 2 * MIX_WIDTH:3 * MIX_WIDTH]))
    u = _gelu(_dot(h, win_ref[:, MIX_WIDTH:2 * MIX_WIDTH]))
    gate = jax.nn.sigmoid(_dot(h, win_ref[:, 3 * MIX_WIDTH:D_IN]))
    ya = _dot(fa_ref[...], wf_ref[...])
    mu = jnp.mean(v, axis=-1, keepdims=True)
    vc = v - mu
    var = jnp.mean(vc * vc, axis=-1, keepdims=True)
    vln = (vc * lax.rsqrt(var + LN_EPS) * lng_ref[l:l + 1, :] + lnb_ref[l:l + 1, :]).astype(BF16)
    ws = [ws_ref[g].astype(BF16) for g in range(N_GROUPS)]
    for c in range(x_ref.shape[0] // CHUNK):
        rows = slice(c * CHUNK, (c + 1) * CHUNK)
        for g in range(N_GROUPS):
            cols = slice(g * GROUP_DIM, (g + 1) * GROUP_DIM)
            sv = _dot(ws[g], vln[rows, cols]) + bs_ref[:, cols]
            sg_ref[rows, cols] = (u[rows, cols] * sv).astype(BF16)
    yb = _dot(sg_ref[...], wg_ref[...])
    m = (gate[:, 0:D_MODEL] * ya + gate[:, D_MODEL:2 * D_MODEL] * yb).astype(BF16)
    half = x_ref.shape[0] // 2
    for r in range(2):
        rows = slice(r * half, (r + 1) * half)
        y = _dot(m[rows, :], wo_ref[...])
        x1_ref[rows, :] = x[rows, :] + _rms(y, pg_ref[l:l + 1, :])


def _mix_kernel(l, *refs):
    _mix_common(l, *refs)


def _mix_router_kernel(l, *refs):
    (*ins, fg_ref, rwt_ref, rb_ref, tri_ref,
     x1_ref, h_ref, gw_ref, sel_ref, rank_ref, cnt_ref, sg_ref, carry_ref) = refs

    @pl.when(pl.program_id(0) == 0)
    def _():
        carry_ref[...] = jnp.zeros_like(carry_ref)

    _mix_common(l, *ins, x1_ref, sg_ref)
    h = _rms(x1_ref[...], fg_ref[l:l + 1, :])
    h_hi, h_lo = _split_bf16(h)
    h_ref[...] = h_hi

    w_hi, w_lo = _split_bf16(rwt_ref[...])
    logits = _dot_nt(w_hi, h_hi) + _dot_nt(w_hi, h_lo) + _dot_nt(w_lo, h_hi) + rb_ref[...]
    row = lax.broadcasted_iota(I32, logits.shape, 0)
    m1 = jnp.max(logits, axis=0, keepdims=True)
    i1 = jnp.min(jnp.where(logits == m1, row, N_EXPERTS), axis=0, keepdims=True)
    rest = jnp.where(row == i1, -jnp.inf, logits)
    m2 = jnp.max(rest, axis=0, keepdims=True)
    i2 = jnp.min(jnp.where(rest == m2, row, N_EXPERTS), axis=0, keepdims=True)
    e2 = jnp.exp(m2 - m1)
    den = 1.0 + e2
    gw_ref[...] = jnp.where(row == i1, 1.0 / den, 0.0) + jnp.where(row == i2, e2 / den, 0.0)
    sel = jnp.where((row == i1) | (row == i2), 1.0, 0.0)
    sel_ref[...] = sel.astype(I32)
    before = _dot(sel.astype(BF16), tri_ref[...]) + carry_ref[...]
    rank_ref[...] = before.astype(I32)
    carry_ref[...] = carry_ref[...] + jnp.sum(sel, axis=1, keepdims=True)
    cnt_ref[...] = carry_ref[...].astype(I32)


def _mix_specs(l, tile):
    row = lambda w: pl.BlockSpec((tile, w), lambda i: (i, 0))
    in_specs = [row(D_MODEL), row(MIX_WIDTH), _resident((DEPTH, D_MODEL)), _resident((D_MODEL, D_IN)),
                _layer((N_GROUPS, CHUNK, CHUNK), l), _layer((CHUNK, MIX_WIDTH), l),
                _resident((DEPTH, MIX_WIDTH)), _resident((DEPTH, MIX_WIDTH)),
                _resident((MIX_WIDTH, D_MODEL)), _resident((MIX_WIDTH, D_MODEL)),
                _resident((D_MODEL, D_MODEL)), _resident((DEPTH, D_MODEL))]
    return row, in_specs, [pltpu.VMEM((tile, MIX_WIDTH), BF16)]


def _mixer_body(l, x, fa, pre_g, w_in, w_s, b_full, ln_g, ln_b, w_f, w_g, w_o, post_g, side=()):
    row, in_specs, scratch = _mix_specs(l, MIX_TILE)
    steps = TOKENS // MIX_TILE
    side_in, side_out, side_shapes, side_ops = _side_specs(side, steps)
    outs = pl.pallas_call(
        _with_side_casts(functools.partial(_mix_kernel, l), len(in_specs), 1, len(side)),
        grid=(steps,),
        in_specs=in_specs + side_in,
        out_specs=[row(D_MODEL)] + side_out,
        out_shape=[jax.ShapeDtypeStruct((TOKENS, D_MODEL), F32)] + side_shapes,
        scratch_shapes=scratch,
        compiler_params=_params("arbitrary"),
        name="mixer_body",
    )(x, fa, pre_g, w_in, w_s, b_full, ln_g, ln_b, w_f, w_g, w_o, post_g, *side_ops)
    return outs[0], outs[1:]


def _mixer_body_router(l, x, fa, pre_g, w_in, w_s, b_full, ln_g, ln_b, w_f, w_g, w_o, post_g,
                       ffn_g, router_wt, router_b, tri):
    row, in_specs, scratch = _mix_specs(l, ROUTER_TILE)
    in_specs += [_resident((DEPTH, D_MODEL)), _layer((N_EXPERTS, D_MODEL), l // 2),
                 _layer((N_EXPERTS, 1), l // 2), _resident((ROUTER_TILE, ROUTER_TILE))]
    col = pl.BlockSpec((N_EXPERTS, ROUTER_TILE), lambda i: (0, i))
    per_token = lambda dt: jax.ShapeDtypeStruct((N_EXPERTS, TOKENS), dt)
    return pl.pallas_call(
        functools.partial(_mix_router_kernel, l),
        grid=(TOKENS // ROUTER_TILE,),
        in_specs=in_specs,
        out_specs=[row(D_MODEL), row(D_MODEL), col, col, col,
                   pl.BlockSpec((N_EXPERTS, 1), lambda i: (0, 0))],
        out_shape=[jax.ShapeDtypeStruct((TOKENS, D_MODEL), F32),
                   jax.ShapeDtypeStruct((TOKENS, D_MODEL), BF16),
                   per_token(F32), per_token(I32), per_token(I32),
                   jax.ShapeDtypeStruct((N_EXPERTS, 1), I32)],
        scratch_shapes=scratch + [pltpu.VMEM((N_EXPERTS, 1), F32)],
        compiler_params=_params("arbitrary"),
        name="mixer_body_router",
    )(x, fa, pre_g, w_in, w_s, b_full, ln_g, ln_b, w_f, w_g, w_o, post_g,
      ffn_g, router_wt, router_b, tri)


def _dense_ffn_kernel(l, x_ref, fg_ref, w1_ref, w3_ref, w2_ref, pg_ref, out_ref):
    half = x_ref.shape[0] // 2
    for r in range(2):
        rows = slice(r * half, (r + 1) * half)
        x = x_ref[rows, :]
        h = _rms(x, fg_ref[l:l + 1, :]).astype(BF16)
        a = _dot(h, w1_ref[...])
        b = _dot(h, w3_ref[...])
        y = _dot((jax.nn.silu(a) * b).astype(BF16), w2_ref[...])
        out_ref[rows, :] = x + _rms(y, pg_ref[l:l + 1, :])


def _dense_ffn(l, x, ffn_g, w1, w3, w2, post_g, side=()):
    row = pl.BlockSpec((ROW_TILE, D_MODEL), lambda i: (i, 0))
    steps = TOKENS // ROW_TILE
    side_in, side_out, side_shapes, side_ops = _side_specs(side, steps)
    outs = pl.pallas_call(
        _with_side_casts(functools.partial(_dense_ffn_kernel, l), 6, 1, len(side)),
        grid=(steps,),
        in_specs=[row, _resident((DEPTH, D_MODEL)), _resident((D_MODEL, D_FF_DENSE)),
                  _resident((D_MODEL, D_FF_DENSE)), _resident((D_FF_DENSE, D_MODEL)),
                  _resident((DEPTH, D_MODEL))] + side_in,
        out_specs=[row] + side_out,
        out_shape=[jax.ShapeDtypeStruct((TOKENS, D_MODEL), F32)] + side_shapes,
        compiler_params=_params("arbitrary"),
        name="dense_ffn",
    )(x, ffn_g, w1, w3, w2, post_g, *side_ops)
    return outs[0], outs[1:]


def _window_hits(k, block, start_ref, dest_ref):
    wanted = start_ref[block] + k * TOKEN_WINDOW
    begin = pl.multiple_of(jnp.minimum(wanted, TOKENS - TOKEN_WINDOW), LANES)
    tok = lax.broadcasted_iota(I32, (1, TOKEN_WINDOW), 1) + begin
    dest = jnp.where(tok >= wanted, dest_ref[:, pl.ds(begin, TOKEN_WINDOW)], -1)
    rows = lax.broadcasted_iota(I32, (SORT_BLOCK, TOKEN_WINDOW), 0) + block * SORT_BLOCK
    return begin, rows == dest


def _dispatch_kernel(exp_ref, start_ref, nwin_ref, dest_ref, gw_ref, h_ref, xs_ref, gs_ref):
    j = pl.program_id(0)

    def window(k):
        begin, hit = _window_hits(k, j, start_ref, dest_ref)
        onehot = jnp.where(hit, 1.0, 0.0).astype(BF16)
        picked = _dot(onehot, h_ref[pl.ds(begin, TOKEN_WINDOW), :]).astype(BF16)
        gw = gw_ref[:, pl.ds(begin, TOKEN_WINDOW)]
        return picked, jnp.sum(jnp.where(hit, gw, 0.0), axis=-1, keepdims=True)

    @pl.when(nwin_ref[j] == 0)
    def _():
        xs_ref[...] = jnp.zeros_like(xs_ref)
        gs_ref[...] = jnp.zeros_like(gs_ref)

    @pl.when(nwin_ref[j] > 0)
    def _():
        xs_ref[...], gs_ref[...] = window(0)

        def more(k, carry):
            picked, gsum = window(k)
            xs_ref[...] += picked
            gs_ref[...] += gsum
            return carry
        lax.fori_loop(1, nwin_ref[j], more, 0)


def _per_expert_row():
    return pl.BlockSpec((None, 1, TOKENS),
                        lambda j, e, s, n: (e[jnp.minimum(j, N_SORT_BLOCKS - 1)], 0, 0))


def _dispatch(plan, dest3, gw3, h):
    exp, start, nwin = plan
    grid_spec = pltpu.PrefetchScalarGridSpec(
        num_scalar_prefetch=3,
        grid=(N_SORT_BLOCKS,),
        in_specs=[_per_expert_row(), _per_expert_row(),
                  pl.BlockSpec((TOKENS, D_MODEL), lambda j, e, s, n: (0, 0),
                               pipeline_mode=pl.Buffered(1))],
        out_specs=[pl.BlockSpec((SORT_BLOCK, D_MODEL), lambda j, e, s, n: (j, 0)),
                   pl.BlockSpec((SORT_BLOCK, 1), lambda j, e, s, n: (j, 0))],
    )
    return pl.pallas_call(
        _dispatch_kernel,
        grid_spec=grid_spec,
        out_shape=[jax.ShapeDtypeStruct((SORTED_ROWS, D_MODEL), BF16),
                   jax.ShapeDtypeStruct((SORTED_ROWS, 1), F32)],
        compiler_params=_params("arbitrary"),
        name="moe_dispatch",
    )(exp, start, nwin, dest3, gw3, h)


def _expert_kernel(rexp_ref, nunit_ref, xs_ref, gs_ref, w1_ref, w3_ref, w2_ref, ys_ref, acc_ref):
    q = pl.program_id(0)
    f = pl.program_id(1)
    last = pl.num_programs(1) - 1
    nunit = nunit_ref[q]

    @pl.when((q == 0) & (f == 0))
    def _():
        acc_ref[...] = jnp.zeros_like(acc_ref)

    def swiglu_part(n_rows):
        for start in range(0, n_rows, EXPERT_BLOCK):
            rows = slice(start, min(start + EXPERT_BLOCK, n_rows))
            x = xs_ref[rows, :]
            a = _dot(x, w1_ref[...])
            b = _dot(x, w3_ref[...])
            part = _dot((jax.nn.silu(a) * b).astype(BF16), w2_ref[...])
            acc_ref[rows, :] = jnp.where(f == 0, 0.0, acc_ref[rows, :]) + part

        @pl.when(f == last)
        def _():
            ys_ref[0:n_rows, :] = (acc_ref[0:n_rows, :] * gs_ref[0:n_rows, :]).astype(BF16)
            if n_rows < EXPERT_REGION:
                ys_ref[n_rows:EXPERT_REGION, :] = jnp.zeros((EXPERT_REGION - n_rows, D_MODEL), BF16)

    for units in range(1, EXPERT_REGION // EXPERT_PAD + 1):
        pl.when(nunit == units)(functools.partial(swiglu_part, units * EXPERT_PAD))

    @pl.when((nunit == 0) & (f == last))
    def _():
        ys_ref[...] = jnp.zeros_like(ys_ref)


def _experts(rexp, nunit, xs, gs, w1, w3, w2):
    n_ff = D_FF_EXPERT // EXPERT_FF_TILE
    tile = lambda q, f: jnp.where(q % 2 == 0, f, n_ff - 1 - f)
    grid_spec = pltpu.PrefetchScalarGridSpec(
        num_scalar_prefetch=2,
        grid=(N_REGIONS, D_FF_EXPERT // EXPERT_FF_TILE),
        in_specs=[
            pl.BlockSpec((EXPERT_REGION, D_MODEL), lambda q, f, re, nh: (q, 0)),
            pl.BlockSpec((EXPERT_REGION, 1), lambda q, f, re, nh: (q, 0)),
            pl.BlockSpec((None, D_MODEL, EXPERT_FF_TILE), lambda q, f, re, nh: (re[q], 0, tile(q, f))),
            pl.BlockSpec((None, D_MODEL, EXPERT_FF_TILE), lambda q, f, re, nh: (re[q], 0, tile(q, f))),
            pl.BlockSpec((None, EXPERT_FF_TILE, D_MODEL), lambda q, f, re, nh: (re[q], tile(q, f), 0)),
        ],
        out_specs=pl.BlockSpec((EXPERT_REGION, D_MODEL), lambda q, f, re, nh: (q, 0)),
        scratch_shapes=[pltpu.VMEM((EXPERT_REGION, D_MODEL), F32)],
    )
    return pl.pallas_call(
        _expert_kernel,
        grid_spec=grid_spec,
        out_shape=jax.ShapeDtypeStruct((SORTED_ROWS, D_MODEL), BF16),
        compiler_params=_params("arbitrary", "arbitrary"),
        name="moe_experts",
    )(rexp, nunit, xs, gs, w1, w3, w2)


def _combine_copies(wstart_ref, ys_hbm, buf_ref, sem_ref, tile, rnd, slot):
    copies = []
    for e in range(N_EXPERTS):
        wanted = wstart_ref[tile * N_EXPERTS + e] + rnd * COMBINE_WINDOW
        begin = pl.multiple_of(jnp.minimum(wanted, SORTED_ROWS - COMBINE_WINDOW), BF16_ROWS)
        copies.append(pltpu.make_async_copy(
            ys_hbm.at[pl.ds(begin, COMBINE_WINDOW), :],
            buf_ref.at[slot, pl.ds(e * COMBINE_WINDOW, COMBINE_WINDOW), :],
            sem_ref.at[slot, e]))
    return copies


def _combine_kernel(l, wstart_ref, nround_ref, dest_ref, ys_hbm, x_ref, pg_ref, out_ref,
                    buf_ref, sem_ref):
    i = pl.program_id(0)
    n_tiles = pl.num_programs(0)
    slot = lax.rem(i, 2)

    @pl.when(i == 0)
    def _():
        for e, c in enumerate(_combine_copies(wstart_ref, ys_hbm, buf_ref, sem_ref, 0, 0, 0)):
            c.start(priority=e % 2)

    @pl.when(i + 1 < n_tiles)
    def _():
        copies = _combine_copies(wstart_ref, ys_hbm, buf_ref, sem_ref, i + 1, 0, 1 - slot)
        for e, c in enumerate(copies):
            c.start(priority=e % 2)

    def gather(rnd, fetch_slot):
        sub = lax.broadcasted_iota(I32, (COMBINE_WINDOW, 1), 0)
        pieces = []
        for e in range(N_EXPERTS):
            wanted = wstart_ref[i * N_EXPERTS + e] + rnd * COMBINE_WINDOW
            begin = jnp.minimum(wanted, SORTED_ROWS - COMBINE_WINDOW)
            rowid = sub + begin
            rowid = jnp.where(rowid >= wanted, rowid, -2)
            pieces.append(jnp.where(rowid == dest_ref[e:e + 1, :], 1.0, 0.0).astype(BF16))
        onehot = jnp.concatenate(pieces, axis=0)
        return _dot_tn(onehot, buf_ref[fetch_slot])

    for c in _combine_copies(wstart_ref, ys_hbm, buf_ref, sem_ref, i, 0, slot):
        c.wait()
    y = gather(0, slot)

    def extra(rnd, y):
        copies = _combine_copies(wstart_ref, ys_hbm, buf_ref, sem_ref, i, rnd, 2)
        for c in copies:
            c.start()
        for c in copies:
            c.wait()
        return y + gather(rnd, 2)

    y = lax.fori_loop(1, nround_ref[i], extra, y)
    out_ref[...] = x_ref[...] + _rms(y, pg_ref[l:l + 1, :])


def _combine(l, wstart, nround, dest, ys, x, post_g):
    tile = lambda i, ws, nr: (i, 0)
    grid_spec = pltpu.PrefetchScalarGridSpec(
        num_scalar_prefetch=2,
        grid=(TOKENS // COMBINE_TILE,),
        in_specs=[
            pl.BlockSpec((N_EXPERTS, COMBINE_TILE), lambda i, ws, nr: (0, i)),
            pl.BlockSpec(memory_space=pl.ANY),
            pl.BlockSpec((COMBINE_TILE, D_MODEL), tile),
            pl.BlockSpec((DEPTH, D_MODEL), lambda i, ws, nr: (0, 0)),
        ],
        out_specs=pl.BlockSpec((COMBINE_TILE, D_MODEL), tile),
        scratch_shapes=[pltpu.VMEM((3, N_EXPERTS * COMBINE_WINDOW, D_MODEL), BF16),
                        pltpu.SemaphoreType.DMA((3, N_EXPERTS))],
    )
    return pl.pallas_call(
        functools.partial(_combine_kernel, l),
        grid_spec=grid_spec,
        out_shape=jax.ShapeDtypeStruct((TOKENS, D_MODEL), F32),
        compiler_params=_params("arbitrary"),
        name="moe_combine",
    )(wstart, nround, dest, ys, x, post_g)


def _routing_plan(sel, rank, counts):
    counts = counts.reshape(N_EXPERTS)
    padded = ((counts + EXPERT_PAD - 1) // EXPERT_PAD) * EXPERT_PAD
    region = ((counts + EXPERT_REGION - 1) // EXPERT_REGION) * EXPERT_REGION
    offs = jnp.cumsum(region) - region
    dest = jnp.where(sel == 1, rank + offs[:, None], -1)

    n_reg = region // EXPERT_REGION
    cum_reg = jnp.cumsum(n_reg)
    qs = jnp.arange(N_REGIONS, dtype=I32)
    qc = jnp.minimum(qs, cum_reg[-1] - 1)[:, None]
    owner = ((qc >= (cum_reg - n_reg)[None, :]) & (qc < cum_reg[None, :])).astype(I32)
    pick = lambda own, v: jnp.sum(own * v[None, :], axis=1)
    experts = jnp.arange(N_EXPERTS, dtype=I32)
    rexp = pick(owner, experts).astype(I32)
    rows_left = pick(owner, offs + padded) - qs * EXPERT_REGION
    n_unit = jnp.clip(rows_left // EXPERT_PAD, 0, EXPERT_REGION // EXPERT_PAD).astype(I32)

    group_end = (rank + sel)[:, LANES - 1::LANES]
    js = jnp.arange(N_SORT_BLOCKS, dtype=I32)
    owner_s = jnp.repeat(owner, EXPERT_REGION // SORT_BLOCK, axis=0)
    e_s = pick(owner_s, experts)
    counts_s = pick(owner_s, counts)
    lo_row = js * SORT_BLOCK - pick(owner_s, offs)
    has_rows = lo_row < counts_s
    hi_row = jnp.minimum(lo_row + SORT_BLOCK, counts_s)
    ends = jnp.sum(owner_s[:, :, None] * group_end[None, :, :], axis=1)
    g_lo = jnp.sum((ends <= lo_row[:, None]).astype(I32), axis=1)
    g_hi = jnp.sum((ends < hi_row[:, None]).astype(I32), axis=1)
    span = (g_hi - g_lo + 1) * LANES
    n_win = jnp.where(has_rows, (span + TOKEN_WINDOW - 1) // TOKEN_WINDOW, 0).astype(I32)
    start = jnp.where(has_rows, g_lo * LANES, 0).astype(I32)
    d_plan = (e_s.astype(I32), start, n_win)

    tile_lo = rank[:, ::COMBINE_TILE]
    tile_hi = jnp.concatenate([tile_lo[:, 1:], counts[:, None]], axis=1)
    lo = tile_lo + offs[:, None]
    hi = tile_hi + offs[:, None]
    wstart = jnp.minimum((lo // BF16_ROWS) * BF16_ROWS, SORTED_ROWS - COMBINE_WINDOW)
    rounds = jnp.where(hi > lo, (hi - wstart + COMBINE_WINDOW - 1) // COMBINE_WINDOW, 0)
    n_round = jnp.maximum(jnp.max(rounds, axis=0), 1).astype(I32)
    c_plan = (wstart.T.reshape(-1).astype(I32), n_round)
    return dest, (rexp, n_unit), d_plan, c_plan


def _moe(l, x1, h, gw, sel, rank, counts, w1, w3, w2, post_g):
    dest, e_plan, d_plan, c_plan = _routing_plan(sel, rank, counts)
    dest3 = dest.reshape(N_EXPERTS, 1, TOKENS)
    gw3 = gw.reshape(N_EXPERTS, 1, TOKENS)
    xs, gs = _dispatch(d_plan, dest3, gw3, h)
    ys = _experts(*e_plan, xs, gs, w1, w3, w2)
    return _combine(l, *c_plan, dest, ys, x1, post_g)


def kernel(x, pre_mix_g, post_mix_g, w_in, w_fourier, w_gmlp, w_out, w_spatial, b_spatial,
           gmlp_ln_g, gmlp_ln_b, pre_ffn_g, post_ffn_g, ffn_w1, ffn_w3, ffn_w2,
           router_w, router_b, moe_w1, moe_w3, moe_w2):
    pos = jnp.asarray(_POS_DFT).astype(BF16)
    chan = jnp.asarray(_CHAN_DFT).astype(BF16)
    tri = jnp.asarray(np.triu(np.ones((ROUTER_TILE, ROUTER_TILE), np.float32), 1)).astype(BF16)

    b_full = jnp.repeat(jnp.swapaxes(b_spatial, 1, 2), GROUP_DIM, axis=2)
    router_wt = jnp.swapaxes(router_w, 1, 2)
    router_b = router_b.reshape(router_b.shape[0], N_EXPERTS, 1)

    assert DEPTH == 2
    moe_rows = lambda w: w[0].reshape(-1, w.shape[-1])
    moe_back = lambda w, like: w.reshape(like.shape[1:])
    mixer_weights = lambda l: [(w, l) for w in (w_in, w_fourier, w_gmlp, w_out)]

    xf = x.reshape(TOKENS, D_MODEL)
    per_batch = lambda v: v.reshape(BATCH, SEQ, D_MODEL)
    fa, (win, wf, wg, wo) = _fourier(0, per_batch(xf), pre_mix_g, w_in, chan, pos,
                                     side=mixer_weights(0))
    x1, (ffn1, ffn3, ffn2, moe1) = _mixer_body(
        0, xf, fa, pre_mix_g, win, w_spatial, b_full, gmlp_ln_g, gmlp_ln_b, wf, wg, wo, post_mix_g,
        side=(ffn_w1[0], ffn_w3[0], ffn_w2[0], moe_rows(moe_w1)))
    xf, (moe3, moe2) = _dense_ffn(0, x1, pre_ffn_g, ffn1, ffn3, ffn2, post_ffn_g,
                                  side=(moe_rows(moe_w3), moe_rows(moe_w2)))

    fa, (win, wf, wg, wo) = _fourier(1, per_batch(xf), pre_mix_g, w_in, chan, pos,
                                     side=mixer_weights(1))
    x1, h, gw, sel, rank, counts = _mixer_body_router(
        1, xf, fa, pre_mix_g, win, w_spatial, b_full, gmlp_ln_g, gmlp_ln_b, wf, wg, wo, post_mix_g,
        pre_ffn_g, router_wt, router_b, tri)
    xf = _moe(1, x1, h, gw, sel, rank, counts, moe_back(moe1, moe_w1), moe_back(moe3, moe_w3),
              moe_back(moe2, moe_w2), post_ffn_g)
    return xf.reshape(BATCH, SEQ, D_MODEL)
```
